```python
import math
import jax, jax.numpy as jnp
from jax import lax
import numpy as np

D_MODEL = 1024
BATCH = 16
SEQ = 2048
DEPTH = 1

DIFF_HEADS = 4
DIFF_HD = 64
DIFF_VD = 2 * DIFF_HD
DIFF_W = DIFF_HEADS * DIFF_VD
NSA_HEADS = 8
NSA_KV = 2
NSA_GQ = NSA_HEADS // NSA_KV
NSA_HD = 64
NSA_W = NSA_HEADS * NSA_HD
MIX_W = DIFF_W + NSA_W
CMP_LEN = 32
CMP_STRIDE = 16
CMP_HIDDEN = 256
SLC_BLOCK = 64
SLC_TOPK = 16
WINDOW = 512
Q_BLOCK = 128
SLC_Q_CHUNK = 64
N_BUCKETS = 32
MAX_DISTANCE = 128
N_BIAS_HEADS = DIFF_HEADS + NSA_HEADS
D_FF = -(-8 * D_MODEL // (3 * 256)) * 256
NSA_KV_W = NSA_KV * NSA_HD
N_IN = 3 * DIFF_HEADS * 2 * DIFF_HD + NSA_W + 6 * NSA_KV_W + 3 * NSA_HEADS
NEG = -1e30
EPS = 1e-6

kernel_name = "hymba_diff_nsa_block"


def rms_norm(x, g):
    xf = x.astype(jnp.float32)
    y = xf * lax.rsqrt(jnp.mean(xf * xf, axis=-1, keepdims=True) + EPS)
    return (y * g.astype(jnp.float32)).astype(x.dtype)


def t5_bucket(dist):
    dist = jnp.maximum(dist, 0)
    max_exact = N_BUCKETS // 2
    large = max_exact + (jnp.log(jnp.maximum(dist, 1).astype(jnp.float32) / max_exact)
                         / math.log(MAX_DISTANCE / max_exact) * (N_BUCKETS - max_exact)).astype(jnp.int32)
    large = jnp.minimum(large, N_BUCKETS - 1)
    return jnp.where(dist < max_exact, dist, large)


def diff_attention(q, k, v, lam, subln_g, lambda_init, bias_tab):
    B, H, _, T, d = q.shape
    nb = T // Q_BLOCK
    scale = d ** -0.5
    q_b = q.reshape(B, H, 2, nb, Q_BLOCK, d).transpose(3, 0, 1, 2, 4, 5)
    kpos = jnp.arange(T)

    def block(args):
        q_blk, i = args
        qpos = i * Q_BLOCK + jnp.arange(Q_BLOCK)
        dist = qpos[:, None] - kpos[None, :]
        bias = bias_tab[t5_bucket(dist)].astype(jnp.float32).transpose(2, 0, 1)
        s = jnp.einsum('bhiqd,bhikd->bhiqk', q_blk, k).astype(jnp.float32) * scale + bias[None, :, None]
        p = jax.nn.softmax(jnp.where((dist >= 0)[None, None, None], s, NEG), axis=-1)
        a = p[:, :, 0] - lam * p[:, :, 1]
        return jnp.einsum('bhqk,bhkv->bhqv', a.astype(v.dtype), v)

    o = lax.map(block, (q_b, jnp.arange(nb)))
    o = o.transpose(1, 2, 0, 3, 4).reshape(B, H, T, -1)
    o = rms_norm(o, subln_g) * (1.0 - lambda_init)
    return o.transpose(0, 2, 1, 3).reshape(B, T, -1)


def nsa_attention(q, k_c, v_c, k_s, v_s, k_w, v_w, gates,
                  pos_k, w1_k, w2_k, pos_v, w1_v, w2_v, bias_nsa):
    B, G, Hg, T, d = q.shape
    dt = q.dtype
    scale = d ** -0.5
    tpos = jnp.arange(T)

    n_cmp = (T - CMP_LEN) // CMP_STRIDE + 1
    cmp_start = jnp.arange(n_cmp) * CMP_STRIDE
    cmp_end = cmp_start + CMP_LEN - 1
    win = cmp_start[:, None] + jnp.arange(CMP_LEN)[None, :]

    def compress(kk, pos, w1, w2):
        blocks = (kk[:, :, win] + pos).reshape(B, G, n_cmp, CMP_LEN * d)
        return jax.nn.gelu(blocks @ w1) @ w2

    kc = compress(k_c, pos_k, w1_k, w2_k)
    vc = compress(v_c, pos_v, w1_v, w2_v)
    cmask = cmp_end[None, :] <= tpos[:, None]
    has_cmp = (tpos >= CMP_LEN - 1).astype(jnp.float32)
    s = jnp.einsum('bgqtd,bgnd->bgqtn', q, kc).astype(jnp.float32) * scale
    p_cmp = jax.nn.softmax(jnp.where(cmask, s, NEG), axis=-1) * has_cmp[:, None]
    o_cmp = jnp.einsum('bgqtn,bgnd->bgqtd', p_cmp.astype(dt), vc)

    n_slc = T // SLC_BLOCK
    slc_start = jnp.arange(n_slc) * SLC_BLOCK
    overlap = ((cmp_start[:, None] < slc_start[None, :] + SLC_BLOCK)
               & (cmp_end[:, None] >= slc_start[None, :])).astype(jnp.float32)
    imp = jnp.einsum('bgqtn,nj->bgtj', p_cmp, overlap)
    cur = tpos // SLC_BLOCK
    j = jnp.arange(n_slc)
    valid = slc_start[None, :] <= tpos[:, None]
    forced = (j[None, :] == 0) | (j[None, :] == cur[:, None]) | (j[None, :] == cur[:, None] - 1)
    imp = jnp.where(forced, jnp.inf, jnp.where(valid, imp, -jnp.inf))
    k_top = min(SLC_TOPK, n_slc)
    _, sel = lax.top_k(imp, k_top)

    ks_blk = k_s.reshape(B, G, n_slc, SLC_BLOCK, d)
    vs_blk = v_s.reshape(B, G, n_slc, SLC_BLOCK, d)
    bias_g = bias_nsa.transpose(1, 0, 2)
    nc = T // SLC_Q_CHUNK
    q_ch = q.reshape(B, G, Hg, nc, SLC_Q_CHUNK, d).transpose(3, 0, 1, 2, 4, 5)
    sel_ch = sel.reshape(B, G, nc, SLC_Q_CHUNK, k_top).transpose(2, 0, 1, 3, 4)
    bi = jnp.arange(B)[:, None, None, None]
    gi = jnp.arange(G)[None, :, None, None]
    gi5 = jnp.arange(G)[None, :, None, None, None]

    def slc_chunk(args):
        q_c, sel_c, ci = args
        qpos = ci * SLC_Q_CHUNK + jnp.arange(SLC_Q_CHUNK)
        kg = ks_blk[bi, gi, sel_c]
        vg = vs_blk[bi, gi, sel_c]
        kpos = sel_c[..., None] * SLC_BLOCK + jnp.arange(SLC_BLOCK)
        dist = qpos[None, None, :, None, None] - kpos
        bias = bias_g[gi5, t5_bucket(dist)].astype(jnp.float32).transpose(0, 1, 5, 2, 3, 4)
        sc = jnp.einsum('bgqcd,bgckld->bgqckl', q_c, kg).astype(jnp.float32) * scale + bias
        sc = jnp.where((dist >= 0)[:, :, None], sc, NEG).reshape(B, G, Hg, SLC_Q_CHUNK, -1)
        p = jax.nn.softmax(sc, axis=-1).reshape(B, G, Hg, SLC_Q_CHUNK, k_top, SLC_BLOCK)
        return jnp.einsum('bgqckl,bgckld->bgqcd', p.astype(dt), vg)

    o_slc = lax.map(slc_chunk, (q_ch, sel_ch, jnp.arange(nc)))
    o_slc = o_slc.transpose(1, 2, 3, 0, 4, 5).reshape(B, G, Hg, T, d)

    k_pad = jnp.pad(k_w, ((0, 0), (0, 0), (WINDOW, 0), (0, 0)))
    v_pad = jnp.pad(v_w, ((0, 0), (0, 0), (WINDOW, 0), (0, 0)))
    nb = T // Q_BLOCK
    q_b = q.reshape(B, G, Hg, nb, Q_BLOCK, d).transpose(3, 0, 1, 2, 4, 5)

    def win_block(args):
        q_blk, i = args
        start = i * Q_BLOCK
        kb = lax.dynamic_slice_in_dim(k_pad, start, WINDOW + Q_BLOCK, axis=2)
        vb = lax.dynamic_slice_in_dim(v_pad, start, WINDOW + Q_BLOCK, axis=2)
        qpos = start + jnp.arange(Q_BLOCK)
        kpos = start - WINDOW + jnp.arange(WINDOW + Q_BLOCK)
        dist = qpos[:, None] - kpos[None, :]
        mask = (dist >= 0) & (dist < WINDOW) & (kpos[None, :] >= 0)
        bias = bias_nsa[t5_bucket(dist)].astype(jnp.float32).transpose(2, 3, 0, 1)
        sc = jnp.einsum('bgqtd,bgsd->bgqts', q_blk, kb).astype(jnp.float32) * scale + bias
        p = jax.nn.softmax(jnp.where(mask, sc, NEG), axis=-1)
        return jnp.einsum('bgqts,bgsd->bgqtd', p.astype(dt), vb)

    o_win = lax.map(win_block, (q_b, jnp.arange(nb)))
    o_win = o_win.transpose(1, 2, 3, 0, 4, 5).reshape(B, G, Hg, T, d)

    o = gates[..., 0:1] * o_cmp + gates[..., 1:2] * o_slc + gates[..., 2:3] * o_win
    return o.transpose(0, 3, 1, 2, 4).reshape(B, T, G * Hg * d)


def setup_inputs(seed: int = 0) -> dict:
    key = jax.random.key(seed)
    ks = jax.random.split(key, 24)
    f32 = jnp.float32
    nrm = lambda k, shape, s: (jax.random.normal(k, shape, f32) * s)
    gain = lambda k, shape: 1.0 + 0.05 * jax.random.normal(k, shape, f32)
    L = DEPTH
    return {
        "x": jax.random.normal(ks[0], (BATCH, SEQ, D_MODEL), f32),
        "norm1": gain(ks[1], (L, D_MODEL)),
        "w_in": nrm(ks[2], (L, D_MODEL, N_IN), D_MODEL ** -0.5),
        "lambda_q1": nrm(ks[3], (L, DIFF_HD), 0.1),
        "lambda_k1": nrm(ks[4], (L, DIFF_HD), 0.1),
        "lambda_q2": nrm(ks[5], (L, DIFF_HD), 0.1),
        "lambda_k2": nrm(ks[6], (L, DIFF_HD), 0.1),
        "diff_subln": gain(ks[7], (L, DIFF_VD)),
        "cmp_pos_k": nrm(ks[8], (L, CMP_LEN, NSA_HD), 0.5),
        "cmp_w1_k": nrm(ks[9], (L, CMP_LEN * NSA_HD, CMP_HIDDEN), (CMP_LEN * NSA_HD) ** -0.5),
        "cmp_w2_k": nrm(ks[10], (L, CMP_HIDDEN, NSA_HD), CMP_HIDDEN ** -0.5),
        "cmp_pos_v": nrm(ks[11], (L, CMP_LEN, NSA_HD), 0.5),
        "cmp_w1_v": nrm(ks[12], (L, CMP_LEN * NSA_HD, CMP_HIDDEN), (CMP_LEN * NSA_HD) ** -0.5),
        "cmp_w2_v": nrm(ks[13], (L, CMP_HIDDEN, NSA_HD), CMP_HIDDEN ** -0.5),
        "nsa_norm": gain(ks[14], (L, NSA_W)),
        "w_out": nrm(ks[15], (L, MIX_W, D_MODEL), MIX_W ** -0.5),
        "norm2": gain(ks[16], (L, D_MODEL)),
        "w_gate": nrm(ks[17], (L, D_MODEL, D_FF), D_MODEL ** -0.5),
        "w_up": nrm(ks[18], (L, D_MODEL, D_FF), D_MODEL ** -0.5),
        "w_down": nrm(ks[19], (L, D_FF, D_MODEL), D_FF ** -0.5),
        "rel_bias": nrm(ks[20], (N_BUCKETS, N_BIAS_HEADS), 0.5),
        "final_norm": gain(ks[21], (D_MODEL,)),
    }


def reference(x, norm1, w_in, lambda_q1, lambda_k1, lambda_q2, lambda_k2, diff_subln,
              cmp_pos_k, cmp_w1_k, cmp_w2_k, cmp_pos_v, cmp_w1_v, cmp_w2_v, nsa_norm,
              w_out, norm2, w_gate, w_up, w_down, rel_bias, final_norm):
    B, T, _ = x.shape
    G, Hg, d = NSA_KV, NSA_GQ, NSA_HD
    bias_diff = rel_bias[:, :DIFF_HEADS]
    bias_nsa = rel_bias[:, DIFF_HEADS:].reshape(N_BUCKETS, G, Hg)
    qk_w = DIFF_HEADS * 2 * DIFF_HD
    splits = np.cumsum([qk_w, qk_w, DIFF_W, NSA_W] + [NSA_KV_W] * 6).tolist()
    for l in range(DEPTH):
        lambda_init = 0.8 - 0.6 * math.exp(-0.3 * l)
        h = rms_norm(x, norm1[l])
        proj = h @ w_in[l]
        (dq, dk, dv, nq, kc, vc, ksl, vsl, kw, vw, gt) = jnp.split(proj, splits, axis=-1)

        dq = dq.reshape(B, T, DIFF_HEADS, 2, DIFF_HD).transpose(0, 2, 3, 1, 4)
        dk = dk.reshape(B, T, DIFF_HEADS, 2, DIFF_HD).transpose(0, 2, 3, 1, 4)
        dv = dv.reshape(B, T, DIFF_HEADS, DIFF_VD).transpose(0, 2, 1, 3)
        lam = (jnp.exp(jnp.sum(lambda_q1[l].astype(jnp.float32) * lambda_k1[l].astype(jnp.float32)))
               - jnp.exp(jnp.sum(lambda_q2[l].astype(jnp.float32) * lambda_k2[l].astype(jnp.float32)))
               + lambda_init)
        o_diff = diff_attention(dq, dk, dv, lam, diff_subln[l], lambda_init, bias_diff)

        to_kv = lambda a: a.reshape(B, T, G, d).transpose(0, 2, 1, 3)
        nq = nq.reshape(B, T, G, Hg, d).transpose(0, 2, 3, 1, 4)
        gates = jax.nn.sigmoid(gt.reshape(B, T, G, Hg, 3).transpose(0, 2, 3, 1, 4))
        o_nsa = nsa_attention(nq, to_kv(kc), to_kv(vc), to_kv(ksl), to_kv(vsl), to_kv(kw), to_kv(vw),
                              gates, cmp_pos_k[l], cmp_w1_k[l], cmp_w2_k[l],
                              cmp_pos_v[l], cmp_w1_v[l], cmp_w2_v[l], bias_nsa)
        o_nsa = rms_norm(o_nsa, nsa_norm[l])

        x = x + jnp.concatenate([o_diff, o_nsa], axis=-1) @ w_out[l]
        h = rms_norm(x, norm2[l])
        x = x + (jax.nn.silu(h @ w_gate[l]) * (h @ w_up[l])) @ w_down[l]
    return rms_norm(x, final_norm)
```

```python
import functools
import math

import numpy as np
import jax
import jax.numpy as jnp
from jax import lax
from jax.experimental import pallas as pl
from jax.experimental.pallas import tpu as pltpu

F32 = jnp.float32
BF16 = jnp.bfloat16

D_MODEL = 1024
DIFF_HEADS = 4
DIFF_HD = 64
DIFF_VD = 128
DIFF_W = 512
NSA_HEADS = 8
NSA_KV = 2
NSA_GQ = 4
NSA_HD = 64
NSA_W = 512
CMP_LEN = 32
CMP_STRIDE = 16
CMP_HIDDEN = 256
SLC_BLOCK = 64
SLC_TOPK = 16
WINDOW = 512
N_BUCKETS = 32
MAX_DISTANCE = 128
D_FF = 2816
N_IN = 2840
NEG = -1e30
EPS = 1e-6
LAMBDA_INIT = 0.8 - 0.6 * math.exp(-0.3 * 0)

LANES = 128
N_MAIN = 2560
N_PROJ = 2944
VMEM_LIMIT = 48 * 1024 * 1024

TQ = 128
TK = 128
TM = 512
TQ_CMP = 256


def _t5_thresholds():
    d = np.arange(0, 4 * MAX_DISTANCE)
    max_exact = N_BUCKETS // 2
    val = (np.log(np.maximum(d, 1).astype(np.float32) / np.float32(max_exact))
           / np.float32(math.log(MAX_DISTANCE / max_exact)) * np.float32(N_BUCKETS - max_exact))
    large = np.minimum(max_exact + val.astype(np.int32), N_BUCKETS - 1)
    bucket = np.where(d < max_exact, d, large)
    assert np.all(np.diff(bucket) >= 0) and bucket[-1] == N_BUCKETS - 1
    return [int(np.argmax(bucket >= k)) for k in range(N_BUCKETS)]


T5_THRESH = _t5_thresholds()
assert T5_THRESH[-1] <= TK + 1


def _nt_dot(a, b):
    return lax.dot_general(a, b, (((1,), (1,)), ((), ())), preferred_element_type=F32)


def _dot(a, b):
    return jnp.dot(a, b, preferred_element_type=F32)


def _rms(x, g):
    ms = jnp.mean(x * x, axis=-1, keepdims=True)
    return x * lax.rsqrt(ms + EPS) * g


def _params(sem):
    return pltpu.CompilerParams(dimension_semantics=sem, vmem_limit_bytes=VMEM_LIMIT)


def _inproj_kernel(x_ref, g_ref, w_ref, main_ref, kc_ref, vc_ref, gate_ref):
    h = _rms(x_ref[...], g_ref[...]).astype(BF16)
    for c in range(0, N_MAIN, 256):
        main_ref[:, c:c + 256] = _dot(h, w_ref[:, c:c + 256]).astype(BF16)
    kc_ref[...] = _dot(h, w_ref[:, N_MAIN:N_MAIN + LANES]).astype(BF16)
    vc_ref[...] = _dot(h, w_ref[:, N_MAIN + LANES:N_MAIN + 2 * LANES]).astype(BF16)
    gt = _dot(h, w_ref[:, N_MAIN + 2 * LANES:N_PROJ])
    gate_ref[...] = 1.0 / (1.0 + jnp.exp(-gt))


def _inproj(x2, g1, w_p):
    n = x2.shape[0]
    return pl.pallas_call(
        _inproj_kernel,
        grid=(n // TM,),
        in_specs=[pl.BlockSpec((TM, D_MODEL), lambda i: (i, 0)),
                  pl.BlockSpec((1, D_MODEL), lambda i: (0, 0)),
                  pl.BlockSpec((D_MODEL, N_PROJ), lambda i: (0, 0))],
        out_specs=[pl.BlockSpec((TM, N_MAIN), lambda i: (i, 0)),
                   pl.BlockSpec((TM, LANES), lambda i: (i, 0)),
                   pl.BlockSpec((TM, LANES), lambda i: (i, 0)),
                   pl.BlockSpec((TM, LANES), lambda i: (i, 0))],
        out_shape=[jax.ShapeDtypeStruct((n, N_MAIN), BF16),
                   jax.ShapeDtypeStruct((n, LANES), BF16),
                   jax.ShapeDtypeStruct((n, LANES), BF16),
                   jax.ShapeDtypeStruct((n, LANES), F32)],
        compiler_params=_params(("parallel",)),
    )(x2, g1, w_p)


def _gelu_tanh(x):
    return x * (0.5 * (1.0 + jnp.tanh(math.sqrt(2.0 / math.pi) * (x + 0.044715 * (x * x * x)))))


def _compress_kernel(rk_ref, rv_ref, pos_ref, w1k_ref, w2k_ref, w1v_ref, w2v_ref, kc_ref, vc_ref):
    def comp(r_ref, p_top, p_bot, w1_ref, w2_ref):
        r = r_ref[0].astype(F32)
        top = _dot((r + p_top).astype(BF16), w1_ref[0])
        bot = _dot((r + p_bot).astype(BF16), w1_ref[1])
        nrow = top.shape[0]
        hid = top + pltpu.roll(bot, nrow - 1, 0)
        return _dot(_gelu_tanh(hid).astype(BF16), w2_ref[...]).astype(BF16)

    kc_ref[0] = comp(rk_ref, pos_ref[0:1, :], pos_ref[1:2, :], w1k_ref, w2k_ref)
    vc_ref[0] = comp(rv_ref, pos_ref[2:3, :], pos_ref[3:4, :], w1v_ref, w2v_ref)


def _compress(rk, rv, pos, w1k, w2k, w1v, w2v):
    bsz, nrow, width = rk.shape
    hid2 = 2 * CMP_HIDDEN
    return pl.pallas_call(
        _compress_kernel,
        grid=(bsz,),
        in_specs=[pl.BlockSpec((1, nrow, width), lambda b: (b, 0, 0)),
                  pl.BlockSpec((1, nrow, width), lambda b: (b, 0, 0)),
                  pl.BlockSpec((8, width), lambda b: (0, 0)),
                  pl.BlockSpec((2, width, hid2), lambda b: (0, 0, 0)),
                  pl.BlockSpec((hid2, LANES), lambda b: (0, 0)),
                  pl.BlockSpec((2, width, hid2), lambda b: (0, 0, 0)),
                  pl.BlockSpec((hid2, LANES), lambda b: (0, 0))],
        out_specs=[pl.BlockSpec((1, nrow, LANES), lambda b: (b, 0, 0)),
                   pl.BlockSpec((1, nrow, LANES), lambda b: (b, 0, 0))],
        out_shape=[jax.ShapeDtypeStruct((bsz, nrow, LANES), BF16),
                   jax.ShapeDtypeStruct((bsz, nrow, LANES), BF16)],
        compiler_params=_params(("parallel",)),
    )(rk, rv, pos, w1k, w2k, w1v, w2v)


def _group_queries(q, nslab):
    lane = lax.broadcasted_iota(jnp.int32, (1, LANES), 1)
    scale = NSA_HD ** -0.5
    out = []
    for g in range(2):
        keep = jnp.where((lane >= g * 64) & (lane < (g + 1) * 64), scale, 0.0).astype(BF16)
        out.append(jnp.concatenate([q[:, j * LANES:(j + 1) * LANES] * keep for j in range(nslab)], axis=0))
    return out


def _fill_bias(bias_ref, rel_ref, slots, tq, tk):
    r = lax.broadcasted_iota(jnp.int32, (tq, tk), 0)
    c = lax.broadcasted_iota(jnp.int32, (tq, tk), 1)
    for i, col in slots:
        for d in range(3):
            dist = r - c + d * tq
            val = jnp.zeros((tq, tk), F32) + rel_ref[0, col]
            for k in range(1, N_BUCKETS):
                val = jnp.where(dist >= T5_THRESH[k], rel_ref[k, col], val)
            bias_ref[i, d] = val


def _softmax_step(s, v, m_ref, l_ref, acc_ref):
    m_prev = m_ref[...]
    m_new = jnp.maximum(m_prev, jnp.max(s, axis=-1, keepdims=True))
    alpha = jnp.exp(m_prev - m_new)
    p = jnp.exp(s - m_new)
    l_ref[...] = alpha * l_ref[...] + jnp.sum(p, axis=-1, keepdims=True)
    acc_ref[...] = alpha * acc_ref[...] + _dot(p.astype(BF16), v)
    m_ref[...] = m_new


def _tile_dist(qi, kj, tq, tk):
    r = lax.broadcasted_iota(jnp.int32, (tq, tk), 0)
    c = lax.broadcasted_iota(jnp.int32, (tq, tk), 1)
    return (qi * tq - kj * tk) + r - c


def _cmp_kernel(q_ref, kc_ref, vc_ref, ov_ref, o_ref, sel_ref, *, tq):
    qi = pl.program_id(1)
    nblk = kc_ref.shape[1]
    lane = lax.broadcasted_iota(jnp.int32, (1, LANES), 1)
    t = qi * tq + lax.broadcasted_iota(jnp.int32, (tq, 1), 0)
    n = lax.broadcasted_iota(jnp.int32, (1, nblk), 1)
    t4 = qi * tq + (lax.broadcasted_iota(jnp.int32, (NSA_GQ * tq, 1), 0) & (tq - 1))
    vis4 = (n * CMP_STRIDE + (CMP_LEN - 1)) <= t4
    has4 = jnp.where(t4 >= CMP_LEN - 1, 1.0, 0.0)
    qg = _group_queries(q_ref[0], NSA_GQ)
    kc = kc_ref[0]
    vc = vc_ref[0]
    outs, imp = [], jnp.zeros((tq, LANES), F32)
    for g in range(NSA_KV):
        s = jnp.where(vis4, _nt_dot(qg[g], kc), NEG)
        e = jnp.exp(s - jnp.max(s, axis=-1, keepdims=True))
        p = e * (1.0 / jnp.sum(e, axis=-1, keepdims=True)) * has4
        outs.append(_dot(p.astype(BF16), vc))
        ps = p[0:tq] + p[tq:2 * tq] + p[2 * tq:3 * tq] + p[3 * tq:4 * tq]
        hi = ps.astype(BF16)
        r1 = ps - hi.astype(F32)
        mid = r1.astype(BF16)
        lo = (r1 - mid.astype(F32)).astype(BF16)
        imp = imp + (_dot(hi, ov_ref[g]) + _dot(mid, ov_ref[g]) + _dot(lo, ov_ref[g]))
    for j in range(NSA_GQ):
        o_ref[0, :, j * LANES:(j + 1) * LANES] = jnp.where(
            lane < 64, outs[0][j * tq:(j + 1) * tq], outs[1][j * tq:(j + 1) * tq]).astype(BF16)

    nslc = 32
    jidx = lane & (nslc - 1)
    cur = t >> 6
    valid = (jidx * SLC_BLOCK) <= t
    forced = (jidx == 0) | (jidx == cur) | (jidx == cur - 1)
    impf = jnp.where(forced, jnp.inf, jnp.where(valid, imp, -jnp.inf))
    cnt = jnp.zeros((tq, LANES), F32)
    for jj in range(nslc):
        other = jnp.where(lane < nslc, impf[:, jj:jj + 1], impf[:, nslc + jj:nslc + jj + 1])
        beats = (other > impf) | ((other == impf) & (jidx > jj))
        cnt = cnt + jnp.where(beats, 1.0, 0.0)
    chosen = (cnt < SLC_TOPK) & (lane < 2 * nslc)
    sel_ref[0] = jnp.where(chosen, 1.0, 0.0).astype(BF16)


def _cmp_attention(main, kc, vc, ov):
    bsz, seq, _ = main.shape
    nblk = kc.shape[1]
    tq = TQ_CMP
    return pl.pallas_call(
        functools.partial(_cmp_kernel, tq=tq),
        grid=(bsz, seq // tq),
        in_specs=[pl.BlockSpec((1, tq, NSA_W), lambda b, i: (b, i, 3)),
                  pl.BlockSpec((1, nblk, LANES), lambda b, i: (b, 0, 0)),
                  pl.BlockSpec((1, nblk, LANES), lambda b, i: (b, 0, 0)),
                  pl.BlockSpec((2, nblk, LANES), lambda b, i: (0, 0, 0))],
        out_specs=[pl.BlockSpec((1, tq, NSA_W), lambda b, i: (b, i, 0)),
                   pl.BlockSpec((1, tq, LANES), lambda b, i: (b, i, 0))],
        out_shape=[jax.ShapeDtypeStruct((bsz, seq, NSA_W), BF16),
                   jax.ShapeDtypeStruct((bsz, seq, LANES), BF16)],
        compiler_params=_params(("parallel", "parallel")),
    )(main, kc, vc, ov)


def _nsa_flash_kernel(rel_ref, q_ref, k_ref, v_ref, *rest, tq, tk, selected):
    if selected:
        sel_ref, o_ref, m_ref, l_ref, acc_ref, bias_ref = rest
    else:
        o_ref, m_ref, l_ref, acc_ref, bias_ref = rest
    b = pl.program_id(0)
    qi = pl.program_id(1)

    @pl.when((b == 0) & (qi == 0))
    def _():
        _fill_bias(bias_ref, rel_ref, [(i, DIFF_HEADS + i) for i in range(NSA_HEADS)], tq, tk)

    qg = _group_queries(q_ref[0], NSA_GQ)
    m_ref[...] = jnp.full(m_ref.shape, NEG, F32)
    l_ref[...] = jnp.zeros(l_ref.shape, F32)
    acc_ref[...] = jnp.zeros(acc_ref.shape, F32)
    if selected:
        sel = sel_ref[0]
        er = lax.broadcasted_iota(jnp.int32, (LANES, tk), 0)
        ec = lax.broadcasted_iota(jnp.int32, (LANES, tk), 1) >> 6

    def tile(kj, carry):
        start = pl.multiple_of(kj * tk, tk)
        k = k_ref[0, pl.ds(start, tk), :]
        v = v_ref[0, pl.ds(start, tk), :]
        dist = _tile_dist(qi, kj, tq, tk)
        d = jnp.minimum(qi - kj, 2)
        if selected:
            ok = dist >= 0
        else:
            ok = (dist >= 0) & (dist < WINDOW)
        for g in range(NSA_KV):
            okg = ok
            if selected:
                expand = jnp.where(er == ec + (g * 32 + kj * (tk // SLC_BLOCK)), 1.0, 0.0).astype(BF16)
                okg = ok & (_dot(sel, expand) > 0.5)
            neg = jnp.where(okg, 0.0, NEG)
            s = _nt_dot(qg[g], k).reshape(NSA_GQ, tq, tk)
            s = s + bias_ref[pl.ds(g * NSA_GQ, NSA_GQ), d] + neg[None]
            _softmax_step(s.reshape(NSA_GQ * tq, tk), v, m_ref.at[g], l_ref.at[g], acc_ref.at[g])
        return carry

    first = 0 if selected else jnp.maximum(qi - WINDOW // tk, 0)
    lax.fori_loop(first, qi + 1, tile, 0)

    lane = lax.broadcasted_iota(jnp.int32, (1, LANES), 1)
    o = [acc_ref[g] * (1.0 / l_ref[g]) for g in range(NSA_KV)]
    for j in range(NSA_GQ):
        o_ref[0, :, j * LANES:(j + 1) * LANES] = jnp.where(
            lane < 64, o[0][j * tq:(j + 1) * tq], o[1][j * tq:(j + 1) * tq]).astype(BF16)


def _nsa_flash(rel_bias, main, sel, kcol, vcol):
    bsz, seq, _ = main.shape
    tq, tk = TQ, TK
    selected = sel is not None
    in_specs = [pl.BlockSpec(memory_space=pltpu.SMEM),
                pl.BlockSpec((1, tq, NSA_W), lambda b, i: (b, i, 3)),
                pl.BlockSpec((1, seq, LANES), lambda b, i: (b, 0, kcol)),
                pl.BlockSpec((1, seq, LANES), lambda b, i: (b, 0, vcol))]
    args = [rel_bias, main, main, main]
    if selected:
        in_specs.append(pl.BlockSpec((1, tq, LANES), lambda b, i: (b, i, 0)))
        args.append(sel)
    rows = NSA_GQ * tq
    return pl.pallas_call(
        functools.partial(_nsa_flash_kernel, tq=tq, tk=tk, selected=selected),
        grid=(bsz, seq // tq),
        in_specs=in_specs,
        out_specs=pl.BlockSpec((1, tq, NSA_W), lambda b, i: (b, i, 0)),
        out_shape=jax.ShapeDtypeStruct((bsz, seq, NSA_W), BF16),
        scratch_shapes=[pltpu.VMEM((NSA_KV, rows, 1), F32),
                        pltpu.VMEM((NSA_KV, rows, 1), F32),
                        pltpu.VMEM((NSA_KV, rows, LANES), F32),
                        pltpu.VMEM((NSA_HEADS, 3, tq, tk), F32)],
        compiler_params=_params(("arbitrary", "arbitrary")),
    )(*args)


def _diff_kernel(rel_ref, lam_ref, q_ref, k_ref, v_ref, g_ref, o_ref, m_ref, l_ref, acc_ref, bias_ref, *, tq, tk):
    b = pl.program_id(0)
    h = pl.program_id(1)
    qi = pl.program_id(2)

    @pl.when((b == 0) & (qi == 0))
    def _():
        _fill_bias(bias_ref, rel_ref, [(h, h)], tq, tk)

    q1, q2 = _group_queries(q_ref[0], 1)
    qq = jnp.concatenate([q1, q2], axis=0)
    m_ref[...] = jnp.full(m_ref.shape, NEG, F32)
    l_ref[...] = jnp.zeros(l_ref.shape, F32)
    acc_ref[...] = jnp.zeros(acc_ref.shape, F32)

    def tile(kj, carry):
        start = pl.multiple_of(kj * tk, tk)
        k = k_ref[0, pl.ds(start, tk), :]
        v = v_ref[0, pl.ds(start, tk), :]
        dist = _tile_dist(qi, kj, tq, tk)
        d = jnp.minimum(qi - kj, 2)
        add = bias_ref[h, d] + jnp.where(dist >= 0, 0.0, NEG)
        s = _nt_dot(qq, k).reshape(2, tq, tk) + add[None]
        _softmax_step(s.reshape(2 * tq, tk), v, m_ref, l_ref, acc_ref)
        return carry

    lax.fori_loop(0, qi + 1, tile, 0)

    lm = lam_ref[...]
    lam = (jnp.exp(jnp.sum(lm[0:1] * lm[1:2], axis=-1, keepdims=True))
           - jnp.exp(jnp.sum(lm[2:3] * lm[3:4], axis=-1, keepdims=True)) + LAMBDA_INIT)
    o = acc_ref[...] * (1.0 / l_ref[...])
    a = o[0:tq] - lam * o[tq:2 * tq]
    o_ref[0] = (_rms(a, g_ref[...]) * (1.0 - LAMBDA_INIT)).astype(BF16)


def _diff_attention(rel_bias, lam_rows, main, subln):
    bsz, seq, _ = main.shape
    tq, tk = TQ, TK
    return pl.pallas_call(
        functools.partial(_diff_kernel, tq=tq, tk=tk),
        grid=(bsz, DIFF_HEADS, seq // tq),
        in_specs=[pl.BlockSpec(memory_space=pltpu.SMEM),
                  pl.BlockSpec((8, LANES), lambda b, h, i: (0, 0)),
                  pl.BlockSpec((1, tq, LANES), lambda b, h, i: (b, i, h)),
                  pl.BlockSpec((1, seq, LANES), lambda b, h, i: (b, 0, DIFF_HEADS + h)),
                  pl.BlockSpec((1, seq, LANES), lambda b, h, i: (b, 0, 2 * DIFF_HEADS + h)),
                  pl.BlockSpec((1, LANES), lambda b, h, i: (0, 0))],
        out_specs=pl.BlockSpec((1, tq, LANES), lambda b, h, i: (b, i, h)),
        out_shape=jax.ShapeDtypeStruct((bsz, seq, DIFF_W), BF16),
        scratch_shapes=[pltpu.VMEM((2 * tq, 1), F32),
                        pltpu.VMEM((2 * tq, 1), F32),
                        pltpu.VMEM((2 * tq, LANES), F32),
                        pltpu.VMEM((DIFF_HEADS, 3, tq, tk), F32)],
        compiler_params=_params(("arbitrary", "arbitrary", "arbitrary")),
    )(rel_bias, lam_rows, main, main, main, subln)


def _outproj_kernel(x_ref, od_ref, oc_ref, os_ref, ow_ref, gate_ref, gn_ref, w_ref, g2_ref, x1_ref, h2_ref):
    lane = lax.broadcasted_iota(jnp.int32, (1, LANES), 1)
    gates = gate_ref[...]
    branches = (oc_ref, os_ref, ow_ref)
    slabs = []
    for j in range(NSA_GQ):
        acc = None
        for br in range(3):
            c0 = (0 * NSA_GQ + j) * 3 + br
            c1 = (1 * NSA_GQ + j) * 3 + br
            gexp = jnp.where(lane < 64, gates[:, c0:c0 + 1], gates[:, c1:c1 + 1])
            term = gexp * branches[br][:, j * LANES:(j + 1) * LANES].astype(F32)
            acc = term if acc is None else acc + term
        slabs.append(acc)
    o_nsa = jnp.concatenate(slabs, axis=-1)
    o_nsa = _rms(o_nsa, gn_ref[...]).astype(BF16)
    attn = _dot(od_ref[...], w_ref[0:DIFF_W, :]) + _dot(o_nsa, w_ref[DIFF_W:DIFF_W + NSA_W, :])
    x1 = x_ref[...] + attn
    x1_ref[...] = x1
    h2_ref[...] = _rms(x1, g2_ref[...]).astype(BF16)


def _outproj(x2, od, oc, os_, ow, gates, gn, w_out, g2):
    n = x2.shape[0]
    row = lambda i: (i, 0)
    fixed = lambda i: (0, 0)
    return pl.pallas_call(
        _outproj_kernel,
        grid=(n // TM,),
        in_specs=[pl.BlockSpec((TM, D_MODEL), row),
                  pl.BlockSpec((TM, DIFF_W), row),
                  pl.BlockSpec((TM, NSA_W), row),
                  pl.BlockSpec((TM, NSA_W), row),
                  pl.BlockSpec((TM, NSA_W), row),
                  pl.BlockSpec((TM, LANES), row),
                  pl.BlockSpec((1, NSA_W), fixed),
                  pl.BlockSpec((DIFF_W + NSA_W, D_MODEL), fixed),
                  pl.BlockSpec((1, D_MODEL), fixed)],
        out_specs=[pl.BlockSpec((TM, D_MODEL), row), pl.BlockSpec((TM, D_MODEL), row)],
        out_shape=[jax.ShapeDtypeStruct((n, D_MODEL), F32), jax.ShapeDtypeStruct((n, D_MODEL), BF16)],
        compiler_params=_params(("parallel",)),
    )(x2, od, oc, os_, ow, gates, gn, w_out, g2)


FF_CHUNK = 256


def _ffn_kernel(x1_ref, h_ref, wg_ref, wu_ref, wd_ref, gf_ref, o_ref, acc_ref):
    h = h_ref[...]
    acc_ref[...] = x1_ref[...]
    for c in range(0, D_FF, FF_CHUNK):
        gate = _dot(h, wg_ref[:, c:c + FF_CHUNK])
        up = _dot(h, wu_ref[:, c:c + FF_CHUNK])
        act = (gate * (1.0 / (1.0 + jnp.exp(-gate))) * up).astype(BF16)
        acc_ref[...] += _dot(act, wd_ref[c:c + FF_CHUNK, :])
    o_ref[...] = _rms(acc_ref[...], gf_ref[...])


def _ffn(x1, h2, wg, wu, wd, gf):
    n = x1.shape[0]
    row = lambda i: (i, 0)
    fixed = lambda i: (0, 0)
    once = pl.Buffered(1)
    return pl.pallas_call(
        _ffn_kernel,
        grid=(n // TM,),
        in_specs=[pl.BlockSpec((TM, D_MODEL), row),
                  pl.BlockSpec((TM, D_MODEL), row),
                  pl.BlockSpec((D_MODEL, D_FF), fixed, pipeline_mode=once),
                  pl.BlockSpec((D_MODEL, D_FF), fixed, pipeline_mode=once),
                  pl.BlockSpec((D_FF, D_MODEL), fixed, pipeline_mode=once),
                  pl.BlockSpec((1, D_MODEL), fixed)],
        out_specs=pl.BlockSpec((TM, D_MODEL), row),
        out_shape=jax.ShapeDtypeStruct((n, D_MODEL), F32),
        scratch_shapes=[pltpu.VMEM((TM, D_MODEL), F32)],
        compiler_params=_params(("parallel",)),
    )(x1, h2, wg, wu, wd, gf)


def _nsa_perm():
    p = np.arange(NSA_W)
    j, g, d = p // LANES, (p % LANES) // NSA_HD, p % NSA_HD
    return (g * NSA_GQ + j) * NSA_HD + d


def _overlap_matrices(nblk):
    n_cmp = nblk - 1
    nslc = nblk * CMP_STRIDE // SLC_BLOCK
    cmp_start = np.arange(n_cmp) * CMP_STRIDE
    cmp_end = cmp_start + CMP_LEN - 1
    slc_start = np.arange(nslc) * SLC_BLOCK
    overlap = ((cmp_start[:, None] < slc_start[None, :] + SLC_BLOCK) & (cmp_end[:, None] >= slc_start[None, :]))
    ov = np.zeros((NSA_KV, nblk, LANES), np.float32)
    for g in range(NSA_KV):
        ov[g, :n_cmp, g * nslc:(g + 1) * nslc] = overlap
    return ov


def _compress_weights(pos, w1, w2):
    half = CMP_LEN // 2
    eye = jnp.eye(NSA_KV, dtype=F32)
    w1r = w1.reshape(2, half, NSA_HD, CMP_HIDDEN)
    w1big = jnp.einsum('slcj,pg->slpcgj', w1r, eye).reshape(2, half * NSA_KV * NSA_HD, NSA_KV * CMP_HIDDEN)
    w2big = jnp.einsum('jd,pg->pjgd', w2, eye).reshape(NSA_KV * CMP_HIDDEN, NSA_KV * NSA_HD)
    posr = jnp.broadcast_to(pos.reshape(2, half, 1, NSA_HD), (2, half, NSA_KV, NSA_HD)).reshape(2, -1)
    return posr, w1big.astype(BF16), w2big.astype(BF16)


def kernel(x, norm1, w_in, lambda_q1, lambda_k1, lambda_q2, lambda_k2, diff_subln, cmp_pos_k, cmp_w1_k, cmp_w2_k,
           cmp_pos_v, cmp_w1_v, cmp_w2_v, nsa_norm, w_out, norm2, w_gate, w_up, w_down, rel_bias, final_norm):
    bsz, seq, dm = x.shape
    assert dm == D_MODEL and seq % TQ_CMP == 0 and (bsz * seq) % TM == 0
    assert norm1.shape[0] == 1, "single layer"
    perm = _nsa_perm()
    x2 = x.reshape(bsz * seq, dm)

    w = w_in[0]
    w_p = jnp.concatenate([w[:, :1536], w[:, 1536 + perm], w[:, 2304:2816], w[:, 2048:2304], w[:, 2816:N_IN],
                           jnp.zeros((dm, N_PROJ - N_MAIN - 2 * LANES - 24), F32)], axis=1).astype(BF16)
    wo = w_out[0]
    wo_p = jnp.concatenate([wo[:DIFF_W], wo[DIFF_W + perm]], axis=0).astype(BF16)
    gn_p = nsa_norm[0][perm].reshape(1, NSA_W)
    posk, w1k, w2k = _compress_weights(cmp_pos_k[0], cmp_w1_k[0], cmp_w2_k[0])
    posv, w1v, w2v = _compress_weights(cmp_pos_v[0], cmp_w1_v[0], cmp_w2_v[0])
    pos = jnp.concatenate([posk, posv, jnp.zeros((4, posk.shape[1]), F32)], axis=0)
    lam_rows = jnp.zeros((8, LANES), F32).at[0:4, 0:DIFF_HD].set(
        jnp.stack([lambda_q1[0], lambda_k1[0], lambda_q2[0], lambda_k2[0]]).astype(F32))

    main, kc_tok, vc_tok, gates = _inproj(x2, norm1[0].reshape(1, dm), w_p)
    main = main.reshape(bsz, seq, N_MAIN)

    nrow = seq // CMP_STRIDE
    kc, vc = _compress(kc_tok.reshape(bsz, nrow, CMP_STRIDE * LANES), vc_tok.reshape(bsz, nrow, CMP_STRIDE * LANES),
                       pos, w1k, w2k, w1v, w2v)
    ov = jnp.asarray(_overlap_matrices(nrow), BF16)
    o_cmp, sel = _cmp_attention(main, kc, vc, ov)
    o_slc = _nsa_flash(rel_bias, main, sel, 16, 17)
    o_win = _nsa_flash(rel_bias, main, None, 18, 19)
    o_diff = _diff_attention(rel_bias, lam_rows, main, diff_subln[0].reshape(1, DIFF_VD))

    n = bsz * seq
    x1, h2 = _outproj(x2, o_diff.reshape(n, DIFF_W), o_cmp.reshape(n, NSA_W), o_slc.reshape(n, NSA_W),
                      o_win.reshape(n, NSA_W), gates, gn_p, wo_p, norm2[0].reshape(1, dm))
    out = _ffn(x1, h2, w_gate[0].astype(BF16), w_up[0].astype(BF16), w_down[0].astype(BF16),
               final_norm.reshape(1, dm))
    return out.reshape(bsz, seq, dm)
```

```python
import functools
import math

import numpy as np
import jax
import jax.numpy as jnp
from jax import lax
from jax.experimental import pallas as pl
from jax.experimental.pallas import tpu as pltpu

F32 = jnp.float32
BF16 = jnp.bfloat16

D_MODEL = 1024
DIFF_HEADS = 4
DIFF_HD = 64
DIFF_VD = 128
DIFF_W = 512
NSA_HEADS = 8
NSA_KV = 2
NSA_GQ = 4
NSA_HD = 64
NSA_W = 512
CMP_LEN = 32
CMP_STRIDE = 16
CMP_HIDDEN = 256
SLC_BLOCK = 64
SLC_TOPK = 16
N_SLC = 32
WINDOW = 512
N_BUCKETS = 32
MAX_DISTANCE = 128
D_FF = 2816
N_IN = 2840
NEG = -1e30
EPS = 1e-6
LAMBDA_INIT = 0.8 - 0.6 * math.exp(-0.3 * 0)

LANES = 128
HALF = LANES // 2
COL_DQ, COL_DK, COL_DV, COL_NQ, COL_KSL, COL_VSL, COL_KW, COL_VW, N_MAIN = (
    0, 512, 1024, 1536, 2048, 2304, 2432, 2560, 2688)
N_PROJ = N_MAIN + 3 * LANES
VMEM_LIMIT = 48 * 1024 * 1024

TQ = 256
TK = 256
TM = 512
TQ_CMP = 256


def _t5_thresholds():
    d = np.arange(0, 4 * MAX_DISTANCE)
    max_exact = N_BUCKETS // 2
    val = (np.log(np.maximum(d, 1).astype(np.float32) / np.float32(max_exact))
           / np.float32(math.log(MAX_DISTANCE / max_exact)) * np.float32(N_BUCKETS - max_exact))
    large = np.minimum(max_exact + val.astype(np.int32), N_BUCKETS - 1)
    bucket = np.where(d < max_exact, d, large)
    assert np.all(np.diff(bucket) >= 0) and bucket[-1] == N_BUCKETS - 1
    return [int(np.argmax(bucket >= k)) for k in range(N_BUCKETS)]


T5_THRESH = _t5_thresholds()
assert T5_THRESH[-1] <= TK + 1
assert WINDOW == 2 * TK and TQ == TK


def _nt_dot(a, b):
    return lax.dot_general(a, b, (((1,), (1,)), ((), ())), preferred_element_type=F32)


def _dot(a, b):
    return jnp.dot(a, b, preferred_element_type=F32)


def _rms(x, g):
    ms = jnp.mean(x * x, axis=-1, keepdims=True)
    return x * lax.rsqrt(ms + EPS) * g


def _params(sem):
    return pltpu.CompilerParams(dimension_semantics=sem, vmem_limit_bytes=VMEM_LIMIT)


def _inproj_kernel(x_ref, g_ref, w_ref, main_ref, kc_ref, vc_ref, gate_ref, *, seq):
    tm = x_ref.shape[0]
    h = _rms(x_ref[...], g_ref[...]).astype(BF16)
    for c in range(0, N_MAIN, 256):
        wd = min(256, N_MAIN - c)
        y = _dot(h, w_ref[:, c:c + wd])
        if c == COL_KSL:
            t0 = lax.rem(pl.program_id(0) * tm, seq)
            blk = (t0 + lax.broadcasted_iota(jnp.int32, (tm, 2 * LANES), 0)) >> 6
            lane = lax.broadcasted_iota(jnp.int32, (tm, 2 * LANES), 1)
            slot = jnp.where(lane < LANES, lane - HALF, lane - LANES)
            y = y + jnp.where((slot == blk) & (slot >= 0) & (slot < N_SLC), NEG, 0.0)
        main_ref[:, c:c + wd] = y.astype(BF16)
    kc_ref[...] = _dot(h, w_ref[:, N_MAIN:N_MAIN + LANES]).astype(BF16)
    vc_ref[...] = _dot(h, w_ref[:, N_MAIN + LANES:N_MAIN + 2 * LANES]).astype(BF16)
    gt = _dot(h, w_ref[:, N_MAIN + 2 * LANES:N_PROJ])
    gate_ref[...] = 1.0 / (1.0 + jnp.exp(-gt))


def _inproj(x2, g1, w_p, seq):
    n = x2.shape[0]
    return pl.pallas_call(
        functools.partial(_inproj_kernel, seq=seq),
        grid=(n // TM,),
        in_specs=[pl.BlockSpec((TM, D_MODEL), lambda i: (i, 0)),
                  pl.BlockSpec((1, D_MODEL), lambda i: (0, 0)),
                  pl.BlockSpec((D_MODEL, N_PROJ), lambda i: (0, 0))],
        out_specs=[pl.BlockSpec((TM, N_MAIN), lambda i: (i, 0)),
                   pl.BlockSpec((TM, LANES), lambda i: (i, 0)),
                   pl.BlockSpec((TM, LANES), lambda i: (i, 0)),
                   pl.BlockSpec((TM, LANES), lambda i: (i, 0))],
        out_shape=[jax.ShapeDtypeStruct((n, N_MAIN), BF16),
                   jax.ShapeDtypeStruct((n, LANES), BF16),
                   jax.ShapeDtypeStruct((n, LANES), BF16),
                   jax.ShapeDtypeStruct((n, LANES), F32)],
        compiler_params=_params(("parallel",)),
    )(x2, g1, w_p)


def _gelu_tanh(x):
    return x * (0.5 * (1.0 + jnp.tanh(math.sqrt(2.0 / math.pi) * (x + 0.044715 * (x * x * x)))))


def _compress_kernel(rk_ref, rv_ref, pos_ref, w1k_ref, w2k_ref, w1v_ref, w2v_ref, kc_ref, vc_ref):
    def comp(r_ref, p_top, p_bot, w1_ref, w2_ref):
        r = r_ref[0].astype(F32)
        top = _dot((r + p_top).astype(BF16), w1_ref[0])
        bot = _dot((r + p_bot).astype(BF16), w1_ref[1])
        nrow = top.shape[0]
        hid = top + pltpu.roll(bot, nrow - 1, 0)
        return _dot(_gelu_tanh(hid).astype(BF16), w2_ref[...]).astype(BF16)

    kc_ref[0] = comp(rk_ref, pos_ref[0:1, :], pos_ref[1:2, :], w1k_ref, w2k_ref)
    vc_ref[0] = comp(rv_ref, pos_ref[2:3, :], pos_ref[3:4, :], w1v_ref, w2v_ref)


def _compress(rk, rv, pos, w1k, w2k, w1v, w2v):
    bsz, nrow, width = rk.shape
    hid2 = 2 * CMP_HIDDEN
    return pl.pallas_call(
        _compress_kernel,
        grid=(bsz,),
        in_specs=[pl.BlockSpec((1, nrow, width), lambda b: (b, 0, 0)),
                  pl.BlockSpec((1, nrow, width), lambda b: (b, 0, 0)),
                  pl.BlockSpec((8, width), lambda b: (0, 0)),
                  pl.BlockSpec((2, width, hid2), lambda b: (0, 0, 0)),
                  pl.BlockSpec((hid2, LANES), lambda b: (0, 0)),
                  pl.BlockSpec((2, width, hid2), lambda b: (0, 0, 0)),
                  pl.BlockSpec((hid2, LANES), lambda b: (0, 0))],
        out_specs=[pl.BlockSpec((1, nrow, LANES), lambda b: (b, 0, 0)),
                   pl.BlockSpec((1, nrow, LANES), lambda b: (b, 0, 0))],
        out_shape=[jax.ShapeDtypeStruct((bsz, nrow, LANES), BF16),
                   jax.ShapeDtypeStruct((bsz, nrow, LANES), BF16)],
        compiler_params=_params(("parallel",)),
    )(rk, rv, pos, w1k, w2k, w1v, w2v)


def _group_queries(q, nslab, extra=None):
    lane = lax.broadcasted_iota(jnp.int32, (1, LANES), 1)
    scale = NSA_HD ** -0.5
    out = []
    for g in range(2):
        keep = jnp.where((lane >= g * HALF) & (lane < (g + 1) * HALF), scale, 0.0).astype(BF16)
        slabs = [q[:, j * LANES:(j + 1) * LANES] * keep for j in range(nslab)]
        if extra is not None:
            slabs = [s + extra[g] for s in slabs]
        out.append(jnp.concatenate(slabs, axis=0))
    return out


def _fill_bias(bias_ref, rel_ref, slots, tq, tk):
    r = lax.broadcasted_iota(jnp.int32, (tq, tk), 0)
    c = lax.broadcasted_iota(jnp.int32, (tq, tk), 1)
    for i, col in slots:
        far = rel_ref[N_BUCKETS - 1, col]
        for d in range(2):
            dist = r - c + d * tq
            val = jnp.zeros((tq, tk), F32) + rel_ref[0, col]
            for k in range(1, N_BUCKETS):
                val = jnp.where(dist >= T5_THRESH[k], rel_ref[k, col], val)
            val = val - far
            if d == 0:
                val = jnp.where(dist >= 0, val, NEG)
            bias_ref[i, d] = val


def _softmax_step(s, v, m_ref, acc_ref):
    tk = s.shape[1]
    m_prev = m_ref[...]
    m_new = jnp.maximum(m_prev, jnp.max(s, axis=-1, keepdims=True))
    alpha = jnp.exp(m_prev - m_new)
    p = jnp.exp(s - jnp.concatenate([m_new] * (tk // LANES), axis=1))
    vaug = jnp.concatenate([v, jnp.ones((tk, LANES), BF16)], axis=1)
    acc_ref[...] = jnp.concatenate([alpha, alpha], axis=1) * acc_ref[...] + _dot(p.astype(BF16), vaug)
    m_ref[...] = m_new


def _attend(q, k, v, add, m_ref, acc_ref):
    s = _nt_dot(q, k)
    if add is not None:
        tq, tk = add.shape[-2:]
        nmap = s.shape[0] // tq
        add3 = add if add.ndim == 3 else add[None]
        s = (s.reshape(nmap, tq, tk) + add3).reshape(nmap * tq, tk)
    _softmax_step(s, v, m_ref, acc_ref)


def _init_state(m_ref, acc_ref):
    m_ref[...] = jnp.full(m_ref.shape, NEG, F32)
    acc_ref[...] = jnp.zeros(acc_ref.shape, F32)


def _normalized(acc_ref, i):
    return acc_ref[i, :, 0:LANES] * (1.0 / acc_ref[i, :, LANES:2 * LANES])


def _store_nsa_out(o_ref, acc_ref, tq):
    lane = lax.broadcasted_iota(jnp.int32, (1, LANES), 1)
    o = [_normalized(acc_ref, g) for g in range(NSA_KV)]
    for j in range(NSA_GQ):
        o_ref[0, :, j * LANES:(j + 1) * LANES] = jnp.where(
            lane < HALF, o[0][j * tq:(j + 1) * tq], o[1][j * tq:(j + 1) * tq]).astype(BF16)


def _cmp_kernel(q_ref, kc_ref, vc_ref, ov_ref, o_ref, nsel_ref, *, tq):
    qi = pl.program_id(1)
    nblk = kc_ref.shape[1]
    lane = lax.broadcasted_iota(jnp.int32, (1, LANES), 1)
    t = qi * tq + lax.broadcasted_iota(jnp.int32, (tq, 1), 0)
    n = lax.broadcasted_iota(jnp.int32, (1, nblk), 1)
    t4 = qi * tq + (lax.broadcasted_iota(jnp.int32, (NSA_GQ * tq, 1), 0) & (tq - 1))
    vis4 = (n * CMP_STRIDE + (CMP_LEN - 1)) <= t4
    has4 = jnp.where(t4 >= CMP_LEN - 1, 1.0, 0.0)
    qg = _group_queries(q_ref[0], NSA_GQ)
    kc = kc_ref[0]
    vc = vc_ref[0]
    outs, imp = [], jnp.zeros((tq, LANES), F32)
    for g in range(NSA_KV):
        s = jnp.where(vis4, _nt_dot(qg[g], kc), NEG)
        e = jnp.exp(s - jnp.max(s, axis=-1, keepdims=True))
        p = e * (1.0 / jnp.sum(e, axis=-1, keepdims=True)) * has4
        outs.append(_dot(p.astype(BF16), vc))
        ps = p[0:tq] + p[tq:2 * tq] + p[2 * tq:3 * tq] + p[3 * tq:4 * tq]
        hi = ps.astype(BF16)
        r1 = ps - hi.astype(F32)
        mid = r1.astype(BF16)
        lo = (r1 - mid.astype(F32)).astype(BF16)
        imp = imp + (_dot(hi, ov_ref[g]) + _dot(mid, ov_ref[g]) + _dot(lo, ov_ref[g]))
    for j in range(NSA_GQ):
        o_ref[0, :, j * LANES:(j + 1) * LANES] = jnp.where(
            lane < HALF, outs[0][j * tq:(j + 1) * tq], outs[1][j * tq:(j + 1) * tq]).astype(BF16)

    jidx = lane & (N_SLC - 1)
    cur = t >> 6
    valid = (jidx * SLC_BLOCK) <= t
    forced = (jidx == 0) | (jidx == cur) | (jidx == cur - 1)
    impf = jnp.where(forced, jnp.inf, jnp.where(valid, imp, -jnp.inf))
    cnt = jnp.zeros((tq, LANES), F32)
    for jj in range(N_SLC):
        other = jnp.where(lane < HALF, impf[:, jj:jj + 1], impf[:, HALF + jj:HALF + jj + 1])
        beats = (other > impf) | ((other == impf) & (jidx > jj))
        cnt = cnt + jnp.where(beats, 1.0, 0.0)
    dropped = jnp.where(cnt < SLC_TOPK, 0.0, 1.0)
    nsel_ref[0, :, 0:LANES] = jnp.where((lane >= HALF) & (lane < HALF + N_SLC), dropped, 0.0).astype(BF16)
    nsel_ref[0, :, LANES:2 * LANES] = jnp.where(lane < N_SLC, dropped, 0.0).astype(BF16)


def _cmp_attention(main, kc, vc, ov):
    bsz, seq, _ = main.shape
    nblk = kc.shape[1]
    tq = TQ_CMP
    return pl.pallas_call(
        functools.partial(_cmp_kernel, tq=tq),
        grid=(bsz, seq // tq),
        in_specs=[pl.BlockSpec((1, tq, NSA_W), lambda b, i: (b, i, COL_NQ // NSA_W)),
                  pl.BlockSpec((1, nblk, LANES), lambda b, i: (b, 0, 0)),
                  pl.BlockSpec((1, nblk, LANES), lambda b, i: (b, 0, 0)),
                  pl.BlockSpec((2, nblk, LANES), lambda b, i: (0, 0, 0))],
        out_specs=[pl.BlockSpec((1, tq, NSA_W), lambda b, i: (b, i, 0)),
                   pl.BlockSpec((1, tq, 2 * LANES), lambda b, i: (b, i, 0))],
        out_shape=[jax.ShapeDtypeStruct((bsz, seq, NSA_W), BF16),
                   jax.ShapeDtypeStruct((bsz, seq, 2 * LANES), BF16)],
        compiler_params=_params(("parallel", "parallel")),
    )(main, kc, vc, ov)


def _slc_kernel(rel_ref, q_ref, k_ref, v_ref, nsel_ref, o_ref, m_ref, acc_ref, bias_ref, *, tq, tk):
    b = pl.program_id(0)
    qi = pl.program_id(1)

    @pl.when((b == 0) & (qi == 0))
    def _():
        _fill_bias(bias_ref, rel_ref, [(i, DIFF_HEADS + i) for i in range(NSA_HEADS)], tq, tk)

    nsel = nsel_ref[0]
    qg = _group_queries(q_ref[0], NSA_GQ, extra=[nsel[:, 0:LANES], nsel[:, LANES:2 * LANES]])
    _init_state(m_ref, acc_ref)

    def tile(kj, d):
        start = pl.multiple_of(kj * tk, tk)
        v = v_ref[0, pl.ds(start, tk), :]
        for g in range(NSA_KV):
            k = k_ref[0, pl.ds(start, tk), g * LANES:(g + 1) * LANES]
            add = None if d is None else bias_ref[g * NSA_GQ:(g + 1) * NSA_GQ, d]
            _attend(qg[g], k, v, add, m_ref.at[g], acc_ref.at[g])

    def far(kj, carry):
        tile(kj, None)
        return carry

    lax.fori_loop(0, qi - 1, far, 0)

    @pl.when(qi >= 1)
    def _():
        tile(qi - 1, 1)

    tile(qi, 0)
    _store_nsa_out(o_ref, acc_ref, tq)


def _slc_attention(rel_bias, main, nsel):
    bsz, seq, _ = main.shape
    tq, tk = TQ, TK
    rows = NSA_GQ * tq
    return pl.pallas_call(
        functools.partial(_slc_kernel, tq=tq, tk=tk),
        grid=(bsz, seq // tq),
        in_specs=[pl.BlockSpec(memory_space=pltpu.SMEM),
                  pl.BlockSpec((1, tq, NSA_W), lambda b, i: (b, i, COL_NQ // NSA_W)),
                  pl.BlockSpec((1, seq, 2 * LANES), lambda b, i: (b, 0, COL_KSL // (2 * LANES))),
                  pl.BlockSpec((1, seq, LANES), lambda b, i: (b, 0, COL_VSL // LANES)),
                  pl.BlockSpec((1, tq, 2 * LANES), lambda b, i: (b, i, 0))],
        out_specs=pl.BlockSpec((1, tq, NSA_W), lambda b, i: (b, i, 0)),
        out_shape=jax.ShapeDtypeStruct((bsz, seq, NSA_W), BF16),
        scratch_shapes=[pltpu.VMEM((NSA_KV, rows, LANES), F32),
                        pltpu.VMEM((NSA_KV, rows, 2 * LANES), F32),
                        pltpu.VMEM((NSA_HEADS, 2, tq, tk), F32)],
        compiler_params=_params(("arbitrary", "arbitrary")),
    )(rel_bias, main, main, main, nsel)


def _win_kernel(rel_ref, q_ref, k_ref, v_ref, o_ref, m_ref, acc_ref, bias_ref, band_ref, *, tq, tk):
    b = pl.program_id(0)
    qi = pl.program_id(1)

    @pl.when((b == 0) & (qi == 0))
    def _():
        _fill_bias(bias_ref, rel_ref, [(i, DIFF_HEADS + i) for i in range(NSA_HEADS)], tq, tk)
        r = lax.broadcasted_iota(jnp.int32, (tq, tk), 0)
        c = lax.broadcasted_iota(jnp.int32, (tq, tk), 1)
        band_ref[...] = jnp.where(r < c, 0.0, NEG)

    qg = _group_queries(q_ref[0], NSA_GQ)
    _init_state(m_ref, acc_ref)

    def tile(kj, d):
        start = pl.multiple_of(kj * tk, tk)
        k = k_ref[0, pl.ds(start, tk), :]
        v = v_ref[0, pl.ds(start, tk), :]
        for g in range(NSA_KV):
            add = band_ref[...] if d == 2 else bias_ref[g * NSA_GQ:(g + 1) * NSA_GQ, d]
            _attend(qg[g], k, v, add, m_ref.at[g], acc_ref.at[g])

    for d in (2, 1):
        @pl.when(qi >= d)
        def _():
            tile(qi - d, d)

    tile(qi, 0)
    _store_nsa_out(o_ref, acc_ref, tq)


def _win_attention(rel_bias, main):
    bsz, seq, _ = main.shape
    tq, tk = TQ, TK
    rows = NSA_GQ * tq
    return pl.pallas_call(
        functools.partial(_win_kernel, tq=tq, tk=tk),
        grid=(bsz, seq // tq),
        in_specs=[pl.BlockSpec(memory_space=pltpu.SMEM),
                  pl.BlockSpec((1, tq, NSA_W), lambda b, i: (b, i, COL_NQ // NSA_W)),
                  pl.BlockSpec((1, seq, LANES), lambda b, i: (b, 0, COL_KW // LANES)),
                  pl.BlockSpec((1, seq, LANES), lambda b, i: (b, 0, COL_VW // LANES))],
        out_specs=pl.BlockSpec((1, tq, NSA_W), lambda b, i: (b, i, 0)),
        out_shape=jax.ShapeDtypeStruct((bsz, seq, NSA_W), BF16),
        scratch_shapes=[pltpu.VMEM((NSA_KV, rows, LANES), F32),
                        pltpu.VMEM((NSA_KV, rows, 2 * LANES), F32),
                        pltpu.VMEM((NSA_HEADS, 2, tq, tk), F32),
                        pltpu.VMEM((tq, tk), F32)],
        compiler_params=_params(("arbitrary", "arbitrary")),
    )(rel_bias, main, main, main)


def _diff_kernel(rel_ref, lam_ref, q_ref, k_ref, v_ref, g_ref, o_ref, m_ref, acc_ref, bias_ref, *, tq, tk):
    b = pl.program_id(0)
    qi = pl.program_id(1)

    @pl.when((b == 0) & (qi == 0))
    def _():
        _fill_bias(bias_ref, rel_ref, [(h, h) for h in range(DIFF_HEADS)], tq, tk)

    q = q_ref[0]
    qq = []
    for h in range(DIFF_HEADS):
        q1, q2 = _group_queries(q[:, h * LANES:(h + 1) * LANES], 1)
        qq.append(jnp.concatenate([q1, q2], axis=0))
    _init_state(m_ref, acc_ref)

    def tile(kj, d):
        start = pl.multiple_of(kj * tk, tk)
        for h in range(DIFF_HEADS):
            k = k_ref[0, pl.ds(start, tk), h * LANES:(h + 1) * LANES]
            v = v_ref[0, pl.ds(start, tk), h * LANES:(h + 1) * LANES]
            add = None if d is None else bias_ref[h, d]
            _attend(qq[h], k, v, add, m_ref.at[h], acc_ref.at[h])

    def far(kj, carry):
        tile(kj, None)
        return carry

    lax.fori_loop(0, qi - 1, far, 0)

    @pl.when(qi >= 1)
    def _():
        tile(qi - 1, 1)

    tile(qi, 0)

    lm = lam_ref[...]
    lam = (jnp.exp(jnp.sum(lm[0:1] * lm[1:2], axis=-1, keepdims=True))
           - jnp.exp(jnp.sum(lm[2:3] * lm[3:4], axis=-1, keepdims=True)) + LAMBDA_INIT)
    for h in range(DIFF_HEADS):
        o = _normalized(acc_ref, h)
        a = o[0:tq] - lam * o[tq:2 * tq]
        o_ref[0, :, h * LANES:(h + 1) * LANES] = (_rms(a, g_ref[...]) * (1.0 - LAMBDA_INIT)).astype(BF16)


def _diff_attention(rel_bias, lam_rows, main, subln):
    bsz, seq, _ = main.shape
    tq, tk = TQ, TK
    return pl.pallas_call(
        functools.partial(_diff_kernel, tq=tq, tk=tk),
        grid=(bsz, seq // tq),
        in_specs=[pl.BlockSpec(memory_space=pltpu.SMEM),
                  pl.BlockSpec((8, LANES), lambda b, i: (0, 0)),
                  pl.BlockSpec((1, tq, DIFF_W), lambda b, i: (b, i, COL_DQ // DIFF_W)),
                  pl.BlockSpec((1, seq, DIFF_W), lambda b, i: (b, 0, COL_DK // DIFF_W)),
                  pl.BlockSpec((1, seq, DIFF_W), lambda b, i: (b, 0, COL_DV // DIFF_W)),
                  pl.BlockSpec((1, LANES), lambda b, i: (0, 0))],
        out_specs=pl.BlockSpec((1, tq, DIFF_W), lambda b, i: (b, i, 0)),
        out_shape=jax.ShapeDtypeStruct((bsz, seq, DIFF_W), BF16),
        scratch_shapes=[pltpu.VMEM((DIFF_HEADS, 2 * tq, LANES), F32),
                        pltpu.VMEM((DIFF_HEADS, 2 * tq, 2 * LANES), F32),
                        pltpu.VMEM((DIFF_HEADS, 2, tq, tk), F32)],
        compiler_params=_params(("arbitrary", "arbitrary")),
    )(rel_bias, lam_rows, main, main, main, subln)


def _outproj_kernel(x_ref, od_ref, oc_ref, os_ref, ow_ref, gate_ref, gn_ref, w_ref, g2_ref, x1_ref, h2_ref):
    lane = lax.broadcasted_iota(jnp.int32, (1, LANES), 1)
    gates = gate_ref[...]
    branches = (oc_ref, os_ref, ow_ref)
    slabs = []
    for j in range(NSA_GQ):
        acc = None
        for br in range(3):
            c0 = (0 * NSA_GQ + j) * 3 + br
            c1 = (1 * NSA_GQ + j) * 3 + br
            gexp = jnp.where(lane < HALF, gates[:, c0:c0 + 1], gates[:, c1:c1 + 1])
            term = gexp * branches[br][:, j * LANES:(j + 1) * LANES].astype(F32)
            acc = term if acc is None else acc + term
        slabs.append(acc)
    o_nsa = jnp.concatenate(slabs, axis=-1)
    o_nsa = _rms(o_nsa, gn_ref[...]).astype(BF16)
    attn = _dot(od_ref[...], w_ref[0:DIFF_W, :]) + _dot(o_nsa, w_ref[DIFF_W:DIFF_W + NSA_W, :])
    x1 = x_ref[...] + attn
    x1_ref[...] = x1
    h2_ref[...] = _rms(x1, g2_ref[...]).astype(BF16)


def _outproj(x2, od, oc, os_, ow, gates, gn, w_out, g2):
    n = x2.shape[0]
    row = lambda i: (i, 0)
    fixed = lambda i: (0, 0)
    return pl.pallas_call(
        _outproj_kernel,
        grid=(n // TM,),
        in_specs=[pl.BlockSpec((TM, D_MODEL), row),
                  pl.BlockSpec((TM, DIFF_W), row),
                  pl.BlockSpec((TM, NSA_W), row),
                  pl.BlockSpec((TM, NSA_W), row),
                  pl.BlockSpec((TM, NSA_W), row),
                  pl.BlockSpec((TM, LANES), row),
                  pl.BlockSpec((1, NSA_W), fixed),
                  pl.BlockSpec((DIFF_W + NSA_W, D_MODEL), fixed),
                  pl.BlockSpec((1, D_MODEL), fixed)],
        out_specs=[pl.BlockSpec((TM, D_MODEL), row), pl.BlockSpec((TM, D_MODEL), row)],
        out_shape=[jax.ShapeDtypeStruct((n, D_MODEL), F32), jax.ShapeDtypeStruct((n, D_MODEL), BF16)],
        compiler_params=_params(("parallel",)),
    )(x2, od, oc, os_, ow, gates, gn, w_out, g2)


FF_CHUNK = 256


def _ffn_kernel(x1_ref, h_ref, wg_ref, wu_ref, wd_ref, gf_ref, o_ref, acc_ref):
    h = h_ref[...]
    acc_ref[...] = x1_ref[...]
    for c in range(0, D_FF, FF_CHUNK):
        gate = _dot(h, wg_ref[:, c:c + FF_CHUNK])
        up = _dot(h, wu_ref[:, c:c + FF_CHUNK])
        act = (gate * (1.0 / (1.0 + jnp.exp(-gate))) * up).astype(BF16)
        acc_ref[...] += _dot(act, wd_ref[c:c + FF_CHUNK, :])
    o_ref[...] = _rms(acc_ref[...], gf_ref[...])


def _ffn(x1, h2, wg, wu, wd, gf):
    n = x1.shape[0]
    row = lambda i: (i, 0)
    fixed = lambda i: (0, 0)
    once = pl.Buffered(1)
    return pl.pallas_call(
        _ffn_kernel,
        grid=(n // TM,),
        in_specs=[pl.BlockSpec((TM, D_MODEL), row),
                  pl.BlockSpec((TM, D_MODEL), row),
                  pl.BlockSpec((D_MODEL, D_FF), fixed, pipeline_mode=once),
                  pl.BlockSpec((D_MODEL, D_FF), fixed, pipeline_mode=once),
                  pl.BlockSpec((D_FF, D_MODEL), fixed, pipeline_mode=once),
                  pl.BlockSpec((1, D_MODEL), fixed)],
        out_specs=pl.BlockSpec((TM, D_MODEL), row),
        out_shape=jax.ShapeDtypeStruct((n, D_MODEL), F32),
        scratch_shapes=[pltpu.VMEM((TM, D_MODEL), F32)],
        compiler_params=_params(("parallel",)),
    )(x1, h2, wg, wu, wd, gf)


def _nsa_perm():
    p = np.arange(NSA_W)
    j, g, d = p // LANES, (p % LANES) // NSA_HD, p % NSA_HD
    return (g * NSA_GQ + j) * NSA_HD + d


def _overlap_matrices(nblk):
    n_cmp = nblk - 1
    nslc = nblk * CMP_STRIDE // SLC_BLOCK
    assert nslc == N_SLC
    cmp_start = np.arange(n_cmp) * CMP_STRIDE
    cmp_end = cmp_start + CMP_LEN - 1
    slc_start = np.arange(nslc) * SLC_BLOCK
    overlap = ((cmp_start[:, None] < slc_start[None, :] + SLC_BLOCK) & (cmp_end[:, None] >= slc_start[None, :]))
    ov = np.zeros((NSA_KV, nblk, LANES), np.float32)
    for g, base in enumerate((HALF, 0)):
        ov[g, :n_cmp, base:base + nslc] = overlap
    return ov


def _compress_weights(pos, w1, w2):
    half = CMP_LEN // 2
    eye = jnp.eye(NSA_KV, dtype=F32)
    w1r = w1.reshape(2, half, NSA_HD, CMP_HIDDEN)
    w1big = jnp.einsum('slcj,pg->slpcgj', w1r, eye).reshape(2, half * NSA_KV * NSA_HD, NSA_KV * CMP_HIDDEN)
    w2big = jnp.einsum('jd,pg->pjgd', w2, eye).reshape(NSA_KV * CMP_HIDDEN, NSA_KV * NSA_HD)
    posr = jnp.broadcast_to(pos.reshape(2, half, 1, NSA_HD), (2, half, NSA_KV, NSA_HD)).reshape(2, -1)
    return posr, w1big.astype(BF16), w2big.astype(BF16)


def kernel(x, norm1, w_in, lambda_q1, lambda_k1, lambda_q2, lambda_k2, diff_subln, cmp_pos_k, cmp_w1_k, cmp_w2_k,
           cmp_pos_v, cmp_w1_v, cmp_w2_v, nsa_norm, w_out, norm2, w_gate, w_up, w_down, rel_bias, final_norm):
    bsz, seq, dm = x.shape
    assert dm == D_MODEL and seq % TQ_CMP == 0 and seq % TM == 0 and seq == N_SLC * SLC_BLOCK
    assert norm1.shape[0] == 1, "single layer"
    perm = _nsa_perm()
    x2 = x.reshape(bsz * seq, dm)

    w = w_in[0]
    zhalf = jnp.zeros((dm, HALF), F32)
    w_p = jnp.concatenate([w[:, :1536], w[:, 1536 + perm],
                           w[:, 2304:2368], zhalf, zhalf, w[:, 2368:2432],
                           w[:, 2432:2816],
                           w[:, 2048:2304],
                           w[:, 2816:N_IN], jnp.zeros((dm, LANES - (N_IN - 2816)), F32)], axis=1).astype(BF16)
    assert w_p.shape[1] == N_PROJ
    wo = w_out[0]
    wo_p = jnp.concatenate([wo[:DIFF_W], wo[DIFF_W + perm]], axis=0).astype(BF16)
    gn_p = nsa_norm[0][perm].reshape(1, NSA_W)
    posk, w1k, w2k = _compress_weights(cmp_pos_k[0], cmp_w1_k[0], cmp_w2_k[0])
    posv, w1v, w2v = _compress_weights(cmp_pos_v[0], cmp_w1_v[0], cmp_w2_v[0])
    pos = jnp.concatenate([posk, posv, jnp.zeros((4, posk.shape[1]), F32)], axis=0)
    lam_rows = jnp.zeros((8, LANES), F32).at[0:4, 0:DIFF_HD].set(
        jnp.stack([lambda_q1[0], lambda_k1[0], lambda_q2[0], lambda_k2[0]]).astype(F32))

    main, kc_tok, vc_tok, gates = _inproj(x2, norm1[0].reshape(1, dm), w_p, seq)
    main = main.reshape(bsz, seq, N_MAIN)

    nrow = seq // CMP_STRIDE
    kc, vc = _compress(kc_tok.reshape(bsz, nrow, CMP_STRIDE * LANES), vc_tok.reshape(bsz, nrow, CMP_STRIDE * LANES),
                       pos, w1k, w2k, w1v, w2v)
    ov = jnp.asarray(_overlap_matrices(nrow), BF16)
    o_cmp, nsel = _cmp_attention(main, kc, vc, ov)
    o_slc = _slc_attention(rel_bias, main, nsel)
    o_win = _win_attention(rel_bias, main)
    o_diff = _diff_attention(rel_bias, lam_rows, main, diff_subln[0].reshape(1, DIFF_VD))

    n = bsz * seq
    x1, h2 = _outproj(x2, o_diff.reshape(n, DIFF_W), o_cmp.reshape(n, NSA_W), o_slc.reshape(n, NSA_W),
                      o_win.reshape(n, NSA_W), gates, gn_p, wo_p, norm2[0].reshape(1, dm))
    out = _ffn(x1, h2, w_gate[0].astype(BF16), w_up[0].astype(BF16), w_down[0].astype(BF16),
               final_norm.reshape(1, dm))
    return out.reshape(bsz, seq, dm)
```

```python
import functools
import math

import numpy as np
import jax
import jax.numpy as jnp
from jax import lax
from jax.experimental import pallas as pl
from jax.experimental.pallas import tpu as pltpu

F32 = jnp.float32
BF16 = jnp.bfloat16

D_MODEL = 1024
DIFF_HEADS = 4
DIFF_HD = 64
DIFF_VD = 128
DIFF_W = 512
NSA_HEADS = 8
NSA_KV = 2
NSA_GQ = 4
NSA_HD = 64
NSA_W = 512
CMP_LEN = 32
CMP_STRIDE = 16
CMP_HIDDEN = 256
SLC_BLOCK = 64
SLC_TOPK = 16
N_SLC = 32
WINDOW = 512
N_BUCKETS = 32
MAX_DISTANCE = 128
D_FF = 2816
N_IN = 2840
NEG = -1e30
EPS = 1e-6
LAMBDA_INIT = 0.8 - 0.6 * math.exp(-0.3 * 0)

LANES = 128
HALF = LANES // 2
COL_DQ, COL_DK, COL_DV, COL_NQ, COL_KSL, COL_VSL, COL_KW, COL_VW, N_MAIN = (
    0, 512, 1024, 1536, 2048, 2304, 2432, 2560, 2688)
N_PROJ = N_MAIN + 3 * LANES
VMEM_LIMIT = 48 * 1024 * 1024

TQ = 256
TK = 256
TM = 512
TQ_CMP = 256


def _t5_thresholds():
    d = np.arange(0, 4 * MAX_DISTANCE)
    max_exact = N_BUCKETS // 2
    val = (np.log(np.maximum(d, 1).astype(np.float32) / np.float32(max_exact))
           / np.float32(math.log(MAX_DISTANCE / max_exact)) * np.float32(N_BUCKETS - max_exact))
    large = np.minimum(max_exact + val.astype(np.int32), N_BUCKETS - 1)
    bucket = np.where(d < max_exact, d, large)
    assert np.all(np.diff(bucket) >= 0) and bucket[-1] == N_BUCKETS - 1
    return [int(np.argmax(bucket >= k)) for k in range(N_BUCKETS)]


T5_THRESH = _t5_thresholds()
assert T5_THRESH[-1] <= TK + 1
assert WINDOW == 2 * TK and TQ == TK


def _nt_dot(a, b):
    return lax.dot_general(a, b, (((1,), (1,)), ((), ())), preferred_element_type=F32)


def _dot(a, b):
    return jnp.dot(a, b, preferred_element_type=F32)


def _rms(x, g):
    ms = jnp.mean(x * x, axis=-1, keepdims=True)
    return x * lax.rsqrt(ms + EPS) * g


def _params(sem):
    return pltpu.CompilerParams(dimension_semantics=sem, vmem_limit_bytes=VMEM_LIMIT)


def _inproj_kernel(x_ref, g_ref, w_ref, main_ref, kc_ref, vc_ref, gate_ref, *, seq):
    tm = x_ref.shape[0]
    h = _rms(x_ref[...], g_ref[...]).astype(BF16)
    for c in range(0, N_MAIN, 256):
        wd = min(256, N_MAIN - c)
        y = _dot(h, w_ref[:, c:c + wd])
        if c == COL_KSL:
            t0 = lax.rem(pl.program_id(0) * tm, seq)
            blk = (t0 + lax.broadcasted_iota(jnp.int32, (tm, 2 * LANES), 0)) >> 6
            lane = lax.broadcasted_iota(jnp.int32, (tm, 2 * LANES), 1)
            slot = jnp.where(lane < LANES, lane - HALF, lane - LANES)
            y = y + jnp.where((slot == blk) & (slot >= 0) & (slot < N_SLC), NEG, 0.0)
        main_ref[:, c:c + wd] = y.astype(BF16)
    kc_ref[...] = _dot(h, w_ref[:, N_MAIN:N_MAIN + LANES]).astype(BF16)
    vc_ref[...] = _dot(h, w_ref[:, N_MAIN + LANES:N_MAIN + 2 * LANES]).astype(BF16)
    gt = _dot(h, w_ref[:, N_MAIN + 2 * LANES:N_PROJ])
    gate_ref[...] = 1.0 / (1.0 + jnp.exp(-gt))


def _inproj(x2, g1, w_p, seq):
    n = x2.shape[0]
    return pl.pallas_call(
        functools.partial(_inproj_kernel, seq=seq),
        grid=(n // TM,),
        in_specs=[pl.BlockSpec((TM, D_MODEL), lambda i: (i, 0)),
                  pl.BlockSpec((1, D_MODEL), lambda i: (0, 0)),
                  pl.BlockSpec((D_MODEL, N_PROJ), lambda i: (0, 0))],
        out_specs=[pl.BlockSpec((TM, N_MAIN), lambda i: (i, 0)),
                   pl.BlockSpec((TM, LANES), lambda i: (i, 0)),
                   pl.BlockSpec((TM, LANES), lambda i: (i, 0)),
                   pl.BlockSpec((TM, LANES), lambda i: (i, 0))],
        out_shape=[jax.ShapeDtypeStruct((n, N_MAIN), BF16),
                   jax.ShapeDtypeStruct((n, LANES), BF16),
                   jax.ShapeDtypeStruct((n, LANES), BF16),
                   jax.ShapeDtypeStruct((n, LANES), F32)],
        compiler_params=_params(("parallel",)),
    )(x2, g1, w_p)


def _gelu_tanh(x):
    return x * (0.5 * (1.0 + jnp.tanh(math.sqrt(2.0 / math.pi) * (x + 0.044715 * (x * x * x)))))


def _compress_kernel(rk_ref, rv_ref, pos_ref, w1k_ref, w2k_ref, w1v_ref, w2v_ref, kc_ref, vc_ref):
    def comp(r_ref, p_top, p_bot, w1_ref, w2_ref):
        r = r_ref[0].astype(F32)
        top = _dot((r + p_top).astype(BF16), w1_ref[0])
        bot = _dot((r + p_bot).astype(BF16), w1_ref[1])
        nrow = top.shape[0]
        hid = top + pltpu.roll(bot, nrow - 1, 0)
        return _dot(_gelu_tanh(hid).astype(BF16), w2_ref[...]).astype(BF16)

    kc_ref[0] = comp(rk_ref, pos_ref[0:1, :], pos_ref[1:2, :], w1k_ref, w2k_ref)
    vc_ref[0] = comp(rv_ref, pos_ref[2:3, :], pos_ref[3:4, :], w1v_ref, w2v_ref)


def _compress(rk, rv, pos, w1k, w2k, w1v, w2v):
    bsz, nrow, width = rk.shape
    hid2 = 2 * CMP_HIDDEN
    return pl.pallas_call(
        _compress_kernel,
        grid=(bsz,),
        in_specs=[pl.BlockSpec((1, nrow, width), lambda b: (b, 0, 0)),
                  pl.BlockSpec((1, nrow, width), lambda b: (b, 0, 0)),
                  pl.BlockSpec((8, width), lambda b: (0, 0)),
                  pl.BlockSpec((2, width, hid2), lambda b: (0, 0, 0)),
                  pl.BlockSpec((hid2, LANES), lambda b: (0, 0)),
                  pl.BlockSpec((2, width, hid2), lambda b: (0, 0, 0)),
                  pl.BlockSpec((hid2, LANES), lambda b: (0, 0))],
        out_specs=[pl.BlockSpec((1, nrow, LANES), lambda b: (b, 0, 0)),
                   pl.BlockSpec((1, nrow, LANES), lambda b: (b, 0, 0))],
        out_shape=[jax.ShapeDtypeStruct((bsz, nrow, LANES), BF16),
                   jax.ShapeDtypeStruct((bsz, nrow, LANES), BF16)],
        compiler_params=_params(("parallel",)),
    )(rk, rv, pos, w1k, w2k, w1v, w2v)


def _group_queries(q, nslab, extra=None):
    lane = lax.broadcasted_iota(jnp.int32, (1, LANES), 1)
    scale = NSA_HD ** -0.5
    out = []
    for g in range(2):
        keep = jnp.where((lane >= g * HALF) & (lane < (g + 1) * HALF), scale, 0.0).astype(BF16)
        slabs = [q[:, j * LANES:(j + 1) * LANES] * keep for j in range(nslab)]
        if extra is not None:
            slabs = [s + extra[g] for s in slabs]
        out.append(jnp.concatenate(slabs, axis=0))
    return out


def _fill_bias(bias_ref, rel_ref, slots, tq, tk):
    r = lax.broadcasted_iota(jnp.int32, (tq, tk), 0)
    c = lax.broadcasted_iota(jnp.int32, (tq, tk), 1)
    for i, col in slots:
        far = rel_ref[N_BUCKETS - 1, col]
        for d in range(2):
            dist = r - c + d * tq
            val = jnp.zeros((tq, tk), F32) + rel_ref[0, col]
            for k in range(1, N_BUCKETS):
                val = jnp.where(dist >= T5_THRESH[k], rel_ref[k, col], val)
            val = val - far
            if d == 0:
                val = jnp.where(dist >= 0, val, NEG)
            bias_ref[i, d] = val


def _softmax_step(s, v, m_ref, acc_ref):
    tk = s.shape[1]
    m_prev = m_ref[...]
    m_new = jnp.maximum(m_prev, jnp.max(s, axis=-1, keepdims=True))
    alpha = jnp.exp(m_prev - m_new)
    p = jnp.exp(s - jnp.concatenate([m_new] * (tk // LANES), axis=1))
    vaug = jnp.concatenate([v, jnp.ones((tk, LANES), BF16)], axis=1)
    acc_ref[...] = jnp.concatenate([alpha, alpha], axis=1) * acc_ref[...] + _dot(p.astype(BF16), vaug)
    m_ref[...] = m_new


def _attend(q, k, v, add, m_ref, acc_ref):
    s = _nt_dot(q, k)
    if add is not None:
        tq, tk = add.shape[-2:]
        nmap = s.shape[0] // tq
        add3 = add if add.ndim == 3 else add[None]
        s = (s.reshape(nmap, tq, tk) + add3).reshape(nmap * tq, tk)
    _softmax_step(s, v, m_ref, acc_ref)


def _init_state(m_ref, acc_ref):
    m_ref[...] = jnp.full(m_ref.shape, NEG, F32)
    acc_ref[...] = jnp.zeros(acc_ref.shape, F32)


def _normalized(acc_ref, i):
    return acc_ref[i, :, 0:LANES] * (1.0 / acc_ref[i, :, LANES:2 * LANES])


def _store_nsa_out(o_ref, acc_ref, tq):
    lane = lax.broadcasted_iota(jnp.int32, (1, LANES), 1)
    o = [_normalized(acc_ref, g) for g in range(NSA_KV)]
    for j in range(NSA_GQ):
        o_ref[0, :, j * LANES:(j + 1) * LANES] = jnp.where(
            lane < HALF, o[0][j * tq:(j + 1) * tq], o[1][j * tq:(j + 1) * tq]).astype(BF16)


def _cmp_kernel(q_ref, kc_ref, vc_ref, ovt_ref, o_ref, nsel_ref, *, tq):
    qi = pl.program_id(1)
    nblk = kc_ref.shape[1]
    lane = lax.broadcasted_iota(jnp.int32, (1, LANES), 1)
    n = lax.broadcasted_iota(jnp.int32, (1, nblk), 1)
    t4 = qi * tq + (lax.broadcasted_iota(jnp.int32, (NSA_GQ * tq, 1), 0) & (tq - 1))
    vis4 = (n * CMP_STRIDE + (CMP_LEN - 1)) <= t4
    has4 = jnp.where(t4 >= CMP_LEN - 1, 1.0, 0.0)
    qg = _group_queries(q_ref[0], NSA_GQ)
    kc = kc_ref[0]
    vc = vc_ref[0]
    outs, imp_t = [], jnp.zeros((LANES, tq), F32)
    for g in range(NSA_KV):
        s = jnp.where(vis4, _nt_dot(qg[g], kc), NEG)
        e = jnp.exp(s - jnp.max(s, axis=-1, keepdims=True))
        p = e * (has4 / jnp.sum(e, axis=-1, keepdims=True))
        outs.append(_dot(p.astype(BF16), vc))
        ps = p[0:tq] + p[tq:2 * tq] + p[2 * tq:3 * tq] + p[3 * tq:4 * tq]
        hi = ps.astype(BF16)
        r1 = ps - hi.astype(F32)
        mid = r1.astype(BF16)
        lo = (r1 - mid.astype(F32)).astype(BF16)
        imp_t = imp_t + (_nt_dot(ovt_ref[g], hi) + _nt_dot(ovt_ref[g], mid) + _nt_dot(ovt_ref[g], lo))
    for j in range(NSA_GQ):
        o_ref[0, :, j * LANES:(j + 1) * LANES] = jnp.where(
            lane < HALF, outs[0][j * tq:(j + 1) * tq], outs[1][j * tq:(j + 1) * tq]).astype(BF16)

    tl = qi * tq + lax.broadcasted_iota(jnp.int32, (N_SLC, tq), 1)
    jrow = lax.broadcasted_iota(jnp.int32, (N_SLC, tq), 0)
    cur = tl >> 6
    valid = (jrow * SLC_BLOCK) <= tl
    forced = (jrow == 0) | (jrow == cur) | (jrow == cur - 1)
    sub = 8
    zero = jnp.zeros((N_SLC, tq), F32)
    dropped = []
    for base in (0, HALF):
        x = jnp.where(forced, jnp.inf, jnp.where(valid, imp_t[base:base + N_SLC], -jnp.inf))
        others = [jnp.broadcast_to(x[jj:jj + 1], (sub, tq)) for jj in range(N_SLC)]
        chunks = []
        for r0 in range(0, N_SLC, sub):
            xr = x[r0:r0 + sub]
            jr = r0 + lax.broadcasted_iota(jnp.int32, (sub, tq), 0)
            cnt = jnp.zeros((sub, tq), F32)
            for jj in range(N_SLC):
                if jj < r0:
                    beats = others[jj] >= xr
                elif jj >= r0 + sub:
                    beats = others[jj] > xr
                else:
                    beats = (others[jj] > xr) | ((others[jj] == xr) & (jr > jj))
                cnt = cnt + jnp.where(beats, 1.0, 0.0)
            chunks.append(jnp.where(cnt < SLC_TOPK, 0.0, 1.0))
        dropped.append(jnp.concatenate(chunks, axis=0))
    drop = jnp.concatenate([dropped[0], zero, dropped[1], zero], axis=0).T
    nsel_ref[0, :, 0:LANES] = jnp.where(lane >= HALF, drop, 0.0).astype(BF16)
    nsel_ref[0, :, LANES:2 * LANES] = jnp.where(lane < HALF, drop, 0.0).astype(BF16)


def _cmp_attention(main, kc, vc, ovt):
    bsz, seq, _ = main.shape
    nblk = kc.shape[1]
    tq = TQ_CMP
    return pl.pallas_call(
        functools.partial(_cmp_kernel, tq=tq),
        grid=(bsz, seq // tq),
        in_specs=[pl.BlockSpec((1, tq, NSA_W), lambda b, i: (b, i, COL_NQ // NSA_W)),
                  pl.BlockSpec((1, nblk, LANES), lambda b, i: (b, 0, 0)),
                  pl.BlockSpec((1, nblk, LANES), lambda b, i: (b, 0, 0)),
                  pl.BlockSpec((2, LANES, nblk), lambda b, i: (0, 0, 0))],
        out_specs=[pl.BlockSpec((1, tq, NSA_W), lambda b, i: (b, i, 0)),
                   pl.BlockSpec((1, tq, 2 * LANES), lambda b, i: (b, i, 0))],
        out_shape=[jax.ShapeDtypeStruct((bsz, seq, NSA_W), BF16),
                   jax.ShapeDtypeStruct((bsz, seq, 2 * LANES), BF16)],
        compiler_params=_params(("parallel", "parallel")),
    )(main, kc, vc, ovt)


def _slc_kernel(rel_ref, q_ref, k_ref, v_ref, nsel_ref, o_ref, m_ref, acc_ref, bias_ref, *, tq, tk):
    b = pl.program_id(0)
    qi = pl.program_id(1)

    @pl.when((b == 0) & (qi == 0))
    def _():
        _fill_bias(bias_ref, rel_ref, [(i, DIFF_HEADS + i) for i in range(NSA_HEADS)], tq, tk)

    nsel = nsel_ref[0]
    qg = _group_queries(q_ref[0], NSA_GQ, extra=[nsel[:, 0:LANES], nsel[:, LANES:2 * LANES]])
    _init_state(m_ref, acc_ref)

    def tile(kj, d):
        start = pl.multiple_of(kj * tk, tk)
        v = v_ref[0, pl.ds(start, tk), :]
        for g in range(NSA_KV):
            k = k_ref[0, pl.ds(start, tk), g * LANES:(g + 1) * LANES]
            add = None if d is None else bias_ref[g * NSA_GQ:(g + 1) * NSA_GQ, d]
            _attend(qg[g], k, v, add, m_ref.at[g], acc_ref.at[g])

    def far(kj, carry):
        tile(kj, None)
        return carry

    lax.fori_loop(0, qi - 1, far, 0)

    @pl.when(qi >= 1)
    def _():
        tile(qi - 1, 1)

    tile(qi, 0)
    _store_nsa_out(o_ref, acc_ref, tq)


def _slc_attention(rel_bias, main, nsel):
    bsz, seq, _ = main.shape
    tq, tk = TQ, TK
    rows = NSA_GQ * tq
    return pl.pallas_call(
        functools.partial(_slc_kernel, tq=tq, tk=tk),
        grid=(bsz, seq // tq),
        in_specs=[pl.BlockSpec(memory_space=pltpu.SMEM),
                  pl.BlockSpec((1, tq, NSA_W), lambda b, i: (b, i, COL_NQ // NSA_W)),
                  pl.BlockSpec((1, seq, 2 * LANES), lambda b, i: (b, 0, COL_KSL // (2 * LANES))),
                  pl.BlockSpec((1, seq, LANES), lambda b, i: (b, 0, COL_VSL // LANES)),
                  pl.BlockSpec((1, tq, 2 * LANES), lambda b, i: (b, i, 0))],
        out_specs=pl.BlockSpec((1, tq, NSA_W), lambda b, i: (b, i, 0)),
        out_shape=jax.ShapeDtypeStruct((bsz, seq, NSA_W), BF16),
        scratch_shapes=[pltpu.VMEM((NSA_KV, rows, LANES), F32),
                        pltpu.VMEM((NSA_KV, rows, 2 * LANES), F32),
                        pltpu.VMEM((NSA_HEADS, 2, tq, tk), F32)],
        compiler_params=_params(("arbitrary", "arbitrary")),
    )(rel_bias, main, main, main, nsel)


def _win_kernel(rel_ref, q_ref, k_ref, v_ref, o_ref, m_ref, acc_ref, bias_ref, band_ref, *, tq, tk):
    b = pl.program_id(0)
    qi = pl.program_id(1)

    @pl.when((b == 0) & (qi == 0))
    def _():
        _fill_bias(bias_ref, rel_ref, [(i, DIFF_HEADS + i) for i in range(NSA_HEADS)], tq, tk)
        r = lax.broadcasted_iota(jnp.int32, (tq, tk), 0)
        c = lax.broadcasted_iota(jnp.int32, (tq, tk), 1)
        band_ref[...] = jnp.where(r < c, 0.0, NEG)

    qg = _group_queries(q_ref[0], NSA_GQ)
    _init_state(m_ref, acc_ref)

    def tile(kj, d):
        start = pl.multiple_of(kj * tk, tk)
        k = k_ref[0, pl.ds(start, tk), :]
        v = v_ref[0, pl.ds(start, tk), :]
        for g in range(NSA_KV):
            add = band_ref[...] if d == 2 else bias_ref[g * NSA_GQ:(g + 1) * NSA_GQ, d]
            _attend(qg[g], k, v, add, m_ref.at[g], acc_ref.at[g])

    for d in (2, 1):
        @pl.when(qi >= d)
        def _():
            tile(qi - d, d)

    tile(qi, 0)
    _store_nsa_out(o_ref, acc_ref, tq)


def _win_attention(rel_bias, main):
    bsz, seq, _ = main.shape
    tq, tk = TQ, TK
    rows = NSA_GQ * tq
    return pl.pallas_call(
        functools.partial(_win_kernel, tq=tq, tk=tk),
        grid=(bsz, seq // tq),
        in_specs=[pl.BlockSpec(memory_space=pltpu.SMEM),
                  pl.BlockSpec((1, tq, NSA_W), lambda b, i: (b, i, COL_NQ // NSA_W)),
                  pl.BlockSpec((1, seq, LANES), lambda b, i: (b, 0, COL_KW // LANES)),
                  pl.BlockSpec((1, seq, LANES), lambda b, i: (b, 0, COL_VW // LANES))],
        out_specs=pl.BlockSpec((1, tq, NSA_W), lambda b, i: (b, i, 0)),
        out_shape=jax.ShapeDtypeStruct((bsz, seq, NSA_W), BF16),
        scratch_shapes=[pltpu.VMEM((NSA_KV, rows, LANES), F32),
                        pltpu.VMEM((NSA_KV, rows, 2 * LANES), F32),
                        pltpu.VMEM((NSA_HEADS, 2, tq, tk), F32),
                        pltpu.VMEM((tq, tk), F32)],
        compiler_params=_params(("arbitrary", "arbitrary")),
    )(rel_bias, main, main, main)


def _diff_kernel(rel_ref, lam_ref, q_ref, k_ref, v_ref, g_ref, o_ref, m_ref, acc_ref, bias_ref, *, tq, tk):
    b = pl.program_id(0)
    qi = pl.program_id(1)

    @pl.when((b == 0) & (qi == 0))
    def _():
        _fill_bias(bias_ref, rel_ref, [(h, h) for h in range(DIFF_HEADS)], tq, tk)

    q = q_ref[0]
    qq = []
    for h in range(DIFF_HEADS):
        q1, q2 = _group_queries(q[:, h * LANES:(h + 1) * LANES], 1)
        qq.append(jnp.concatenate([q1, q2], axis=0))
    _init_state(m_ref, acc_ref)

    def tile(kj, d):
        start = pl.multiple_of(kj * tk, tk)
        for h in range(DIFF_HEADS):
            k = k_ref[0, pl.ds(start, tk), h * LANES:(h + 1) * LANES]
            v = v_ref[0, pl.ds(start, tk), h * LANES:(h + 1) * LANES]
            add = None if d is None else bias_ref[h, d]
            _attend(qq[h], k, v, add, m_ref.at[h], acc_ref.at[h])

    def far(kj, carry):
        tile(kj, None)
        return carry

    lax.fori_loop(0, qi - 1, far, 0)

    @pl.when(qi >= 1)
    def _():
        tile(qi - 1, 1)

    tile(qi, 0)

    lm = lam_ref[...]
    lam = (jnp.exp(jnp.sum(lm[0:1] * lm[1:2], axis=-1, keepdims=True))
           - jnp.exp(jnp.sum(lm[2:3] * lm[3:4], axis=-1, keepdims=True)) + LAMBDA_INIT)
    for h in range(DIFF_HEADS):
        o = _normalized(acc_ref, h)
        a = o[0:tq] - lam * o[tq:2 * tq]
        o_ref[0, :, h * LANES:(h + 1) * LANES] = (_rms(a, g_ref[...]) * (1.0 - LAMBDA_INIT)).astype(BF16)


def _diff_attention(rel_bias, lam_rows, main, subln):
    bsz, seq, _ = main.shape
    tq, tk = TQ, TK
    return pl.pallas_call(
        functools.partial(_diff_kernel, tq=tq, tk=tk),
        grid=(bsz, seq // tq),
        in_specs=[pl.BlockSpec(memory_space=pltpu.SMEM),
                  pl.BlockSpec((8, LANES), lambda b, i: (0, 0)),
                  pl.BlockSpec((1, tq, DIFF_W), lambda b, i: (b, i, COL_DQ // DIFF_W)),
                  pl.BlockSpec((1, seq, DIFF_W), lambda b, i: (b, 0, COL_DK // DIFF_W)),
                  pl.BlockSpec((1, seq, DIFF_W), lambda b, i: (b, 0, COL_DV // DIFF_W)),
                  pl.BlockSpec((1, LANES), lambda b, i: (0, 0))],
        out_specs=pl.BlockSpec((1, tq, DIFF_W), lambda b, i: (b, i, 0)),
        out_shape=jax.ShapeDtypeStruct((bsz, seq, DIFF_W), BF16),
        scratch_shapes=[pltpu.VMEM((DIFF_HEADS, 2 * tq, LANES), F32),
                        pltpu.VMEM((DIFF_HEADS, 2 * tq, 2 * LANES), F32),
                        pltpu.VMEM((DIFF_HEADS, 2, tq, tk), F32)],
        compiler_params=_params(("arbitrary", "arbitrary")),
    )(rel_bias, lam_rows, main, main, main, subln)


FF_CHUNK = 256


def _tail_kernel(x_ref, od_ref, oc_ref, os_ref, ow_ref, gate_ref, gn_ref, wo_ref, g2_ref,
                 wg_ref, wu_ref, wd_ref, gf_ref, o_ref, acc_ref):
    lane = lax.broadcasted_iota(jnp.int32, (1, LANES), 1)
    gates = gate_ref[...]
    branches = (oc_ref, os_ref, ow_ref)
    slabs = []
    for j in range(NSA_GQ):
        acc = None
        for br in range(3):
            c0 = (0 * NSA_GQ + j) * 3 + br
            c1 = (1 * NSA_GQ + j) * 3 + br
            gexp = jnp.where(lane < HALF, gates[:, c0:c0 + 1], gates[:, c1:c1 + 1])
            term = gexp * branches[br][:, j * LANES:(j + 1) * LANES].astype(F32)
            acc = term if acc is None else acc + term
        slabs.append(acc)
    o_nsa = _rms(jnp.concatenate(slabs, axis=-1), gn_ref[...]).astype(BF16)
    attn = _dot(od_ref[...], wo_ref[0:DIFF_W, :]) + _dot(o_nsa, wo_ref[DIFF_W:DIFF_W + NSA_W, :])
    x1 = x_ref[...] + attn
    acc_ref[...] = x1
    h = _rms(x1, g2_ref[...]).astype(BF16)
    for c in range(0, D_FF, FF_CHUNK):
        gate = _dot(h, wg_ref[:, c:c + FF_CHUNK])
        up = _dot(h, wu_ref[:, c:c + FF_CHUNK])
        act = (gate * (1.0 / (1.0 + jnp.exp(-gate))) * up).astype(BF16)
        acc_ref[...] += _dot(act, wd_ref[c:c + FF_CHUNK, :])
    o_ref[...] = _rms(acc_ref[...], gf_ref[...])


def _tail(x2, od, oc, os_, ow, gates, gn, w_out, g2, wg, wu, wd, gf):
    n = x2.shape[0]
    row = lambda i: (i, 0)
    fixed = lambda i: (0, 0)
    once = pl.Buffered(1)
    return pl.pallas_call(
        _tail_kernel,
        grid=(n // TM,),
        in_specs=[pl.BlockSpec((TM, D_MODEL), row),
                  pl.BlockSpec((TM, DIFF_W), row),
                  pl.BlockSpec((TM, NSA_W), row),
                  pl.BlockSpec((TM, NSA_W), row),
                  pl.BlockSpec((TM, NSA_W), row),
                  pl.BlockSpec((TM, LANES), row),
                  pl.BlockSpec((1, NSA_W), fixed),
                  pl.BlockSpec((DIFF_W + NSA_W, D_MODEL), fixed, pipeline_mode=once),
                  pl.BlockSpec((1, D_MODEL), fixed),
                  pl.BlockSpec((D_MODEL, D_FF), fixed, pipeline_mode=once),
                  pl.BlockSpec((D_MODEL, D_FF), fixed, pipeline_mode=once),
                  pl.BlockSpec((D_FF, D_MODEL), fixed, pipeline_mode=once),
                  pl.BlockSpec((1, D_MODEL), fixed)],
        out_specs=pl.BlockSpec((TM, D_MODEL), row),
        out_shape=jax.ShapeDtypeStruct((n, D_MODEL), F32),
        scratch_shapes=[pltpu.VMEM((TM, D_MODEL), F32)],
        compiler_params=_params(("parallel",)),
    )(x2, od, oc, os_, ow, gates, gn, w_out, g2, wg, wu, wd, gf)


def _nsa_perm():
    p = np.arange(NSA_W)
    j, g, d = p // LANES, (p % LANES) // NSA_HD, p % NSA_HD
    return (g * NSA_GQ + j) * NSA_HD + d


def _overlap_matrices(nblk):
    n_cmp = nblk - 1
    nslc = nblk * CMP_STRIDE // SLC_BLOCK
    assert nslc == N_SLC
    cmp_start = np.arange(n_cmp) * CMP_STRIDE
    cmp_end = cmp_start + CMP_LEN - 1
    slc_start = np.arange(nslc) * SLC_BLOCK
    overlap = ((cmp_start[:, None] < slc_start[None, :] + SLC_BLOCK) & (cmp_end[:, None] >= slc_start[None, :]))
    ovt = np.zeros((NSA_KV, LANES, nblk), np.float32)
    for g, base in enumerate((HALF, 0)):
        ovt[g, base:base + nslc, :n_cmp] = overlap.T
    return ovt


def _compress_weights(pos, w1, w2):
    half = CMP_LEN // 2
    eye = jnp.eye(NSA_KV, dtype=F32)
    w1r = w1.reshape(2, half, NSA_HD, CMP_HIDDEN)
    w1big = jnp.einsum('slcj,pg->slpcgj', w1r, eye).reshape(2, half * NSA_KV * NSA_HD, NSA_KV * CMP_HIDDEN)
    w2big = jnp.einsum('jd,pg->pjgd', w2, eye).reshape(NSA_KV * CMP_HIDDEN, NSA_KV * NSA_HD)
    posr = jnp.broadcast_to(pos.reshape(2, half, 1, NSA_HD), (2, half, NSA_KV, NSA_HD)).reshape(2, -1)
    return posr, w1big.astype(BF16), w2big.astype(BF16)


def kernel(x, norm1, w_in, lambda_q1, lambda_k1, lambda_q2, lambda_k2, diff_subln, cmp_pos_k, cmp_w1_k, cmp_w2_k,
           cmp_pos_v, cmp_w1_v, cmp_w2_v, nsa_norm, w_out, norm2, w_gate, w_up, w_down, rel_bias, final_norm):
    bsz, seq, dm = x.shape
    assert dm == D_MODEL and seq % TQ_CMP == 0 and seq % TM == 0 and seq == N_SLC * SLC_BLOCK
    assert norm1.shape[0] == 1, "single layer"
    perm = _nsa_perm()
    x2 = x.reshape(bsz * seq, dm)

    w = w_in[0]
    zhalf = jnp.zeros((dm, HALF), F32)
    w_p = jnp.concatenate([w[:, :1536], w[:, 1536 + perm],
                           w[:, 2304:2368], zhalf, zhalf, w[:, 2368:2432],
                           w[:, 2432:2816],
                           w[:, 2048:2304],
                           w[:, 2816:N_IN], jnp.zeros((dm, LANES - (N_IN - 2816)), F32)], axis=1).astype(BF16)
    assert w_p.shape[1] == N_PROJ
    wo = w_out[0]
    wo_p = jnp.concatenate([wo[:DIFF_W], wo[DIFF_W + perm]], axis=0).astype(BF16)
    gn_p = nsa_norm[0][perm].reshape(1, NSA_W)
    posk, w1k, w2k = _compress_weights(cmp_pos_k[0], cmp_w1_k[0], cmp_w2_k[0])
    posv, w1v, w2v = _compress_weights(cmp_pos_v[0], cmp_w1_v[0], cmp_w2_v[0])
    pos = jnp.concatenate([posk, posv, jnp.zeros((4, posk.shape[1]), F32)], axis=0)
    lam_rows = jnp.zeros((8, LANES), F32).at[0:4, 0:DIFF_HD].set(
        jnp.stack([lambda_q1[0], lambda_k1[0], lambda_q2[0], lambda_k2[0]]).astype(F32))

    main, kc_tok, vc_tok, gates = _inproj(x2, norm1[0].reshape(1, dm), w_p, seq)
    main = main.reshape(bsz, seq, N_MAIN)

    nrow = seq // CMP_STRIDE
    kc, vc = _compress(kc_tok.reshape(bsz, nrow, CMP_STRIDE * LANES), vc_tok.reshape(bsz, nrow, CMP_STRIDE * LANES),
                       pos, w1k, w2k, w1v, w2v)
    ovt = jnp.asarray(_overlap_matrices(nrow), BF16)
    o_cmp, nsel = _cmp_attention(main, kc, vc, ovt)
    o_slc = _slc_attention(rel_bias, main, nsel)
    o_win = _win_attention(rel_bias, main)
    o_diff = _diff_attention(rel_bias, lam_rows, main, diff_subln[0].reshape(1, DIFF_VD))

    n = bsz * seq
    out = _tail(x2, o_diff.reshape(n, DIFF_W), o_cmp.reshape(n, NSA_W), o_slc.reshape(n, NSA_W),
                o_win.reshape(n, NSA_W), gates, gn_p, wo_p, norm2[0].reshape(1, dm),
                w_gate[0].astype(BF16), w_up[0].astype(BF16), w_down[0].astype(BF16), final_norm.reshape(1, dm))
    return out.reshape(bsz, seq, dm)
```

```python
import functools
import math

import numpy as np
import jax
import jax.numpy as jnp
from jax import lax
from jax.experimental import pallas as pl
from jax.experimental.pallas import tpu as pltpu

F32 = jnp.float32
BF16 = jnp.bfloat16

D_MODEL = 1024
DIFF_HEADS = 4
DIFF_HD = 64
DIFF_VD = 128
DIFF_W = 512
NSA_HEADS = 8
NSA_KV = 2
NSA_GQ = 4
NSA_HD = 64
NSA_W = 512
CMP_LEN = 32
CMP_STRIDE = 16
CMP_HIDDEN = 256
SLC_BLOCK = 64
SLC_TOPK = 16
N_SLC = 32
WINDOW = 512
N_BUCKETS = 32
MAX_DISTANCE = 128
D_FF = 2816
N_IN = 2840
NEG = -1e30
EPS = 1e-6
LAMBDA_INIT = 0.8 - 0.6 * math.exp(-0.3 * 0)
LOG2E = math.log2(math.e)
QSCALE = NSA_HD ** -0.5 * LOG2E

LANES = 128
HALF = LANES // 2
MXU_N = 256
COL_DQ, COL_DK, COL_DV, COL_NQ, COL_KSL, COL_VSL, COL_KW, COL_VW, N_MAIN = (
    0, 512, 1024, 1536, 2048, 2304, 2432, 2560, 2688)
N_PROJ = N_MAIN + 3 * LANES
VMEM_LIMIT = 48 * 1024 * 1024

TQ = 256
TK = 256
TM = 512
TQ_CMP = 256


def _t5_thresholds():
    d = np.arange(0, 4 * MAX_DISTANCE)
    max_exact = N_BUCKETS // 2
    val = (np.log(np.maximum(d, 1).astype(np.float32) / np.float32(max_exact))
           / np.float32(math.log(MAX_DISTANCE / max_exact)) * np.float32(N_BUCKETS - max_exact))
    large = np.minimum(max_exact + val.astype(np.int32), N_BUCKETS - 1)
    bucket = np.where(d < max_exact, d, large)
    assert np.all(np.diff(bucket) >= 0) and bucket[-1] == N_BUCKETS - 1
    return [int(np.argmax(bucket >= k)) for k in range(N_BUCKETS)]


T5_THRESH = _t5_thresholds()
assert T5_THRESH[-1] <= TK + 1
assert WINDOW == 2 * TK and TQ == TK


def _nt_dot(a, b):
    return lax.dot_general(a, b, (((1,), (1,)), ((), ())), preferred_element_type=F32)


def _dot(a, b):
    return jnp.dot(a, b, preferred_element_type=F32)


def _rms(x, g):
    ms = jnp.mean(x * x, axis=-1, keepdims=True)
    return x * lax.rsqrt(ms + EPS) * g


def _params(sem):
    return pltpu.CompilerParams(dimension_semantics=sem, vmem_limit_bytes=VMEM_LIMIT)


def _inproj_kernel(x_ref, g_ref, w_ref, main_ref, kc_ref, vc_ref, gate_ref, *, seq):
    tm = x_ref.shape[0]
    h = _rms(x_ref[...], g_ref[...]).astype(BF16)
    for c in range(0, N_PROJ, MXU_N):
        y = _dot(h, w_ref[:, c:c + MXU_N])
        if c == COL_KSL:
            t0 = lax.rem(pl.program_id(0) * tm, seq)
            blk = (t0 + lax.broadcasted_iota(jnp.int32, (tm, 2 * LANES), 0)) >> 6
            lane = lax.broadcasted_iota(jnp.int32, (tm, 2 * LANES), 1)
            slot = jnp.where(lane < LANES, lane - HALF, lane - LANES)
            y = y + jnp.where((slot == blk) & (slot >= 0) & (slot < N_SLC), NEG, 0.0)
        if c + MXU_N <= N_MAIN:
            main_ref[:, c:c + MXU_N] = y.astype(BF16)
        elif c == COL_VW:
            main_ref[:, c:c + LANES] = y[:, 0:LANES].astype(BF16)
            kc_ref[...] = y[:, LANES:MXU_N].astype(BF16)
        else:
            vc_ref[...] = y[:, 0:LANES].astype(BF16)
            gate_ref[...] = 1.0 / (1.0 + jnp.exp(-y[:, LANES:MXU_N]))


def _inproj(x2, g1, w_p, seq):
    n = x2.shape[0]
    return pl.pallas_call(
        functools.partial(_inproj_kernel, seq=seq),
        grid=(n // TM,),
        in_specs=[pl.BlockSpec((TM, D_MODEL), lambda i: (i, 0)),
                  pl.BlockSpec((1, D_MODEL), lambda i: (0, 0)),
                  pl.BlockSpec((D_MODEL, N_PROJ), lambda i: (0, 0))],
        out_specs=[pl.BlockSpec((TM, N_MAIN), lambda i: (i, 0)),
                   pl.BlockSpec((TM, LANES), lambda i: (i, 0)),
                   pl.BlockSpec((TM, LANES), lambda i: (i, 0)),
                   pl.BlockSpec((TM, LANES), lambda i: (i, 0))],
        out_shape=[jax.ShapeDtypeStruct((n, N_MAIN), BF16),
                   jax.ShapeDtypeStruct((n, LANES), BF16),
                   jax.ShapeDtypeStruct((n, LANES), BF16),
                   jax.ShapeDtypeStruct((n, LANES), F32)],
        compiler_params=_params(("parallel",)),
    )(x2, g1, w_p)


def _gelu_tanh(x):
    return x * (0.5 * (1.0 + jnp.tanh(math.sqrt(2.0 / math.pi) * (x + 0.044715 * (x * x * x)))))


def _compress_kernel(rk_ref, rv_ref, pos_ref, w1k_ref, w2k_ref, w1v_ref, w2v_ref, kc_ref, vc_ref):
    def comp(r_ref, p_top, p_bot, w1_ref, w2_ref):
        r = r_ref[0].astype(F32)
        top = _dot((r + p_top).astype(BF16), w1_ref[0])
        bot = _dot((r + p_bot).astype(BF16), w1_ref[1])
        nrow = top.shape[0]
        hid = top + pltpu.roll(bot, nrow - 1, 0)
        return _dot(_gelu_tanh(hid).astype(BF16), w2_ref[...]).astype(BF16)

    kc_ref[0] = comp(rk_ref, pos_ref[0:1, :], pos_ref[1:2, :], w1k_ref, w2k_ref)
    vc_ref[0] = comp(rv_ref, pos_ref[2:3, :], pos_ref[3:4, :], w1v_ref, w2v_ref)


def _compress(rk, rv, pos, w1k, w2k, w1v, w2v):
    bsz, nrow, width = rk.shape
    hid2 = 2 * CMP_HIDDEN
    return pl.pallas_call(
        _compress_kernel,
        grid=(bsz,),
        in_specs=[pl.BlockSpec((1, nrow, width), lambda b: (b, 0, 0)),
                  pl.BlockSpec((1, nrow, width), lambda b: (b, 0, 0)),
                  pl.BlockSpec((8, width), lambda b: (0, 0)),
                  pl.BlockSpec((2, width, hid2), lambda b: (0, 0, 0)),
                  pl.BlockSpec((hid2, LANES), lambda b: (0, 0)),
                  pl.BlockSpec((2, width, hid2), lambda b: (0, 0, 0)),
                  pl.BlockSpec((hid2, LANES), lambda b: (0, 0))],
        out_specs=[pl.BlockSpec((1, nrow, LANES), lambda b: (b, 0, 0)),
                   pl.BlockSpec((1, nrow, LANES), lambda b: (b, 0, 0))],
        out_shape=[jax.ShapeDtypeStruct((bsz, nrow, LANES), BF16),
                   jax.ShapeDtypeStruct((bsz, nrow, LANES), BF16)],
        compiler_params=_params(("parallel",)),
    )(rk, rv, pos, w1k, w2k, w1v, w2v)


def _group_queries(q, nslab, extra=None):
    lane = lax.broadcasted_iota(jnp.int32, (1, LANES), 1)
    out = []
    for g in range(2):
        keep = jnp.where((lane >= g * HALF) & (lane < (g + 1) * HALF), 1.0, 0.0).astype(BF16)
        slabs = [q[:, j * LANES:(j + 1) * LANES] * keep for j in range(nslab)]
        if extra is not None:
            slabs = [s + extra[g] for s in slabs]
        out.append(jnp.concatenate(slabs, axis=0))
    return out


def _fill_bias(bias_ref, rel_ref, slots, tq, tk):
    r = lax.broadcasted_iota(jnp.int32, (tq, tk), 0)
    c = lax.broadcasted_iota(jnp.int32, (tq, tk), 1)
    for i, col in slots:
        far = rel_ref[N_BUCKETS - 1, col]
        for d in range(2):
            dist = r - c + d * tq
            val = jnp.zeros((tq, tk), F32) + rel_ref[0, col]
            for k in range(1, N_BUCKETS):
                val = jnp.where(dist >= T5_THRESH[k], rel_ref[k, col], val)
            val = (val - far) * LOG2E
            if d == 0:
                val = jnp.where(dist >= 0, val, NEG)
            bias_ref[i, d] = val


def _softmax_step(s, v, m_ref, acc_ref):
    tk = s.shape[1]
    m_prev = m_ref[...]
    m_new = jnp.maximum(m_prev, jnp.max(s, axis=-1, keepdims=True))
    alpha = jnp.exp2(m_prev - m_new)
    p = jnp.exp2(s - jnp.concatenate([m_new] * (tk // LANES), axis=1))
    vaug = jnp.concatenate([v, jnp.ones((tk, LANES), BF16)], axis=1)
    acc_ref[...] = jnp.concatenate([alpha, alpha], axis=1) * acc_ref[...] + _dot(p.astype(BF16), vaug)
    m_ref[...] = m_new


def _attend(q, k, v, add, m_ref, acc_ref):
    s = _nt_dot(q, k)
    if add is not None:
        tq, tk = add.shape[-2:]
        nmap = s.shape[0] // tq
        add3 = add if add.ndim == 3 else add[None]
        s = (s.reshape(nmap, tq, tk) + add3).reshape(nmap * tq, tk)
    _softmax_step(s, v, m_ref, acc_ref)


def _far_tiles(tile, qi, tk):
    nfar = jnp.maximum(qi - 1, 0)

    def pair(j, carry):
        tile(2 * j, 2 * tk, None)
        return carry

    lax.fori_loop(0, nfar >> 1, pair, 0)

    @pl.when((nfar & 1) == 1)
    def _():
        tile(nfar - 1, tk, None)


def _init_state(m_ref, acc_ref):
    m_ref[...] = jnp.full(m_ref.shape, NEG, F32)
    acc_ref[...] = jnp.zeros(acc_ref.shape, F32)


def _normalized(acc_ref, i):
    return acc_ref[i, :, 0:LANES] * (1.0 / acc_ref[i, :, LANES:2 * LANES])


def _store_gated(o_ref, gate_ref, o, branch, tq):
    lane = lax.broadcasted_iota(jnp.int32, (1, LANES), 1)
    gates = gate_ref[0]
    for j in range(NSA_GQ):
        c0 = (0 * NSA_GQ + j) * 3 + branch
        c1 = (1 * NSA_GQ + j) * 3 + branch
        gexp = jnp.where(lane < HALF, gates[:, c0:c0 + 1], gates[:, c1:c1 + 1])
        slab = jnp.where(lane < HALF, o[0][j * tq:(j + 1) * tq], o[1][j * tq:(j + 1) * tq])
        o_ref[0, :, j * LANES:(j + 1) * LANES] = (gexp * slab).astype(BF16)


def _cmp_kernel(q_ref, kc_ref, vc_ref, ovt_ref, gate_ref, o_ref, nsel_ref, *, tq):
    qi = pl.program_id(1)
    nblk = kc_ref.shape[1]
    lane = lax.broadcasted_iota(jnp.int32, (1, LANES), 1)
    n = lax.broadcasted_iota(jnp.int32, (1, nblk), 1)
    t4 = qi * tq + (lax.broadcasted_iota(jnp.int32, (NSA_GQ * tq, 1), 0) & (tq - 1))
    vis4 = (n * CMP_STRIDE + (CMP_LEN - 1)) <= t4
    has4 = jnp.where(t4 >= CMP_LEN - 1, 1.0, 0.0)
    qg = _group_queries(q_ref[0], NSA_GQ)
    kc = kc_ref[0]
    vc = vc_ref[0]
    outs, imp_t = [], jnp.zeros((LANES, tq), F32)
    for g in range(NSA_KV):
        s = jnp.where(vis4, _nt_dot(qg[g], kc), NEG)
        e = jnp.exp2(s - jnp.max(s, axis=-1, keepdims=True))
        p = e * (has4 / jnp.sum(e, axis=-1, keepdims=True))
        outs.append(_dot(p.astype(BF16), vc))
        ps = p[0:tq] + p[tq:2 * tq] + p[2 * tq:3 * tq] + p[3 * tq:4 * tq]
        hi = ps.astype(BF16)
        r1 = ps - hi.astype(F32)
        mid = r1.astype(BF16)
        lo = (r1 - mid.astype(F32)).astype(BF16)
        imp_t = imp_t + (_nt_dot(ovt_ref[g], hi) + _nt_dot(ovt_ref[g], mid) + _nt_dot(ovt_ref[g], lo))
    _store_gated(o_ref, gate_ref, outs, 0, tq)

    tl = qi * tq + lax.broadcasted_iota(jnp.int32, (N_SLC, tq), 1)
    jrow = lax.broadcasted_iota(jnp.int32, (N_SLC, tq), 0)
    cur = tl >> 6
    valid = (jrow * SLC_BLOCK) <= tl
    forced = (jrow == 0) | (jrow == cur) | (jrow == cur - 1)
    sub = 8
    zero = jnp.zeros((N_SLC, tq), F32)
    dropped = []
    for base in (0, HALF):
        x = jnp.where(forced, jnp.inf, jnp.where(valid, imp_t[base:base + N_SLC], -jnp.inf))
        others = [jnp.broadcast_to(x[jj:jj + 1], (sub, tq)) for jj in range(N_SLC)]
        chunks = []
        for r0 in range(0, N_SLC, sub):
            xr = x[r0:r0 + sub]
            jr = r0 + lax.broadcasted_iota(jnp.int32, (sub, tq), 0)
            cnt = jnp.zeros((sub, tq), F32)
            for jj in range(N_SLC):
                if jj < r0:
                    beats = others[jj] >= xr
                elif jj >= r0 + sub:
                    beats = others[jj] > xr
                else:
                    beats = (others[jj] > xr) | ((others[jj] == xr) & (jr > jj))
                cnt = cnt + jnp.where(beats, 1.0, 0.0)
            chunks.append(jnp.where(cnt < SLC_TOPK, 0.0, 1.0))
        dropped.append(jnp.concatenate(chunks, axis=0))
    drop = jnp.concatenate([dropped[0], zero, dropped[1], zero], axis=0).T
    nsel_ref[0, :, 0:LANES] = jnp.where(lane >= HALF, drop, 0.0).astype(BF16)
    nsel_ref[0, :, LANES:2 * LANES] = jnp.where(lane < HALF, drop, 0.0).astype(BF16)


def _cmp_attention(main, kc, vc, ovt, gates):
    bsz, seq, _ = main.shape
    nblk = kc.shape[1]
    tq = TQ_CMP
    return pl.pallas_call(
        functools.partial(_cmp_kernel, tq=tq),
        grid=(bsz, seq // tq),
        in_specs=[pl.BlockSpec((1, tq, NSA_W), lambda b, i: (b, i, COL_NQ // NSA_W)),
                  pl.BlockSpec((1, nblk, LANES), lambda b, i: (b, 0, 0)),
                  pl.BlockSpec((1, nblk, LANES), lambda b, i: (b, 0, 0)),
                  pl.BlockSpec((2, LANES, nblk), lambda b, i: (0, 0, 0)),
                  pl.BlockSpec((1, tq, LANES), lambda b, i: (b, i, 0))],
        out_specs=[pl.BlockSpec((1, tq, NSA_W), lambda b, i: (b, i, 0)),
                   pl.BlockSpec((1, tq, 2 * LANES), lambda b, i: (b, i, 0))],
        out_shape=[jax.ShapeDtypeStruct((bsz, seq, NSA_W), BF16),
                   jax.ShapeDtypeStruct((bsz, seq, 2 * LANES), BF16)],
        compiler_params=_params(("parallel", "parallel")),
    )(main, kc, vc, ovt, gates)


def _slc_kernel(rel_ref, q_ref, k_ref, v_ref, nsel_ref, gate_ref, o_ref, m_ref, acc_ref, bias_ref, *, tq, tk):
    b = pl.program_id(0)
    qi = pl.program_id(1)

    @pl.when((b == 0) & (qi == 0))
    def _():
        _fill_bias(bias_ref, rel_ref, [(i, DIFF_HEADS + i) for i in range(NSA_HEADS)], tq, tk)

    nsel = nsel_ref[0]
    qg = _group_queries(q_ref[0], NSA_GQ, extra=[nsel[:, 0:LANES], nsel[:, LANES:2 * LANES]])
    _init_state(m_ref, acc_ref)

    def tile(kj, width, d):
        start = pl.multiple_of(kj * tk, tk)
        v = v_ref[0, pl.ds(start, width), :]
        for g in range(NSA_KV):
            k = k_ref[0, pl.ds(start, width), g * LANES:(g + 1) * LANES]
            add = None if d is None else bias_ref[g * NSA_GQ:(g + 1) * NSA_GQ, d]
            _attend(qg[g], k, v, add, m_ref.at[g], acc_ref.at[g])

    _far_tiles(tile, qi, tk)

    @pl.when(qi >= 1)
    def _():
        tile(qi - 1, tk, 1)

    tile(qi, tk, 0)
    _store_gated(o_ref, gate_ref, [_normalized(acc_ref, g) for g in range(NSA_KV)], 1, tq)


def _slc_attention(rel_bias, main, nsel, gates):
    bsz, seq, _ = main.shape
    tq, tk = TQ, TK
    rows = NSA_GQ * tq
    return pl.pallas_call(
        functools.partial(_slc_kernel, tq=tq, tk=tk),
        grid=(bsz, seq // tq),
        in_specs=[pl.BlockSpec(memory_space=pltpu.SMEM),
                  pl.BlockSpec((1, tq, NSA_W), lambda b, i: (b, i, COL_NQ // NSA_W)),
                  pl.BlockSpec((1, seq, 2 * LANES), lambda b, i: (b, 0, COL_KSL // (2 * LANES))),
                  pl.BlockSpec((1, seq, LANES), lambda b, i: (b, 0, COL_VSL // LANES)),
                  pl.BlockSpec((1, tq, 2 * LANES), lambda b, i: (b, i, 0)),
                  pl.BlockSpec((1, tq, LANES), lambda b, i: (b, i, 0))],
        out_specs=pl.BlockSpec((1, tq, NSA_W), lambda b, i: (b, i, 0)),
        out_shape=jax.ShapeDtypeStruct((bsz, seq, NSA_W), BF16),
        scratch_shapes=[pltpu.VMEM((NSA_KV, rows, LANES), F32),
                        pltpu.VMEM((NSA_KV, rows, 2 * LANES), F32),
                        pltpu.VMEM((NSA_HEADS, 2, tq, tk), F32)],
        compiler_params=_params(("arbitrary", "arbitrary")),
    )(rel_bias, main, main, main, nsel, gates)


def _win_kernel(rel_ref, q_ref, k_ref, v_ref, gate_ref, o_ref, m_ref, acc_ref, bias_ref, band_ref, *, tq, tk):
    b = pl.program_id(0)
    qi = pl.program_id(1)

    @pl.when((b == 0) & (qi == 0))
    def _():
        _fill_bias(bias_ref, rel_ref, [(i, DIFF_HEADS + i) for i in range(NSA_HEADS)], tq, tk)
        r = lax.broadcasted_iota(jnp.int32, (tq, tk), 0)
        c = lax.broadcasted_iota(jnp.int32, (tq, tk), 1)
        band_ref[...] = jnp.where(r < c, 0.0, NEG)

    qg = _group_queries(q_ref[0], NSA_GQ)
    _init_state(m_ref, acc_ref)

    def tile(kj, d):
        start = pl.multiple_of(kj * tk, tk)
        k = k_ref[0, pl.ds(start, tk), :]
        v = v_ref[0, pl.ds(start, tk), :]
        for g in range(NSA_KV):
            add = band_ref[...] if d == 2 else bias_ref[g * NSA_GQ:(g + 1) * NSA_GQ, d]
            _attend(qg[g], k, v, add, m_ref.at[g], acc_ref.at[g])

    for d in (2, 1):
        @pl.when(qi >= d)
        def _():
            tile(qi - d, d)

    tile(qi, 0)
    _store_gated(o_ref, gate_ref, [_normalized(acc_ref, g) for g in range(NSA_KV)], 2, tq)


def _win_attention(rel_bias, main, gates):
    bsz, seq, _ = main.shape
    tq, tk = TQ, TK
    rows = NSA_GQ * tq
    return pl.pallas_call(
        functools.partial(_win_kernel, tq=tq, tk=tk),
        grid=(bsz, seq // tq),
        in_specs=[pl.BlockSpec(memory_space=pltpu.SMEM),
                  pl.BlockSpec((1, tq, NSA_W), lambda b, i: (b, i, COL_NQ // NSA_W)),
                  pl.BlockSpec((1, seq, LANES), lambda b, i: (b, 0, COL_KW // LANES)),
                  pl.BlockSpec((1, seq, LANES), lambda b, i: (b, 0, COL_VW // LANES)),
                  pl.BlockSpec((1, tq, LANES), lambda b, i: (b, i, 0))],
        out_specs=pl.BlockSpec((1, tq, NSA_W), lambda b, i: (b, i, 0)),
        out_shape=jax.ShapeDtypeStruct((bsz, seq, NSA_W), BF16),
        scratch_shapes=[pltpu.VMEM((NSA_KV, rows, LANES), F32),
                        pltpu.VMEM((NSA_KV, rows, 2 * LANES), F32),
                        pltpu.VMEM((NSA_HEADS, 2, tq, tk), F32),
                        pltpu.VMEM((tq, tk), F32)],
        compiler_params=_params(("arbitrary", "arbitrary")),
    )(rel_bias, main, main, main, gates)


def _diff_kernel(rel_ref, lam_ref, q_ref, k_ref, v_ref, g_ref, o_ref, m_ref, acc_ref, bias_ref, *, tq, tk):
    b = pl.program_id(0)
    qi = pl.program_id(1)

    @pl.when((b == 0) & (qi == 0))
    def _():
        _fill_bias(bias_ref, rel_ref, [(h, h) for h in range(DIFF_HEADS)], tq, tk)

    q = q_ref[0]
    qq = []
    for h in range(DIFF_HEADS):
        q1, q2 = _group_queries(q[:, h * LANES:(h + 1) * LANES], 1)
        qq.append(jnp.concatenate([q1, q2], axis=0))
    _init_state(m_ref, acc_ref)

    def tile(kj, width, d):
        start = pl.multiple_of(kj * tk, tk)
        for h in range(DIFF_HEADS):
            k = k_ref[0, pl.ds(start, width), h * LANES:(h + 1) * LANES]
            v = v_ref[0, pl.ds(start, width), h * LANES:(h + 1) * LANES]
            add = None if d is None else bias_ref[h, d]
            _attend(qq[h], k, v, add, m_ref.at[h], acc_ref.at[h])

    _far_tiles(tile, qi, tk)

    @pl.when(qi >= 1)
    def _():
        tile(qi - 1, tk, 1)

    tile(qi, tk, 0)

    lm = lam_ref[...]
    lam = (jnp.exp(jnp.sum(lm[0:1] * lm[1:2], axis=-1, keepdims=True))
           - jnp.exp(jnp.sum(lm[2:3] * lm[3:4], axis=-1, keepdims=True)) + LAMBDA_INIT)
    for h in range(DIFF_HEADS):
        o = _normalized(acc_ref, h)
        a = o[0:tq] - lam * o[tq:2 * tq]
        o_ref[0, :, h * LANES:(h + 1) * LANES] = (_rms(a, g_ref[...]) * (1.0 - LAMBDA_INIT)).astype(BF16)


def _diff_attention(rel_bias, lam_rows, main, subln):
    bsz, seq, _ = main.shape
    tq, tk = TQ, TK
    return pl.pallas_call(
        functools.partial(_diff_kernel, tq=tq, tk=tk),
        grid=(bsz, seq // tq),
        in_specs=[pl.BlockSpec(memory_space=pltpu.SMEM),
                  pl.BlockSpec((8, LANES), lambda b, i: (0, 0)),
                  pl.BlockSpec((1, tq, DIFF_W), lambda b, i: (b, i, COL_DQ // DIFF_W)),
                  pl.BlockSpec((1, seq, DIFF_W), lambda b, i: (b, 0, COL_DK // DIFF_W)),
                  pl.BlockSpec((1, seq, DIFF_W), lambda b, i: (b, 0, COL_DV // DIFF_W)),
                  pl.BlockSpec((1, LANES), lambda b, i: (0, 0))],
        out_specs=pl.BlockSpec((1, tq, DIFF_W), lambda b, i: (b, i, 0)),
        out_shape=jax.ShapeDtypeStruct((bsz, seq, DIFF_W), BF16),
        scratch_shapes=[pltpu.VMEM((DIFF_HEADS, 2 * tq, LANES), F32),
                        pltpu.VMEM((DIFF_HEADS, 2 * tq, 2 * LANES), F32),
                        pltpu.VMEM((DIFF_HEADS, 2, tq, tk), F32)],
        compiler_params=_params(("arbitrary", "arbitrary")),
    )(rel_bias, lam_rows, main, main, main, subln)


FF_CHUNK = 256


def _tail_kernel(x_ref, od_ref, oc_ref, os_ref, ow_ref, gn_ref, wo_ref, g2_ref,
                 wg_ref, wu_ref, wd_ref, gf_ref, o_ref, acc_ref):
    o_nsa = oc_ref[...].astype(F32) + os_ref[...].astype(F32) + ow_ref[...].astype(F32)
    o_nsa = _rms(o_nsa, gn_ref[...]).astype(BF16)
    attn = _dot(od_ref[...], wo_ref[0:DIFF_W, :]) + _dot(o_nsa, wo_ref[DIFF_W:DIFF_W + NSA_W, :])
    x1 = x_ref[...] + attn
    acc_ref[...] = x1
    h = _rms(x1, g2_ref[...]).astype(BF16)
    for c in range(0, D_FF, FF_CHUNK):
        gate = _dot(h, wg_ref[:, c:c + FF_CHUNK])
        up = _dot(h, wu_ref[:, c:c + FF_CHUNK])
        act = (gate * (1.0 / (1.0 + jnp.exp(-gate))) * up).astype(BF16)
        acc_ref[...] += _dot(act, wd_ref[c:c + FF_CHUNK, :])
    o_ref[...] = _rms(acc_ref[...], gf_ref[...])


def _tail(x2, od, oc, os_, ow, gn, w_out, g2, wg, wu, wd, gf):
    n = x2.shape[0]
    row = lambda i: (i, 0)
    fixed = lambda i: (0, 0)
    once = pl.Buffered(1)
    return pl.pallas_call(
        _tail_kernel,
        grid=(n // TM,),
        in_specs=[pl.BlockSpec((TM, D_MODEL), row),
                  pl.BlockSpec((TM, DIFF_W), row),
                  pl.BlockSpec((TM, NSA_W), row),
                  pl.BlockSpec((TM, NSA_W), row),
                  pl.BlockSpec((TM, NSA_W), row),
                  pl.BlockSpec((1, NSA_W), fixed),
                  pl.BlockSpec((DIFF_W + NSA_W, D_MODEL), fixed, pipeline_mode=once),
                  pl.BlockSpec((1, D_MODEL), fixed),
                  pl.BlockSpec((D_MODEL, D_FF), fixed, pipeline_mode=once),
                  pl.BlockSpec((D_MODEL, D_FF), fixed, pipeline_mode=once),
                  pl.BlockSpec((D_FF, D_MODEL), fixed, pipeline_mode=once),
                  pl.BlockSpec((1, D_MODEL), fixed)],
        out_specs=pl.BlockSpec((TM, D_MODEL), row),
        out_shape=jax.ShapeDtypeStruct((n, D_MODEL), F32),
        scratch_shapes=[pltpu.VMEM((TM, D_MODEL), F32)],
        compiler_params=_params(("parallel",)),
    )(x2, od, oc, os_, ow, gn, w_out, g2, wg, wu, wd, gf)


def _nsa_perm():
    p = np.arange(NSA_W)
    j, g, d = p // LANES, (p % LANES) // NSA_HD, p % NSA_HD
    return (g * NSA_GQ + j) * NSA_HD + d


def _overlap_matrices(nblk):
    n_cmp = nblk - 1
    nslc = nblk * CMP_STRIDE // SLC_BLOCK
    assert nslc == N_SLC
    cmp_start = np.arange(n_cmp) * CMP_STRIDE
    cmp_end = cmp_start + CMP_LEN - 1
    slc_start = np.arange(nslc) * SLC_BLOCK
    overlap = ((cmp_start[:, None] < slc_start[None, :] + SLC_BLOCK) & (cmp_end[:, None] >= slc_start[None, :]))
    ovt = np.zeros((NSA_KV, LANES, nblk), np.float32)
    for g, base in enumerate((HALF, 0)):
        ovt[g, base:base + nslc, :n_cmp] = overlap.T
    return ovt


def _compress_weights(pos, w1, w2):
    half = CMP_LEN // 2
    eye = jnp.eye(NSA_KV, dtype=F32)
    w1r = w1.reshape(2, half, NSA_HD, CMP_HIDDEN)
    w1big = jnp.einsum('slcj,pg->slpcgj', w1r, eye).reshape(2, half * NSA_KV * NSA_HD, NSA_KV * CMP_HIDDEN)
    w2big = jnp.einsum('jd,pg->pjgd', w2, eye).reshape(NSA_KV * CMP_HIDDEN, NSA_KV * NSA_HD)
    posr = jnp.broadcast_to(pos.reshape(2, half, 1, NSA_HD), (2, half, NSA_KV, NSA_HD)).reshape(2, -1)
    return posr, w1big.astype(BF16), w2big.astype(BF16)


def kernel(x, norm1, w_in, lambda_q1, lambda_k1, lambda_q2, lambda_k2, diff_subln, cmp_pos_k, cmp_w1_k, cmp_w2_k,
           cmp_pos_v, cmp_w1_v, cmp_w2_v, nsa_norm, w_out, norm2, w_gate, w_up, w_down, rel_bias, final_norm):
    bsz, seq, dm = x.shape
    assert dm == D_MODEL and seq % TQ_CMP == 0 and seq % TM == 0 and seq == N_SLC * SLC_BLOCK
    assert norm1.shape[0] == 1, "single layer"
    perm = _nsa_perm()
    x2 = x.reshape(bsz * seq, dm)

    w = w_in[0]
    zhalf = jnp.zeros((dm, HALF), F32)
    w_p = jnp.concatenate([w[:, :512] * QSCALE, w[:, 512:1536], w[:, 1536 + perm] * QSCALE,
                           w[:, 2304:2368], zhalf, zhalf, w[:, 2368:2432],
                           w[:, 2432:2816],
                           w[:, 2048:2304],
                           w[:, 2816:N_IN], jnp.zeros((dm, LANES - (N_IN - 2816)), F32)], axis=1).astype(BF16)
    assert w_p.shape[1] == N_PROJ
    wo = w_out[0]
    wo_p = jnp.concatenate([wo[:DIFF_W], wo[DIFF_W + perm]], axis=0).astype(BF16)
    gn_p = nsa_norm[0][perm].reshape(1, NSA_W)
    posk, w1k, w2k = _compress_weights(cmp_pos_k[0], cmp_w1_k[0], cmp_w2_k[0])
    posv, w1v, w2v = _compress_weights(cmp_pos_v[0], cmp_w1_v[0], cmp_w2_v[0])
    pos = jnp.concatenate([posk, posv, jnp.zeros((4, posk.shape[1]), F32)], axis=0)
    lam_rows = jnp.zeros((8, LANES), F32).at[0:4, 0:DIFF_HD].set(
        jnp.stack([lambda_q1[0], lambda_k1[0], lambda_q2[0], lambda_k2[0]]).astype(F32))

    main, kc_tok, vc_tok, gates = _inproj(x2, norm1[0].reshape(1, dm), w_p, seq)
    main = main.reshape(bsz, seq, N_MAIN)

    nrow = seq // CMP_STRIDE
    kc, vc = _compress(kc_tok.reshape(bsz, nrow, CMP_STRIDE * LANES), vc_tok.reshape(bsz, nrow, CMP_STRIDE * LANES),
                       pos, w1k, w2k, w1v, w2v)
    ovt = jnp.asarray(_overlap_matrices(nrow), BF16)
    gates = gates.reshape(bsz, seq, LANES)
    o_cmp, nsel = _cmp_attention(main, kc, vc, ovt, gates)
    o_slc = _slc_attention(rel_bias, main, nsel, gates)
    o_win = _win_attention(rel_bias, main, gates)
    o_diff = _diff_attention(rel_bias, lam_rows, main, diff_subln[0].reshape(1, DIFF_VD))

    n = bsz * seq
    out = _tail(x2, o_diff.reshape(n, DIFF_W), o_cmp.reshape(n, NSA_W), o_slc.reshape(n, NSA_W),
                o_win.reshape(n, NSA_W), gn_p, wo_p, norm2[0].reshape(1, dm),
                w_gate[0].astype(BF16), w_up[0].astype(BF16), w_down[0].astype(BF16), final_norm.reshape(1, dm))
    return out.reshape(bsz, seq, dm)
```

```python
import functools
import math

import numpy as np
import jax
import jax.numpy as jnp
from jax import lax
from jax.experimental import pallas as pl
from jax.experimental.pallas import tpu as pltpu

F32 = jnp.float32
BF16 = jnp.bfloat16

D_MODEL = 1024
DIFF_HEADS = 4
DIFF_HD = 64
DIFF_VD = 128
DIFF_W = 512
NSA_HEADS = 8
NSA_KV = 2
NSA_GQ = 4
NSA_HD = 64
NSA_W = 512
CMP_LEN = 32
CMP_STRIDE = 16
CMP_HIDDEN = 256
SLC_BLOCK = 64
SLC_TOPK = 16
N_SLC = 32
WINDOW = 512
N_BUCKETS = 32
MAX_DISTANCE = 128
D_FF = 2816
N_IN = 2840
NEG = -1e30
EPS = 1e-6
LAMBDA_INIT = 0.8 - 0.6 * math.exp(-0.3 * 0)
LOG2E = math.log2(math.e)
QSCALE = NSA_HD ** -0.5 * LOG2E

LANES = 128
HALF = LANES // 2
MXU_N = 256
COL_DQ, COL_DK, COL_DV, COL_NQ, COL_KSL, COL_VSL, COL_KW, COL_VW, N_MAIN = (
    0, 512, 1024, 1536, 2048, 2304, 2432, 2560, 2688)
N_PROJ = N_MAIN + 3 * LANES
VMEM_LIMIT = 48 * 1024 * 1024

TQ = 256
TK = 256
TM = 512
TQ_CMP = 256


def _t5_thresholds():
    d = np.arange(0, 4 * MAX_DISTANCE)
    max_exact = N_BUCKETS // 2
    val = (np.log(np.maximum(d, 1).astype(np.float32) / np.float32(max_exact))
           / np.float32(math.log(MAX_DISTANCE / max_exact)) * np.float32(N_BUCKETS - max_exact))
    large = np.minimum(max_exact + val.astype(np.int32), N_BUCKETS - 1)
    bucket = np.where(d < max_exact, d, large)
    assert np.all(np.diff(bucket) >= 0) and bucket[-1] == N_BUCKETS - 1
    return [int(np.argmax(bucket >= k)) for k in range(N_BUCKETS)]


T5_THRESH = _t5_thresholds()
assert T5_THRESH[-1] <= TK + 1
assert WINDOW == 2 * TK and TQ == TK


def _nt_dot(a, b):
    return lax.dot_general(a, b, (((1,), (1,)), ((), ())), preferred_element_type=F32)


def _dot(a, b):
    return jnp.dot(a, b, preferred_element_type=F32)


def _rms(x, g):
    ms = jnp.mean(x * x, axis=-1, keepdims=True)
    return x * lax.rsqrt(ms + EPS) * g


def _params(sem):
    return pltpu.CompilerParams(dimension_semantics=sem, vmem_limit_bytes=VMEM_LIMIT)


def _inproj_kernel(x_ref, g_ref, w_ref, main_ref, kc_ref, vc_ref, gate_ref, *, seq):
    tm = x_ref.shape[0]
    h = _rms(x_ref[...], g_ref[...]).astype(BF16)
    for c in range(0, N_PROJ, MXU_N):
        y = _dot(h, w_ref[:, c:c + MXU_N])
        if c == COL_KSL:
            t0 = lax.rem(pl.program_id(0) * tm, seq)
            blk = (t0 + lax.broadcasted_iota(jnp.int32, (tm, 2 * LANES), 0)) >> 6
            lane = lax.broadcasted_iota(jnp.int32, (tm, 2 * LANES), 1)
            slot = jnp.where(lane < LANES, lane - HALF, lane - LANES)
            y = y + jnp.where((slot == blk) & (slot >= 0) & (slot < N_SLC), NEG, 0.0)
        if c + MXU_N <= N_MAIN:
            main_ref[:, c:c + MXU_N] = y.astype(BF16)
        elif c == COL_VW:
            main_ref[:, c:c + LANES] = y[:, 0:LANES].astype(BF16)
            kc_ref[...] = y[:, LANES:MXU_N].astype(BF16)
        else:
            vc_ref[...] = y[:, 0:LANES].astype(BF16)
            gate_ref[...] = 1.0 / (1.0 + jnp.exp(-y[:, LANES:MXU_N]))


def _inproj(x2, g1, w_p, seq):
    n = x2.shape[0]
    return pl.pallas_call(
        functools.partial(_inproj_kernel, seq=seq),
        grid=(n // TM,),
        in_specs=[pl.BlockSpec((TM, D_MODEL), lambda i: (i, 0)),
                  pl.BlockSpec((1, D_MODEL), lambda i: (0, 0)),
                  pl.BlockSpec((D_MODEL, N_PROJ), lambda i: (0, 0))],
        out_specs=[pl.BlockSpec((TM, N_MAIN), lambda i: (i, 0)),
                   pl.BlockSpec((TM, LANES), lambda i: (i, 0)),
                   pl.BlockSpec((TM, LANES), lambda i: (i, 0)),
                   pl.BlockSpec((TM, LANES), lambda i: (i, 0))],
        out_shape=[jax.ShapeDtypeStruct((n, N_MAIN), BF16),
                   jax.ShapeDtypeStruct((n, LANES), BF16),
                   jax.ShapeDtypeStruct((n, LANES), BF16),
                   jax.ShapeDtypeStruct((n, LANES), F32)],
        compiler_params=_params(("parallel",)),
    )(x2, g1, w_p)


def _gelu_tanh(x):
    return x * (0.5 * (1.0 + jnp.tanh(math.sqrt(2.0 / math.pi) * (x + 0.044715 * (x * x * x)))))


def _compress_kernel(rk_ref, rv_ref, pos_ref, w1k_ref, w2k_ref, w1v_ref, w2v_ref, kc_ref, vc_ref):
    def comp(r_ref, p_top, p_bot, w1_ref, w2_ref):
        r = r_ref[0].astype(F32)
        top = _dot((r + p_top).astype(BF16), w1_ref[0])
        bot = _dot((r + p_bot).astype(BF16), w1_ref[1])
        nrow = top.shape[0]
        hid = top + pltpu.roll(bot, nrow - 1, 0)
        return _dot(_gelu_tanh(hid).astype(BF16), w2_ref[...]).astype(BF16)

    kc_ref[0] = comp(rk_ref, pos_ref[0:1, :], pos_ref[1:2, :], w1k_ref, w2k_ref)
    vc_ref[0] = comp(rv_ref, pos_ref[2:3, :], pos_ref[3:4, :], w1v_ref, w2v_ref)


def _compress(rk, rv, pos, w1k, w2k, w1v, w2v):
    bsz, nrow, width = rk.shape
    hid2 = 2 * CMP_HIDDEN
    return pl.pallas_call(
        _compress_kernel,
        grid=(bsz,),
        in_specs=[pl.BlockSpec((1, nrow, width), lambda b: (b, 0, 0)),
                  pl.BlockSpec((1, nrow, width), lambda b: (b, 0, 0)),
                  pl.BlockSpec((8, width), lambda b: (0, 0)),
                  pl.BlockSpec((2, width, hid2), lambda b: (0, 0, 0)),
                  pl.BlockSpec((hid2, LANES), lambda b: (0, 0)),
                  pl.BlockSpec((2, width, hid2), lambda b: (0, 0, 0)),
                  pl.BlockSpec((hid2, LANES), lambda b: (0, 0))],
        out_specs=[pl.BlockSpec((1, nrow, LANES), lambda b: (b, 0, 0)),
                   pl.BlockSpec((1, nrow, LANES), lambda b: (b, 0, 0))],
        out_shape=[jax.ShapeDtypeStruct((bsz, nrow, LANES), BF16),
                   jax.ShapeDtypeStruct((bsz, nrow, LANES), BF16)],
        compiler_params=_params(("parallel",)),
    )(rk, rv, pos, w1k, w2k, w1v, w2v)


def _half_mask(g):
    lane = lax.broadcasted_iota(jnp.int32, (1, LANES), 1)
    return jnp.where((lane >= g * HALF) & (lane < (g + 1) * HALF), 1.0, 0.0).astype(BF16)


def _group_queries(q, nslab):
    return [jnp.concatenate([q[:, j * LANES:(j + 1) * LANES] * _half_mask(g) for j in range(nslab)], axis=0)
            for g in range(2)]


def _nsa_chains(q_ref, extra_ref, tq):
    chains = []
    for qt in range(2):
        rows = slice(qt * tq, (qt + 1) * tq)
        for g in range(NSA_KV):
            keep = _half_mask(g)
            for j in range(NSA_GQ):
                q = q_ref[0, rows, j * LANES:(j + 1) * LANES] * keep
                if extra_ref is not None:
                    q = q + extra_ref[0, rows, g * LANES:(g + 1) * LANES]
                chains.append((q, qt, g, j))
    return chains


def _fill_bias(bias_ref, rel_ref, slots, tq, tk):
    r = lax.broadcasted_iota(jnp.int32, (tq, tk), 0)
    c = lax.broadcasted_iota(jnp.int32, (tq, tk), 1)
    for i, col in slots:
        far = rel_ref[N_BUCKETS - 1, col]
        for d in range(2):
            dist = r - c + d * tq
            val = jnp.zeros((tq, tk), F32) + rel_ref[0, col]
            for k in range(1, N_BUCKETS):
                val = jnp.where(dist >= T5_THRESH[k], rel_ref[k, col], val)
            val = (val - far) * LOG2E
            if d == 0:
                val = jnp.where(dist >= 0, val, NEG)
            bias_ref[i, d] = val


def _with_near_bias(s, near):
    add = near[0] if len(near) == 1 else jnp.concatenate(near, axis=1)
    width, nw = s.shape[1], add.shape[1]
    if nw == width:
        return s + add
    return jnp.concatenate([s[:, 0:width - nw], s[:, width - nw:width] + add], axis=1)


def _softmax_step(s, v, m_ref, acc_ref):
    tk = s.shape[1]
    m_prev = m_ref[...]
    m_new = jnp.maximum(m_prev, jnp.max(s, axis=-1, keepdims=True))
    alpha = jnp.exp2(m_prev - m_new)
    p = jnp.exp2(s - jnp.concatenate([m_new] * (tk // LANES), axis=1))
    vaug = jnp.concatenate([v, jnp.ones((tk, LANES), BF16)], axis=1)
    acc_ref[...] = jnp.concatenate([alpha, alpha], axis=1) * acc_ref[...] + _dot(p.astype(BF16), vaug)
    m_ref[...] = m_new


def _softmax_once(s, v):
    rows, tk = s.shape
    m = jnp.broadcast_to(jnp.max(s, axis=-1, keepdims=True), (rows, LANES))
    p = jnp.exp2(s - jnp.concatenate([m] * (tk // LANES), axis=1))
    vaug = jnp.concatenate([v, jnp.ones((tk, LANES), BF16)], axis=1)
    pv = _dot(p.astype(BF16), vaug)
    return pv[:, 0:LANES] * (1.0 / pv[:, LANES:2 * LANES])


def _causal_flash(i, tq, tk, chains, kv_of, bias_of, m_ref, acc_ref):
    m_ref[...] = jnp.full(m_ref.shape, NEG, F32)
    acc_ref[...] = jnp.zeros(acc_ref.shape, F32)

    def far(j, carry):
        start = pl.multiple_of(j * 2 * tk, 2 * tk)
        kv = {}
        for c, (q, _, kv_id, _) in enumerate(chains):
            if kv_id not in kv:
                kv[kv_id] = kv_of(kv_id, start, 2 * tk)
            k, v = kv[kv_id]
            _softmax_step(_nt_dot(q, k), v, m_ref.at[c], acc_ref.at[c])
        return carry

    lax.fori_loop(0, i - 1, far, 0)

    def tail(first):
        kv = {}
        for c, (q, qt, kv_id, bias_id) in enumerate(chains):
            ntile = (1 if first else 3) + qt
            if (kv_id, ntile) not in kv:
                start = 0 if first else pl.multiple_of((i - 1) * 2 * tk, 2 * tk)
                kv[(kv_id, ntile)] = kv_of(kv_id, start, ntile * tk)
            k, v = kv[(kv_id, ntile)]
            near = [bias_of(bias_id, d) for d in range(min(ntile, 2) - 1, -1, -1)]
            _softmax_step(_with_near_bias(_nt_dot(q, k), near), v, m_ref.at[c], acc_ref.at[c])

    @pl.when(i == 0)
    def _():
        tail(True)

    @pl.when(i >= 1)
    def _():
        tail(False)


def _normalized(acc_ref, c):
    return acc_ref[c, :, 0:LANES] * (1.0 / acc_ref[c, :, LANES:2 * LANES])


def _store_gated(o_ref, gate_ref, o_of, branch, tq, qt):
    lane = lax.broadcasted_iota(jnp.int32, (1, LANES), 1)
    rows = slice(qt * tq, (qt + 1) * tq)
    gates = gate_ref[0, rows, :]
    for j in range(NSA_GQ):
        c0 = (0 * NSA_GQ + j) * 3 + branch
        c1 = (1 * NSA_GQ + j) * 3 + branch
        gexp = jnp.where(lane < HALF, gates[:, c0:c0 + 1], gates[:, c1:c1 + 1])
        slab = jnp.where(lane < HALF, o_of(0, j), o_of(1, j))
        o_ref[0, rows, j * LANES:(j + 1) * LANES] = (gexp * slab).astype(BF16)


def _nsa_chain_index(qt, g, j):
    return (qt * NSA_KV + g) * NSA_GQ + j


N_CHAINS = 2 * NSA_HEADS


def _cmp_kernel(q_ref, kc_ref, vc_ref, ovt_ref, gate_ref, o_ref, nsel_ref, *, tq):
    qi = pl.program_id(1)
    nblk = kc_ref.shape[1]
    lane = lax.broadcasted_iota(jnp.int32, (1, LANES), 1)
    n = lax.broadcasted_iota(jnp.int32, (1, nblk), 1)
    t4 = qi * tq + (lax.broadcasted_iota(jnp.int32, (NSA_GQ * tq, 1), 0) & (tq - 1))
    vis4 = (n * CMP_STRIDE + (CMP_LEN - 1)) <= t4
    has4 = jnp.where(t4 >= CMP_LEN - 1, 1.0, 0.0)
    qg = _group_queries(q_ref[0], NSA_GQ)
    kc = kc_ref[0]
    vc = vc_ref[0]
    outs, imp_t = [], jnp.zeros((LANES, tq), F32)
    for g in range(NSA_KV):
        s = jnp.where(vis4, _nt_dot(qg[g], kc), NEG)
        e = jnp.exp2(s - jnp.max(s, axis=-1, keepdims=True))
        p = e * (has4 / jnp.sum(e, axis=-1, keepdims=True))
        outs.append(_dot(p.astype(BF16), vc))
        ps = p[0:tq] + p[tq:2 * tq] + p[2 * tq:3 * tq] + p[3 * tq:4 * tq]
        hi = ps.astype(BF16)
        r1 = ps - hi.astype(F32)
        mid = r1.astype(BF16)
        lo = (r1 - mid.astype(F32)).astype(BF16)
        imp_t = imp_t + (_nt_dot(ovt_ref[g], hi) + _nt_dot(ovt_ref[g], mid) + _nt_dot(ovt_ref[g], lo))
    _store_gated(o_ref, gate_ref, lambda g, j: outs[g][j * tq:(j + 1) * tq], 0, tq, 0)

    tl = qi * tq + lax.broadcasted_iota(jnp.int32, (N_SLC, tq), 1)
    jrow = lax.broadcasted_iota(jnp.int32, (N_SLC, tq), 0)
    cur = tl >> 6
    valid = (jrow * SLC_BLOCK) <= tl
    forced = (jrow == 0) | (jrow == cur) | (jrow == cur - 1)
    sub = 8
    zero = jnp.zeros((N_SLC, tq), F32)
    dropped = []
    for base in (0, HALF):
        x = jnp.where(forced, jnp.inf, jnp.where(valid, imp_t[base:base + N_SLC], -jnp.inf))
        others = [jnp.broadcast_to(x[jj:jj + 1], (sub, tq)) for jj in range(N_SLC)]
        chunks = []
        for r0 in range(0, N_SLC, sub):
            xr = x[r0:r0 + sub]
            jr = r0 + lax.broadcasted_iota(jnp.int32, (sub, tq), 0)
            cnt = jnp.zeros((sub, tq), F32)
            for jj in range(N_SLC):
                if jj < r0:
                    beats = others[jj] >= xr
                elif jj >= r0 + sub:
                    beats = others[jj] > xr
                else:
                    beats = (others[jj] > xr) | ((others[jj] == xr) & (jr > jj))
                cnt = cnt + jnp.where(beats, 1.0, 0.0)
            chunks.append(jnp.where(cnt < SLC_TOPK, 0.0, 1.0))
        dropped.append(jnp.concatenate(chunks, axis=0))
    drop = jnp.concatenate([dropped[0], zero, dropped[1], zero], axis=0).T
    nsel_ref[0, :, 0:LANES] = jnp.where(lane >= HALF, drop, 0.0).astype(BF16)
    nsel_ref[0, :, LANES:2 * LANES] = jnp.where(lane < HALF, drop, 0.0).astype(BF16)


def _cmp_attention(main, kc, vc, ovt, gates):
    bsz, seq, _ = main.shape
    nblk = kc.shape[1]
    tq = TQ_CMP
    return pl.pallas_call(
        functools.partial(_cmp_kernel, tq=tq),
        grid=(bsz, seq // tq),
        in_specs=[pl.BlockSpec((1, tq, NSA_W), lambda b, i: (b, i, COL_NQ // NSA_W)),
                  pl.BlockSpec((1, nblk, LANES), lambda b, i: (b, 0, 0)),
                  pl.BlockSpec((1, nblk, LANES), lambda b, i: (b, 0, 0)),
                  pl.BlockSpec((2, LANES, nblk), lambda b, i: (0, 0, 0)),
                  pl.BlockSpec((1, tq, LANES), lambda b, i: (b, i, 0))],
        out_specs=[pl.BlockSpec((1, tq, NSA_W), lambda b, i: (b, i, 0)),
                   pl.BlockSpec((1, tq, 2 * LANES), lambda b, i: (b, i, 0))],
        out_shape=[jax.ShapeDtypeStruct((bsz, seq, NSA_W), BF16),
                   jax.ShapeDtypeStruct((bsz, seq, 2 * LANES), BF16)],
        compiler_params=_params(("parallel", "parallel")),
    )(main, kc, vc, ovt, gates)


def _slc_kernel(rel_ref, q_ref, k_ref, v_ref, nsel_ref, gate_ref, o_ref, m_ref, acc_ref, bias_ref, *, tq, tk):
    b = pl.program_id(0)
    i = pl.program_id(1)

    @pl.when((b == 0) & (i == 0))
    def _():
        _fill_bias(bias_ref, rel_ref, [(h, DIFF_HEADS + h) for h in range(NSA_HEADS)], tq, tk)

    chains = [(q, qt, g, g * NSA_GQ + j) for q, qt, g, j in _nsa_chains(q_ref, nsel_ref, tq)]

    def kv_of(g, start, width):
        return (k_ref[0, pl.ds(start, width), g * LANES:(g + 1) * LANES], v_ref[0, pl.ds(start, width), :])

    _causal_flash(i, tq, tk, chains, kv_of, lambda h, d: bias_ref[h, d], m_ref, acc_ref)
    for qt in range(2):
        _store_gated(o_ref, gate_ref, lambda g, j: _normalized(acc_ref, _nsa_chain_index(qt, g, j)), 1, tq, qt)


def _slc_attention(rel_bias, main, nsel, gates):
    bsz, seq, _ = main.shape
    tq, tk = TQ, TK
    return pl.pallas_call(
        functools.partial(_slc_kernel, tq=tq, tk=tk),
        grid=(bsz, seq // (2 * tq)),
        in_specs=[pl.BlockSpec(memory_space=pltpu.SMEM),
                  pl.BlockSpec((1, 2 * tq, NSA_W), lambda b, i: (b, i, COL_NQ // NSA_W)),
                  pl.BlockSpec((1, seq, 2 * LANES), lambda b, i: (b, 0, COL_KSL // (2 * LANES))),
                  pl.BlockSpec((1, seq, LANES), lambda b, i: (b, 0, COL_VSL // LANES)),
                  pl.BlockSpec((1, 2 * tq, 2 * LANES), lambda b, i: (b, i, 0)),
                  pl.BlockSpec((1, 2 * tq, LANES), lambda b, i: (b, i, 0))],
        out_specs=pl.BlockSpec((1, 2 * tq, NSA_W), lambda b, i: (b, i, 0)),
        out_shape=jax.ShapeDtypeStruct((bsz, seq, NSA_W), BF16),
        scratch_shapes=[pltpu.VMEM((N_CHAINS, tq, LANES), F32),
                        pltpu.VMEM((N_CHAINS, tq, 2 * LANES), F32),
                        pltpu.VMEM((NSA_HEADS, 2, tq, tk), F32)],
        compiler_params=_params(("arbitrary", "arbitrary")),
    )(rel_bias, main, main, main, nsel, gates)


def _win_kernel(rel_ref, q_ref, k_ref, v_ref, gate_ref, o_ref, bias_ref, band_ref, out_ref, *, tq, tk):
    b = pl.program_id(0)
    i = pl.program_id(1)

    @pl.when((b == 0) & (i == 0))
    def _():
        _fill_bias(bias_ref, rel_ref, [(h, DIFF_HEADS + h) for h in range(NSA_HEADS)], tq, tk)
        r = lax.broadcasted_iota(jnp.int32, (tq, tk), 0)
        c = lax.broadcasted_iota(jnp.int32, (tq, tk), 1)
        band_ref[...] = jnp.where(r < c, 0.0, NEG)

    chains = _nsa_chains(q_ref, None, tq)

    def window(first):
        kv = {}
        for q, qt, g, j in chains:
            ntile = 1 + qt if first else 3
            if (qt, ntile) not in kv:
                start = 0 if first else pl.multiple_of((2 * i - 2 + qt) * tk, tk)
                kv[(qt, ntile)] = (k_ref[0, pl.ds(start, ntile * tk), :], v_ref[0, pl.ds(start, ntile * tk), :])
            k, v = kv[(qt, ntile)]
            near = [bias_ref[g * NSA_GQ + j, d] for d in range(min(ntile, 2) - 1, -1, -1)]
            if ntile == 3:
                near.insert(0, band_ref[...])
            out_ref[_nsa_chain_index(qt, g, j)] = _softmax_once(_with_near_bias(_nt_dot(q, k), near), v)

    @pl.when(i == 0)
    def _():
        window(True)

    @pl.when(i >= 1)
    def _():
        window(False)

    for qt in range(2):
        _store_gated(o_ref, gate_ref, lambda g, j: out_ref[_nsa_chain_index(qt, g, j)], 2, tq, qt)


def _win_attention(rel_bias, main, gates):
    bsz, seq, _ = main.shape
    tq, tk = TQ, TK
    return pl.pallas_call(
        functools.partial(_win_kernel, tq=tq, tk=tk),
        grid=(bsz, seq // (2 * tq)),
        in_specs=[pl.BlockSpec(memory_space=pltpu.SMEM),
                  pl.BlockSpec((1, 2 * tq, NSA_W), lambda b, i: (b, i, COL_NQ // NSA_W)),
                  pl.BlockSpec((1, seq, LANES), lambda b, i: (b, 0, COL_KW // LANES)),
                  pl.BlockSpec((1, seq, LANES), lambda b, i: (b, 0, COL_VW // LANES)),
                  pl.BlockSpec((1, 2 * tq, LANES), lambda b, i: (b, i, 0))],
        out_specs=pl.BlockSpec((1, 2 * tq, NSA_W), lambda b, i: (b, i, 0)),
        out_shape=jax.ShapeDtypeStruct((bsz, seq, NSA_W), BF16),
        scratch_shapes=[pltpu.VMEM((NSA_HEADS, 2, tq, tk), F32),
                        pltpu.VMEM((tq, tk), F32),
                        pltpu.VMEM((N_CHAINS, tq, LANES), F32)],
        compiler_params=_params(("arbitrary", "arbitrary")),
    )(rel_bias, main, main, main, gates)


def _diff_kernel(rel_ref, lam_ref, q_ref, k_ref, v_ref, g_ref, o_ref, m_ref, acc_ref, bias_ref, *, tq, tk):
    b = pl.program_id(0)
    i = pl.program_id(1)

    @pl.when((b == 0) & (i == 0))
    def _():
        _fill_bias(bias_ref, rel_ref, [(h, h) for h in range(DIFF_HEADS)], tq, tk)

    chains = []
    for qt in range(2):
        for h in range(DIFF_HEADS):
            slab = q_ref[0, qt * tq:(qt + 1) * tq, h * LANES:(h + 1) * LANES]
            for half in range(2):
                chains.append((slab * _half_mask(half), qt, h, h))

    def kv_of(h, start, width):
        cols = slice(h * LANES, (h + 1) * LANES)
        return (k_ref[0, pl.ds(start, width), cols], v_ref[0, pl.ds(start, width), cols])

    _causal_flash(i, tq, tk, chains, kv_of, lambda h, d: bias_ref[h, d], m_ref, acc_ref)

    lm = lam_ref[...]
    lam = (jnp.exp(jnp.sum(lm[0:1] * lm[1:2], axis=-1, keepdims=True))
           - jnp.exp(jnp.sum(lm[2:3] * lm[3:4], axis=-1, keepdims=True)) + LAMBDA_INIT)
    for qt in range(2):
        for h in range(DIFF_HEADS):
            c = (qt * DIFF_HEADS + h) * 2
            a = _normalized(acc_ref, c) - lam * _normalized(acc_ref, c + 1)
            o_ref[0, qt * tq:(qt + 1) * tq, h * LANES:(h + 1) * LANES] = (
                _rms(a, g_ref[...]) * (1.0 - LAMBDA_INIT)).astype(BF16)


def _diff_attention(rel_bias, lam_rows, main, subln):
    bsz, seq, _ = main.shape
    tq, tk = TQ, TK
    return pl.pallas_call(
        functools.partial(_diff_kernel, tq=tq, tk=tk),
        grid=(bsz, seq // (2 * tq)),
        in_specs=[pl.BlockSpec(memory_space=pltpu.SMEM),
                  pl.BlockSpec((8, LANES), lambda b, i: (0, 0)),
                  pl.BlockSpec((1, 2 * tq, DIFF_W), lambda b, i: (b, i, COL_DQ // DIFF_W)),
                  pl.BlockSpec((1, seq, DIFF_W), lambda b, i: (b, 0, COL_DK // DIFF_W)),
                  pl.BlockSpec((1, seq, DIFF_W), lambda b, i: (b, 0, COL_DV // DIFF_W)),
                  pl.BlockSpec((1, LANES), lambda b, i: (0, 0))],
        out_specs=pl.BlockSpec((1, 2 * tq, DIFF_W), lambda b, i: (b, i, 0)),
        out_shape=jax.ShapeDtypeStruct((bsz, seq, DIFF_W), BF16),
        scratch_shapes=[pltpu.VMEM((N_CHAINS, tq, LANES), F32),
                        pltpu.VMEM((N_CHAINS, tq, 2 * LANES), F32),
                        pltpu.VMEM((DIFF_HEADS, 2, tq, tk), F32)],
        compiler_params=_params(("arbitrary", "arbitrary")),
    )(rel_bias, lam_rows, main, main, main, subln)


FF_CHUNK = 256


def _tail_kernel(x_ref, od_ref, oc_ref, os_ref, ow_ref, gn_ref, wo_ref, g2_ref,
                 wg_ref, wu_ref, wd_ref, gf_ref, o_ref, acc_ref):
    o_nsa = oc_ref[...].astype(F32) + os_ref[...].astype(F32) + ow_ref[...].astype(F32)
    o_nsa = _rms(o_nsa, gn_ref[...]).astype(BF16)
    attn = _dot(od_ref[...], wo_ref[0:DIFF_W, :]) + _dot(o_nsa, wo_ref[DIFF_W:DIFF_W + NSA_W, :])
    x1 = x_ref[...] + attn
    acc_ref[...] = x1
    h = _rms(x1, g2_ref[...]).astype(BF16)
    for c in range(0, D_FF, FF_CHUNK):
        gate = _dot(h, wg_ref[:, c:c + FF_CHUNK])
        up = _dot(h, wu_ref[:, c:c + FF_CHUNK])
        act = (gate * (1.0 / (1.0 + jnp.exp(-gate))) * up).astype(BF16)
        acc_ref[...] += _dot(act, wd_ref[c:c + FF_CHUNK, :])
    o_ref[...] = _rms(acc_ref[...], gf_ref[...])


def _tail(x2, od, oc, os_, ow, gn, w_out, g2, wg, wu, wd, gf):
    n = x2.shape[0]
    row = lambda i: (i, 0)
    fixed = lambda i: (0, 0)
    once = pl.Buffered(1)
    return pl.pallas_call(
        _tail_kernel,
        grid=(n // TM,),
        in_specs=[pl.BlockSpec((TM, D_MODEL), row),
                  pl.BlockSpec((TM, DIFF_W), row),
                  pl.BlockSpec((TM, NSA_W), row),
                  pl.BlockSpec((TM, NSA_W), row),
                  pl.BlockSpec((TM, NSA_W), row),
                  pl.BlockSpec((1, NSA_W), fixed),
                  pl.BlockSpec((DIFF_W + NSA_W, D_MODEL), fixed, pipeline_mode=once),
                  pl.BlockSpec((1, D_MODEL), fixed),
                  pl.BlockSpec((D_MODEL, D_FF), fixed, pipeline_mode=once),
                  pl.BlockSpec((D_MODEL, D_FF), fixed, pipeline_mode=once),
                  pl.BlockSpec((D_FF, D_MODEL), fixed, pipeline_mode=once),
                  pl.BlockSpec((1, D_MODEL), fixed)],
        out_specs=pl.BlockSpec((TM, D_MODEL), row),
        out_shape=jax.ShapeDtypeStruct((n, D_MODEL), F32),
        scratch_shapes=[pltpu.VMEM((TM, D_MODEL), F32)],
        compiler_params=_params(("parallel",)),
    )(x2, od, oc, os_, ow, gn, w_out, g2, wg, wu, wd, gf)


def _nsa_perm():
    p = np.arange(NSA_W)
    j, g, d = p // LANES, (p % LANES) // NSA_HD, p % NSA_HD
    return (g * NSA_GQ + j) * NSA_HD + d


def _overlap_matrices(nblk):
    n_cmp = nblk - 1
    nslc = nblk * CMP_STRIDE // SLC_BLOCK
    assert nslc == N_SLC
    cmp_start = np.arange(n_cmp) * CMP_STRIDE
    cmp_end = cmp_start + CMP_LEN - 1
    slc_start = np.arange(nslc) * SLC_BLOCK
    overlap = ((cmp_start[:, None] < slc_start[None, :] + SLC_BLOCK) & (cmp_end[:, None] >= slc_start[None, :]))
    ovt = np.zeros((NSA_KV, LANES, nblk), np.float32)
    for g, base in enumerate((HALF, 0)):
        ovt[g, base:base + nslc, :n_cmp] = overlap.T
    return ovt


def _compress_weights(pos, w1, w2):
    half = CMP_LEN // 2
    eye = jnp.eye(NSA_KV, dtype=F32)
    w1r = w1.reshape(2, half, NSA_HD, CMP_HIDDEN)
    w1big = jnp.einsum('slcj,pg->slpcgj', w1r, eye).reshape(2, half * NSA_KV * NSA_HD, NSA_KV * CMP_HIDDEN)
    w2big = jnp.einsum('jd,pg->pjgd', w2, eye).reshape(NSA_KV * CMP_HIDDEN, NSA_KV * NSA_HD)
    posr = jnp.broadcast_to(pos.reshape(2, half, 1, NSA_HD), (2, half, NSA_KV, NSA_HD)).reshape(2, -1)
    return posr, w1big.astype(BF16), w2big.astype(BF16)


def kernel(x, norm1, w_in, lambda_q1, lambda_k1, lambda_q2, lambda_k2, diff_subln, cmp_pos_k, cmp_w1_k, cmp_w2_k,
           cmp_pos_v, cmp_w1_v, cmp_w2_v, nsa_norm, w_out, norm2, w_gate, w_up, w_down, rel_bias, final_norm):
    bsz, seq, dm = x.shape
    assert dm == D_MODEL and seq % (2 * TQ) == 0 and seq % TM == 0 and seq == N_SLC * SLC_BLOCK
    assert norm1.shape[0] == 1, "single layer"
    perm = _nsa_perm()
    x2 = x.reshape(bsz * seq, dm)

    w = w_in[0]
    zhalf = jnp.zeros((dm, HALF), F32)
    w_p = jnp.concatenate([w[:, :512] * QSCALE, w[:, 512:1536], w[:, 1536 + perm] * QSCALE,
                           w[:, 2304:2368], zhalf, zhalf, w[:, 2368:2432],
                           w[:, 2432:2816],
                           w[:, 2048:2304],
                           w[:, 2816:N_IN], jnp.zeros((dm, LANES - (N_IN - 2816)), F32)], axis=1).astype(BF16)
    assert w_p.shape[1] == N_PROJ
    wo = w_out[0]
    wo_p = jnp.concatenate([wo[:DIFF_W], wo[DIFF_W + perm]], axis=0).astype(BF16)
    gn_p = nsa_norm[0][perm].reshape(1, NSA_W)
    posk, w1k, w2k = _compress_weights(cmp_pos_k[0], cmp_w1_k[0], cmp_w2_k[0])
    posv, w1v, w2v = _compress_weights(cmp_pos_v[0], cmp_w1_v[0], cmp_w2_v[0])
    pos = jnp.concatenate([posk, posv, jnp.zeros((4, posk.shape[1]), F32)], axis=0)
    lam_rows = jnp.zeros((8, LANES), F32).at[0:4, 0:DIFF_HD].set(
        jnp.stack([lambda_q1[0], lambda_k1[0], lambda_q2[0], lambda_k2[0]]).astype(F32))

    main, kc_tok, vc_tok, gates = _inproj(x2, norm1[0].reshape(1, dm), w_p, seq)
    main = main.reshape(bsz, seq, N_MAIN)

    nrow = seq // CMP_STRIDE
    kc, vc = _compress(kc_tok.reshape(bsz, nrow, CMP_STRIDE * LANES), vc_tok.reshape(bsz, nrow, CMP_STRIDE * LANES),
                       pos, w1k, w2k, w1v, w2v)
    ovt = jnp.asarray(_overlap_matrices(nrow), BF16)
    gates = gates.reshape(bsz, seq, LANES)
    o_cmp, nsel = _cmp_attention(main, kc, vc, ovt, gates)
    o_slc = _slc_attention(rel_bias, main, nsel, gates)
    o_win = _win_attention(rel_bias, main, gates)
    o_diff = _diff_attention(rel_bias, lam_rows, main, diff_subln[0].reshape(1, DIFF_VD))

    n = bsz * seq
    out = _tail(x2, o_diff.reshape(n, DIFF_W), o_cmp.reshape(n, NSA_W), o_slc.reshape(n, NSA_W),
                o_win.reshape(n, NSA_W), gn_p, wo_p, norm2[0].reshape(1, dm),
                w_gate[0].astype(BF16), w_up[0].astype(BF16), w_down[0].astype(BF16), final_norm.reshape(1, dm))
    return out.reshape(bsz, seq, dm)
```

```python
import functools
import math

import numpy as np
import jax
import jax.numpy as jnp
from jax import lax
from jax.experimental import pallas as pl
from jax.experimental.pallas import tpu as pltpu

F32 = jnp.float32
BF16 = jnp.bfloat16

D_MODEL = 1024
DIFF_HEADS = 4
DIFF_HD = 64
DIFF_VD = 128
DIFF_W = 512
NSA_HEADS = 8
NSA_KV = 2
NSA_GQ = 4
NSA_HD = 64
NSA_W = 512
CMP_LEN = 32
CMP_STRIDE = 16
CMP_HIDDEN = 256
SLC_BLOCK = 64
SLC_TOPK = 16
N_SLC = 32
WINDOW = 512
N_BUCKETS = 32
MAX_DISTANCE = 128
D_FF = 2816
N_IN = 2840
NEG = -1e30
EPS = 1e-6
LAMBDA_INIT = 0.8 - 0.6 * math.exp(-0.3 * 0)
LOG2E = math.log2(math.e)
QSCALE = NSA_HD ** -0.5 * LOG2E

LANES = 128
HALF = LANES // 2
MXU_N = 256
COL_DQ, COL_DK, COL_DV, COL_NQ, COL_KSL, COL_VSL, COL_KW, COL_VW, N_MAIN = (
    0, 512, 1024, 1536, 2048, 2304, 2432, 2560, 2688)
N_PROJ = N_MAIN + 3 * LANES
VMEM_LIMIT = 48 * 1024 * 1024

TQ = 256
TK = 256
TM = 512
TQ_CMP = 256


def _t5_thresholds():
    d = np.arange(0, 4 * MAX_DISTANCE)
    max_exact = N_BUCKETS // 2
    val = (np.log(np.maximum(d, 1).astype(np.float32) / np.float32(max_exact))
           / np.float32(math.log(MAX_DISTANCE / max_exact)) * np.float32(N_BUCKETS - max_exact))
    large = np.minimum(max_exact + val.astype(np.int32), N_BUCKETS - 1)
    bucket = np.where(d < max_exact, d, large)
    assert np.all(np.diff(bucket) >= 0) and bucket[-1] == N_BUCKETS - 1
    return [int(np.argmax(bucket >= k)) for k in range(N_BUCKETS)]


T5_THRESH = _t5_thresholds()
assert T5_THRESH[-1] <= TK + 1
assert WINDOW == 2 * TK and TQ == TK


def _nt_dot(a, b):
    return lax.dot_general(a, b, (((1,), (1,)), ((), ())), preferred_element_type=F32)


def _dot(a, b):
    return jnp.dot(a, b, preferred_element_type=F32)


def _rms(x, g):
    ms = jnp.mean(x * x, axis=-1, keepdims=True)
    return x * lax.rsqrt(ms + EPS) * g


def _params(sem):
    return pltpu.CompilerParams(dimension_semantics=sem, vmem_limit_bytes=VMEM_LIMIT)


def _store_grouped(out_ref, tok_ref, y):
    tok_ref[...] = y
    nrow = out_ref.shape[0]
    for l in range(CMP_STRIDE):
        out_ref[:, l * LANES:(l + 1) * LANES] = tok_ref[pl.ds(l, nrow, stride=CMP_STRIDE), :].astype(BF16)


def _inproj_kernel(x_ref, g_ref, w_ref, main_ref, kc_ref, vc_ref, gate_ref, tok_ref, *, seq):
    tm = x_ref.shape[0]
    h = _rms(x_ref[...], g_ref[...]).astype(BF16)
    for c in range(0, N_PROJ, MXU_N):
        y = _dot(h, w_ref[:, c:c + MXU_N])
        if c == COL_KSL:
            t0 = lax.rem(pl.program_id(0) * tm, seq)
            blk = (t0 + lax.broadcasted_iota(jnp.int32, (tm, 2 * LANES), 0)) >> 6
            lane = lax.broadcasted_iota(jnp.int32, (tm, 2 * LANES), 1)
            slot = jnp.where(lane < LANES, lane - HALF, lane - LANES)
            y = y + jnp.where((slot == blk) & (slot >= 0) & (slot < N_SLC), NEG, 0.0)
        if c + MXU_N <= N_MAIN:
            main_ref[:, c:c + MXU_N] = y.astype(BF16)
        elif c == COL_VW:
            main_ref[:, c:c + LANES] = y[:, 0:LANES].astype(BF16)
            _store_grouped(kc_ref, tok_ref, y[:, LANES:MXU_N])
        else:
            _store_grouped(vc_ref, tok_ref, y[:, 0:LANES])
            gate_ref[...] = 1.0 / (1.0 + jnp.exp(-y[:, LANES:MXU_N]))


def _inproj(x2, g1, w_p, seq):
    n = x2.shape[0]
    return pl.pallas_call(
        functools.partial(_inproj_kernel, seq=seq),
        grid=(n // TM,),
        in_specs=[pl.BlockSpec((TM, D_MODEL), lambda i: (i, 0)),
                  pl.BlockSpec((1, D_MODEL), lambda i: (0, 0)),
                  pl.BlockSpec((D_MODEL, N_PROJ), lambda i: (0, 0))],
        out_specs=[pl.BlockSpec((TM, N_MAIN), lambda i: (i, 0)),
                   pl.BlockSpec((TM // CMP_STRIDE, CMP_STRIDE * LANES), lambda i: (i, 0)),
                   pl.BlockSpec((TM // CMP_STRIDE, CMP_STRIDE * LANES), lambda i: (i, 0)),
                   pl.BlockSpec((TM, LANES), lambda i: (i, 0))],
        out_shape=[jax.ShapeDtypeStruct((n, N_MAIN), BF16),
                   jax.ShapeDtypeStruct((n // CMP_STRIDE, CMP_STRIDE * LANES), BF16),
                   jax.ShapeDtypeStruct((n // CMP_STRIDE, CMP_STRIDE * LANES), BF16),
                   jax.ShapeDtypeStruct((n, LANES), F32)],
        scratch_shapes=[pltpu.VMEM((TM, LANES), F32)],
        compiler_params=_params(("parallel",)),
    )(x2, g1, w_p)


def _gelu_tanh(x):
    return x * (0.5 * (1.0 + jnp.tanh(math.sqrt(2.0 / math.pi) * (x + 0.044715 * (x * x * x)))))


def _compress_kernel(rk_ref, rv_ref, pos_ref, w1k_ref, w2k_ref, w1v_ref, w2v_ref, kc_ref, vc_ref):
    width = rk_ref.shape[2]
    lane = lax.broadcasted_iota(jnp.int32, (1, width), 1)
    in_group = [jnp.where(((lane >> 6) & 1) == g, 1.0, 0.0) for g in range(NSA_KV)]

    def comp(r_ref, p_top, p_bot, w1_ref, w2_ref):
        r = r_ref[0].astype(F32)
        hid = []
        for g in range(NSA_KV):
            top = _dot(((r + p_top) * in_group[g]).astype(BF16), w1_ref[0])
            bot = _dot(((r + p_bot) * in_group[g]).astype(BF16), w1_ref[1])
            nrow = top.shape[0]
            hid.append(top + pltpu.roll(bot, nrow - 1, 0))
        act = _gelu_tanh(jnp.concatenate(hid, axis=1)).astype(BF16)
        return _dot(act, w2_ref[...]).astype(BF16)

    kc_ref[0] = comp(rk_ref, pos_ref[0:1, :], pos_ref[1:2, :], w1k_ref, w2k_ref)
    vc_ref[0] = comp(rv_ref, pos_ref[2:3, :], pos_ref[3:4, :], w1v_ref, w2v_ref)


def _compress(rk, rv, pos, w1k, w2k, w1v, w2v):
    bsz, nrow, width = rk.shape
    hid2 = 2 * CMP_HIDDEN
    return pl.pallas_call(
        _compress_kernel,
        grid=(bsz,),
        in_specs=[pl.BlockSpec((1, nrow, width), lambda b: (b, 0, 0)),
                  pl.BlockSpec((1, nrow, width), lambda b: (b, 0, 0)),
                  pl.BlockSpec((8, width), lambda b: (0, 0)),
                  pl.BlockSpec((2, width, CMP_HIDDEN), lambda b: (0, 0, 0)),
                  pl.BlockSpec((hid2, LANES), lambda b: (0, 0)),
                  pl.BlockSpec((2, width, CMP_HIDDEN), lambda b: (0, 0, 0)),
                  pl.BlockSpec((hid2, LANES), lambda b: (0, 0))],
        out_specs=[pl.BlockSpec((1, nrow, LANES), lambda b: (b, 0, 0)),
                   pl.BlockSpec((1, nrow, LANES), lambda b: (b, 0, 0))],
        out_shape=[jax.ShapeDtypeStruct((bsz, nrow, LANES), BF16),
                   jax.ShapeDtypeStruct((bsz, nrow, LANES), BF16)],
        compiler_params=_params(("parallel",)),
    )(rk, rv, pos, w1k, w2k, w1v, w2v)


def _half_mask(g):
    lane = lax.broadcasted_iota(jnp.int32, (1, LANES), 1)
    return jnp.where((lane >= g * HALF) & (lane < (g + 1) * HALF), 1.0, 0.0).astype(BF16)


def _group_queries(q, nslab):
    return [jnp.concatenate([q[:, j * LANES:(j + 1) * LANES] * _half_mask(g) for j in range(nslab)], axis=0)
            for g in range(2)]


def _nsa_chains(q_ref, extra_ref, tq):
    chains = []
    for qt in range(2):
        rows = slice(qt * tq, (qt + 1) * tq)
        for g in range(NSA_KV):
            keep = _half_mask(g)
            for j in range(NSA_GQ):
                q = q_ref[0, rows, j * LANES:(j + 1) * LANES] * keep
                if extra_ref is not None:
                    q = q + extra_ref[0, rows, g * LANES:(g + 1) * LANES]
                chains.append((q, qt, g, j))
    return chains


def _fill_bias(bias_ref, rel_ref, slots, tq, tk):
    r = lax.broadcasted_iota(jnp.int32, (tq, tk), 0)
    c = lax.broadcasted_iota(jnp.int32, (tq, tk), 1)
    for i, col in slots:
        far = rel_ref[N_BUCKETS - 1, col]
        for d in range(2):
            dist = r - c + d * tq
            val = jnp.zeros((tq, tk), F32) + rel_ref[0, col]
            for k in range(1, N_BUCKETS):
                val = jnp.where(dist >= T5_THRESH[k], rel_ref[k, col], val)
            val = (val - far) * LOG2E
            if d == 0:
                val = jnp.where(dist >= 0, val, NEG)
            bias_ref[i, d] = val


def _with_near_bias(s, near):
    add = near[0] if len(near) == 1 else jnp.concatenate(near, axis=1)
    width, nw = s.shape[1], add.shape[1]
    if nw == width:
        return s + add
    return jnp.concatenate([s[:, 0:width - nw], s[:, width - nw:width] + add], axis=1)


def _softmax_step(s, v, m_ref, acc_ref):
    tk = s.shape[1]
    m_prev = m_ref[...]
    m_new = jnp.maximum(m_prev, jnp.max(s, axis=-1, keepdims=True))
    alpha = jnp.exp2(m_prev - m_new)
    p = jnp.exp2(s - jnp.concatenate([m_new] * (tk // LANES), axis=1))
    vaug = jnp.concatenate([v, jnp.ones((tk, LANES), BF16)], axis=1)
    acc_ref[...] = jnp.concatenate([alpha, alpha], axis=1) * acc_ref[...] + _dot(p.astype(BF16), vaug)
    m_ref[...] = m_new


def _softmax_first(s, v, m_ref, acc_ref):
    rows, tk = s.shape
    m = jnp.broadcast_to(jnp.max(s, axis=-1, keepdims=True), (rows, LANES))
    p = jnp.exp2(s - jnp.concatenate([m] * (tk // LANES), axis=1))
    vaug = jnp.concatenate([v, jnp.ones((tk, LANES), BF16)], axis=1)
    acc_ref[...] = _dot(p.astype(BF16), vaug)
    m_ref[...] = m


def _softmax_once(s, v):
    rows, tk = s.shape
    m = jnp.broadcast_to(jnp.max(s, axis=-1, keepdims=True), (rows, LANES))
    p = jnp.exp2(s - jnp.concatenate([m] * (tk // LANES), axis=1))
    vaug = jnp.concatenate([v, jnp.ones((tk, LANES), BF16)], axis=1)
    pv = _dot(p.astype(BF16), vaug)
    return pv[:, 0:LANES] * (1.0 / pv[:, LANES:2 * LANES])


def _causal_flash(i, tq, tk, chains, kv_of, bias_of, m_ref, acc_ref):
    def tail(first):
        kv = {}
        for c, (q, qt, kv_id, bias_id) in enumerate(chains):
            ntile = (1 if first else 3) + qt
            if (kv_id, ntile) not in kv:
                start = 0 if first else pl.multiple_of((i - 1) * 2 * tk, 2 * tk)
                kv[(kv_id, ntile)] = kv_of(kv_id, start, ntile * tk)
            k, v = kv[(kv_id, ntile)]
            near = [bias_of(bias_id, d) for d in range(min(ntile, 2) - 1, -1, -1)]
            _softmax_first(_with_near_bias(_nt_dot(q, k), near), v, m_ref.at[c], acc_ref.at[c])

    @pl.when(i == 0)
    def _():
        tail(True)

    @pl.when(i >= 1)
    def _():
        tail(False)

    def far(j, carry):
        start = pl.multiple_of(j * 2 * tk, 2 * tk)
        kv = {}
        for c, (q, _, kv_id, _) in enumerate(chains):
            if kv_id not in kv:
                kv[kv_id] = kv_of(kv_id, start, 2 * tk)
            k, v = kv[kv_id]
            _softmax_step(_nt_dot(q, k), v, m_ref.at[c], acc_ref.at[c])
        return carry

    lax.fori_loop(0, i - 1, far, 0)


def _normalized(acc_ref, c):
    return acc_ref[c, :, 0:LANES] * (1.0 / acc_ref[c, :, LANES:2 * LANES])


def _store_gated(o_ref, gate_ref, o_of, branch, tq, qt):
    lane = lax.broadcasted_iota(jnp.int32, (1, LANES), 1)
    rows = slice(qt * tq, (qt + 1) * tq)
    gates = gate_ref[0, rows, :]
    for j in range(NSA_GQ):
        c0 = (0 * NSA_GQ + j) * 3 + branch
        c1 = (1 * NSA_GQ + j) * 3 + branch
        gexp = jnp.where(lane < HALF, gates[:, c0:c0 + 1], gates[:, c1:c1 + 1])
        slab = jnp.where(lane < HALF, o_of(0, j), o_of(1, j))
        o_ref[0, rows, j * LANES:(j + 1) * LANES] = (gexp * slab).astype(BF16)


def _nsa_chain_index(qt, g, j):
    return (qt * NSA_KV + g) * NSA_GQ + j


N_CHAINS = 2 * NSA_HEADS


def _cmp_kernel(q_ref, kc_ref, vc_ref, ovt_ref, gate_ref, o_ref, nsel_ref, *, tq):
    qi = pl.program_id(1)
    nblk = kc_ref.shape[1]
    lane = lax.broadcasted_iota(jnp.int32, (1, LANES), 1)
    n = lax.broadcasted_iota(jnp.int32, (1, nblk), 1)
    t4 = qi * tq + (lax.broadcasted_iota(jnp.int32, (NSA_GQ * tq, 1), 0) & (tq - 1))
    vis4 = (n * CMP_STRIDE + (CMP_LEN - 1)) <= t4
    has4 = jnp.where(t4 >= CMP_LEN - 1, 1.0, 0.0)
    qg = _group_queries(q_ref[0], NSA_GQ)
    kc = kc_ref[0]
    vc = vc_ref[0]
    outs, imp_t = [], jnp.zeros((LANES, tq), F32)
    for g in range(NSA_KV):
        s = jnp.where(vis4, _nt_dot(qg[g], kc), NEG)
        e = jnp.exp2(s - jnp.max(s, axis=-1, keepdims=True))
        p = e * (has4 / jnp.sum(e, axis=-1, keepdims=True))
        outs.append(_dot(p.astype(BF16), vc))
        ps = p[0:tq] + p[tq:2 * tq] + p[2 * tq:3 * tq] + p[3 * tq:4 * tq]
        hi = ps.astype(BF16)
        r1 = ps - hi.astype(F32)
        mid = r1.astype(BF16)
        lo = (r1 - mid.astype(F32)).astype(BF16)
        imp_t = imp_t + (_nt_dot(ovt_ref[g], hi) + _nt_dot(ovt_ref[g], mid) + _nt_dot(ovt_ref[g], lo))
    _store_gated(o_ref, gate_ref, lambda g, j: outs[g][j * tq:(j + 1) * tq], 0, tq, 0)

    tl = qi * tq + lax.broadcasted_iota(jnp.int32, (N_SLC, tq), 1)
    jrow = lax.broadcasted_iota(jnp.int32, (N_SLC, tq), 0)
    cur = tl >> 6
    valid = (jrow * SLC_BLOCK) <= tl
    forced = (jrow == 0) | (jrow == cur) | (jrow == cur - 1)
    sub = 8
    zero = jnp.zeros((N_SLC, tq), F32)
    dropped = []
    for base in (0, HALF):
        x = jnp.where(forced, jnp.inf, jnp.where(valid, imp_t[base:base + N_SLC], -jnp.inf))
        others = [jnp.broadcast_to(x[jj:jj + 1], (sub, tq)) for jj in range(N_SLC)]
        chunks = []
        for r0 in range(0, N_SLC, sub):
            xr = x[r0:r0 + sub]
            jr = r0 + lax.broadcasted_iota(jnp.int32, (sub, tq), 0)
            cnt = jnp.zeros((sub, tq), F32)
            for jj in range(N_SLC):
                if jj < r0:
                    beats = others[jj] >= xr
                elif jj >= r0 + sub:
                    beats = others[jj] > xr
                else:
                    beats = (others[jj] > xr) | ((others[jj] == xr) & (jr > jj))
                cnt = cnt + jnp.where(beats, 1.0, 0.0)
            chunks.append(jnp.where(cnt < SLC_TOPK, 0.0, 1.0))
        dropped.append(jnp.concatenate(chunks, axis=0))
    drop = jnp.concatenate([dropped[0], zero, dropped[1], zero], axis=0).T
    nsel_ref[0, :, 0:LANES] = jnp.where(lane >= HALF, drop, 0.0).astype(BF16)
    nsel_ref[0, :, LANES:2 * LANES] = jnp.where(lane < HALF, drop, 0.0).astype(BF16)


def _cmp_attention(main, kc, vc, ovt, gates):
    bsz, seq, _ = main.shape
    nblk = kc.shape[1]
    tq = TQ_CMP
    return pl.pallas_call(
        functools.partial(_cmp_kernel, tq=tq),
        grid=(bsz, seq // tq),
        in_specs=[pl.BlockSpec((1, tq, NSA_W), lambda b, i: (b, i, COL_NQ // NSA_W)),
                  pl.BlockSpec((1, nblk, LANES), lambda b, i: (b, 0, 0)),
                  pl.BlockSpec((1, nblk, LANES), lambda b, i: (b, 0, 0)),
                  pl.BlockSpec((2, LANES, nblk), lambda b, i: (0, 0, 0)),
                  pl.BlockSpec((1, tq, LANES), lambda b, i: (b, i, 0))],
        out_specs=[pl.BlockSpec((1, tq, NSA_W), lambda b, i: (b, i, 0)),
                   pl.BlockSpec((1, tq, 2 * LANES), lambda b, i: (b, i, 0))],
        out_shape=[jax.ShapeDtypeStruct((bsz, seq, NSA_W), BF16),
                   jax.ShapeDtypeStruct((bsz, seq, 2 * LANES), BF16)],
        compiler_params=_params(("parallel", "parallel")),
    )(main, kc, vc, ovt, gates)


def _slc_kernel(rel_ref, q_ref, k_ref, v_ref, nsel_ref, gate_ref, o_ref, m_ref, acc_ref, bias_ref, *, tq, tk):
    b = pl.program_id(0)
    i = pl.program_id(1)

    @pl.when((b == 0) & (i == 0))
    def _():
        _fill_bias(bias_ref, rel_ref, [(h, DIFF_HEADS + h) for h in range(NSA_HEADS)], tq, tk)

    chains = [(q, qt, g, g * NSA_GQ + j) for q, qt, g, j in _nsa_chains(q_ref, nsel_ref, tq)]

    def kv_of(g, start, width):
        return (k_ref[0, pl.ds(start, width), g * LANES:(g + 1) * LANES], v_ref[0, pl.ds(start, width), :])

    _causal_flash(i, tq, tk, chains, kv_of, lambda h, d: bias_ref[h, d], m_ref, acc_ref)
    for qt in range(2):
        _store_gated(o_ref, gate_ref, lambda g, j: _normalized(acc_ref, _nsa_chain_index(qt, g, j)), 1, tq, qt)


def _slc_attention(rel_bias, main, nsel, gates):
    bsz, seq, _ = main.shape
    tq, tk = TQ, TK
    return pl.pallas_call(
        functools.partial(_slc_kernel, tq=tq, tk=tk),
        grid=(bsz, seq // (2 * tq)),
        in_specs=[pl.BlockSpec(memory_space=pltpu.SMEM),
                  pl.BlockSpec((1, 2 * tq, NSA_W), lambda b, i: (b, i, COL_NQ // NSA_W)),
                  pl.BlockSpec((1, seq, 2 * LANES), lambda b, i: (b, 0, COL_KSL // (2 * LANES))),
                  pl.BlockSpec((1, seq, LANES), lambda b, i: (b, 0, COL_VSL // LANES)),
                  pl.BlockSpec((1, 2 * tq, 2 * LANES), lambda b, i: (b, i, 0)),
                  pl.BlockSpec((1, 2 * tq, LANES), lambda b, i: (b, i, 0))],
        out_specs=pl.BlockSpec((1, 2 * tq, NSA_W), lambda b, i: (b, i, 0)),
        out_shape=jax.ShapeDtypeStruct((bsz, seq, NSA_W), BF16),
        scratch_shapes=[pltpu.VMEM((N_CHAINS, tq, LANES), F32),
                        pltpu.VMEM((N_CHAINS, tq, 2 * LANES), F32),
                        pltpu.VMEM((NSA_HEADS, 2, tq, tk), F32)],
        compiler_params=_params(("arbitrary", "arbitrary")),
    )(rel_bias, main, main, main, nsel, gates)


def _win_kernel(rel_ref, q_ref, k_ref, v_ref, gate_ref, o_ref, bias_ref, band_ref, out_ref, *, tq, tk):
    b = pl.program_id(0)
    i = pl.program_id(1)

    @pl.when((b == 0) & (i == 0))
    def _():
        _fill_bias(bias_ref, rel_ref, [(h, DIFF_HEADS + h) for h in range(NSA_HEADS)], tq, tk)
        r = lax.broadcasted_iota(jnp.int32, (tq, tk), 0)
        c = lax.broadcasted_iota(jnp.int32, (tq, tk), 1)
        band_ref[...] = jnp.where(r < c, 0.0, NEG)

    chains = _nsa_chains(q_ref, None, tq)

    def window(first):
        kv = {}
        for q, qt, g, j in chains:
            ntile = 1 + qt if first else 3
            if (qt, ntile) not in kv:
                start = 0 if first else pl.multiple_of((2 * i - 2 + qt) * tk, tk)
                kv[(qt, ntile)] = (k_ref[0, pl.ds(start, ntile * tk), :], v_ref[0, pl.ds(start, ntile * tk), :])
            k, v = kv[(qt, ntile)]
            near = [bias_ref[g * NSA_GQ + j, d] for d in range(min(ntile, 2) - 1, -1, -1)]
            if ntile == 3:
                near.insert(0, band_ref[...])
            out_ref[_nsa_chain_index(qt, g, j)] = _softmax_once(_with_near_bias(_nt_dot(q, k), near), v)

    @pl.when(i == 0)
    def _():
        window(True)

    @pl.when(i >= 1)
    def _():
        window(False)

    for qt in range(2):
        _store_gated(o_ref, gate_ref, lambda g, j: out_ref[_nsa_chain_index(qt, g, j)], 2, tq, qt)


def _win_attention(rel_bias, main, gates):
    bsz, seq, _ = main.shape
    tq, tk = TQ, TK
    return pl.pallas_call(
        functools.partial(_win_kernel, tq=tq, tk=tk),
        grid=(bsz, seq // (2 * tq)),
        in_specs=[pl.BlockSpec(memory_space=pltpu.SMEM),
                  pl.BlockSpec((1, 2 * tq, NSA_W), lambda b, i: (b, i, COL_NQ // NSA_W)),
                  pl.BlockSpec((1, seq, LANES), lambda b, i: (b, 0, COL_KW // LANES)),
                  pl.BlockSpec((1, seq, LANES), lambda b, i: (b, 0, COL_VW // LANES)),
                  pl.BlockSpec((1, 2 * tq, LANES), lambda b, i: (b, i, 0))],
        out_specs=pl.BlockSpec((1, 2 * tq, NSA_W), lambda b, i: (b, i, 0)),
        out_shape=jax.ShapeDtypeStruct((bsz, seq, NSA_W), BF16),
        scratch_shapes=[pltpu.VMEM((NSA_HEADS, 2, tq, tk), F32),
                        pltpu.VMEM((tq, tk), F32),
                        pltpu.VMEM((N_CHAINS, tq, LANES), F32)],
        compiler_params=_params(("arbitrary", "arbitrary")),
    )(rel_bias, main, main, main, gates)


def _diff_kernel(rel_ref, lam_ref, q_ref, k_ref, v_ref, g_ref, o_ref, m_ref, acc_ref, bias_ref, *, tq, tk):
    b = pl.program_id(0)
    i = pl.program_id(1)

    @pl.when((b == 0) & (i == 0))
    def _():
        _fill_bias(bias_ref, rel_ref, [(h, h) for h in range(DIFF_HEADS)], tq, tk)

    chains = []
    for qt in range(2):
        for h in range(DIFF_HEADS):
            slab = q_ref[0, qt * tq:(qt + 1) * tq, h * LANES:(h + 1) * LANES]
            for half in range(2):
                chains.append((slab * _half_mask(half), qt, h, h))

    def kv_of(h, start, width):
        cols = slice(h * LANES, (h + 1) * LANES)
        return (k_ref[0, pl.ds(start, width), cols], v_ref[0, pl.ds(start, width), cols])

    _causal_flash(i, tq, tk, chains, kv_of, lambda h, d: bias_ref[h, d], m_ref, acc_ref)

    lm = lam_ref[...]
    lam = (jnp.exp(jnp.sum(lm[0:1] * lm[1:2], axis=-1, keepdims=True))
           - jnp.exp(jnp.sum(lm[2:3] * lm[3:4], axis=-1, keepdims=True)) + LAMBDA_INIT)
    for qt in range(2):
        for h in range(DIFF_HEADS):
            c = (qt * DIFF_HEADS + h) * 2
            a = _normalized(acc_ref, c) - lam * _normalized(acc_ref, c + 1)
            o_ref[0, qt * tq:(qt + 1) * tq, h * LANES:(h + 1) * LANES] = (
                _rms(a, g_ref[...]) * (1.0 - LAMBDA_INIT)).astype(BF16)


def _diff_attention(rel_bias, lam_rows, main, subln):
    bsz, seq, _ = main.shape
    tq, tk = TQ, TK
    return pl.pallas_call(
        functools.partial(_diff_kernel, tq=tq, tk=tk),
        grid=(bsz, seq // (2 * tq)),
        in_specs=[pl.BlockSpec(memory_space=pltpu.SMEM),
                  pl.BlockSpec((8, LANES), lambda b, i: (0, 0)),
                  pl.BlockSpec((1, 2 * tq, DIFF_W), lambda b, i: (b, i, COL_DQ // DIFF_W)),
                  pl.BlockSpec((1, seq, DIFF_W), lambda b, i: (b, 0, COL_DK // DIFF_W)),
                  pl.BlockSpec((1, seq, DIFF_W), lambda b, i: (b, 0, COL_DV // DIFF_W)),
                  pl.BlockSpec((1, LANES), lambda b, i: (0, 0))],
        out_specs=pl.BlockSpec((1, 2 * tq, DIFF_W), lambda b, i: (b, i, 0)),
        out_shape=jax.ShapeDtypeStruct((bsz, seq, DIFF_W), BF16),
        scratch_shapes=[pltpu.VMEM((N_CHAINS, tq, LANES), F32),
                        pltpu.VMEM((N_CHAINS, tq, 2 * LANES), F32),
                        pltpu.VMEM((DIFF_HEADS, 2, tq, tk), F32)],
        compiler_params=_params(("arbitrary", "arbitrary")),
    )(rel_bias, lam_rows, main, main, main, subln)


FF_CHUNK = 256


def _tail_kernel(x_ref, od_ref, oc_ref, os_ref, ow_ref, gn_ref, wo_ref, g2_ref,
                 wg_ref, wu_ref, wd_ref, gf_ref, o_ref, acc_ref):
    o_nsa = oc_ref[...].astype(F32) + os_ref[...].astype(F32) + ow_ref[...].astype(F32)
    o_nsa = _rms(o_nsa, gn_ref[...]).astype(BF16)
    attn = _dot(od_ref[...], wo_ref[0:DIFF_W, :]) + _dot(o_nsa, wo_ref[DIFF_W:DIFF_W + NSA_W, :])
    x1 = x_ref[...] + attn
    acc_ref[...] = x1
    h = _rms(x1, g2_ref[...]).astype(BF16)
    for c in range(0, D_FF, FF_CHUNK):
        gate = _dot(h, wg_ref[:, c:c + FF_CHUNK])
        up = _dot(h, wu_ref[:, c:c + FF_CHUNK])
        act = (gate * (1.0 / (1.0 + jnp.exp(-gate))) * up).astype(BF16)
        acc_ref[...] += _dot(act, wd_ref[c:c + FF_CHUNK, :])
    o_ref[...] = _rms(acc_ref[...], gf_ref[...])


def _tail(x2, od, oc, os_, ow, gn, w_out, g2, wg, wu, wd, gf):
    n = x2.shape[0]
    row = lambda i: (i, 0)
    fixed = lambda i: (0, 0)
    once = pl.Buffered(1)
    return pl.pallas_call(
        _tail_kernel,
        grid=(n // TM,),
        in_specs=[pl.BlockSpec((TM, D_MODEL), row),
                  pl.BlockSpec((TM, DIFF_W), row),
                  pl.BlockSpec((TM, NSA_W), row),
                  pl.BlockSpec((TM, NSA_W), row),
                  pl.BlockSpec((TM, NSA_W), row),
                  pl.BlockSpec((1, NSA_W), fixed),
                  pl.BlockSpec((DIFF_W + NSA_W, D_MODEL), fixed, pipeline_mode=once),
                  pl.BlockSpec((1, D_MODEL), fixed),
                  pl.BlockSpec((D_MODEL, D_FF), fixed, pipeline_mode=once),
                  pl.BlockSpec((D_MODEL, D_FF), fixed, pipeline_mode=once),
                  pl.BlockSpec((D_FF, D_MODEL), fixed, pipeline_mode=once),
                  pl.BlockSpec((1, D_MODEL), fixed)],
        out_specs=pl.BlockSpec((TM, D_MODEL), row),
        out_shape=jax.ShapeDtypeStruct((n, D_MODEL), F32),
        scratch_shapes=[pltpu.VMEM((TM, D_MODEL), F32)],
        compiler_params=_params(("parallel",)),
    )(x2, od, oc, os_, ow, gn, w_out, g2, wg, wu, wd, gf)


def _nsa_perm():
    p = np.arange(NSA_W)
    j, g, d = p // LANES, (p % LANES) // NSA_HD, p % NSA_HD
    return (g * NSA_GQ + j) * NSA_HD + d


def _overlap_matrices(nblk):
    n_cmp = nblk - 1
    nslc = nblk * CMP_STRIDE // SLC_BLOCK
    assert nslc == N_SLC
    cmp_start = np.arange(n_cmp) * CMP_STRIDE
    cmp_end = cmp_start + CMP_LEN - 1
    slc_start = np.arange(nslc) * SLC_BLOCK
    overlap = ((cmp_start[:, None] < slc_start[None, :] + SLC_BLOCK) & (cmp_end[:, None] >= slc_start[None, :]))
    ovt = np.zeros((NSA_KV, LANES, nblk), np.float32)
    for g, base in enumerate((HALF, 0)):
        ovt[g, base:base + nslc, :n_cmp] = overlap.T
    return ovt


def _compress_weights(pos, w1, w2):
    half = CMP_LEN // 2
    shape = (2, half, NSA_KV, NSA_HD, CMP_HIDDEN)
    w1rep = jnp.broadcast_to(w1.reshape(2, half, 1, NSA_HD, CMP_HIDDEN), shape).reshape(2, -1, CMP_HIDDEN)
    w2big = jnp.einsum('jd,pg->pjgd', w2, jnp.eye(NSA_KV, dtype=F32)).reshape(NSA_KV * CMP_HIDDEN, NSA_KV * NSA_HD)
    posr = jnp.broadcast_to(pos.reshape(2, half, 1, NSA_HD), (2, half, NSA_KV, NSA_HD)).reshape(2, -1)
    return posr, w1rep.astype(BF16), w2big.astype(BF16)


def kernel(x, norm1, w_in, lambda_q1, lambda_k1, lambda_q2, lambda_k2, diff_subln, cmp_pos_k, cmp_w1_k, cmp_w2_k,
           cmp_pos_v, cmp_w1_v, cmp_w2_v, nsa_norm, w_out, norm2, w_gate, w_up, w_down, rel_bias, final_norm):
    bsz, seq, dm = x.shape
    assert dm == D_MODEL and seq % (2 * TQ) == 0 and seq % TM == 0 and seq == N_SLC * SLC_BLOCK
    assert norm1.shape[0] == 1, "single layer"
    perm = _nsa_perm()
    x2 = x.reshape(bsz * seq, dm)

    w = w_in[0]
    zhalf = jnp.zeros((dm, HALF), F32)
    w_p = jnp.concatenate([w[:, :512] * QSCALE, w[:, 512:1536], w[:, 1536 + perm] * QSCALE,
                           w[:, 2304:2368], zhalf, zhalf, w[:, 2368:2432],
                           w[:, 2432:2816],
                           w[:, 2048:2304],
                           w[:, 2816:N_IN], jnp.zeros((dm, LANES - (N_IN - 2816)), F32)], axis=1).astype(BF16)
    assert w_p.shape[1] == N_PROJ
    wo = w_out[0]
    wo_p = jnp.concatenate([wo[:DIFF_W], wo[DIFF_W + perm]], axis=0).astype(BF16)
    gn_p = nsa_norm[0][perm].reshape(1, NSA_W)
    posk, w1k, w2k = _compress_weights(cmp_pos_k[0], cmp_w1_k[0], cmp_w2_k[0])
    posv, w1v, w2v = _compress_weights(cmp_pos_v[0], cmp_w1_v[0], cmp_w2_v[0])
    pos = jnp.concatenate([posk, posv, jnp.zeros((4, posk.shape[1]), F32)], axis=0)
    lam_rows = jnp.zeros((8, LANES), F32).at[0:4, 0:DIFF_HD].set(
        jnp.stack([lambda_q1[0], lambda_k1[0], lambda_q2[0], lambda_k2[0]]).astype(F32))

    main, kc_grp, vc_grp, gates = _inproj(x2, norm1[0].reshape(1, dm), w_p, seq)
    main = main.reshape(bsz, seq, N_MAIN)

    nrow = seq // CMP_STRIDE
    kc, vc = _compress(kc_grp.reshape(bsz, nrow, CMP_STRIDE * LANES), vc_grp.reshape(bsz, nrow, CMP_STRIDE * LANES),
                       pos, w1k, w2k, w1v, w2v)
    ovt = jnp.asarray(_overlap_matrices(nrow), BF16)
    gates = gates.reshape(bsz, seq, LANES)
    o_cmp, nsel = _cmp_attention(main, kc, vc, ovt, gates)
    o_slc = _slc_attention(rel_bias, main, nsel, gates)
    o_win = _win_attention(rel_bias, main, gates)
    o_diff = _diff_attention(rel_bias, lam_rows, main, diff_subln[0].reshape(1, DIFF_VD))

    n = bsz * seq
    out = _tail(x2, o_diff.reshape(n, DIFF_W), o_cmp.reshape(n, NSA_W), o_slc.reshape(n, NSA_W),
                o_win.reshape(n, NSA_W), gn_p, wo_p, norm2[0].reshape(1, dm),
                w_gate[0].astype(BF16), w_up[0].astype(BF16), w_down[0].astype(BF16), final_norm.reshape(1, dm))
    return out.reshape(bsz, seq, dm)
```

```python
import functools
import math

import numpy as np
import jax
import jax.numpy as jnp
from jax import lax
from jax.experimental import pallas as pl
from jax.experimental.pallas import tpu as pltpu

F32 = jnp.float32
BF16 = jnp.bfloat16

D_MODEL = 1024
DIFF_HEADS = 4
DIFF_HD = 64
DIFF_VD = 128
DIFF_W = 512
NSA_HEADS = 8
NSA_KV = 2
NSA_GQ = 4
NSA_HD = 64
NSA_W = 512
CMP_LEN = 32
CMP_STRIDE = 16
CMP_HIDDEN = 256
SLC_BLOCK = 64
SLC_TOPK = 16
N_SLC = 32
WINDOW = 512
N_BUCKETS = 32
MAX_DISTANCE = 128
D_FF = 2816
N_IN = 2840
NEG = -1e30
EPS = 1e-6
LAMBDA_INIT = 0.8 - 0.6 * math.exp(-0.3 * 0)
LOG2E = math.log2(math.e)
QSCALE = NSA_HD ** -0.5 * LOG2E

LANES = 128
HALF = LANES // 2
MXU_N = 256
COL_DQ, COL_DK, COL_DV, COL_NQ, COL_KSL, COL_VSL, COL_KW, COL_VW, N_MAIN = (
    0, 512, 1024, 1536, 2048, 2304, 2432, 2560, 2688)
N_PROJ = N_MAIN + 3 * LANES
VMEM_LIMIT = 48 * 1024 * 1024

TQ = 256
TK = 256
TM = 512
TQ_CMP = 256


def _t5_thresholds():
    d = np.arange(0, 4 * MAX_DISTANCE)
    max_exact = N_BUCKETS // 2
    val = (np.log(np.maximum(d, 1).astype(np.float32) / np.float32(max_exact))
           / np.float32(math.log(MAX_DISTANCE / max_exact)) * np.float32(N_BUCKETS - max_exact))
    large = np.minimum(max_exact + val.astype(np.int32), N_BUCKETS - 1)
    bucket = np.where(d < max_exact, d, large)
    assert np.all(np.diff(bucket) >= 0) and bucket[-1] == N_BUCKETS - 1
    return [int(np.argmax(bucket >= k)) for k in range(N_BUCKETS)]


T5_THRESH = _t5_thresholds()
assert T5_THRESH[-1] <= TK + 1
assert WINDOW == 2 * TK and TQ == TK


def _nt_dot(a, b):
    return lax.dot_general(a, b, (((1,), (1,)), ((), ())), preferred_element_type=F32)


def _dot(a, b):
    return jnp.dot(a, b, preferred_element_type=F32)


def _rms(x, g):
    ms = jnp.mean(x * x, axis=-1, keepdims=True)
    return x * lax.rsqrt(ms + EPS) * g


def _params(sem):
    return pltpu.CompilerParams(dimension_semantics=sem, vmem_limit_bytes=VMEM_LIMIT)


def _store_grouped(out_ref, tok_ref, y):
    tok_ref[...] = y
    nrow = out_ref.shape[0]
    for l in range(CMP_STRIDE):
        out_ref[:, l * LANES:(l + 1) * LANES] = tok_ref[pl.ds(l, nrow, stride=CMP_STRIDE), :].astype(BF16)


def _inproj_kernel(x_ref, g_ref, w_ref, main_ref, kc_ref, vc_ref, gate_ref, tok_ref, *, seq):
    tm = x_ref.shape[0]
    h = _rms(x_ref[...], g_ref[...]).astype(BF16)
    for c in range(0, N_PROJ, MXU_N):
        y = _dot(h, w_ref[:, c:c + MXU_N])
        if c == COL_KSL:
            t0 = lax.rem(pl.program_id(0) * tm, seq)
            blk = (t0 + lax.broadcasted_iota(jnp.int32, (tm, 2 * LANES), 0)) >> 6
            lane = lax.broadcasted_iota(jnp.int32, (tm, 2 * LANES), 1)
            slot = jnp.where(lane < LANES, lane - HALF, lane - LANES)
            y = y + jnp.where((slot == blk) & (slot >= 0) & (slot < N_SLC), NEG, 0.0)
        if c + MXU_N <= N_MAIN:
            main_ref[:, c:c + MXU_N] = y.astype(BF16)
        elif c == COL_VW:
            main_ref[:, c:c + LANES] = y[:, 0:LANES].astype(BF16)
            _store_grouped(kc_ref, tok_ref, y[:, LANES:MXU_N])
        else:
            _store_grouped(vc_ref, tok_ref, y[:, 0:LANES])
            gate_ref[...] = 1.0 / (1.0 + jnp.exp(-y[:, LANES:MXU_N]))


def _inproj(x2, g1, w_p, seq):
    n = x2.shape[0]
    return pl.pallas_call(
        functools.partial(_inproj_kernel, seq=seq),
        grid=(n // TM,),
        in_specs=[pl.BlockSpec((TM, D_MODEL), lambda i: (i, 0)),
                  pl.BlockSpec((1, D_MODEL), lambda i: (0, 0)),
                  pl.BlockSpec((D_MODEL, N_PROJ), lambda i: (0, 0))],
        out_specs=[pl.BlockSpec((TM, N_MAIN), lambda i: (i, 0)),
                   pl.BlockSpec((TM // CMP_STRIDE, CMP_STRIDE * LANES), lambda i: (i, 0)),
                   pl.BlockSpec((TM // CMP_STRIDE, CMP_STRIDE * LANES), lambda i: (i, 0)),
                   pl.BlockSpec((TM, LANES), lambda i: (i, 0))],
        out_shape=[jax.ShapeDtypeStruct((n, N_MAIN), BF16),
                   jax.ShapeDtypeStruct((n // CMP_STRIDE, CMP_STRIDE * LANES), BF16),
                   jax.ShapeDtypeStruct((n // CMP_STRIDE, CMP_STRIDE * LANES), BF16),
                   jax.ShapeDtypeStruct((n, LANES), F32)],
        scratch_shapes=[pltpu.VMEM((TM, LANES), F32)],
        compiler_params=_params(("parallel",)),
    )(x2, g1, w_p)


def _gelu_tanh(x):
    return x * (0.5 * (1.0 + jnp.tanh(math.sqrt(2.0 / math.pi) * (x + 0.044715 * (x * x * x)))))


def _compress_kernel(rk_ref, rv_ref, pos_ref, w1k_ref, w2k_ref, w1v_ref, w2v_ref, kc_ref, vc_ref):
    width = rk_ref.shape[2]
    lane = lax.broadcasted_iota(jnp.int32, (1, width), 1)
    in_group = [jnp.where(((lane >> 6) & 1) == g, 1.0, 0.0) for g in range(NSA_KV)]

    def comp(r_ref, p_top, p_bot, w1_ref, w2_ref):
        r = r_ref[0].astype(F32)
        hid = []
        for g in range(NSA_KV):
            top = _dot(((r + p_top) * in_group[g]).astype(BF16), w1_ref[0])
            bot = _dot(((r + p_bot) * in_group[g]).astype(BF16), w1_ref[1])
            nrow = top.shape[0]
            hid.append(top + pltpu.roll(bot, nrow - 1, 0))
        act = _gelu_tanh(jnp.concatenate(hid, axis=1)).astype(BF16)
        return _dot(act, w2_ref[...]).astype(BF16)

    kc_ref[0] = comp(rk_ref, pos_ref[0:1, :], pos_ref[1:2, :], w1k_ref, w2k_ref)
    vc_ref[0] = comp(rv_ref, pos_ref[2:3, :], pos_ref[3:4, :], w1v_ref, w2v_ref)


def _compress(rk, rv, pos, w1k, w2k, w1v, w2v):
    bsz, nrow, width = rk.shape
    hid2 = 2 * CMP_HIDDEN
    return pl.pallas_call(
        _compress_kernel,
        grid=(bsz,),
        in_specs=[pl.BlockSpec((1, nrow, width), lambda b: (b, 0, 0)),
                  pl.BlockSpec((1, nrow, width), lambda b: (b, 0, 0)),
                  pl.BlockSpec((8, width), lambda b: (0, 0)),
                  pl.BlockSpec((2, width, CMP_HIDDEN), lambda b: (0, 0, 0)),
                  pl.BlockSpec((hid2, LANES), lambda b: (0, 0)),
                  pl.BlockSpec((2, width, CMP_HIDDEN), lambda b: (0, 0, 0)),
                  pl.BlockSpec((hid2, LANES), lambda b: (0, 0))],
        out_specs=[pl.BlockSpec((1, nrow, LANES), lambda b: (b, 0, 0)),
                   pl.BlockSpec((1, nrow, LANES), lambda b: (b, 0, 0))],
        out_shape=[jax.ShapeDtypeStruct((bsz, nrow, LANES), BF16),
                   jax.ShapeDtypeStruct((bsz, nrow, LANES), BF16)],
        compiler_params=_params(("parallel",)),
    )(rk, rv, pos, w1k, w2k, w1v, w2v)


def _half_mask(g):
    lane = lax.broadcasted_iota(jnp.int32, (1, LANES), 1)
    return jnp.where((lane >= g * HALF) & (lane < (g + 1) * HALF), 1.0, 0.0).astype(BF16)


def _group_queries(q, nslab):
    return [jnp.concatenate([q[:, j * LANES:(j + 1) * LANES] * _half_mask(g) for j in range(nslab)], axis=0)
            for g in range(2)]


def _nsa_chains(q_ref, extra_ref, tq):
    chains = []
    for qt in range(2):
        rows = slice(qt * tq, (qt + 1) * tq)
        for g in range(NSA_KV):
            keep = _half_mask(g)
            for j in range(NSA_GQ):
                q = q_ref[0, rows, j * LANES:(j + 1) * LANES] * keep
                if extra_ref is not None:
                    q = q + extra_ref[0, rows, g * LANES:(g + 1) * LANES]
                chains.append((q, qt, g, j))
    return chains


def _bias_kernel(rel_ref, o_ref, *, tq, tk):
    row = pl.program_id(0)
    col = jnp.where(row < NSA_HEADS, row + DIFF_HEADS, row - NSA_HEADS)
    r = lax.broadcasted_iota(jnp.int32, (tq, tk), 0)
    c = lax.broadcasted_iota(jnp.int32, (tq, tk), 1)
    far = rel_ref[N_BUCKETS - 1, col]
    for d in range(2):
        dist = r - c + d * tq
        val = jnp.zeros((tq, tk), F32) + rel_ref[0, col]
        for k in range(1, N_BUCKETS):
            val = jnp.where(dist >= T5_THRESH[k], rel_ref[k, col], val)
        val = (val - far) * LOG2E
        if d == 0:
            val = jnp.where(dist >= 0, val, NEG)
        o_ref[0, d] = val
    o_ref[0, 2] = jnp.where(r < c, 0.0, NEG)


def _bias_tiles(rel_bias):
    nhead = rel_bias.shape[1]
    return pl.pallas_call(
        functools.partial(_bias_kernel, tq=TQ, tk=TK),
        grid=(nhead,),
        in_specs=[pl.BlockSpec(memory_space=pltpu.SMEM)],
        out_specs=pl.BlockSpec((1, 3, TQ, TK), lambda h: (h, 0, 0, 0)),
        out_shape=jax.ShapeDtypeStruct((nhead, 3, TQ, TK), F32),
        compiler_params=_params(("parallel",)),
    )(rel_bias)


def _with_near_bias(s, near):
    add = near[0] if len(near) == 1 else jnp.concatenate(near, axis=1)
    width, nw = s.shape[1], add.shape[1]
    if nw == width:
        return s + add
    return jnp.concatenate([s[:, 0:width - nw], s[:, width - nw:width] + add], axis=1)


def _softmax_step(s, v, m_ref, acc_ref):
    tk = s.shape[1]
    m_prev = m_ref[...]
    m_new = jnp.maximum(m_prev, jnp.max(s, axis=-1, keepdims=True))
    alpha = jnp.exp2(m_prev - m_new)
    p = jnp.exp2(s - jnp.concatenate([m_new] * (tk // LANES), axis=1))
    vaug = jnp.concatenate([v, jnp.ones((tk, LANES), BF16)], axis=1)
    acc_ref[...] = jnp.concatenate([alpha, alpha], axis=1) * acc_ref[...] + _dot(p.astype(BF16), vaug)
    m_ref[...] = m_new


def _softmax_first(s, v, m_ref, acc_ref):
    rows, tk = s.shape
    m = jnp.broadcast_to(jnp.max(s, axis=-1, keepdims=True), (rows, LANES))
    p = jnp.exp2(s - jnp.concatenate([m] * (tk // LANES), axis=1))
    vaug = jnp.concatenate([v, jnp.ones((tk, LANES), BF16)], axis=1)
    acc_ref[...] = _dot(p.astype(BF16), vaug)
    m_ref[...] = m


def _softmax_once(s, v):
    rows, tk = s.shape
    m = jnp.broadcast_to(jnp.max(s, axis=-1, keepdims=True), (rows, LANES))
    p = jnp.exp2(s - jnp.concatenate([m] * (tk // LANES), axis=1))
    vaug = jnp.concatenate([v, jnp.ones((tk, LANES), BF16)], axis=1)
    pv = _dot(p.astype(BF16), vaug)
    return pv[:, 0:LANES] * (1.0 / pv[:, LANES:2 * LANES])


def _causal_flash(i, tq, tk, chains, kv_of, bias_of, m_ref, acc_ref):
    def tail(first):
        kv = {}
        for c, (q, qt, kv_id, bias_id) in enumerate(chains):
            ntile = (1 if first else 3) + qt
            if (kv_id, ntile) not in kv:
                start = 0 if first else pl.multiple_of((i - 1) * 2 * tk, 2 * tk)
                kv[(kv_id, ntile)] = kv_of(kv_id, start, ntile * tk)
            k, v = kv[(kv_id, ntile)]
            near = [bias_of(bias_id, d) for d in range(min(ntile, 2) - 1, -1, -1)]
            _softmax_first(_with_near_bias(_nt_dot(q, k), near), v, m_ref.at[c], acc_ref.at[c])

    @pl.when(i == 0)
    def _():
        tail(True)

    @pl.when(i >= 1)
    def _():
        tail(False)

    def far(j, carry):
        start = pl.multiple_of(j * 2 * tk, 2 * tk)
        kv = {}
        for c, (q, _, kv_id, _) in enumerate(chains):
            if kv_id not in kv:
                kv[kv_id] = kv_of(kv_id, start, 2 * tk)
            k, v = kv[kv_id]
            _softmax_step(_nt_dot(q, k), v, m_ref.at[c], acc_ref.at[c])
        return carry

    lax.fori_loop(0, i - 1, far, 0)


def _normalized(acc_ref, c):
    return acc_ref[c, :, 0:LANES] * (1.0 / acc_ref[c, :, LANES:2 * LANES])


def _store_gated(o_ref, gate_ref, o_of, branch, tq, qt):
    lane = lax.broadcasted_iota(jnp.int32, (1, LANES), 1)
    rows = slice(qt * tq, (qt + 1) * tq)
    gates = gate_ref[0, rows, :]
    for j in range(NSA_GQ):
        c0 = (0 * NSA_GQ + j) * 3 + branch
        c1 = (1 * NSA_GQ + j) * 3 + branch
        gexp = jnp.where(lane < HALF, gates[:, c0:c0 + 1], gates[:, c1:c1 + 1])
        slab = jnp.where(lane < HALF, o_of(0, j), o_of(1, j))
        o_ref[0, rows, j * LANES:(j + 1) * LANES] = (gexp * slab).astype(BF16)


def _nsa_chain_index(qt, g, j):
    return (qt * NSA_KV + g) * NSA_GQ + j


N_CHAINS = 2 * NSA_HEADS


def _cmp_kernel(q_ref, kc_ref, vc_ref, ovt_ref, gate_ref, o_ref, nsel_ref, *, tq):
    qi = pl.program_id(1)
    nblk = kc_ref.shape[1]
    lane = lax.broadcasted_iota(jnp.int32, (1, LANES), 1)
    n = lax.broadcasted_iota(jnp.int32, (1, nblk), 1)
    t4 = qi * tq + (lax.broadcasted_iota(jnp.int32, (NSA_GQ * tq, 1), 0) & (tq - 1))
    vis4 = (n * CMP_STRIDE + (CMP_LEN - 1)) <= t4
    has4 = jnp.where(t4 >= CMP_LEN - 1, 1.0, 0.0)
    qg = _group_queries(q_ref[0], NSA_GQ)
    kc = kc_ref[0]
    vc = vc_ref[0]
    outs, imp_t = [], jnp.zeros((LANES, tq), F32)
    for g in range(NSA_KV):
        s = jnp.where(vis4, _nt_dot(qg[g], kc), NEG)
        e = jnp.exp2(s - jnp.max(s, axis=-1, keepdims=True))
        p = e * (has4 / jnp.sum(e, axis=-1, keepdims=True))
        outs.append(_dot(p.astype(BF16), vc))
        ps = p[0:tq] + p[tq:2 * tq] + p[2 * tq:3 * tq] + p[3 * tq:4 * tq]
        hi = ps.astype(BF16)
        r1 = ps - hi.astype(F32)
        mid = r1.astype(BF16)
        lo = (r1 - mid.astype(F32)).astype(BF16)
        imp_t = imp_t + (_nt_dot(ovt_ref[g], hi) + _nt_dot(ovt_ref[g], mid) + _nt_dot(ovt_ref[g], lo))
    _store_gated(o_ref, gate_ref, lambda g, j: outs[g][j * tq:(j + 1) * tq], 0, tq, 0)

    tl = qi * tq + lax.broadcasted_iota(jnp.int32, (N_SLC, tq), 1)
    jrow = lax.broadcasted_iota(jnp.int32, (N_SLC, tq), 0)
    cur = tl >> 6
    valid = (jrow * SLC_BLOCK) <= tl
    forced = (jrow == 0) | (jrow == cur) | (jrow == cur - 1)
    sub = 8
    zero = jnp.zeros((N_SLC, tq), F32)
    dropped = []
    for base in (0, HALF):
        x = jnp.where(forced, jnp.inf, jnp.where(valid, imp_t[base:base + N_SLC], -jnp.inf))
        others = [jnp.broadcast_to(x[jj:jj + 1], (sub, tq)) for jj in range(N_SLC)]
        chunks = []
        for r0 in range(0, N_SLC, sub):
            xr = x[r0:r0 + sub]
            jr = r0 + lax.broadcasted_iota(jnp.int32, (sub, tq), 0)
            cnt = jnp.zeros((sub, tq), F32)
            for jj in range(N_SLC):
                if jj < r0:
                    beats = others[jj] >= xr
                elif jj >= r0 + sub:
                    beats = others[jj] > xr
                else:
                    beats = (others[jj] > xr) | ((others[jj] == xr) & (jr > jj))
                cnt = cnt + jnp.where(beats, 1.0, 0.0)
            chunks.append(jnp.where(cnt < SLC_TOPK, 0.0, 1.0))
        dropped.append(jnp.concatenate(chunks, axis=0))
    drop = jnp.concatenate([dropped[0], zero, dropped[1], zero], axis=0).T
    nsel_ref[0, :, 0:LANES] = jnp.where(lane >= HALF, drop, 0.0).astype(BF16)
    nsel_ref[0, :, LANES:2 * LANES] = jnp.where(lane < HALF, drop, 0.0).astype(BF16)


def _cmp_attention(main, kc, vc, ovt, gates):
    bsz, seq, _ = main.shape
    nblk = kc.shape[1]
    tq = TQ_CMP
    return pl.pallas_call(
        functools.partial(_cmp_kernel, tq=tq),
        grid=(bsz, seq // tq),
        in_specs=[pl.BlockSpec((1, tq, NSA_W), lambda b, i: (b, i, COL_NQ // NSA_W)),
                  pl.BlockSpec((1, nblk, LANES), lambda b, i: (b, 0, 0)),
                  pl.BlockSpec((1, nblk, LANES), lambda b, i: (b, 0, 0)),
                  pl.BlockSpec((2, LANES, nblk), lambda b, i: (0, 0, 0)),
                  pl.BlockSpec((1, tq, LANES), lambda b, i: (b, i, 0))],
        out_specs=[pl.BlockSpec((1, tq, NSA_W), lambda b, i: (b, i, 0)),
                   pl.BlockSpec((1, tq, 2 * LANES), lambda b, i: (b, i, 0))],
        out_shape=[jax.ShapeDtypeStruct((bsz, seq, NSA_W), BF16),
                   jax.ShapeDtypeStruct((bsz, seq, 2 * LANES), BF16)],
        compiler_params=_params(("parallel", "parallel")),
    )(main, kc, vc, ovt, gates)


def _slc_kernel(bias_ref, q_ref, k_ref, v_ref, nsel_ref, gate_ref, o_ref, m_ref, acc_ref, *, tq, tk):
    i = pl.program_id(1)
    chains = [(q, qt, g, g * NSA_GQ + j) for q, qt, g, j in _nsa_chains(q_ref, nsel_ref, tq)]

    def kv_of(g, start, width):
        return (k_ref[0, pl.ds(start, width), g * LANES:(g + 1) * LANES], v_ref[0, pl.ds(start, width), :])

    _causal_flash(i, tq, tk, chains, kv_of, lambda h, d: bias_ref[h, d], m_ref, acc_ref)
    for qt in range(2):
        _store_gated(o_ref, gate_ref, lambda g, j: _normalized(acc_ref, _nsa_chain_index(qt, g, j)), 1, tq, qt)


def _slc_attention(bias, main, nsel, gates):
    bsz, seq, _ = main.shape
    tq, tk = TQ, TK
    return pl.pallas_call(
        functools.partial(_slc_kernel, tq=tq, tk=tk),
        grid=(bsz, seq // (2 * tq)),
        in_specs=[pl.BlockSpec((NSA_HEADS, 3, tq, tk), lambda b, i: (0, 0, 0, 0), pipeline_mode=pl.Buffered(1)),
                  pl.BlockSpec((1, 2 * tq, NSA_W), lambda b, i: (b, i, COL_NQ // NSA_W)),
                  pl.BlockSpec((1, seq, 2 * LANES), lambda b, i: (b, 0, COL_KSL // (2 * LANES))),
                  pl.BlockSpec((1, seq, LANES), lambda b, i: (b, 0, COL_VSL // LANES)),
                  pl.BlockSpec((1, 2 * tq, 2 * LANES), lambda b, i: (b, i, 0)),
                  pl.BlockSpec((1, 2 * tq, LANES), lambda b, i: (b, i, 0))],
        out_specs=pl.BlockSpec((1, 2 * tq, NSA_W), lambda b, i: (b, i, 0)),
        out_shape=jax.ShapeDtypeStruct((bsz, seq, NSA_W), BF16),
        scratch_shapes=[pltpu.VMEM((N_CHAINS, tq, LANES), F32),
                        pltpu.VMEM((N_CHAINS, tq, 2 * LANES), F32)],
        compiler_params=_params(("parallel", "parallel")),
    )(bias, main, main, main, nsel, gates)


def _win_kernel(bias_ref, q_ref, k_ref, v_ref, gate_ref, o_ref, out_ref, *, tq, tk):
    i = pl.program_id(1)
    chains = _nsa_chains(q_ref, None, tq)

    def window(first):
        kv = {}
        for q, qt, g, j in chains:
            ntile = 1 + qt if first else 3
            if (qt, ntile) not in kv:
                start = 0 if first else pl.multiple_of((2 * i - 2 + qt) * tk, tk)
                kv[(qt, ntile)] = (k_ref[0, pl.ds(start, ntile * tk), :], v_ref[0, pl.ds(start, ntile * tk), :])
            k, v = kv[(qt, ntile)]
            near = [bias_ref[g * NSA_GQ + j, d] for d in range(min(ntile, 2) - 1, -1, -1)]
            if ntile == 3:
                near.insert(0, bias_ref[g * NSA_GQ + j, 2])
            out_ref[_nsa_chain_index(qt, g, j)] = _softmax_once(_with_near_bias(_nt_dot(q, k), near), v)

    @pl.when(i == 0)
    def _():
        window(True)

    @pl.when(i >= 1)
    def _():
        window(False)

    for qt in range(2):
        _store_gated(o_ref, gate_ref, lambda g, j: out_ref[_nsa_chain_index(qt, g, j)], 2, tq, qt)


def _win_attention(bias, main, gates):
    bsz, seq, _ = main.shape
    tq, tk = TQ, TK
    return pl.pallas_call(
        functools.partial(_win_kernel, tq=tq, tk=tk),
        grid=(bsz, seq // (2 * tq)),
        in_specs=[pl.BlockSpec((NSA_HEADS, 3, tq, tk), lambda b, i: (0, 0, 0, 0), pipeline_mode=pl.Buffered(1)),
                  pl.BlockSpec((1, 2 * tq, NSA_W), lambda b, i: (b, i, COL_NQ // NSA_W)),
                  pl.BlockSpec((1, seq, LANES), lambda b, i: (b, 0, COL_KW // LANES)),
                  pl.BlockSpec((1, seq, LANES), lambda b, i: (b, 0, COL_VW // LANES)),
                  pl.BlockSpec((1, 2 * tq, LANES), lambda b, i: (b, i, 0))],
        out_specs=pl.BlockSpec((1, 2 * tq, NSA_W), lambda b, i: (b, i, 0)),
        out_shape=jax.ShapeDtypeStruct((bsz, seq, NSA_W), BF16),
        scratch_shapes=[pltpu.VMEM((N_CHAINS, tq, LANES), F32)],
        compiler_params=_params(("parallel", "parallel")),
    )(bias, main, main, main, gates)


def _diff_kernel(bias_ref, lam_ref, q_ref, k_ref, v_ref, g_ref, o_ref, m_ref, acc_ref, *, tq, tk):
    i = pl.program_id(1)
    chains = []
    for qt in range(2):
        for h in range(DIFF_HEADS):
            slab = q_ref[0, qt * tq:(qt + 1) * tq, h * LANES:(h + 1) * LANES]
            for half in range(2):
                chains.append((slab * _half_mask(half), qt, h, h))

    def kv_of(h, start, width):
        cols = slice(h * LANES, (h + 1) * LANES)
        return (k_ref[0, pl.ds(start, width), cols], v_ref[0, pl.ds(start, width), cols])

    _causal_flash(i, tq, tk, chains, kv_of, lambda h, d: bias_ref[h, d], m_ref, acc_ref)

    lm = lam_ref[...]
    lam = (jnp.exp(jnp.sum(lm[0:1] * lm[1:2], axis=-1, keepdims=True))
           - jnp.exp(jnp.sum(lm[2:3] * lm[3:4], axis=-1, keepdims=True)) + LAMBDA_INIT)
    for qt in range(2):
        for h in range(DIFF_HEADS):
            c = (qt * DIFF_HEADS + h) * 2
            a = _normalized(acc_ref, c) - lam * _normalized(acc_ref, c + 1)
            o_ref[0, qt * tq:(qt + 1) * tq, h * LANES:(h + 1) * LANES] = (
                _rms(a, g_ref[...]) * (1.0 - LAMBDA_INIT)).astype(BF16)


def _diff_attention(bias, lam_rows, main, subln):
    bsz, seq, _ = main.shape
    tq, tk = TQ, TK
    return pl.pallas_call(
        functools.partial(_diff_kernel, tq=tq, tk=tk),
        grid=(bsz, seq // (2 * tq)),
        in_specs=[pl.BlockSpec((DIFF_HEADS, 3, tq, tk), lambda b, i: (NSA_HEADS // DIFF_HEADS, 0, 0, 0),
                               pipeline_mode=pl.Buffered(1)),
                  pl.BlockSpec((8, LANES), lambda b, i: (0, 0)),
                  pl.BlockSpec((1, 2 * tq, DIFF_W), lambda b, i: (b, i, COL_DQ // DIFF_W)),
                  pl.BlockSpec((1, seq, DIFF_W), lambda b, i: (b, 0, COL_DK // DIFF_W)),
                  pl.BlockSpec((1, seq, DIFF_W), lambda b, i: (b, 0, COL_DV // DIFF_W)),
                  pl.BlockSpec((1, LANES), lambda b, i: (0, 0))],
        out_specs=pl.BlockSpec((1, 2 * tq, DIFF_W), lambda b, i: (b, i, 0)),
        out_shape=jax.ShapeDtypeStruct((bsz, seq, DIFF_W), BF16),
        scratch_shapes=[pltpu.VMEM((N_CHAINS, tq, LANES), F32),
                        pltpu.VMEM((N_CHAINS, tq, 2 * LANES), F32)],
        compiler_params=_params(("parallel", "parallel")),
    )(bias, lam_rows, main, main, main, subln)


FF_CHUNK = 256


def _tail_kernel(x_ref, od_ref, oc_ref, os_ref, ow_ref, gn_ref, wo_ref, g2_ref,
                 wg_ref, wu_ref, wd_ref, gf_ref, o_ref, acc_ref):
    o_nsa = oc_ref[...].astype(F32) + os_ref[...].astype(F32) + ow_ref[...].astype(F32)
    o_nsa = _rms(o_nsa, gn_ref[...]).astype(BF16)
    attn = _dot(od_ref[...], wo_ref[0:DIFF_W, :]) + _dot(o_nsa, wo_ref[DIFF_W:DIFF_W + NSA_W, :])
    x1 = x_ref[...] + attn
    acc_ref[...] = x1
    h = _rms(x1, g2_ref[...]).astype(BF16)
    for c in range(0, D_FF, FF_CHUNK):
        gate = _dot(h, wg_ref[:, c:c + FF_CHUNK])
        up = _dot(h, wu_ref[:, c:c + FF_CHUNK])
        act = (gate * (1.0 / (1.0 + jnp.exp(-gate))) * up).astype(BF16)
        acc_ref[...] += _dot(act, wd_ref[c:c + FF_CHUNK, :])
    o_ref[...] = _rms(acc_ref[...], gf_ref[...])


def _tail(x2, od, oc, os_, ow, gn, w_out, g2, wg, wu, wd, gf):
    n = x2.shape[0]
    row = lambda i: (i, 0)
    fixed = lambda i: (0, 0)
    once = pl.Buffered(1)
    return pl.pallas_call(
        _tail_kernel,
        grid=(n // TM,),
        in_specs=[pl.BlockSpec((TM, D_MODEL), row),
                  pl.BlockSpec((TM, DIFF_W), row),
                  pl.BlockSpec((TM, NSA_W), row),
                  pl.BlockSpec((TM, NSA_W), row),
                  pl.BlockSpec((TM, NSA_W), row),
                  pl.BlockSpec((1, NSA_W), fixed),
                  pl.BlockSpec((DIFF_W + NSA_W, D_MODEL), fixed, pipeline_mode=once),
                  pl.BlockSpec((1, D_MODEL), fixed),
                  pl.BlockSpec((D_MODEL, D_FF), fixed, pipeline_mode=once),
                  pl.BlockSpec((D_MODEL, D_FF), fixed, pipeline_mode=once),
                  pl.BlockSpec((D_FF, D_MODEL), fixed, pipeline_mode=once),
                  pl.BlockSpec((1, D_MODEL), fixed)],
        out_specs=pl.BlockSpec((TM, D_MODEL), row),
        out_shape=jax.ShapeDtypeStruct((n, D_MODEL), F32),
        scratch_shapes=[pltpu.VMEM((TM, D_MODEL), F32)],
        compiler_params=_params(("parallel",)),
    )(x2, od, oc, os_, ow, gn, w_out, g2, wg, wu, wd, gf)


def _nsa_perm():
    p = np.arange(NSA_W)
    j, g, d = p // LANES, (p % LANES) // NSA_HD, p % NSA_HD
    return (g * NSA_GQ + j) * NSA_HD + d


def _overlap_matrices(nblk):
    n_cmp = nblk - 1
    nslc = nblk * CMP_STRIDE // SLC_BLOCK
    assert nslc == N_SLC
    cmp_start = np.arange(n_cmp) * CMP_STRIDE
    cmp_end = cmp_start + CMP_LEN - 1
    slc_start = np.arange(nslc) * SLC_BLOCK
    overlap = ((cmp_start[:, None] < slc_start[None, :] + SLC_BLOCK) & (cmp_end[:, None] >= slc_start[None, :]))
    ovt = np.zeros((NSA_KV, LANES, nblk), np.float32)
    for g, base in enumerate((HALF, 0)):
        ovt[g, base:base + nslc, :n_cmp] = overlap.T
    return ovt


def _compress_weights(pos, w1, w2):
    half = CMP_LEN // 2
    shape = (2, half, NSA_KV, NSA_HD, CMP_HIDDEN)
    w1rep = jnp.broadcast_to(w1.reshape(2, half, 1, NSA_HD, CMP_HIDDEN), shape).reshape(2, -1, CMP_HIDDEN)
    w2big = jnp.einsum('jd,pg->pjgd', w2, jnp.eye(NSA_KV, dtype=F32)).reshape(NSA_KV * CMP_HIDDEN, NSA_KV * NSA_HD)
    posr = jnp.broadcast_to(pos.reshape(2, half, 1, NSA_HD), (2, half, NSA_KV, NSA_HD)).reshape(2, -1)
    return posr, w1rep.astype(BF16), w2big.astype(BF16)


def kernel(x, norm1, w_in, lambda_q1, lambda_k1, lambda_q2, lambda_k2, diff_subln, cmp_pos_k, cmp_w1_k, cmp_w2_k,
           cmp_pos_v, cmp_w1_v, cmp_w2_v, nsa_norm, w_out, norm2, w_gate, w_up, w_down, rel_bias, final_norm):
    bsz, seq, dm = x.shape
    assert dm == D_MODEL and seq % (2 * TQ) == 0 and seq % TM == 0 and seq == N_SLC * SLC_BLOCK
    assert norm1.shape[0] == 1, "single layer"
    perm = _nsa_perm()
    x2 = x.reshape(bsz * seq, dm)

    w = w_in[0]
    zhalf = jnp.zeros((dm, HALF), F32)
    w_p = jnp.concatenate([w[:, :512] * QSCALE, w[:, 512:1536], w[:, 1536 + perm] * QSCALE,
                           w[:, 2304:2368], zhalf, zhalf, w[:, 2368:2432],
                           w[:, 2432:2816],
                           w[:, 2048:2304],
                           w[:, 2816:N_IN], jnp.zeros((dm, LANES - (N_IN - 2816)), F32)], axis=1).astype(BF16)
    assert w_p.shape[1] == N_PROJ
    wo = w_out[0]
    wo_p = jnp.concatenate([wo[:DIFF_W], wo[DIFF_W + perm]], axis=0).astype(BF16)
    gn_p = nsa_norm[0][perm].reshape(1, NSA_W)
    posk, w1k, w2k = _compress_weights(cmp_pos_k[0], cmp_w1_k[0], cmp_w2_k[0])
    posv, w1v, w2v = _compress_weights(cmp_pos_v[0], cmp_w1_v[0], cmp_w2_v[0])
    pos = jnp.concatenate([posk, posv, jnp.zeros((4, posk.shape[1]), F32)], axis=0)
    lam_rows = jnp.zeros((8, LANES), F32).at[0:4, 0:DIFF_HD].set(
        jnp.stack([lambda_q1[0], lambda_k1[0], lambda_q2[0], lambda_k2[0]]).astype(F32))

    main, kc_grp, vc_grp, gates = _inproj(x2, norm1[0].reshape(1, dm), w_p, seq)
    main = main.reshape(bsz, seq, N_MAIN)

    nrow = seq // CMP_STRIDE
    kc, vc = _compress(kc_grp.reshape(bsz, nrow, CMP_STRIDE * LANES), vc_grp.reshape(bsz, nrow, CMP_STRIDE * LANES),
                       pos, w1k, w2k, w1v, w2v)
    ovt = jnp.asarray(_overlap_matrices(nrow), BF16)
    gates = gates.reshape(bsz, seq, LANES)
    o_cmp, nsel = _cmp_attention(main, kc, vc, ovt, gates)
    bias = _bias_tiles(rel_bias)
    o_slc = _slc_attention(bias, main, nsel, gates)
    o_win = _win_attention(bias, main, gates)
    o_diff = _diff_attention(bias, lam_rows, main, diff_subln[0].reshape(1, DIFF_VD))

    n = bsz * seq
    out = _tail(x2, o_diff.reshape(n, DIFF_W), o_cmp.reshape(n, NSA_W), o_slc.reshape(n, NSA_W),
                o_win.reshape(n, NSA_W), gn_p, wo_p, norm2[0].reshape(1, dm),
                w_gate[0].astype(BF16), w_up[0].astype(BF16), w_down[0].astype(BF16), final_norm.reshape(1, dm))
    return out.reshape(bsz, seq, dm)
```

```python
import functools
import math

import numpy as np
import jax
import jax.numpy as jnp
from jax import lax
from jax.experimental import pallas as pl
from jax.experimental.pallas import tpu as pltpu

F32 = jnp.float32
BF16 = jnp.bfloat16

D_MODEL = 1024
DIFF_HEADS = 4
DIFF_HD = 64
DIFF_VD = 128
DIFF_W = 512
NSA_HEADS = 8
NSA_KV = 2
NSA_GQ = 4
NSA_HD = 64
NSA_W = 512
CMP_LEN = 32
CMP_STRIDE = 16
CMP_HIDDEN = 256
SLC_BLOCK = 64
SLC_TOPK = 16
N_SLC = 32
WINDOW = 512
N_BUCKETS = 32
MAX_DISTANCE = 128
D_FF = 2816
N_IN = 2840
NEG = -1e30
EPS = 1e-6
LAMBDA_INIT = 0.8 - 0.6 * math.exp(-0.3 * 0)
LOG2E = math.log2(math.e)
QSCALE = NSA_HD ** -0.5 * LOG2E

LANES = 128
HALF = LANES // 2
MXU_N = 256
COL_DQ, COL_DK, COL_DV, COL_NQ, COL_KSL, COL_VSL, COL_KW, COL_VW, N_MAIN = (
    0, 512, 1024, 1536, 2048, 2304, 2432, 2560, 2688)
N_PROJ = N_MAIN + 3 * LANES
VMEM_LIMIT = 48 * 1024 * 1024

TQ = 256
TK = 256
TM = 512
TQ_CMP = 1024


def _t5_thresholds():
    d = np.arange(0, 4 * MAX_DISTANCE)
    max_exact = N_BUCKETS // 2
    val = (np.log(np.maximum(d, 1).astype(np.float32) / np.float32(max_exact))
           / np.float32(math.log(MAX_DISTANCE / max_exact)) * np.float32(N_BUCKETS - max_exact))
    large = np.minimum(max_exact + val.astype(np.int32), N_BUCKETS - 1)
    bucket = np.where(d < max_exact, d, large)
    assert np.all(np.diff(bucket) >= 0) and bucket[-1] == N_BUCKETS - 1
    return [int(np.argmax(bucket >= k)) for k in range(N_BUCKETS)]


T5_THRESH = _t5_thresholds()
assert T5_THRESH[-1] <= TK + 1
assert WINDOW == 2 * TK and TQ == TK


def _nt_dot(a, b):
    return lax.dot_general(a, b, (((1,), (1,)), ((), ())), preferred_element_type=F32)


def _dot(a, b):
    return jnp.dot(a, b, preferred_element_type=F32)


def _rms(x, g):
    ms = jnp.mean(x * x, axis=-1, keepdims=True)
    return x * lax.rsqrt(ms + EPS) * g


def _params(sem):
    return pltpu.CompilerParams(dimension_semantics=sem, vmem_limit_bytes=VMEM_LIMIT)


def _store_grouped(out_ref, tok_ref, y):
    tok_ref[...] = y
    nrow = out_ref.shape[0]
    for l in range(CMP_STRIDE):
        out_ref[:, l * LANES:(l + 1) * LANES] = tok_ref[pl.ds(l, nrow, stride=CMP_STRIDE), :].astype(BF16)


def _inproj_kernel(x_ref, g_ref, w_ref, main_ref, kc_ref, vc_ref, gate_ref, tok_ref, *, seq):
    tm = x_ref.shape[0]
    h = _rms(x_ref[...], g_ref[...]).astype(BF16)
    for c in range(0, N_PROJ, MXU_N):
        y = _dot(h, w_ref[:, c:c + MXU_N])
        if c == COL_KSL:
            t0 = lax.rem(pl.program_id(0) * tm, seq)
            blk = (t0 + lax.broadcasted_iota(jnp.int32, (tm, 2 * LANES), 0)) >> 6
            lane = lax.broadcasted_iota(jnp.int32, (tm, 2 * LANES), 1)
            slot = jnp.where(lane < LANES, lane - HALF, lane - LANES)
            y = y + jnp.where((slot == blk) & (slot >= 0) & (slot < N_SLC), NEG, 0.0)
        if c + MXU_N <= N_MAIN:
            main_ref[:, c:c + MXU_N] = y.astype(BF16)
        elif c == COL_VW:
            main_ref[:, c:c + LANES] = y[:, 0:LANES].astype(BF16)
            _store_grouped(kc_ref, tok_ref, y[:, LANES:MXU_N])
        else:
            _store_grouped(vc_ref, tok_ref, y[:, 0:LANES])
            gate_ref[...] = 1.0 / (1.0 + jnp.exp(-y[:, LANES:MXU_N]))


def _inproj(x2, g1, w_p, seq):
    n = x2.shape[0]
    return pl.pallas_call(
        functools.partial(_inproj_kernel, seq=seq),
        grid=(n // TM,),
        in_specs=[pl.BlockSpec((TM, D_MODEL), lambda i: (i, 0)),
                  pl.BlockSpec((1, D_MODEL), lambda i: (0, 0)),
                  pl.BlockSpec((D_MODEL, N_PROJ), lambda i: (0, 0))],
        out_specs=[pl.BlockSpec((TM, N_MAIN), lambda i: (i, 0)),
                   pl.BlockSpec((TM // CMP_STRIDE, CMP_STRIDE * LANES), lambda i: (i, 0)),
                   pl.BlockSpec((TM // CMP_STRIDE, CMP_STRIDE * LANES), lambda i: (i, 0)),
                   pl.BlockSpec((TM, LANES), lambda i: (i, 0))],
        out_shape=[jax.ShapeDtypeStruct((n, N_MAIN), BF16),
                   jax.ShapeDtypeStruct((n // CMP_STRIDE, CMP_STRIDE * LANES), BF16),
                   jax.ShapeDtypeStruct((n // CMP_STRIDE, CMP_STRIDE * LANES), BF16),
                   jax.ShapeDtypeStruct((n, LANES), F32)],
        scratch_shapes=[pltpu.VMEM((TM, LANES), F32)],
        compiler_params=_params(("parallel",)),
    )(x2, g1, w_p)


def _gelu_tanh(x):
    return x * (0.5 * (1.0 + jnp.tanh(math.sqrt(2.0 / math.pi) * (x + 0.044715 * (x * x * x)))))


def _compress_kernel(rk_ref, rv_ref, pos_ref, w1k_ref, w2k_ref, w1v_ref, w2v_ref, kc_ref, vc_ref):
    width = rk_ref.shape[2]
    lane = lax.broadcasted_iota(jnp.int32, (1, width), 1)
    in_group = [jnp.where(((lane >> 6) & 1) == g, 1.0, 0.0) for g in range(NSA_KV)]

    def comp(r_ref, p_top, p_bot, w1_ref, w2_ref):
        r = r_ref[0].astype(F32)
        hid = []
        for g in range(NSA_KV):
            top = _dot(((r + p_top) * in_group[g]).astype(BF16), w1_ref[0])
            bot = _dot(((r + p_bot) * in_group[g]).astype(BF16), w1_ref[1])
            nrow = top.shape[0]
            hid.append(top + pltpu.roll(bot, nrow - 1, 0))
        act = _gelu_tanh(jnp.concatenate(hid, axis=1)).astype(BF16)
        return _dot(act, w2_ref[...]).astype(BF16)

    kc_ref[0] = comp(rk_ref, pos_ref[0:1, :], pos_ref[1:2, :], w1k_ref, w2k_ref)
    vc_ref[0] = comp(rv_ref, pos_ref[2:3, :], pos_ref[3:4, :], w1v_ref, w2v_ref)


def _compress(rk, rv, pos, w1k, w2k, w1v, w2v):
    bsz, nrow, width = rk.shape
    hid2 = 2 * CMP_HIDDEN
    return pl.pallas_call(
        _compress_kernel,
        grid=(bsz,),
        in_specs=[pl.BlockSpec((1, nrow, width), lambda b: (b, 0, 0)),
                  pl.BlockSpec((1, nrow, width), lambda b: (b, 0, 0)),
                  pl.BlockSpec((8, width), lambda b: (0, 0)),
                  pl.BlockSpec((2, width, CMP_HIDDEN), lambda b: (0, 0, 0)),
                  pl.BlockSpec((hid2, LANES), lambda b: (0, 0)),
                  pl.BlockSpec((2, width, CMP_HIDDEN), lambda b: (0, 0, 0)),
                  pl.BlockSpec((hid2, LANES), lambda b: (0, 0))],
        out_specs=[pl.BlockSpec((1, nrow, LANES), lambda b: (b, 0, 0)),
                   pl.BlockSpec((1, nrow, LANES), lambda b: (b, 0, 0))],
        out_shape=[jax.ShapeDtypeStruct((bsz, nrow, LANES), BF16),
                   jax.ShapeDtypeStruct((bsz, nrow, LANES), BF16)],
        compiler_params=_params(("parallel",)),
    )(rk, rv, pos, w1k, w2k, w1v, w2v)


def _half_mask(g):
    lane = lax.broadcasted_iota(jnp.int32, (1, LANES), 1)
    return jnp.where((lane >= g * HALF) & (lane < (g + 1) * HALF), 1.0, 0.0).astype(BF16)


def _group_queries(q, nslab):
    return [jnp.concatenate([q[:, j * LANES:(j + 1) * LANES] * _half_mask(g) for j in range(nslab)], axis=0)
            for g in range(2)]


def _nsa_chains(q_ref, extra_ref, tq):
    chains = []
    for qt in range(2):
        rows = slice(qt * tq, (qt + 1) * tq)
        for g in range(NSA_KV):
            keep = _half_mask(g)
            for j in range(NSA_GQ):
                q = q_ref[0, rows, j * LANES:(j + 1) * LANES] * keep
                if extra_ref is not None:
                    q = q + extra_ref[0, rows, g * LANES:(g + 1) * LANES]
                chains.append((q, qt, g, j))
    return chains


def _bias_kernel(rel_ref, o_ref, *, tq, tk):
    row = pl.program_id(0)
    col = jnp.where(row < NSA_HEADS, row + DIFF_HEADS, row - NSA_HEADS)
    r = lax.broadcasted_iota(jnp.int32, (tq, tk), 0)
    c = lax.broadcasted_iota(jnp.int32, (tq, tk), 1)
    far = rel_ref[N_BUCKETS - 1, col]
    for d in range(2):
        dist = r - c + d * tq
        val = jnp.zeros((tq, tk), F32) + rel_ref[0, col]
        for k in range(1, N_BUCKETS):
            val = jnp.where(dist >= T5_THRESH[k], rel_ref[k, col], val)
        val = (val - far) * LOG2E
        if d == 0:
            val = jnp.where(dist >= 0, val, NEG)
        o_ref[0, d] = val
    o_ref[0, 2] = jnp.where(r < c, 0.0, NEG)


def _bias_tiles(rel_bias):
    nhead = rel_bias.shape[1]
    return pl.pallas_call(
        functools.partial(_bias_kernel, tq=TQ, tk=TK),
        grid=(nhead,),
        in_specs=[pl.BlockSpec(memory_space=pltpu.SMEM)],
        out_specs=pl.BlockSpec((1, 3, TQ, TK), lambda h: (h, 0, 0, 0)),
        out_shape=jax.ShapeDtypeStruct((nhead, 3, TQ, TK), F32),
        compiler_params=_params(("parallel",)),
    )(rel_bias)


def _with_near_bias(s, near):
    add = near[0] if len(near) == 1 else jnp.concatenate(near, axis=1)
    width, nw = s.shape[1], add.shape[1]
    if nw == width:
        return s + add
    return jnp.concatenate([s[:, 0:width - nw], s[:, width - nw:width] + add], axis=1)


def _softmax_step(s, v, m_ref, acc_ref):
    tk = s.shape[1]
    m_prev = m_ref[...]
    m_new = jnp.maximum(m_prev, jnp.max(s, axis=-1, keepdims=True))
    alpha = jnp.exp2(m_prev - m_new)
    p = jnp.exp2(s - jnp.concatenate([m_new] * (tk // LANES), axis=1))
    vaug = jnp.concatenate([v, jnp.ones((tk, LANES), BF16)], axis=1)
    acc_ref[...] = jnp.concatenate([alpha, alpha], axis=1) * acc_ref[...] + _dot(p.astype(BF16), vaug)
    m_ref[...] = m_new


def _softmax_first(s, v, m_ref, acc_ref):
    rows, tk = s.shape
    m = jnp.broadcast_to(jnp.max(s, axis=-1, keepdims=True), (rows, LANES))
    p = jnp.exp2(s - jnp.concatenate([m] * (tk // LANES), axis=1))
    vaug = jnp.concatenate([v, jnp.ones((tk, LANES), BF16)], axis=1)
    acc_ref[...] = _dot(p.astype(BF16), vaug)
    m_ref[...] = m


def _softmax_once(s, v):
    rows, tk = s.shape
    m = jnp.broadcast_to(jnp.max(s, axis=-1, keepdims=True), (rows, LANES))
    p = jnp.exp2(s - jnp.concatenate([m] * (tk // LANES), axis=1))
    vaug = jnp.concatenate([v, jnp.ones((tk, LANES), BF16)], axis=1)
    pv = _dot(p.astype(BF16), vaug)
    return pv[:, 0:LANES] * (1.0 / pv[:, LANES:2 * LANES])


def _causal_flash(i, tq, tk, chains, kv_of, bias_of, m_ref, acc_ref):
    def tail(first):
        kv = {}
        for c, (q, qt, kv_id, bias_id) in enumerate(chains):
            ntile = (1 if first else 3) + qt
            if (kv_id, ntile) not in kv:
                start = 0 if first else pl.multiple_of((i - 1) * 2 * tk, 2 * tk)
                kv[(kv_id, ntile)] = kv_of(kv_id, start, ntile * tk)
            k, v = kv[(kv_id, ntile)]
            near = [bias_of(bias_id, d) for d in range(min(ntile, 2) - 1, -1, -1)]
            _softmax_first(_with_near_bias(_nt_dot(q, k), near), v, m_ref.at[c], acc_ref.at[c])

    @pl.when(i == 0)
    def _():
        tail(True)

    @pl.when(i >= 1)
    def _():
        tail(False)

    def far(j, carry):
        start = pl.multiple_of(j * 2 * tk, 2 * tk)
        kv = {}
        for c, (q, _, kv_id, _) in enumerate(chains):
            if kv_id not in kv:
                kv[kv_id] = kv_of(kv_id, start, 2 * tk)
            k, v = kv[kv_id]
            _softmax_step(_nt_dot(q, k), v, m_ref.at[c], acc_ref.at[c])
        return carry

    lax.fori_loop(0, i - 1, far, 0)


def _normalized(acc_ref, c):
    return acc_ref[c, :, 0:LANES] * (1.0 / acc_ref[c, :, LANES:2 * LANES])


def _store_gated(o_ref, gate_ref, o_of, branch, tq, qt):
    lane = lax.broadcasted_iota(jnp.int32, (1, LANES), 1)
    rows = slice(qt * tq, (qt + 1) * tq)
    gates = gate_ref[0, rows, :]
    for j in range(NSA_GQ):
        c0 = (0 * NSA_GQ + j) * 3 + branch
        c1 = (1 * NSA_GQ + j) * 3 + branch
        gexp = jnp.where(lane < HALF, gates[:, c0:c0 + 1], gates[:, c1:c1 + 1])
        slab = jnp.where(lane < HALF, o_of(0, j), o_of(1, j))
        o_ref[0, rows, j * LANES:(j + 1) * LANES] = (gexp * slab).astype(BF16)


def _nsa_chain_index(qt, g, j):
    return (qt * NSA_KV + g) * NSA_GQ + j


N_CHAINS = 2 * NSA_HEADS


def _cmp_kernel(q_ref, kc_ref, vc_ref, ovt_ref, gate_ref, o_ref, nsel_ref, *, tq):
    qi = pl.program_id(1)
    nblk = kc_ref.shape[1]
    lane = lax.broadcasted_iota(jnp.int32, (1, LANES), 1)
    n = lax.broadcasted_iota(jnp.int32, (1, nblk), 1)
    t4 = qi * tq + (lax.broadcasted_iota(jnp.int32, (NSA_GQ * tq, 1), 0) & (tq - 1))
    vis4 = (n * CMP_STRIDE + (CMP_LEN - 1)) <= t4
    has4 = jnp.where(t4 >= CMP_LEN - 1, 1.0, 0.0)
    qg = _group_queries(q_ref[0], NSA_GQ)
    kc = kc_ref[0]
    vc = vc_ref[0]
    outs, imp_t = [], jnp.zeros((LANES, tq), F32)
    for g in range(NSA_KV):
        s = jnp.where(vis4, _nt_dot(qg[g], kc), NEG)
        e = jnp.exp2(s - jnp.max(s, axis=-1, keepdims=True))
        p = e * (has4 / jnp.sum(e, axis=-1, keepdims=True))
        outs.append(_dot(p.astype(BF16), vc))
        ps = p[0:tq] + p[tq:2 * tq] + p[2 * tq:3 * tq] + p[3 * tq:4 * tq]
        hi = ps.astype(BF16)
        r1 = ps - hi.astype(F32)
        mid = r1.astype(BF16)
        lo = (r1 - mid.astype(F32)).astype(BF16)
        imp_t = imp_t + (_nt_dot(ovt_ref[g], hi) + _nt_dot(ovt_ref[g], mid) + _nt_dot(ovt_ref[g], lo))
    _store_gated(o_ref, gate_ref, lambda g, j: outs[g][j * tq:(j + 1) * tq], 0, tq, 0)

    tl = qi * tq + lax.broadcasted_iota(jnp.int32, (N_SLC, tq), 1)
    jrow = lax.broadcasted_iota(jnp.int32, (N_SLC, tq), 0)
    cur = tl >> 6
    valid = (jrow * SLC_BLOCK) <= tl
    forced = (jrow == 0) | (jrow == cur) | (jrow == cur - 1)
    sub = 8
    zero = jnp.zeros((N_SLC, tq), F32)
    dropped = []
    for base in (0, HALF):
        x = jnp.where(forced, jnp.inf, jnp.where(valid, imp_t[base:base + N_SLC], -jnp.inf))
        others = [jnp.broadcast_to(x[jj:jj + 1], (sub, tq)) for jj in range(N_SLC)]
        chunks = []
        for r0 in range(0, N_SLC, sub):
            xr = x[r0:r0 + sub]
            jr = r0 + lax.broadcasted_iota(jnp.int32, (sub, tq), 0)
            cnt = jnp.zeros((sub, tq), F32)
            for jj in range(N_SLC):
                if jj < r0:
                    beats = others[jj] >= xr
                elif jj >= r0 + sub:
                    beats = others[jj] > xr
                else:
                    beats = (others[jj] > xr) | ((others[jj] == xr) & (jr > jj))
                cnt = cnt + jnp.where(beats, 1.0, 0.0)
            chunks.append(jnp.where(cnt < SLC_TOPK, 0.0, 1.0))
        dropped.append(jnp.concatenate(chunks, axis=0))
    drop = jnp.concatenate([dropped[0], zero, dropped[1], zero], axis=0).T
    nsel_ref[0, :, 0:LANES] = jnp.where(lane >= HALF, drop, 0.0).astype(BF16)
    nsel_ref[0, :, LANES:2 * LANES] = jnp.where(lane < HALF, drop, 0.0).astype(BF16)


def _cmp_attention(main, kc, vc, ovt, gates):
    bsz, seq, _ = main.shape
    nblk = kc.shape[1]
    tq = TQ_CMP
    return pl.pallas_call(
        functools.partial(_cmp_kernel, tq=tq),
        grid=(bsz, seq // tq),
        in_specs=[pl.BlockSpec((1, tq, NSA_W), lambda b, i: (b, i, COL_NQ // NSA_W)),
                  pl.BlockSpec((1, nblk, LANES), lambda b, i: (b, 0, 0)),
                  pl.BlockSpec((1, nblk, LANES), lambda b, i: (b, 0, 0)),
                  pl.BlockSpec((2, LANES, nblk), lambda b, i: (0, 0, 0)),
                  pl.BlockSpec((1, tq, LANES), lambda b, i: (b, i, 0))],
        out_specs=[pl.BlockSpec((1, tq, NSA_W), lambda b, i: (b, i, 0)),
                   pl.BlockSpec((1, tq, 2 * LANES), lambda b, i: (b, i, 0))],
        out_shape=[jax.ShapeDtypeStruct((bsz, seq, NSA_W), BF16),
                   jax.ShapeDtypeStruct((bsz, seq, 2 * LANES), BF16)],
        compiler_params=_params(("parallel", "parallel")),
    )(main, kc, vc, ovt, gates)


def _slc_kernel(bias_ref, q_ref, k_ref, v_ref, nsel_ref, gate_ref, o_ref, m_ref, acc_ref, *, tq, tk):
    i = pl.program_id(1)
    chains = [(q, qt, g, g * NSA_GQ + j) for q, qt, g, j in _nsa_chains(q_ref, nsel_ref, tq)]

    def kv_of(g, start, width):
        return (k_ref[0, pl.ds(start, width), g * LANES:(g + 1) * LANES], v_ref[0, pl.ds(start, width), :])

    _causal_flash(i, tq, tk, chains, kv_of, lambda h, d: bias_ref[h, d], m_ref, acc_ref)
    for qt in range(2):
        _store_gated(o_ref, gate_ref, lambda g, j: _normalized(acc_ref, _nsa_chain_index(qt, g, j)), 1, tq, qt)


def _slc_attention(bias, main, nsel, gates):
    bsz, seq, _ = main.shape
    tq, tk = TQ, TK
    return pl.pallas_call(
        functools.partial(_slc_kernel, tq=tq, tk=tk),
        grid=(bsz, seq // (2 * tq)),
        in_specs=[pl.BlockSpec((NSA_HEADS, 3, tq, tk), lambda b, i: (0, 0, 0, 0), pipeline_mode=pl.Buffered(1)),
                  pl.BlockSpec((1, 2 * tq, NSA_W), lambda b, i: (b, i, COL_NQ // NSA_W)),
                  pl.BlockSpec((1, seq, 2 * LANES), lambda b, i: (b, 0, COL_KSL // (2 * LANES))),
                  pl.BlockSpec((1, seq, LANES), lambda b, i: (b, 0, COL_VSL // LANES)),
                  pl.BlockSpec((1, 2 * tq, 2 * LANES), lambda b, i: (b, i, 0)),
                  pl.BlockSpec((1, 2 * tq, LANES), lambda b, i: (b, i, 0))],
        out_specs=pl.BlockSpec((1, 2 * tq, NSA_W), lambda b, i: (b, i, 0)),
        out_shape=jax.ShapeDtypeStruct((bsz, seq, NSA_W), BF16),
        scratch_shapes=[pltpu.VMEM((N_CHAINS, tq, LANES), F32),
                        pltpu.VMEM((N_CHAINS, tq, 2 * LANES), F32)],
        compiler_params=_params(("parallel", "parallel")),
    )(bias, main, main, main, nsel, gates)


def _win_kernel(bias_ref, q_ref, k_ref, v_ref, gate_ref, o_ref, out_ref, *, tq, tk):
    i = pl.program_id(1)
    chains = _nsa_chains(q_ref, None, tq)

    def window(first):
        kv = {}
        for q, qt, g, j in chains:
            ntile = 1 + qt if first else 3
            if (qt, ntile) not in kv:
                start = 0 if first else pl.multiple_of((2 * i - 2 + qt) * tk, tk)
                kv[(qt, ntile)] = (k_ref[0, pl.ds(start, ntile * tk), :], v_ref[0, pl.ds(start, ntile * tk), :])
            k, v = kv[(qt, ntile)]
            near = [bias_ref[g * NSA_GQ + j, d] for d in range(min(ntile, 2) - 1, -1, -1)]
            if ntile == 3:
                near.insert(0, bias_ref[g * NSA_GQ + j, 2])
            out_ref[_nsa_chain_index(qt, g, j)] = _softmax_once(_with_near_bias(_nt_dot(q, k), near), v)

    @pl.when(i == 0)
    def _():
        window(True)

    @pl.when(i >= 1)
    def _():
        window(False)

    for qt in range(2):
        _store_gated(o_ref, gate_ref, lambda g, j: out_ref[_nsa_chain_index(qt, g, j)], 2, tq, qt)


def _win_attention(bias, main, gates):
    bsz, seq, _ = main.shape
    tq, tk = TQ, TK
    return pl.pallas_call(
        functools.partial(_win_kernel, tq=tq, tk=tk),
        grid=(bsz, seq // (2 * tq)),
        in_specs=[pl.BlockSpec((NSA_HEADS, 3, tq, tk), lambda b, i: (0, 0, 0, 0), pipeline_mode=pl.Buffered(1)),
                  pl.BlockSpec((1, 2 * tq, NSA_W), lambda b, i: (b, i, COL_NQ // NSA_W)),
                  pl.BlockSpec((1, seq, LANES), lambda b, i: (b, 0, COL_KW // LANES)),
                  pl.BlockSpec((1, seq, LANES), lambda b, i: (b, 0, COL_VW // LANES)),
                  pl.BlockSpec((1, 2 * tq, LANES), lambda b, i: (b, i, 0))],
        out_specs=pl.BlockSpec((1, 2 * tq, NSA_W), lambda b, i: (b, i, 0)),
        out_shape=jax.ShapeDtypeStruct((bsz, seq, NSA_W), BF16),
        scratch_shapes=[pltpu.VMEM((N_CHAINS, tq, LANES), F32)],
        compiler_params=_params(("parallel", "parallel")),
    )(bias, main, main, main, gates)


def _diff_kernel(bias_ref, lam_ref, q_ref, k_ref, v_ref, g_ref, o_ref, m_ref, acc_ref, *, tq, tk):
    i = pl.program_id(1)
    chains = []
    for qt in range(2):
        for h in range(DIFF_HEADS):
            slab = q_ref[0, qt * tq:(qt + 1) * tq, h * LANES:(h + 1) * LANES]
            for half in range(2):
                chains.append((slab * _half_mask(half), qt, h, h))

    def kv_of(h, start, width):
        cols = slice(h * LANES, (h + 1) * LANES)
        return (k_ref[0, pl.ds(start, width), cols], v_ref[0, pl.ds(start, width), cols])

    _causal_flash(i, tq, tk, chains, kv_of, lambda h, d: bias_ref[h, d], m_ref, acc_ref)

    lm = lam_ref[...]
    lam = (jnp.exp(jnp.sum(lm[0:1] * lm[1:2], axis=-1, keepdims=True))
           - jnp.exp(jnp.sum(lm[2:3] * lm[3:4], axis=-1, keepdims=True)) + LAMBDA_INIT)
    for qt in range(2):
        for h in range(DIFF_HEADS):
            c = (qt * DIFF_HEADS + h) * 2
            a = _normalized(acc_ref, c) - lam * _normalized(acc_ref, c + 1)
            o_ref[0, qt * tq:(qt + 1) * tq, h * LANES:(h + 1) * LANES] = (
                _rms(a, g_ref[...]) * (1.0 - LAMBDA_INIT)).astype(BF16)


def _diff_attention(bias, lam_rows, main, subln):
    bsz, seq, _ = main.shape
    tq, tk = TQ, TK
    return pl.pallas_call(
        functools.partial(_diff_kernel, tq=tq, tk=tk),
        grid=(bsz, seq // (2 * tq)),
        in_specs=[pl.BlockSpec((DIFF_HEADS, 3, tq, tk), lambda b, i: (NSA_HEADS // DIFF_HEADS, 0, 0, 0),
                               pipeline_mode=pl.Buffered(1)),
                  pl.BlockSpec((8, LANES), lambda b, i: (0, 0)),
                  pl.BlockSpec((1, 2 * tq, DIFF_W), lambda b, i: (b, i, COL_DQ // DIFF_W)),
                  pl.BlockSpec((1, seq, DIFF_W), lambda b, i: (b, 0, COL_DK // DIFF_W)),
                  pl.BlockSpec((1, seq, DIFF_W), lambda b, i: (b, 0, COL_DV // DIFF_W)),
                  pl.BlockSpec((1, LANES), lambda b, i: (0, 0))],
        out_specs=pl.BlockSpec((1, 2 * tq, DIFF_W), lambda b, i: (b, i, 0)),
        out_shape=jax.ShapeDtypeStruct((bsz, seq, DIFF_W), BF16),
        scratch_shapes=[pltpu.VMEM((N_CHAINS, tq, LANES), F32),
                        pltpu.VMEM((N_CHAINS, tq, 2 * LANES), F32)],
        compiler_params=_params(("parallel", "parallel")),
    )(bias, lam_rows, main, main, main, subln)


FF_CHUNK = 256


def _tail_kernel(x_ref, od_ref, oc_ref, os_ref, ow_ref, gn_ref, wo_ref, g2_ref,
                 wg_ref, wu_ref, wd_ref, gf_ref, o_ref, acc_ref):
    o_nsa = oc_ref[...].astype(F32) + os_ref[...].astype(F32) + ow_ref[...].astype(F32)
    o_nsa = _rms(o_nsa, gn_ref[...]).astype(BF16)
    attn = _dot(od_ref[...], wo_ref[0:DIFF_W, :]) + _dot(o_nsa, wo_ref[DIFF_W:DIFF_W + NSA_W, :])
    x1 = x_ref[...] + attn
    acc_ref[...] = x1
    h = _rms(x1, g2_ref[...]).astype(BF16)
    for c in range(0, D_FF, FF_CHUNK):
        gate = _dot(h, wg_ref[:, c:c + FF_CHUNK])
        up = _dot(h, wu_ref[:, c:c + FF_CHUNK])
        act = (gate * (1.0 / (1.0 + jnp.exp(-gate))) * up).astype(BF16)
        acc_ref[...] += _dot(act, wd_ref[c:c + FF_CHUNK, :])
    o_ref[...] = _rms(acc_ref[...], gf_ref[...])


def _tail(x2, od, oc, os_, ow, gn, w_out, g2, wg, wu, wd, gf):
    n = x2.shape[0]
    row = lambda i: (i, 0)
    fixed = lambda i: (0, 0)
    once = pl.Buffered(1)
    return pl.pallas_call(
        _tail_kernel,
        grid=(n // TM,),
        in_specs=[pl.BlockSpec((TM, D_MODEL), row),
                  pl.BlockSpec((TM, DIFF_W), row),
                  pl.BlockSpec((TM, NSA_W), row),
                  pl.BlockSpec((TM, NSA_W), row),
                  pl.BlockSpec((TM, NSA_W), row),
                  pl.BlockSpec((1, NSA_W), fixed),
                  pl.BlockSpec((DIFF_W + NSA_W, D_MODEL), fixed, pipeline_mode=once),
                  pl.BlockSpec((1, D_MODEL), fixed),
                  pl.BlockSpec((D_MODEL, D_FF), fixed, pipeline_mode=once),
                  pl.BlockSpec((D_MODEL, D_FF), fixed, pipeline_mode=once),
                  pl.BlockSpec((D_FF, D_MODEL), fixed, pipeline_mode=once),
                  pl.BlockSpec((1, D_MODEL), fixed)],
        out_specs=pl.BlockSpec((TM, D_MODEL), row),
        out_shape=jax.ShapeDtypeStruct((n, D_MODEL), F32),
        scratch_shapes=[pltpu.VMEM((TM, D_MODEL), F32)],
        compiler_params=_params(("parallel",)),
    )(x2, od, oc, os_, ow, gn, w_out, g2, wg, wu, wd, gf)


def _nsa_perm():
    p = np.arange(NSA_W)
    j, g, d = p // LANES, (p % LANES) // NSA_HD, p % NSA_HD
    return (g * NSA_GQ + j) * NSA_HD + d


def _overlap_matrices(nblk):
    n_cmp = nblk - 1
    nslc = nblk * CMP_STRIDE // SLC_BLOCK
    assert nslc == N_SLC
    cmp_start = np.arange(n_cmp) * CMP_STRIDE
    cmp_end = cmp_start + CMP_LEN - 1
    slc_start = np.arange(nslc) * SLC_BLOCK
    overlap = ((cmp_start[:, None] < slc_start[None, :] + SLC_BLOCK) & (cmp_end[:, None] >= slc_start[None, :]))
    ovt = np.zeros((NSA_KV, LANES, nblk), np.float32)
    for g, base in enumerate((HALF, 0)):
        ovt[g, base:base + nslc, :n_cmp] = overlap.T
    return ovt


def _compress_weights(pos, w1, w2):
    half = CMP_LEN // 2
    shape = (2, half, NSA_KV, NSA_HD, CMP_HIDDEN)
    w1rep = jnp.broadcast_to(w1.reshape(2, half, 1, NSA_HD, CMP_HIDDEN), shape).reshape(2, -1, CMP_HIDDEN)
    w2big = jnp.einsum('jd,pg->pjgd', w2, jnp.eye(NSA_KV, dtype=F32)).reshape(NSA_KV * CMP_HIDDEN, NSA_KV * NSA_HD)
    posr = jnp.broadcast_to(pos.reshape(2, half, 1, NSA_HD), (2, half, NSA_KV, NSA_HD)).reshape(2, -1)
    return posr, w1rep.astype(BF16), w2big.astype(BF16)


def kernel(x, norm1, w_in, lambda_q1, lambda_k1, lambda_q2, lambda_k2, diff_subln, cmp_pos_k, cmp_w1_k, cmp_w2_k,
           cmp_pos_v, cmp_w1_v, cmp_w2_v, nsa_norm, w_out, norm2, w_gate, w_up, w_down, rel_bias, final_norm):
    bsz, seq, dm = x.shape
    assert dm == D_MODEL and seq % (2 * TQ) == 0 and seq % TM == 0 and seq == N_SLC * SLC_BLOCK
    assert norm1.shape[0] == 1, "single layer"
    perm = _nsa_perm()
    x2 = x.reshape(bsz * seq, dm)

    w = w_in[0]
    zhalf = jnp.zeros((dm, HALF), F32)
    w_p = jnp.concatenate([w[:, :512] * QSCALE, w[:, 512:1536], w[:, 1536 + perm] * QSCALE,
                           w[:, 2304:2368], zhalf, zhalf, w[:, 2368:2432],
                           w[:, 2432:2816],
                           w[:, 2048:2304],
                           w[:, 2816:N_IN], jnp.zeros((dm, LANES - (N_IN - 2816)), F32)], axis=1).astype(BF16)
    assert w_p.shape[1] == N_PROJ
    wo = w_out[0]
    wo_p = jnp.concatenate([wo[:DIFF_W], wo[DIFF_W + perm]], axis=0).astype(BF16)
    gn_p = nsa_norm[0][perm].reshape(1, NSA_W)
    posk, w1k, w2k = _compress_weights(cmp_pos_k[0], cmp_w1_k[0], cmp_w2_k[0])
    posv, w1v, w2v = _compress_weights(cmp_pos_v[0], cmp_w1_v[0], cmp_w2_v[0])
    pos = jnp.concatenate([posk, posv, jnp.zeros((4, posk.shape[1]), F32)], axis=0)
    lam_rows = jnp.zeros((8, LANES), F32).at[0:4, 0:DIFF_HD].set(
        jnp.stack([lambda_q1[0], lambda_k1[0], lambda_q2[0], lambda_k2[0]]).astype(F32))

    main, kc_grp, vc_grp, gates = _inproj(x2, norm1[0].reshape(1, dm), w_p, seq)
    main = main.reshape(bsz, seq, N_MAIN)

    nrow = seq // CMP_STRIDE
    kc, vc = _compress(kc_grp.reshape(bsz, nrow, CMP_STRIDE * LANES), vc_grp.reshape(bsz, nrow, CMP_STRIDE * LANES),
                       pos, w1k, w2k, w1v, w2v)
    ovt = jnp.asarray(_overlap_matrices(nrow), BF16)
    gates = gates.reshape(bsz, seq, LANES)
    o_cmp, nsel = _cmp_attention(main, kc, vc, ovt, gates)
    bias = _bias_tiles(rel_bias)
    o_slc = _slc_attention(bias, main, nsel, gates)
    o_win = _win_attention(bias, main, gates)
    o_diff = _diff_attention(bias, lam_rows, main, diff_subln[0].reshape(1, DIFF_VD))

    n = bsz * seq
    out = _tail(x2, o_diff.reshape(n, DIFF_W), o_cmp.reshape(n, NSA_W), o_slc.reshape(n, NSA_W),
                o_win.reshape(n, NSA_W), gn_p, wo_p, norm2[0].reshape(1, dm),
                w_gate[0].astype(BF16), w_up[0].astype(BF16), w_down[0].astype(BF16), final_norm.reshape(1, dm))
    return out.reshape(bsz, seq, dm)
```

```python
import functools
import math

import numpy as np
import jax
import jax.numpy as jnp
from jax import lax
from jax.experimental import pallas as pl
from jax.experimental.pallas import tpu as pltpu

F32 = jnp.float32
BF16 = jnp.bfloat16

D_MODEL = 1024
DIFF_HEADS = 4
DIFF_HD = 64
DIFF_VD = 128
DIFF_W = 512
NSA_HEADS = 8
NSA_KV = 2
NSA_GQ = 4
NSA_HD = 64
NSA_W = 512
CMP_LEN = 32
CMP_STRIDE = 16
CMP_HIDDEN = 256
SLC_BLOCK = 64
SLC_TOPK = 16
N_SLC = 32
WINDOW = 512
N_BUCKETS = 32
MAX_DISTANCE = 128
D_FF = 2816
N_IN = 2840
NEG = -1e30
EPS = 1e-6
LAMBDA_INIT = 0.8 - 0.6 * math.exp(-0.3 * 0)
LOG2E = math.log2(math.e)
QSCALE = NSA_HD ** -0.5 * LOG2E

LANES = 128
HALF = LANES // 2
MXU_N = 256
COL_DQ, COL_DK, COL_DV, COL_NQ, COL_KSL, COL_VSL, COL_KW, COL_VW, N_MAIN = (
    0, 512, 1024, 1536, 2048, 2304, 2432, 2560, 2688)
N_PROJ = N_MAIN + 3 * LANES
VMEM_LIMIT = 48 * 1024 * 1024

TQ = 256
TK = 256
TM = 512
TQ_CMP = 2048


def _t5_thresholds():
    d = np.arange(0, 4 * MAX_DISTANCE)
    max_exact = N_BUCKETS // 2
    val = (np.log(np.maximum(d, 1).astype(np.float32) / np.float32(max_exact))
           / np.float32(math.log(MAX_DISTANCE / max_exact)) * np.float32(N_BUCKETS - max_exact))
    large = np.minimum(max_exact + val.astype(np.int32), N_BUCKETS - 1)
    bucket = np.where(d < max_exact, d, large)
    assert np.all(np.diff(bucket) >= 0) and bucket[-1] == N_BUCKETS - 1
    return [int(np.argmax(bucket >= k)) for k in range(N_BUCKETS)]


T5_THRESH = _t5_thresholds()
assert T5_THRESH[-1] <= TK + 1
assert WINDOW == 2 * TK and TQ == TK


def _nt_dot(a, b):
    return lax.dot_general(a, b, (((1,), (1,)), ((), ())), preferred_element_type=F32)


def _dot(a, b):
    return jnp.dot(a, b, preferred_element_type=F32)


def _rms(x, g):
    ms = jnp.mean(x * x, axis=-1, keepdims=True)
    return x * lax.rsqrt(ms + EPS) * g


def _params(sem):
    return pltpu.CompilerParams(dimension_semantics=sem, vmem_limit_bytes=VMEM_LIMIT)


def _store_grouped(out_ref, tok_ref, y):
    tok_ref[...] = y
    nrow = out_ref.shape[0]
    for l in range(CMP_STRIDE):
        out_ref[:, l * LANES:(l + 1) * LANES] = tok_ref[pl.ds(l, nrow, stride=CMP_STRIDE), :].astype(BF16)


def _inproj_kernel(x_ref, g_ref, w_ref, main_ref, kc_ref, vc_ref, gate_ref, tok_ref, *, seq):
    tm = x_ref.shape[0]
    h = _rms(x_ref[...], g_ref[...]).astype(BF16)
    for c in range(0, N_PROJ, MXU_N):
        y = _dot(h, w_ref[:, c:c + MXU_N])
        if c == COL_KSL:
            t0 = lax.rem(pl.program_id(0) * tm, seq)
            blk = (t0 + lax.broadcasted_iota(jnp.int32, (tm, 2 * LANES), 0)) >> 6
            lane = lax.broadcasted_iota(jnp.int32, (tm, 2 * LANES), 1)
            slot = jnp.where(lane < LANES, lane - HALF, lane - LANES)
            y = y + jnp.where((slot == blk) & (slot >= 0) & (slot < N_SLC), NEG, 0.0)
        if c + MXU_N <= N_MAIN:
            main_ref[:, c:c + MXU_N] = y.astype(BF16)
        elif c == COL_VW:
            main_ref[:, c:c + LANES] = y[:, 0:LANES].astype(BF16)
            _store_grouped(kc_ref, tok_ref, y[:, LANES:MXU_N])
        else:
            _store_grouped(vc_ref, tok_ref, y[:, 0:LANES])
            gate_ref[...] = 1.0 / (1.0 + jnp.exp(-y[:, LANES:MXU_N]))


def _inproj(x2, g1, w_p, seq):
    n = x2.shape[0]
    return pl.pallas_call(
        functools.partial(_inproj_kernel, seq=seq),
        grid=(n // TM,),
        in_specs=[pl.BlockSpec((TM, D_MODEL), lambda i: (i, 0)),
                  pl.BlockSpec((1, D_MODEL), lambda i: (0, 0)),
                  pl.BlockSpec((D_MODEL, N_PROJ), lambda i: (0, 0))],
        out_specs=[pl.BlockSpec((TM, N_MAIN), lambda i: (i, 0)),
                   pl.BlockSpec((TM // CMP_STRIDE, CMP_STRIDE * LANES), lambda i: (i, 0)),
                   pl.BlockSpec((TM // CMP_STRIDE, CMP_STRIDE * LANES), lambda i: (i, 0)),
                   pl.BlockSpec((TM, LANES), lambda i: (i, 0))],
        out_shape=[jax.ShapeDtypeStruct((n, N_MAIN), BF16),
                   jax.ShapeDtypeStruct((n // CMP_STRIDE, CMP_STRIDE * LANES), BF16),
                   jax.ShapeDtypeStruct((n // CMP_STRIDE, CMP_STRIDE * LANES), BF16),
                   jax.ShapeDtypeStruct((n, LANES), F32)],
        scratch_shapes=[pltpu.VMEM((TM, LANES), F32)],
        compiler_params=_params(("parallel",)),
    )(x2, g1, w_p)


def _gelu_tanh(x):
    return x * (0.5 * (1.0 + jnp.tanh(math.sqrt(2.0 / math.pi) * (x + 0.044715 * (x * x * x)))))


def _compress_kernel(rk_ref, rv_ref, pos_ref, w1k_ref, w2k_ref, w1v_ref, w2v_ref, kc_ref, vc_ref):
    width = rk_ref.shape[2]
    lane = lax.broadcasted_iota(jnp.int32, (1, width), 1)
    in_group = [jnp.where(((lane >> 6) & 1) == g, 1.0, 0.0) for g in range(NSA_KV)]

    def comp(r_ref, p_top, p_bot, w1_ref, w2_ref):
        r = r_ref[0].astype(F32)
        hid = []
        for g in range(NSA_KV):
            top = _dot(((r + p_top) * in_group[g]).astype(BF16), w1_ref[0])
            bot = _dot(((r + p_bot) * in_group[g]).astype(BF16), w1_ref[1])
            nrow = top.shape[0]
            hid.append(top + pltpu.roll(bot, nrow - 1, 0))
        act = _gelu_tanh(jnp.concatenate(hid, axis=1)).astype(BF16)
        return _dot(act, w2_ref[...]).astype(BF16)

    kc_ref[0] = comp(rk_ref, pos_ref[0:1, :], pos_ref[1:2, :], w1k_ref, w2k_ref)
    vc_ref[0] = comp(rv_ref, pos_ref[2:3, :], pos_ref[3:4, :], w1v_ref, w2v_ref)


def _compress(rk, rv, pos, w1k, w2k, w1v, w2v):
    bsz, nrow, width = rk.shape
    hid2 = 2 * CMP_HIDDEN
    return pl.pallas_call(
        _compress_kernel,
        grid=(bsz,),
        in_specs=[pl.BlockSpec((1, nrow, width), lambda b: (b, 0, 0)),
                  pl.BlockSpec((1, nrow, width), lambda b: (b, 0, 0)),
                  pl.BlockSpec((8, width), lambda b: (0, 0)),
                  pl.BlockSpec((2, width, CMP_HIDDEN), lambda b: (0, 0, 0)),
                  pl.BlockSpec((hid2, LANES), lambda b: (0, 0)),
                  pl.BlockSpec((2, width, CMP_HIDDEN), lambda b: (0, 0, 0)),
                  pl.BlockSpec((hid2, LANES), lambda b: (0, 0))],
        out_specs=[pl.BlockSpec((1, nrow, LANES), lambda b: (b, 0, 0)),
                   pl.BlockSpec((1, nrow, LANES), lambda b: (b, 0, 0))],
        out_shape=[jax.ShapeDtypeStruct((bsz, nrow, LANES), BF16),
                   jax.ShapeDtypeStruct((bsz, nrow, LANES), BF16)],
        compiler_params=_params(("parallel",)),
    )(rk, rv, pos, w1k, w2k, w1v, w2v)


def _half_mask(g):
    lane = lax.broadcasted_iota(jnp.int32, (1, LANES), 1)
    return jnp.where((lane >= g * HALF) & (lane < (g + 1) * HALF), 1.0, 0.0).astype(BF16)


def _group_queries(q, nslab):
    return [jnp.concatenate([q[:, j * LANES:(j + 1) * LANES] * _half_mask(g) for j in range(nslab)], axis=0)
            for g in range(2)]


def _nsa_chains(q_ref, extra_ref, tq):
    chains = []
    for qt in range(2):
        rows = slice(qt * tq, (qt + 1) * tq)
        for g in range(NSA_KV):
            keep = _half_mask(g)
            for j in range(NSA_GQ):
                q = q_ref[0, rows, j * LANES:(j + 1) * LANES] * keep
                if extra_ref is not None:
                    q = q + extra_ref[0, rows, g * LANES:(g + 1) * LANES]
                chains.append((q, qt, g, j))
    return chains


def _bias_kernel(rel_ref, o_ref, *, tq, tk):
    row = pl.program_id(0)
    col = jnp.where(row < NSA_HEADS, row + DIFF_HEADS, row - NSA_HEADS)
    r = lax.broadcasted_iota(jnp.int32, (tq, tk), 0)
    c = lax.broadcasted_iota(jnp.int32, (tq, tk), 1)
    far = rel_ref[N_BUCKETS - 1, col]
    for d in range(2):
        dist = r - c + d * tq
        val = jnp.zeros((tq, tk), F32) + rel_ref[0, col]
        for k in range(1, N_BUCKETS):
            val = jnp.where(dist >= T5_THRESH[k], rel_ref[k, col], val)
        val = (val - far) * LOG2E
        if d == 0:
            val = jnp.where(dist >= 0, val, NEG)
        o_ref[0, d] = val
    o_ref[0, 2] = jnp.where(r < c, 0.0, NEG)


def _bias_tiles(rel_bias):
    nhead = rel_bias.shape[1]
    return pl.pallas_call(
        functools.partial(_bias_kernel, tq=TQ, tk=TK),
        grid=(nhead,),
        in_specs=[pl.BlockSpec(memory_space=pltpu.SMEM)],
        out_specs=pl.BlockSpec((1, 3, TQ, TK), lambda h: (h, 0, 0, 0)),
        out_shape=jax.ShapeDtypeStruct((nhead, 3, TQ, TK), F32),
        compiler_params=_params(("parallel",)),
    )(rel_bias)


def _with_near_bias(s, near):
    add = near[0] if len(near) == 1 else jnp.concatenate(near, axis=1)
    width, nw = s.shape[1], add.shape[1]
    if nw == width:
        return s + add
    return jnp.concatenate([s[:, 0:width - nw], s[:, width - nw:width] + add], axis=1)


def _softmax_step(s, v, m_ref, acc_ref):
    tk = s.shape[1]
    m_prev = m_ref[...]
    m_new = jnp.maximum(m_prev, jnp.max(s, axis=-1, keepdims=True))
    alpha = jnp.exp2(m_prev - m_new)
    p = jnp.exp2(s - jnp.concatenate([m_new] * (tk // LANES), axis=1))
    vaug = jnp.concatenate([v, jnp.ones((tk, LANES), BF16)], axis=1)
    acc_ref[...] = jnp.concatenate([alpha, alpha], axis=1) * acc_ref[...] + _dot(p.astype(BF16), vaug)
    m_ref[...] = m_new


def _softmax_first(s, v, m_ref, acc_ref):
    rows, tk = s.shape
    m = jnp.broadcast_to(jnp.max(s, axis=-1, keepdims=True), (rows, LANES))
    p = jnp.exp2(s - jnp.concatenate([m] * (tk // LANES), axis=1))
    vaug = jnp.concatenate([v, jnp.ones((tk, LANES), BF16)], axis=1)
    acc_ref[...] = _dot(p.astype(BF16), vaug)
    m_ref[...] = m


def _softmax_once(s, v):
    rows, tk = s.shape
    m = jnp.broadcast_to(jnp.max(s, axis=-1, keepdims=True), (rows, LANES))
    p = jnp.exp2(s - jnp.concatenate([m] * (tk // LANES), axis=1))
    vaug = jnp.concatenate([v, jnp.ones((tk, LANES), BF16)], axis=1)
    pv = _dot(p.astype(BF16), vaug)
    return pv[:, 0:LANES] * (1.0 / pv[:, LANES:2 * LANES])


def _causal_flash(i, tq, tk, chains, kv_of, bias_of, m_ref, acc_ref, also=None):
    def tail(first):
        kv = {}
        for c, (q, qt, kv_id, bias_id) in enumerate(chains):
            ntile = (1 if first else 3) + qt
            if (kv_id, ntile) not in kv:
                start = 0 if first else pl.multiple_of((i - 1) * 2 * tk, 2 * tk)
                kv[(kv_id, ntile)] = kv_of(kv_id, start, ntile * tk)
            k, v = kv[(kv_id, ntile)]
            near = [bias_of(bias_id, d) for d in range(min(ntile, 2) - 1, -1, -1)]
            _softmax_first(_with_near_bias(_nt_dot(q, k), near), v, m_ref.at[c], acc_ref.at[c])
        if also is not None:
            also(first)

    @pl.when(i == 0)
    def _():
        tail(True)

    @pl.when(i >= 1)
    def _():
        tail(False)

    def far(j, carry):
        start = pl.multiple_of(j * 2 * tk, 2 * tk)
        kv = {}
        for c, (q, _, kv_id, _) in enumerate(chains):
            if kv_id not in kv:
                kv[kv_id] = kv_of(kv_id, start, 2 * tk)
            k, v = kv[kv_id]
            _softmax_step(_nt_dot(q, k), v, m_ref.at[c], acc_ref.at[c])
        return carry

    lax.fori_loop(0, i - 1, far, 0)


def _normalized(acc_ref, c):
    return acc_ref[c, :, 0:LANES] * (1.0 / acc_ref[c, :, LANES:2 * LANES])


def _store_gated(o_ref, gate_ref, branches, tq, qt):
    lane = lax.broadcasted_iota(jnp.int32, (1, LANES), 1)
    rows = slice(qt * tq, (qt + 1) * tq)
    gates = gate_ref[0, rows, :]
    for j in range(NSA_GQ):
        total = None
        for o_of, branch in branches:
            c0 = (0 * NSA_GQ + j) * 3 + branch
            c1 = (1 * NSA_GQ + j) * 3 + branch
            gexp = jnp.where(lane < HALF, gates[:, c0:c0 + 1], gates[:, c1:c1 + 1])
            term = gexp * jnp.where(lane < HALF, o_of(0, j), o_of(1, j))
            total = term if total is None else total + term
        o_ref[0, rows, j * LANES:(j + 1) * LANES] = total.astype(BF16)


def _nsa_chain_index(qt, g, j):
    return (qt * NSA_KV + g) * NSA_GQ + j


N_CHAINS = 2 * NSA_HEADS


def _cmp_kernel(q_ref, kc_ref, vc_ref, ovt_ref, gate_ref, o_ref, nsel_ref, *, tq):
    qi = pl.program_id(1)
    nblk = kc_ref.shape[1]
    lane = lax.broadcasted_iota(jnp.int32, (1, LANES), 1)
    n = lax.broadcasted_iota(jnp.int32, (1, nblk), 1)
    t4 = qi * tq + (lax.broadcasted_iota(jnp.int32, (NSA_GQ * tq, 1), 0) & (tq - 1))
    vis4 = (n * CMP_STRIDE + (CMP_LEN - 1)) <= t4
    has4 = jnp.where(t4 >= CMP_LEN - 1, 1.0, 0.0)
    qg = _group_queries(q_ref[0], NSA_GQ)
    kc = kc_ref[0]
    vc = vc_ref[0]
    outs, imp_t = [], jnp.zeros((LANES, tq), F32)
    for g in range(NSA_KV):
        s = jnp.where(vis4, _nt_dot(qg[g], kc), NEG)
        e = jnp.exp2(s - jnp.max(s, axis=-1, keepdims=True))
        p = e * (has4 / jnp.sum(e, axis=-1, keepdims=True))
        outs.append(_dot(p.astype(BF16), vc))
        ps = p[0:tq] + p[tq:2 * tq] + p[2 * tq:3 * tq] + p[3 * tq:4 * tq]
        hi = ps.astype(BF16)
        r1 = ps - hi.astype(F32)
        mid = r1.astype(BF16)
        lo = (r1 - mid.astype(F32)).astype(BF16)
        imp_t = imp_t + (_nt_dot(ovt_ref[g], hi) + _nt_dot(ovt_ref[g], mid) + _nt_dot(ovt_ref[g], lo))
    _store_gated(o_ref, gate_ref, [(lambda g, j: outs[g][j * tq:(j + 1) * tq], 0)], tq, 0)

    tl = qi * tq + lax.broadcasted_iota(jnp.int32, (N_SLC, tq), 1)
    jrow = lax.broadcasted_iota(jnp.int32, (N_SLC, tq), 0)
    cur = tl >> 6
    valid = (jrow * SLC_BLOCK) <= tl
    forced = (jrow == 0) | (jrow == cur) | (jrow == cur - 1)
    sub = 8
    zero = jnp.zeros((N_SLC, tq), F32)
    dropped = []
    for base in (0, HALF):
        x = jnp.where(forced, jnp.inf, jnp.where(valid, imp_t[base:base + N_SLC], -jnp.inf))
        others = [jnp.broadcast_to(x[jj:jj + 1], (sub, tq)) for jj in range(N_SLC)]
        chunks = []
        for r0 in range(0, N_SLC, sub):
            xr = x[r0:r0 + sub]
            jr = r0 + lax.broadcasted_iota(jnp.int32, (sub, tq), 0)
            cnt = jnp.zeros((sub, tq), F32)
            for jj in range(N_SLC):
                if jj < r0:
                    beats = others[jj] >= xr
                elif jj >= r0 + sub:
                    beats = others[jj] > xr
                else:
                    beats = (others[jj] > xr) | ((others[jj] == xr) & (jr > jj))
                cnt = cnt + jnp.where(beats, 1.0, 0.0)
            chunks.append(jnp.where(cnt < SLC_TOPK, 0.0, 1.0))
        dropped.append(jnp.concatenate(chunks, axis=0))
    drop = jnp.concatenate([dropped[0], zero, dropped[1], zero], axis=0).T
    nsel_ref[0, :, 0:LANES] = jnp.where(lane >= HALF, drop, 0.0).astype(BF16)
    nsel_ref[0, :, LANES:2 * LANES] = jnp.where(lane < HALF, drop, 0.0).astype(BF16)


def _cmp_attention(main, kc, vc, ovt, gates):
    bsz, seq, _ = main.shape
    nblk = kc.shape[1]
    tq = TQ_CMP
    return pl.pallas_call(
        functools.partial(_cmp_kernel, tq=tq),
        grid=(bsz, seq // tq),
        in_specs=[pl.BlockSpec((1, tq, NSA_W), lambda b, i: (b, i, COL_NQ // NSA_W)),
                  pl.BlockSpec((1, nblk, LANES), lambda b, i: (b, 0, 0)),
                  pl.BlockSpec((1, nblk, LANES), lambda b, i: (b, 0, 0)),
                  pl.BlockSpec((2, LANES, nblk), lambda b, i: (0, 0, 0)),
                  pl.BlockSpec((1, tq, LANES), lambda b, i: (b, i, 0))],
        out_specs=[pl.BlockSpec((1, tq, NSA_W), lambda b, i: (b, i, 0)),
                   pl.BlockSpec((1, tq, 2 * LANES), lambda b, i: (b, i, 0))],
        out_shape=[jax.ShapeDtypeStruct((bsz, seq, NSA_W), BF16),
                   jax.ShapeDtypeStruct((bsz, seq, 2 * LANES), BF16)],
        compiler_params=_params(("parallel", "parallel")),
    )(main, kc, vc, ovt, gates)


def _slc_win_kernel(bias_ref, q_ref, ks_ref, vs_ref, kw_ref, vw_ref, nsel_ref, gate_ref, o_ref,
                    m_ref, acc_ref, win_ref, *, tq, tk):
    i = pl.program_id(1)
    chains = [(q, qt, g, g * NSA_GQ + j) for q, qt, g, j in _nsa_chains(q_ref, nsel_ref, tq)]
    win_chains = _nsa_chains(q_ref, None, tq)

    def kv_of(g, start, width):
        return (ks_ref[0, pl.ds(start, width), g * LANES:(g + 1) * LANES], vs_ref[0, pl.ds(start, width), :])

    def window(first):
        kv = {}
        for q, qt, g, j in win_chains:
            ntile = 1 + qt if first else 3
            if (qt, ntile) not in kv:
                start = 0 if first else pl.multiple_of((2 * i - 2 + qt) * tk, tk)
                kv[(qt, ntile)] = (kw_ref[0, pl.ds(start, ntile * tk), :], vw_ref[0, pl.ds(start, ntile * tk), :])
            k, v = kv[(qt, ntile)]
            near = [bias_ref[g * NSA_GQ + j, d] for d in range(min(ntile, 2) - 1, -1, -1)]
            if ntile == 3:
                near.insert(0, bias_ref[g * NSA_GQ + j, 2])
            win_ref[_nsa_chain_index(qt, g, j)] = _softmax_once(_with_near_bias(_nt_dot(q, k), near), v)

    _causal_flash(i, tq, tk, chains, kv_of, lambda h, d: bias_ref[h, d], m_ref, acc_ref, also=window)
    for qt in range(2):
        _store_gated(o_ref, gate_ref,
                     [(lambda g, j: _normalized(acc_ref, _nsa_chain_index(qt, g, j)), 1),
                      (lambda g, j: win_ref[_nsa_chain_index(qt, g, j)], 2)], tq, qt)


def _slc_win_attention(bias, main, nsel, gates):
    bsz, seq, _ = main.shape
    tq, tk = TQ, TK
    return pl.pallas_call(
        functools.partial(_slc_win_kernel, tq=tq, tk=tk),
        grid=(bsz, seq // (2 * tq)),
        in_specs=[pl.BlockSpec((NSA_HEADS, 3, tq, tk), lambda b, i: (0, 0, 0, 0), pipeline_mode=pl.Buffered(1)),
                  pl.BlockSpec((1, 2 * tq, NSA_W), lambda b, i: (b, i, COL_NQ // NSA_W)),
                  pl.BlockSpec((1, seq, 2 * LANES), lambda b, i: (b, 0, COL_KSL // (2 * LANES))),
                  pl.BlockSpec((1, seq, LANES), lambda b, i: (b, 0, COL_VSL // LANES)),
                  pl.BlockSpec((1, seq, LANES), lambda b, i: (b, 0, COL_KW // LANES)),
                  pl.BlockSpec((1, seq, LANES), lambda b, i: (b, 0, COL_VW // LANES)),
                  pl.BlockSpec((1, 2 * tq, 2 * LANES), lambda b, i: (b, i, 0)),
                  pl.BlockSpec((1, 2 * tq, LANES), lambda b, i: (b, i, 0))],
        out_specs=pl.BlockSpec((1, 2 * tq, NSA_W), lambda b, i: (b, i, 0)),
        out_shape=jax.ShapeDtypeStruct((bsz, seq, NSA_W), BF16),
        scratch_shapes=[pltpu.VMEM((N_CHAINS, tq, LANES), F32),
                        pltpu.VMEM((N_CHAINS, tq, 2 * LANES), F32),
                        pltpu.VMEM((N_CHAINS, tq, LANES), F32)],
        compiler_params=_params(("parallel", "parallel")),
    )(bias, main, main, main, main, main, nsel, gates)


def _diff_kernel(bias_ref, lam_ref, q_ref, k_ref, v_ref, g_ref, o_ref, m_ref, acc_ref, *, tq, tk):
    i = pl.program_id(1)
    chains = []
    for qt in range(2):
        for h in range(DIFF_HEADS):
            slab = q_ref[0, qt * tq:(qt + 1) * tq, h * LANES:(h + 1) * LANES]
            for half in range(2):
                chains.append((slab * _half_mask(half), qt, h, h))

    def kv_of(h, start, width):
        cols = slice(h * LANES, (h + 1) * LANES)
        return (k_ref[0, pl.ds(start, width), cols], v_ref[0, pl.ds(start, width), cols])

    _causal_flash(i, tq, tk, chains, kv_of, lambda h, d: bias_ref[h, d], m_ref, acc_ref)

    lm = lam_ref[...]
    lam = (jnp.exp(jnp.sum(lm[0:1] * lm[1:2], axis=-1, keepdims=True))
           - jnp.exp(jnp.sum(lm[2:3] * lm[3:4], axis=-1, keepdims=True)) + LAMBDA_INIT)
    for qt in range(2):
        for h in range(DIFF_HEADS):
            c = (qt * DIFF_HEADS + h) * 2
            a = _normalized(acc_ref, c) - lam * _normalized(acc_ref, c + 1)
            o_ref[0, qt * tq:(qt + 1) * tq, h * LANES:(h + 1) * LANES] = (
                _rms(a, g_ref[...]) * (1.0 - LAMBDA_INIT)).astype(BF16)


def _diff_attention(bias, lam_rows, main, subln):
    bsz, seq, _ = main.shape
    tq, tk = TQ, TK
    return pl.pallas_call(
        functools.partial(_diff_kernel, tq=tq, tk=tk),
        grid=(bsz, seq // (2 * tq)),
        in_specs=[pl.BlockSpec((DIFF_HEADS, 3, tq, tk), lambda b, i: (NSA_HEADS // DIFF_HEADS, 0, 0, 0),
                               pipeline_mode=pl.Buffered(1)),
                  pl.BlockSpec((8, LANES), lambda b, i: (0, 0)),
                  pl.BlockSpec((1, 2 * tq, DIFF_W), lambda b, i: (b, i, COL_DQ // DIFF_W)),
                  pl.BlockSpec((1, seq, DIFF_W), lambda b, i: (b, 0, COL_DK // DIFF_W)),
                  pl.BlockSpec((1, seq, DIFF_W), lambda b, i: (b, 0, COL_DV // DIFF_W)),
                  pl.BlockSpec((1, LANES), lambda b, i: (0, 0))],
        out_specs=pl.BlockSpec((1, 2 * tq, DIFF_W), lambda b, i: (b, i, 0)),
        out_shape=jax.ShapeDtypeStruct((bsz, seq, DIFF_W), BF16),
        scratch_shapes=[pltpu.VMEM((N_CHAINS, tq, LANES), F32),
                        pltpu.VMEM((N_CHAINS, tq, 2 * LANES), F32)],
        compiler_params=_params(("parallel", "parallel")),
    )(bias, lam_rows, main, main, main, subln)


FF_CHUNK = 256


def _tail_kernel(x_ref, od_ref, oc_ref, osw_ref, gn_ref, wo_ref, g2_ref,
                 wg_ref, wu_ref, wd_ref, gf_ref, o_ref, acc_ref):
    o_nsa = oc_ref[...].astype(F32) + osw_ref[...].astype(F32)
    o_nsa = _rms(o_nsa, gn_ref[...]).astype(BF16)
    attn = _dot(od_ref[...], wo_ref[0:DIFF_W, :]) + _dot(o_nsa, wo_ref[DIFF_W:DIFF_W + NSA_W, :])
    x1 = x_ref[...] + attn
    acc_ref[...] = x1
    h = _rms(x1, g2_ref[...]).astype(BF16)
    for c in range(0, D_FF, FF_CHUNK):
        gate = _dot(h, wg_ref[:, c:c + FF_CHUNK])
        up = _dot(h, wu_ref[:, c:c + FF_CHUNK])
        act = (gate * (1.0 / (1.0 + jnp.exp(-gate))) * up).astype(BF16)
        acc_ref[...] += _dot(act, wd_ref[c:c + FF_CHUNK, :])
    o_ref[...] = _rms(acc_ref[...], gf_ref[...])


def _tail(x2, od, oc, osw, gn, w_out, g2, wg, wu, wd, gf):
    n = x2.shape[0]
    row = lambda i: (i, 0)
    fixed = lambda i: (0, 0)
    once = pl.Buffered(1)
    return pl.pallas_call(
        _tail_kernel,
        grid=(n // TM,),
        in_specs=[pl.BlockSpec((TM, D_MODEL), row),
                  pl.BlockSpec((TM, DIFF_W), row),
                  pl.BlockSpec((TM, NSA_W), row),
                  pl.BlockSpec((TM, NSA_W), row),
                  pl.BlockSpec((1, NSA_W), fixed),
                  pl.BlockSpec((DIFF_W + NSA_W, D_MODEL), fixed, pipeline_mode=once),
                  pl.BlockSpec((1, D_MODEL), fixed),
                  pl.BlockSpec((D_MODEL, D_FF), fixed, pipeline_mode=once),
                  pl.BlockSpec((D_MODEL, D_FF), fixed, pipeline_mode=once),
                  pl.BlockSpec((D_FF, D_MODEL), fixed, pipeline_mode=once),
                  pl.BlockSpec((1, D_MODEL), fixed)],
        out_specs=pl.BlockSpec((TM, D_MODEL), row),
        out_shape=jax.ShapeDtypeStruct((n, D_MODEL), F32),
        scratch_shapes=[pltpu.VMEM((TM, D_MODEL), F32)],
        compiler_params=_params(("parallel",)),
    )(x2, od, oc, osw, gn, w_out, g2, wg, wu, wd, gf)


def _permute_heads(a, axis):
    shape = a.shape
    split = shape[:axis] + (NSA_KV, NSA_GQ, NSA_HD) + shape[axis + 1:]
    return jnp.swapaxes(a.reshape(split), axis, axis + 1).reshape(shape)


def _overlap_matrices(nblk):
    n_cmp = nblk - 1
    nslc = nblk * CMP_STRIDE // SLC_BLOCK
    assert nslc == N_SLC
    cmp_start = np.arange(n_cmp) * CMP_STRIDE
    cmp_end = cmp_start + CMP_LEN - 1
    slc_start = np.arange(nslc) * SLC_BLOCK
    overlap = ((cmp_start[:, None] < slc_start[None, :] + SLC_BLOCK) & (cmp_end[:, None] >= slc_start[None, :]))
    ovt = np.zeros((NSA_KV, LANES, nblk), np.float32)
    for g, base in enumerate((HALF, 0)):
        ovt[g, base:base + nslc, :n_cmp] = overlap.T
    return ovt


def _compress_weights(pos, w1, w2):
    half = CMP_LEN // 2
    shape = (2, half, NSA_KV, NSA_HD, CMP_HIDDEN)
    w1rep = jnp.broadcast_to(w1.reshape(2, half, 1, NSA_HD, CMP_HIDDEN), shape).reshape(2, -1, CMP_HIDDEN)
    w2big = jnp.einsum('jd,pg->pjgd', w2, jnp.eye(NSA_KV, dtype=F32)).reshape(NSA_KV * CMP_HIDDEN, NSA_KV * NSA_HD)
    posr = jnp.broadcast_to(pos.reshape(2, half, 1, NSA_HD), (2, half, NSA_KV, NSA_HD)).reshape(2, -1)
    return posr, w1rep.astype(BF16), w2big.astype(BF16)


def kernel(x, norm1, w_in, lambda_q1, lambda_k1, lambda_q2, lambda_k2, diff_subln, cmp_pos_k, cmp_w1_k, cmp_w2_k,
           cmp_pos_v, cmp_w1_v, cmp_w2_v, nsa_norm, w_out, norm2, w_gate, w_up, w_down, rel_bias, final_norm):
    bsz, seq, dm = x.shape
    assert dm == D_MODEL and seq % (2 * TQ) == 0 and seq % TM == 0 and seq == N_SLC * SLC_BLOCK
    assert norm1.shape[0] == 1, "single layer"
    x2 = x.reshape(bsz * seq, dm)

    w = w_in[0]
    zhalf = jnp.zeros((dm, HALF), F32)
    w_p = jnp.concatenate([w[:, :512] * QSCALE, w[:, 512:1536], _permute_heads(w[:, 1536:2048], 1) * QSCALE,
                           w[:, 2304:2368], zhalf, zhalf, w[:, 2368:2432],
                           w[:, 2432:2816],
                           w[:, 2048:2304],
                           w[:, 2816:N_IN], jnp.zeros((dm, LANES - (N_IN - 2816)), F32)], axis=1).astype(BF16)
    assert w_p.shape[1] == N_PROJ
    wo = w_out[0]
    wo_p = jnp.concatenate([wo[:DIFF_W], _permute_heads(wo[DIFF_W:], 0)], axis=0).astype(BF16)
    gn_p = _permute_heads(nsa_norm[0].reshape(1, NSA_W), 1)
    posk, w1k, w2k = _compress_weights(cmp_pos_k[0], cmp_w1_k[0], cmp_w2_k[0])
    posv, w1v, w2v = _compress_weights(cmp_pos_v[0], cmp_w1_v[0], cmp_w2_v[0])
    pos = jnp.concatenate([posk, posv, jnp.zeros((4, posk.shape[1]), F32)], axis=0)
    lam_rows = jnp.zeros((8, LANES), F32).at[0:4, 0:DIFF_HD].set(
        jnp.stack([lambda_q1[0], lambda_k1[0], lambda_q2[0], lambda_k2[0]]).astype(F32))

    main, kc_grp, vc_grp, gates = _inproj(x2, norm1[0].reshape(1, dm), w_p, seq)
    main = main.reshape(bsz, seq, N_MAIN)

    nrow = seq // CMP_STRIDE
    kc, vc = _compress(kc_grp.reshape(bsz, nrow, CMP_STRIDE * LANES), vc_grp.reshape(bsz, nrow, CMP_STRIDE * LANES),
                       pos, w1k, w2k, w1v, w2v)
    ovt = jnp.asarray(_overlap_matrices(nrow), BF16)
    gates = gates.reshape(bsz, seq, LANES)
    o_cmp, nsel = _cmp_attention(main, kc, vc, ovt, gates)
    bias = _bias_tiles(rel_bias)
    o_sw = _slc_win_attention(bias, main, nsel, gates)
    o_diff = _diff_attention(bias, lam_rows, main, diff_subln[0].reshape(1, DIFF_VD))

    n = bsz * seq
    out = _tail(x2, o_diff.reshape(n, DIFF_W), o_cmp.reshape(n, NSA_W), o_sw.reshape(n, NSA_W),
                gn_p, wo_p, norm2[0].reshape(1, dm),
                w_gate[0].astype(BF16), w_up[0].astype(BF16), w_down[0].astype(BF16), final_norm.reshape(1, dm))
    return out.reshape(bsz, seq, dm)
```

```python
import functools
import math

import numpy as np
import jax
import jax.numpy as jnp
from jax import lax
from jax.experimental import pallas as pl
from jax.experimental.pallas import tpu as pltpu

F32 = jnp.float32
BF16 = jnp.bfloat16

D_MODEL = 1024
DIFF_HEADS = 4
DIFF_HD = 64
DIFF_VD = 128
DIFF_W = 512
NSA_HEADS = 8
NSA_KV = 2
NSA_GQ = 4
NSA_HD = 64
NSA_W = 512
CMP_LEN = 32
CMP_STRIDE = 16
CMP_HIDDEN = 256
SLC_BLOCK = 64
SLC_TOPK = 16
N_SLC = 32
WINDOW = 512
N_BUCKETS = 32
MAX_DISTANCE = 128
D_FF = 2816
N_IN = 2840
NEG = -1e30
EPS = 1e-6
LAMBDA_INIT = 0.8 - 0.6 * math.exp(-0.3 * 0)
LOG2E = math.log2(math.e)
QSCALE = NSA_HD ** -0.5 * LOG2E

LANES = 128
HALF = LANES // 2
MXU_N = 256
COL_DQ, COL_DK, COL_DV, COL_NQ, COL_KSL, COL_VSL, COL_KW, COL_VW, N_MAIN = (
    0, 512, 1024, 1536, 2048, 2304, 2432, 2560, 2688)
N_PROJ = N_MAIN + 3 * LANES
VMEM_LIMIT = 48 * 1024 * 1024

TQ = 256
TK = 256
TM = 512


def _t5_thresholds():
    d = np.arange(0, 4 * MAX_DISTANCE)
    max_exact = N_BUCKETS // 2
    val = (np.log(np.maximum(d, 1).astype(np.float32) / np.float32(max_exact))
           / np.float32(math.log(MAX_DISTANCE / max_exact)) * np.float32(N_BUCKETS - max_exact))
    large = np.minimum(max_exact + val.astype(np.int32), N_BUCKETS - 1)
    bucket = np.where(d < max_exact, d, large)
    assert np.all(np.diff(bucket) >= 0) and bucket[-1] == N_BUCKETS - 1
    return [int(np.argmax(bucket >= k)) for k in range(N_BUCKETS)]


T5_THRESH = _t5_thresholds()
assert T5_THRESH[-1] <= TK + 1
assert WINDOW == 2 * TK and TQ == TK


def _nt_dot(a, b):
    return lax.dot_general(a, b, (((1,), (1,)), ((), ())), preferred_element_type=F32)


def _dot(a, b):
    return jnp.dot(a, b, preferred_element_type=F32)


def _rms(x, g):
    ms = jnp.mean(x * x, axis=-1, keepdims=True)
    return x * lax.rsqrt(ms + EPS) * g


def _params(sem):
    return pltpu.CompilerParams(dimension_semantics=sem, vmem_limit_bytes=VMEM_LIMIT)


def _store_grouped(out_ref, tok_ref, y):
    tok_ref[...] = y
    nrow = out_ref.shape[0]
    for l in range(CMP_STRIDE):
        out_ref[:, l * LANES:(l + 1) * LANES] = tok_ref[pl.ds(l, nrow, stride=CMP_STRIDE), :].astype(BF16)


def _inproj_kernel(x_ref, g_ref, w_ref, main_ref, kc_ref, vc_ref, gate_ref, tok_ref, *, seq):
    tm = x_ref.shape[0]
    h = _rms(x_ref[...], g_ref[...]).astype(BF16)
    for c in range(0, N_PROJ, MXU_N):
        y = _dot(h, w_ref[:, c:c + MXU_N])
        if c == COL_KSL:
            t0 = lax.rem(pl.program_id(0) * tm, seq)
            blk = (t0 + lax.broadcasted_iota(jnp.int32, (tm, 2 * LANES), 0)) >> 6
            lane = lax.broadcasted_iota(jnp.int32, (tm, 2 * LANES), 1)
            slot = jnp.where(lane < LANES, lane - HALF, lane - LANES)
            y = y + jnp.where((slot == blk) & (slot >= 0) & (slot < N_SLC), NEG, 0.0)
        if c + MXU_N <= N_MAIN:
            main_ref[:, c:c + MXU_N] = y.astype(BF16)
        elif c == COL_VW:
            main_ref[:, c:c + LANES] = y[:, 0:LANES].astype(BF16)
            _store_grouped(kc_ref, tok_ref, y[:, LANES:MXU_N])
        else:
            _store_grouped(vc_ref, tok_ref, y[:, 0:LANES])
            gate_ref[...] = 1.0 / (1.0 + jnp.exp(-y[:, LANES:MXU_N]))


def _inproj(x2, g1, w_p, seq):
    n = x2.shape[0]
    return pl.pallas_call(
        functools.partial(_inproj_kernel, seq=seq),
        grid=(n // TM,),
        in_specs=[pl.BlockSpec((TM, D_MODEL), lambda i: (i, 0)),
                  pl.BlockSpec((1, D_MODEL), lambda i: (0, 0)),
                  pl.BlockSpec((D_MODEL, N_PROJ), lambda i: (0, 0))],
        out_specs=[pl.BlockSpec((TM, N_MAIN), lambda i: (i, 0)),
                   pl.BlockSpec((TM // CMP_STRIDE, CMP_STRIDE * LANES), lambda i: (i, 0)),
                   pl.BlockSpec((TM // CMP_STRIDE, CMP_STRIDE * LANES), lambda i: (i, 0)),
                   pl.BlockSpec((TM, LANES), lambda i: (i, 0))],
        out_shape=[jax.ShapeDtypeStruct((n, N_MAIN), BF16),
                   jax.ShapeDtypeStruct((n // CMP_STRIDE, CMP_STRIDE * LANES), BF16),
                   jax.ShapeDtypeStruct((n // CMP_STRIDE, CMP_STRIDE * LANES), BF16),
                   jax.ShapeDtypeStruct((n, LANES), F32)],
        scratch_shapes=[pltpu.VMEM((TM, LANES), F32)],
        compiler_params=_params(("parallel",)),
    )(x2, g1, w_p)


def _gelu_tanh(x):
    return x * (0.5 * (1.0 + jnp.tanh(math.sqrt(2.0 / math.pi) * (x + 0.044715 * (x * x * x)))))


def _compress_kernel(rk_ref, rv_ref, pos_ref, w1k_ref, w2k_ref, w1v_ref, w2v_ref, kc_ref, vc_ref):
    width = rk_ref.shape[2]
    lane = lax.broadcasted_iota(jnp.int32, (1, width), 1)
    in_group = [jnp.where(((lane >> 6) & 1) == g, 1.0, 0.0) for g in range(NSA_KV)]

    def comp(r_ref, p_top, p_bot, w1_ref, w2_ref):
        r = r_ref[0].astype(F32)
        hid = []
        for g in range(NSA_KV):
            top = _dot(((r + p_top) * in_group[g]).astype(BF16), w1_ref[0])
            bot = _dot(((r + p_bot) * in_group[g]).astype(BF16), w1_ref[1])
            nrow = top.shape[0]
            hid.append(top + pltpu.roll(bot, nrow - 1, 0))
        act = _gelu_tanh(jnp.concatenate(hid, axis=1)).astype(BF16)
        return _dot(act, w2_ref[...]).astype(BF16)

    kc_ref[0] = comp(rk_ref, pos_ref[0:1, :], pos_ref[1:2, :], w1k_ref, w2k_ref)
    vc_ref[0] = comp(rv_ref, pos_ref[2:3, :], pos_ref[3:4, :], w1v_ref, w2v_ref)


def _compress(rk, rv, pos, w1k, w2k, w1v, w2v):
    bsz, nrow, width = rk.shape
    hid2 = 2 * CMP_HIDDEN
    return pl.pallas_call(
        _compress_kernel,
        grid=(bsz,),
        in_specs=[pl.BlockSpec((1, nrow, width), lambda b: (b, 0, 0)),
                  pl.BlockSpec((1, nrow, width), lambda b: (b, 0, 0)),
                  pl.BlockSpec((8, width), lambda b: (0, 0)),
                  pl.BlockSpec((2, width, CMP_HIDDEN), lambda b: (0, 0, 0)),
                  pl.BlockSpec((hid2, LANES), lambda b: (0, 0)),
                  pl.BlockSpec((2, width, CMP_HIDDEN), lambda b: (0, 0, 0)),
                  pl.BlockSpec((hid2, LANES), lambda b: (0, 0))],
        out_specs=[pl.BlockSpec((1, nrow, LANES), lambda b: (b, 0, 0)),
                   pl.BlockSpec((1, nrow, LANES), lambda b: (b, 0, 0))],
        out_shape=[jax.ShapeDtypeStruct((bsz, nrow, LANES), BF16),
                   jax.ShapeDtypeStruct((bsz, nrow, LANES), BF16)],
        compiler_params=_params(("parallel",)),
    )(rk, rv, pos, w1k, w2k, w1v, w2v)


def _half_mask(g):
    lane = lax.broadcasted_iota(jnp.int32, (1, LANES), 1)
    return jnp.where((lane >= g * HALF) & (lane < (g + 1) * HALF), 1.0, 0.0).astype(BF16)


def _nsa_chains(q_ref, extra_ref, tq):
    chains = []
    for qt in range(2):
        rows = slice(qt * tq, (qt + 1) * tq)
        for g in range(NSA_KV):
            keep = _half_mask(g)
            for j in range(NSA_GQ):
                q = q_ref[0, rows, j * LANES:(j + 1) * LANES] * keep
                if extra_ref is not None:
                    q = q + extra_ref[0, rows, g * LANES:(g + 1) * LANES]
                chains.append((q, qt, g, j))
    return chains


def _bias_kernel(rel_ref, o_ref, *, tq, tk):
    row = pl.program_id(0)
    col = jnp.where(row < NSA_HEADS, row + DIFF_HEADS, row - NSA_HEADS)
    r = lax.broadcasted_iota(jnp.int32, (tq, tk), 0)
    c = lax.broadcasted_iota(jnp.int32, (tq, tk), 1)
    far = rel_ref[N_BUCKETS - 1, col]
    for d in range(2):
        dist = r - c + d * tq
        val = jnp.zeros((tq, tk), F32) + rel_ref[0, col]
        for k in range(1, N_BUCKETS):
            val = jnp.where(dist >= T5_THRESH[k], rel_ref[k, col], val)
        val = (val - far) * LOG2E
        if d == 0:
            val = jnp.where(dist >= 0, val, NEG)
        o_ref[0, d] = val
    o_ref[0, 2] = jnp.where(r < c, 0.0, NEG)


def _bias_tiles(rel_bias):
    nhead = rel_bias.shape[1]
    return pl.pallas_call(
        functools.partial(_bias_kernel, tq=TQ, tk=TK),
        grid=(nhead,),
        in_specs=[pl.BlockSpec(memory_space=pltpu.SMEM)],
        out_specs=pl.BlockSpec((1, 3, TQ, TK), lambda h: (h, 0, 0, 0)),
        out_shape=jax.ShapeDtypeStruct((nhead, 3, TQ, TK), F32),
        compiler_params=_params(("parallel",)),
    )(rel_bias)


def _with_near_bias(s, near):
    add = near[0] if len(near) == 1 else jnp.concatenate(near, axis=1)
    width, nw = s.shape[1], add.shape[1]
    if nw == width:
        return s + add
    return jnp.concatenate([s[:, 0:width - nw], s[:, width - nw:width] + add], axis=1)


def _softmax_step(s, v, m_ref, acc_ref):
    tk = s.shape[1]
    m_prev = m_ref[...]
    m_new = jnp.maximum(m_prev, jnp.max(s, axis=-1, keepdims=True))
    alpha = jnp.exp2(m_prev - m_new)
    p = jnp.exp2(s - jnp.concatenate([m_new] * (tk // LANES), axis=1))
    vaug = jnp.concatenate([v, jnp.ones((tk, LANES), BF16)], axis=1)
    acc_ref[...] = jnp.concatenate([alpha, alpha], axis=1) * acc_ref[...] + _dot(p.astype(BF16), vaug)
    m_ref[...] = m_new


def _softmax_first(s, v, m_ref, acc_ref):
    rows, tk = s.shape
    m = jnp.broadcast_to(jnp.max(s, axis=-1, keepdims=True), (rows, LANES))
    p = jnp.exp2(s - jnp.concatenate([m] * (tk // LANES), axis=1))
    vaug = jnp.concatenate([v, jnp.ones((tk, LANES), BF16)], axis=1)
    acc_ref[...] = _dot(p.astype(BF16), vaug)
    m_ref[...] = m


def _softmax_once(s, v):
    rows, tk = s.shape
    m = jnp.broadcast_to(jnp.max(s, axis=-1, keepdims=True), (rows, LANES))
    p = jnp.exp2(s - jnp.concatenate([m] * (tk // LANES), axis=1))
    vaug = jnp.concatenate([v, jnp.ones((tk, LANES), BF16)], axis=1)
    pv = _dot(p.astype(BF16), vaug)
    return pv[:, 0:LANES] * (1.0 / pv[:, LANES:2 * LANES])


def _causal_flash(i, tq, tk, chains, kv_of, bias_of, m_ref, acc_ref, also=None):
    def tail(first):
        stages = list(also(first)) if also is not None else []
        every = max(1, len(chains) // (len(stages) + 1))
        kv = {}
        for c, (q, qt, kv_id, bias_id) in enumerate(chains):
            ntile = (1 if first else 3) + qt
            if (kv_id, ntile) not in kv:
                start = 0 if first else pl.multiple_of((i - 1) * 2 * tk, 2 * tk)
                kv[(kv_id, ntile)] = kv_of(kv_id, start, ntile * tk)
            k, v = kv[(kv_id, ntile)]
            near = [bias_of(bias_id, d) for d in range(min(ntile, 2) - 1, -1, -1)]
            _softmax_first(_with_near_bias(_nt_dot(q, k), near), v, m_ref.at[c], acc_ref.at[c])
            if stages and (c + 1) % every == 0:
                stages.pop(0)()
        while stages:
            stages.pop(0)()

    @pl.when(i == 0)
    def _():
        tail(True)

    @pl.when(i >= 1)
    def _():
        tail(False)

    def far(j, carry):
        start = pl.multiple_of(j * 2 * tk, 2 * tk)
        kv = {}
        for c, (q, _, kv_id, _) in enumerate(chains):
            if kv_id not in kv:
                kv[kv_id] = kv_of(kv_id, start, 2 * tk)
            k, v = kv[kv_id]
            _softmax_step(_nt_dot(q, k), v, m_ref.at[c], acc_ref.at[c])
        return carry

    lax.fori_loop(0, i - 1, far, 0)


def _normalized(acc_ref, c):
    return acc_ref[c, :, 0:LANES] * (1.0 / acc_ref[c, :, LANES:2 * LANES])


def _store_gated(o_ref, gate_ref, branches, tq, qt):
    lane = lax.broadcasted_iota(jnp.int32, (1, LANES), 1)
    rows = slice(qt * tq, (qt + 1) * tq)
    gates = gate_ref[0, rows, :]
    for j in range(NSA_GQ):
        total = None
        for o_of, branch in branches:
            c0 = (0 * NSA_GQ + j) * 3 + branch
            c1 = (1 * NSA_GQ + j) * 3 + branch
            gexp = jnp.where(lane < HALF, gates[:, c0:c0 + 1], gates[:, c1:c1 + 1])
            term = gexp * jnp.where(lane < HALF, o_of(0, j), o_of(1, j))
            total = term if total is None else total + term
        o_ref[0, rows, j * LANES:(j + 1) * LANES] = total.astype(BF16)


def _nsa_chain_index(qt, g, j):
    return (qt * NSA_KV + g) * NSA_GQ + j


N_CHAINS = 2 * NSA_HEADS


def _cmp_stages(qi, tq, q_ref, kc_ref, vc_ref, ovt_ref, gate_ref, o_ref, nsel_ref):
    nblk = kc_ref.shape[1]
    lane = lax.broadcasted_iota(jnp.int32, (1, LANES), 1)
    state = {"outs": [], "imp_t": jnp.zeros((LANES, tq), F32), "dropped": []}

    def attend(g):
        n = lax.broadcasted_iota(jnp.int32, (1, nblk), 1)
        t4 = qi * tq + (lax.broadcasted_iota(jnp.int32, (NSA_GQ * tq, 1), 0) & (tq - 1))
        vis4 = (n * CMP_STRIDE + (CMP_LEN - 1)) <= t4
        has4 = jnp.where(t4 >= CMP_LEN - 1, 1.0, 0.0)
        qg = jnp.concatenate([q_ref[0, :, j * LANES:(j + 1) * LANES] * _half_mask(g) for j in range(NSA_GQ)], axis=0)
        s = jnp.where(vis4, _nt_dot(qg, kc_ref[0]), NEG)
        e = jnp.exp2(s - jnp.max(s, axis=-1, keepdims=True))
        p = e * (has4 / jnp.sum(e, axis=-1, keepdims=True))
        state["outs"].append(_dot(p.astype(BF16), vc_ref[0]))
        ps = p[0:tq] + p[tq:2 * tq] + p[2 * tq:3 * tq] + p[3 * tq:4 * tq]
        hi = ps.astype(BF16)
        r1 = ps - hi.astype(F32)
        mid = r1.astype(BF16)
        lo = (r1 - mid.astype(F32)).astype(BF16)
        state["imp_t"] = state["imp_t"] + (
            _nt_dot(ovt_ref[g], hi) + _nt_dot(ovt_ref[g], mid) + _nt_dot(ovt_ref[g], lo))

    def store_out():
        outs = state["outs"]
        _store_gated(o_ref, gate_ref, [(lambda g, j: outs[g][j * tq:(j + 1) * tq], 0)], tq, 0)

    def rank(base):
        tl = qi * tq + lax.broadcasted_iota(jnp.int32, (N_SLC, tq), 1)
        jrow = lax.broadcasted_iota(jnp.int32, (N_SLC, tq), 0)
        cur = tl >> 6
        valid = (jrow * SLC_BLOCK) <= tl
        forced = (jrow == 0) | (jrow == cur) | (jrow == cur - 1)
        sub = 8
        x = jnp.where(forced, jnp.inf, jnp.where(valid, state["imp_t"][base:base + N_SLC], -jnp.inf))
        others = [jnp.broadcast_to(x[jj:jj + 1], (sub, tq)) for jj in range(N_SLC)]
        chunks = []
        for r0 in range(0, N_SLC, sub):
            xr = x[r0:r0 + sub]
            jr = r0 + lax.broadcasted_iota(jnp.int32, (sub, tq), 0)
            cnt = jnp.zeros((sub, tq), F32)
            for jj in range(N_SLC):
                if jj < r0:
                    beats = others[jj] >= xr
                elif jj >= r0 + sub:
                    beats = others[jj] > xr
                else:
                    beats = (others[jj] > xr) | ((others[jj] == xr) & (jr > jj))
                cnt = cnt + jnp.where(beats, 1.0, 0.0)
            chunks.append(jnp.where(cnt < SLC_TOPK, 0.0, 1.0))
        state["dropped"].append(jnp.concatenate(chunks, axis=0))

    def store_selection():
        zero = jnp.zeros((N_SLC, tq), F32)
        dropped = state["dropped"]
        drop = jnp.concatenate([dropped[0], zero, dropped[1], zero], axis=0).T
        nsel_ref[0, :, 0:LANES] = jnp.where(lane >= HALF, drop, 0.0).astype(BF16)
        nsel_ref[0, :, LANES:2 * LANES] = jnp.where(lane < HALF, drop, 0.0).astype(BF16)

    return [functools.partial(attend, 0), functools.partial(attend, 1), store_out,
            functools.partial(rank, 0), functools.partial(rank, HALF), store_selection]


def _slc_win_kernel(bias_ref, q_ref, ks_ref, vs_ref, kw_ref, vw_ref, nsel_ref, gate_ref, o_ref,
                    m_ref, acc_ref, win_ref, *, tq, tk):
    i = pl.program_id(1)
    chains = [(q, qt, g, g * NSA_GQ + j) for q, qt, g, j in _nsa_chains(q_ref, nsel_ref, tq)]
    win_chains = _nsa_chains(q_ref, None, tq)

    def kv_of(g, start, width):
        return (ks_ref[0, pl.ds(start, width), g * LANES:(g + 1) * LANES], vs_ref[0, pl.ds(start, width), :])

    def window(first):
        kv = {}

        def one(q, qt, g, j):
            ntile = 1 + qt if first else 3
            if (qt, ntile) not in kv:
                start = 0 if first else pl.multiple_of((2 * i - 2 + qt) * tk, tk)
                kv[(qt, ntile)] = (kw_ref[0, pl.ds(start, ntile * tk), :], vw_ref[0, pl.ds(start, ntile * tk), :])
            k, v = kv[(qt, ntile)]
            near = [bias_ref[g * NSA_GQ + j, d] for d in range(min(ntile, 2) - 1, -1, -1)]
            if ntile == 3:
                near.insert(0, bias_ref[g * NSA_GQ + j, 2])
            win_ref[_nsa_chain_index(qt, g, j)] = _softmax_once(_with_near_bias(_nt_dot(q, k), near), v)

        def all_chains():
            for chain in win_chains:
                one(*chain)

        return [all_chains]

    _causal_flash(i, tq, tk, chains, kv_of, lambda h, d: bias_ref[h, d], m_ref, acc_ref, also=window)
    for qt in range(2):
        _store_gated(o_ref, gate_ref,
                     [(lambda g, j: _normalized(acc_ref, _nsa_chain_index(qt, g, j)), 1),
                      (lambda g, j: win_ref[_nsa_chain_index(qt, g, j)], 2)], tq, qt)


def _slc_win_attention(bias, main, nsel, gates):
    bsz, seq, _ = main.shape
    tq, tk = TQ, TK
    return pl.pallas_call(
        functools.partial(_slc_win_kernel, tq=tq, tk=tk),
        grid=(bsz, seq // (2 * tq)),
        in_specs=[pl.BlockSpec((NSA_HEADS, 3, tq, tk), lambda b, i: (0, 0, 0, 0), pipeline_mode=pl.Buffered(1)),
                  pl.BlockSpec((1, 2 * tq, NSA_W), lambda b, i: (b, i, COL_NQ // NSA_W)),
                  pl.BlockSpec((1, seq, 2 * LANES), lambda b, i: (b, 0, COL_KSL // (2 * LANES))),
                  pl.BlockSpec((1, seq, LANES), lambda b, i: (b, 0, COL_VSL // LANES)),
                  pl.BlockSpec((1, seq, LANES), lambda b, i: (b, 0, COL_KW // LANES)),
                  pl.BlockSpec((1, seq, LANES), lambda b, i: (b, 0, COL_VW // LANES)),
                  pl.BlockSpec((1, 2 * tq, 2 * LANES), lambda b, i: (b, i, 0)),
                  pl.BlockSpec((1, 2 * tq, LANES), lambda b, i: (b, i, 0))],
        out_specs=pl.BlockSpec((1, 2 * tq, NSA_W), lambda b, i: (b, i, 0)),
        out_shape=jax.ShapeDtypeStruct((bsz, seq, NSA_W), BF16),
        scratch_shapes=[pltpu.VMEM((N_CHAINS, tq, LANES), F32),
                        pltpu.VMEM((N_CHAINS, tq, 2 * LANES), F32),
                        pltpu.VMEM((N_CHAINS, tq, LANES), F32)],
        compiler_params=_params(("parallel", "parallel")),
    )(bias, main, main, main, main, main, nsel, gates)


def _diff_cmp_kernel(bias_ref, lam_ref, q_ref, k_ref, v_ref, g_ref, nq_ref, kc_ref, vc_ref, ovt_ref, gate_ref,
                     o_ref, oc_ref, nsel_ref, m_ref, acc_ref, *, tq, tk):
    i = pl.program_id(1)

    chains = []
    for qt in range(2):
        for h in range(DIFF_HEADS):
            slab = q_ref[0, qt * tq:(qt + 1) * tq, h * LANES:(h + 1) * LANES]
            for half in range(2):
                chains.append((slab * _half_mask(half), qt, h, h))

    def kv_of(h, start, width):
        cols = slice(h * LANES, (h + 1) * LANES)
        return (k_ref[0, pl.ds(start, width), cols], v_ref[0, pl.ds(start, width), cols])

    def compressed(first):
        return _cmp_stages(i, 2 * tq, nq_ref, kc_ref, vc_ref, ovt_ref, gate_ref, oc_ref, nsel_ref)

    _causal_flash(i, tq, tk, chains, kv_of, lambda h, d: bias_ref[h, d], m_ref, acc_ref, also=compressed)

    lm = lam_ref[...]
    lam = (jnp.exp(jnp.sum(lm[0:1] * lm[1:2], axis=-1, keepdims=True))
           - jnp.exp(jnp.sum(lm[2:3] * lm[3:4], axis=-1, keepdims=True)) + LAMBDA_INIT)
    for qt in range(2):
        for h in range(DIFF_HEADS):
            c = (qt * DIFF_HEADS + h) * 2
            a = _normalized(acc_ref, c) - lam * _normalized(acc_ref, c + 1)
            o_ref[0, qt * tq:(qt + 1) * tq, h * LANES:(h + 1) * LANES] = (
                _rms(a, g_ref[...]) * (1.0 - LAMBDA_INIT)).astype(BF16)


def _diff_cmp_attention(bias, lam_rows, main, subln, kc, vc, ovt, gates):
    bsz, seq, _ = main.shape
    nblk = kc.shape[1]
    tq, tk = TQ, TK
    step = lambda b, i: (b, i, 0)
    return pl.pallas_call(
        functools.partial(_diff_cmp_kernel, tq=tq, tk=tk),
        grid=(bsz, seq // (2 * tq)),
        in_specs=[pl.BlockSpec((DIFF_HEADS, 3, tq, tk), lambda b, i: (NSA_HEADS // DIFF_HEADS, 0, 0, 0),
                               pipeline_mode=pl.Buffered(1)),
                  pl.BlockSpec((8, LANES), lambda b, i: (0, 0)),
                  pl.BlockSpec((1, 2 * tq, DIFF_W), lambda b, i: (b, i, COL_DQ // DIFF_W)),
                  pl.BlockSpec((1, seq, DIFF_W), lambda b, i: (b, 0, COL_DK // DIFF_W)),
                  pl.BlockSpec((1, seq, DIFF_W), lambda b, i: (b, 0, COL_DV // DIFF_W)),
                  pl.BlockSpec((1, LANES), lambda b, i: (0, 0)),
                  pl.BlockSpec((1, 2 * tq, NSA_W), lambda b, i: (b, i, COL_NQ // NSA_W)),
                  pl.BlockSpec((1, nblk, LANES), lambda b, i: (b, 0, 0)),
                  pl.BlockSpec((1, nblk, LANES), lambda b, i: (b, 0, 0)),
                  pl.BlockSpec((2, LANES, nblk), lambda b, i: (0, 0, 0)),
                  pl.BlockSpec((1, 2 * tq, LANES), step)],
        out_specs=[pl.BlockSpec((1, 2 * tq, DIFF_W), step),
                   pl.BlockSpec((1, 2 * tq, NSA_W), step),
                   pl.BlockSpec((1, 2 * tq, 2 * LANES), step)],
        out_shape=[jax.ShapeDtypeStruct((bsz, seq, DIFF_W), BF16),
                   jax.ShapeDtypeStruct((bsz, seq, NSA_W), BF16),
                   jax.ShapeDtypeStruct((bsz, seq, 2 * LANES), BF16)],
        scratch_shapes=[pltpu.VMEM((N_CHAINS, tq, LANES), F32),
                        pltpu.VMEM((N_CHAINS, tq, 2 * LANES), F32)],
        compiler_params=_params(("parallel", "parallel")),
    )(bias, lam_rows, main, main, main, subln, main, kc, vc, ovt, gates)


FF_CHUNK = 256


def _tail_kernel(x_ref, od_ref, oc_ref, osw_ref, gn_ref, wo_ref, g2_ref,
                 wg_ref, wu_ref, wd_ref, gf_ref, o_ref, acc_ref):
    o_nsa = oc_ref[...].astype(F32) + osw_ref[...].astype(F32)
    o_nsa = _rms(o_nsa, gn_ref[...]).astype(BF16)
    attn = _dot(od_ref[...], wo_ref[0:DIFF_W, :]) + _dot(o_nsa, wo_ref[DIFF_W:DIFF_W + NSA_W, :])
    x1 = x_ref[...] + attn
    acc_ref[...] = x1
    h = _rms(x1, g2_ref[...]).astype(BF16)
    for c in range(0, D_FF, FF_CHUNK):
        gate = _dot(h, wg_ref[:, c:c + FF_CHUNK])
        up = _dot(h, wu_ref[:, c:c + FF_CHUNK])
        act = (gate * (1.0 / (1.0 + jnp.exp(-gate))) * up).astype(BF16)
        acc_ref[...] += _dot(act, wd_ref[c:c + FF_CHUNK, :])
    o_ref[...] = _rms(acc_ref[...], gf_ref[...])


def _tail(x2, od, oc, osw, gn, w_out, g2, wg, wu, wd, gf):
    n = x2.shape[0]
    row = lambda i: (i, 0)
    fixed = lambda i: (0, 0)
    once = pl.Buffered(1)
    return pl.pallas_call(
        _tail_kernel,
        grid=(n // TM,),
        in_specs=[pl.BlockSpec((TM, D_MODEL), row),
                  pl.BlockSpec((TM, DIFF_W), row),
                  pl.BlockSpec((TM, NSA_W), row),
                  pl.BlockSpec((TM, NSA_W), row),
                  pl.BlockSpec((1, NSA_W), fixed),
                  pl.BlockSpec((DIFF_W + NSA_W, D_MODEL), fixed, pipeline_mode=once),
                  pl.BlockSpec((1, D_MODEL), fixed),
                  pl.BlockSpec((D_MODEL, D_FF), fixed, pipeline_mode=once),
                  pl.BlockSpec((D_MODEL, D_FF), fixed, pipeline_mode=once),
                  pl.BlockSpec((D_FF, D_MODEL), fixed, pipeline_mode=once),
                  pl.BlockSpec((1, D_MODEL), fixed)],
        out_specs=pl.BlockSpec((TM, D_MODEL), row),
        out_shape=jax.ShapeDtypeStruct((n, D_MODEL), F32),
        scratch_shapes=[pltpu.VMEM((TM, D_MODEL), F32)],
        compiler_params=_params(("parallel",)),
    )(x2, od, oc, osw, gn, w_out, g2, wg, wu, wd, gf)


def _permute_heads(a, axis):
    shape = a.shape
    split = shape[:axis] + (NSA_KV, NSA_GQ, NSA_HD) + shape[axis + 1:]
    return jnp.swapaxes(a.reshape(split), axis, axis + 1).reshape(shape)


def _overlap_matrices(nblk):
    n_cmp = nblk - 1
    nslc = nblk * CMP_STRIDE // SLC_BLOCK
    assert nslc == N_SLC
    cmp_start = np.arange(n_cmp) * CMP_STRIDE
    cmp_end = cmp_start + CMP_LEN - 1
    slc_start = np.arange(nslc) * SLC_BLOCK
    overlap = ((cmp_start[:, None] < slc_start[None, :] + SLC_BLOCK) & (cmp_end[:, None] >= slc_start[None, :]))
    ovt = np.zeros((NSA_KV, LANES, nblk), np.float32)
    for g, base in enumerate((HALF, 0)):
        ovt[g, base:base + nslc, :n_cmp] = overlap.T
    return ovt


def _compress_weights(pos, w1, w2):
    half = CMP_LEN // 2
    shape = (2, half, NSA_KV, NSA_HD, CMP_HIDDEN)
    w1rep = jnp.broadcast_to(w1.reshape(2, half, 1, NSA_HD, CMP_HIDDEN), shape).reshape(2, -1, CMP_HIDDEN)
    w2big = jnp.einsum('jd,pg->pjgd', w2, jnp.eye(NSA_KV, dtype=F32)).reshape(NSA_KV * CMP_HIDDEN, NSA_KV * NSA_HD)
    posr = jnp.broadcast_to(pos.reshape(2, half, 1, NSA_HD), (2, half, NSA_KV, NSA_HD)).reshape(2, -1)
    return posr, w1rep.astype(BF16), w2big.astype(BF16)


def kernel(x, norm1, w_in, lambda_q1, lambda_k1, lambda_q2, lambda_k2, diff_subln, cmp_pos_k, cmp_w1_k, cmp_w2_k,
           cmp_pos_v, cmp_w1_v, cmp_w2_v, nsa_norm, w_out, norm2, w_gate, w_up, w_down, rel_bias, final_norm):
    bsz, seq, dm = x.shape
    assert dm == D_MODEL and seq % (2 * TQ) == 0 and seq % TM == 0 and seq == N_SLC * SLC_BLOCK
    assert norm1.shape[0] == 1, "single layer"
    x2 = x.reshape(bsz * seq, dm)

    w = w_in[0]
    zhalf = jnp.zeros((dm, HALF), F32)
    w_p = jnp.concatenate([w[:, :512] * QSCALE, w[:, 512:1536], _permute_heads(w[:, 1536:2048], 1) * QSCALE,
                           w[:, 2304:2368], zhalf, zhalf, w[:, 2368:2432],
                           w[:, 2432:2816],
                           w[:, 2048:2304],
                           w[:, 2816:N_IN], jnp.zeros((dm, LANES - (N_IN - 2816)), F32)], axis=1).astype(BF16)
    assert w_p.shape[1] == N_PROJ
    wo = w_out[0]
    wo_p = jnp.concatenate([wo[:DIFF_W], _permute_heads(wo[DIFF_W:], 0)], axis=0).astype(BF16)
    gn_p = _permute_heads(nsa_norm[0].reshape(1, NSA_W), 1)
    posk, w1k, w2k = _compress_weights(cmp_pos_k[0], cmp_w1_k[0], cmp_w2_k[0])
    posv, w1v, w2v = _compress_weights(cmp_pos_v[0], cmp_w1_v[0], cmp_w2_v[0])
    pos = jnp.concatenate([posk, posv, jnp.zeros((4, posk.shape[1]), F32)], axis=0)
    lam_rows = jnp.zeros((8, LANES), F32).at[0:4, 0:DIFF_HD].set(
        jnp.stack([lambda_q1[0], lambda_k1[0], lambda_q2[0], lambda_k2[0]]).astype(F32))

    main, kc_grp, vc_grp, gates = _inproj(x2, norm1[0].reshape(1, dm), w_p, seq)
    main = main.reshape(bsz, seq, N_MAIN)

    nrow = seq // CMP_STRIDE
    kc, vc = _compress(kc_grp.reshape(bsz, nrow, CMP_STRIDE * LANES), vc_grp.reshape(bsz, nrow, CMP_STRIDE * LANES),
                       pos, w1k, w2k, w1v, w2v)
    ovt = jnp.asarray(_overlap_matrices(nrow), BF16)
    gates = gates.reshape(bsz, seq, LANES)
    bias = _bias_tiles(rel_bias)
    o_diff, o_cmp, nsel = _diff_cmp_attention(bias, lam_rows, main, diff_subln[0].reshape(1, DIFF_VD),
                                              kc, vc, ovt, gates)
    o_sw = _slc_win_attention(bias, main, nsel, gates)

    n = bsz * seq
    out = _tail(x2, o_diff.reshape(n, DIFF_W), o_cmp.reshape(n, NSA_W), o_sw.reshape(n, NSA_W),
                gn_p, wo_p, norm2[0].reshape(1, dm),
                w_gate[0].astype(BF16), w_up[0].astype(BF16), w_down[0].astype(BF16), final_norm.reshape(1, dm))
    return out.reshape(bsz, seq, dm)
```

```python
import functools
import math

import numpy as np
import jax
import jax.numpy as jnp
from jax import lax
from jax.experimental import pallas as pl
from jax.experimental.pallas import tpu as pltpu

F32 = jnp.float32
BF16 = jnp.bfloat16

D_MODEL = 1024
DIFF_HEADS = 4
DIFF_HD = 64
DIFF_VD = 128
DIFF_W = 512
NSA_HEADS = 8
NSA_KV = 2
NSA_GQ = 4
NSA_HD = 64
NSA_W = 512
CMP_LEN = 32
CMP_STRIDE = 16
CMP_HIDDEN = 256
SLC_BLOCK = 64
SLC_TOPK = 16
N_SLC = 32
WINDOW = 512
N_BUCKETS = 32
MAX_DISTANCE = 128
D_FF = 2816
N_IN = 2840
NEG = -1e30
EPS = 1e-6
LAMBDA_INIT = 0.8 - 0.6 * math.exp(-0.3 * 0)
LOG2E = math.log2(math.e)
QSCALE = NSA_HD ** -0.5 * LOG2E

LANES = 128
HALF = LANES // 2
MXU_N = 256
COL_DQ, COL_DK, COL_DV, COL_NQ, COL_KSL, COL_VSL, COL_KW, COL_VW, N_MAIN = (
    0, 512, 1024, 1536, 2048, 2304, 2432, 2560, 2688)
N_PROJ = N_MAIN + 3 * LANES
VMEM_LIMIT = 48 * 1024 * 1024

TQ = 256
TK = 256
TM = 512


def _t5_thresholds():
    d = np.arange(0, 4 * MAX_DISTANCE)
    max_exact = N_BUCKETS // 2
    val = (np.log(np.maximum(d, 1).astype(np.float32) / np.float32(max_exact))
           / np.float32(math.log(MAX_DISTANCE / max_exact)) * np.float32(N_BUCKETS - max_exact))
    large = np.minimum(max_exact + val.astype(np.int32), N_BUCKETS - 1)
    bucket = np.where(d < max_exact, d, large)
    assert np.all(np.diff(bucket) >= 0) and bucket[-1] == N_BUCKETS - 1
    return [int(np.argmax(bucket >= k)) for k in range(N_BUCKETS)]


T5_THRESH = _t5_thresholds()
assert T5_THRESH[-1] <= TK + 1
assert WINDOW == 2 * TK and TQ == TK


def _nt_dot(a, b):
    return lax.dot_general(a, b, (((1,), (1,)), ((), ())), preferred_element_type=F32)


def _dot(a, b):
    return jnp.dot(a, b, preferred_element_type=F32)


def _rms(x, g):
    ms = jnp.mean(x * x, axis=-1, keepdims=True)
    return x * lax.rsqrt(ms + EPS) * g


def _params(sem):
    return pltpu.CompilerParams(dimension_semantics=sem, vmem_limit_bytes=VMEM_LIMIT)


def _store_grouped(out_ref, tok_ref, y):
    tok_ref[...] = y
    nrow = out_ref.shape[0]
    for l in range(CMP_STRIDE):
        out_ref[:, l * LANES:(l + 1) * LANES] = tok_ref[pl.ds(l, nrow, stride=CMP_STRIDE), :].astype(BF16)


def _inproj_kernel(x_ref, g_ref, w_ref, main_ref, kc_ref, vc_ref, gate_ref, tok_ref, *, seq):
    tm = x_ref.shape[0]
    h = _rms(x_ref[...], g_ref[...]).astype(BF16)
    for c in range(0, N_PROJ, MXU_N):
        y = _dot(h, w_ref[:, c:c + MXU_N])
        if c == COL_KSL:
            t0 = lax.rem(pl.program_id(0) * tm, seq)
            blk = (t0 + lax.broadcasted_iota(jnp.int32, (tm, 2 * LANES), 0)) >> 6
            lane = lax.broadcasted_iota(jnp.int32, (tm, 2 * LANES), 1)
            slot = jnp.where(lane < LANES, lane - HALF, lane - LANES)
            y = y + jnp.where((slot == blk) & (slot >= 0) & (slot < N_SLC), NEG, 0.0)
        if c + MXU_N <= N_MAIN:
            main_ref[:, c:c + MXU_N] = y.astype(BF16)
        elif c == COL_VW:
            main_ref[:, c:c + LANES] = y[:, 0:LANES].astype(BF16)
            _store_grouped(kc_ref, tok_ref, y[:, LANES:MXU_N])
        else:
            _store_grouped(vc_ref, tok_ref, y[:, 0:LANES])
            gate_ref[...] = 1.0 / (1.0 + jnp.exp(-y[:, LANES:MXU_N]))


def _inproj(x2, g1, w_p, seq):
    n = x2.shape[0]
    return pl.pallas_call(
        functools.partial(_inproj_kernel, seq=seq),
        grid=(n // TM,),
        in_specs=[pl.BlockSpec((TM, D_MODEL), lambda i: (i, 0)),
                  pl.BlockSpec((1, D_MODEL), lambda i: (0, 0)),
                  pl.BlockSpec((D_MODEL, N_PROJ), lambda i: (0, 0))],
        out_specs=[pl.BlockSpec((TM, N_MAIN), lambda i: (i, 0)),
                   pl.BlockSpec((TM // CMP_STRIDE, CMP_STRIDE * LANES), lambda i: (i, 0)),
                   pl.BlockSpec((TM // CMP_STRIDE, CMP_STRIDE * LANES), lambda i: (i, 0)),
                   pl.BlockSpec((TM, LANES), lambda i: (i, 0))],
        out_shape=[jax.ShapeDtypeStruct((n, N_MAIN), BF16),
                   jax.ShapeDtypeStruct((n // CMP_STRIDE, CMP_STRIDE * LANES), BF16),
                   jax.ShapeDtypeStruct((n // CMP_STRIDE, CMP_STRIDE * LANES), BF16),
                   jax.ShapeDtypeStruct((n, LANES), F32)],
        scratch_shapes=[pltpu.VMEM((TM, LANES), F32)],
        compiler_params=_params(("parallel",)),
    )(x2, g1, w_p)


def _gelu_tanh(x):
    return x * (0.5 * (1.0 + jnp.tanh(math.sqrt(2.0 / math.pi) * (x + 0.044715 * (x * x * x)))))


def _compress_kernel(rk_ref, rv_ref, pos_ref, w1k_ref, w2k_ref, w1v_ref, w2v_ref, kc_ref, vc_ref):
    width = rk_ref.shape[2]
    lane = lax.broadcasted_iota(jnp.int32, (1, width), 1)
    in_group = [jnp.where(((lane >> 6) & 1) == g, 1.0, 0.0) for g in range(NSA_KV)]

    def comp(r_ref, p_top, p_bot, w1_ref, w2_ref):
        r = r_ref[0].astype(F32)
        hid = []
        for g in range(NSA_KV):
            top = _dot(((r + p_top) * in_group[g]).astype(BF16), w1_ref[0])
            bot = _dot(((r + p_bot) * in_group[g]).astype(BF16), w1_ref[1])
            nrow = top.shape[0]
            hid.append(top + pltpu.roll(bot, nrow - 1, 0))
        act = _gelu_tanh(jnp.concatenate(hid, axis=1)).astype(BF16)
        return _dot(act, w2_ref[...]).astype(BF16)

    kc_ref[0] = comp(rk_ref, pos_ref[0:1, :], pos_ref[1:2, :], w1k_ref, w2k_ref)
    vc_ref[0] = comp(rv_ref, pos_ref[2:3, :], pos_ref[3:4, :], w1v_ref, w2v_ref)


def _compress(rk, rv, pos, w1k, w2k, w1v, w2v):
    bsz, nrow, width = rk.shape
    hid2 = 2 * CMP_HIDDEN
    return pl.pallas_call(
        _compress_kernel,
        grid=(bsz,),
        in_specs=[pl.BlockSpec((1, nrow, width), lambda b: (b, 0, 0)),
                  pl.BlockSpec((1, nrow, width), lambda b: (b, 0, 0)),
                  pl.BlockSpec((8, width), lambda b: (0, 0)),
                  pl.BlockSpec((2, width, CMP_HIDDEN), lambda b: (0, 0, 0)),
                  pl.BlockSpec((hid2, LANES), lambda b: (0, 0)),
                  pl.BlockSpec((2, width, CMP_HIDDEN), lambda b: (0, 0, 0)),
                  pl.BlockSpec((hid2, LANES), lambda b: (0, 0))],
        out_specs=[pl.BlockSpec((1, nrow, LANES), lambda b: (b, 0, 0)),
                   pl.BlockSpec((1, nrow, LANES), lambda b: (b, 0, 0))],
        out_shape=[jax.ShapeDtypeStruct((bsz, nrow, LANES), BF16),
                   jax.ShapeDtypeStruct((bsz, nrow, LANES), BF16)],
        compiler_params=_params(("parallel",)),
    )(rk, rv, pos, w1k, w2k, w1v, w2v)


def _half_mask(g):
    lane = lax.broadcasted_iota(jnp.int32, (1, LANES), 1)
    return jnp.where((lane >= g * HALF) & (lane < (g + 1) * HALF), 1.0, 0.0).astype(BF16)


def _nsa_chains(q_ref, extra_ref, tq):
    chains = []
    for qt in range(2):
        rows = slice(qt * tq, (qt + 1) * tq)
        for g in range(NSA_KV):
            keep = _half_mask(g)
            for j in range(NSA_GQ):
                q = q_ref[0, rows, j * LANES:(j + 1) * LANES] * keep
                if extra_ref is not None:
                    q = q + extra_ref[0, rows, g * LANES:(g + 1) * LANES]
                chains.append((q, qt, g, j))
    return chains


def _bias_kernel(rel_ref, o_ref, *, tq, tk):
    row = pl.program_id(0)
    col = jnp.where(row < NSA_HEADS, row + DIFF_HEADS, row - NSA_HEADS)
    r = lax.broadcasted_iota(jnp.int32, (tq, tk), 0)
    c = lax.broadcasted_iota(jnp.int32, (tq, tk), 1)
    far = rel_ref[N_BUCKETS - 1, col]
    for d in range(2):
        dist = r - c + d * tq
        val = jnp.zeros((tq, tk), F32) + rel_ref[0, col]
        for k in range(1, N_BUCKETS):
            val = jnp.where(dist >= T5_THRESH[k], rel_ref[k, col], val)
        val = (val - far) * LOG2E
        if d == 0:
            val = jnp.where(dist >= 0, val, NEG)
        o_ref[0, d] = val
    o_ref[0, 2] = jnp.where(r < c, 0.0, NEG)


def _bias_tiles(rel_bias):
    nhead = rel_bias.shape[1]
    return pl.pallas_call(
        functools.partial(_bias_kernel, tq=TQ, tk=TK),
        grid=(nhead,),
        in_specs=[pl.BlockSpec(memory_space=pltpu.SMEM)],
        out_specs=pl.BlockSpec((1, 3, TQ, TK), lambda h: (h, 0, 0, 0)),
        out_shape=jax.ShapeDtypeStruct((nhead, 3, TQ, TK), F32),
        compiler_params=_params(("parallel",)),
    )(rel_bias)


def _with_near_bias(s, near):
    add = near[0] if len(near) == 1 else jnp.concatenate(near, axis=1)
    width, nw = s.shape[1], add.shape[1]
    if nw == width:
        return s + add
    return jnp.concatenate([s[:, 0:width - nw], s[:, width - nw:width] + add], axis=1)


def _softmax_step(s, v, m_ref, acc_ref):
    tk = s.shape[1]
    m_prev = m_ref[...]
    m_new = jnp.maximum(m_prev, jnp.max(s, axis=-1, keepdims=True))
    alpha = jnp.exp2(m_prev - m_new)
    p = jnp.exp2(s - jnp.concatenate([m_new] * (tk // LANES), axis=1))
    vaug = jnp.concatenate([v, jnp.ones((tk, LANES), BF16)], axis=1)
    acc_ref[...] = jnp.concatenate([alpha, alpha], axis=1) * acc_ref[...] + _dot(p.astype(BF16), vaug)
    m_ref[...] = m_new


def _softmax_first(s, v, m_ref, acc_ref):
    rows, tk = s.shape
    m = jnp.broadcast_to(jnp.max(s, axis=-1, keepdims=True), (rows, LANES))
    p = jnp.exp2(s - jnp.concatenate([m] * (tk // LANES), axis=1))
    vaug = jnp.concatenate([v, jnp.ones((tk, LANES), BF16)], axis=1)
    acc_ref[...] = _dot(p.astype(BF16), vaug)
    m_ref[...] = m


def _softmax_once(s, v):
    rows, tk = s.shape
    m = jnp.broadcast_to(jnp.max(s, axis=-1, keepdims=True), (rows, LANES))
    p = jnp.exp2(s - jnp.concatenate([m] * (tk // LANES), axis=1))
    vaug = jnp.concatenate([v, jnp.ones((tk, LANES), BF16)], axis=1)
    pv = _dot(p.astype(BF16), vaug)
    return pv[:, 0:LANES] * (1.0 / pv[:, LANES:2 * LANES])


def _causal_flash(i, tq, tk, chains, kv_of, bias_of, m_ref, acc_ref, also=None):
    def tail(first):
        stages = list(also(first)) if also is not None else []
        every = max(1, len(chains) // (len(stages) + 1))
        kv = {}
        for c, (q, qt, kv_id, bias_id) in enumerate(chains):
            ntile = (1 if first else 3) + qt
            if (kv_id, ntile) not in kv:
                start = 0 if first else pl.multiple_of((i - 1) * 2 * tk, 2 * tk)
                kv[(kv_id, ntile)] = kv_of(kv_id, start, ntile * tk)
            k, v = kv[(kv_id, ntile)]
            near = [bias_of(bias_id, d) for d in range(min(ntile, 2) - 1, -1, -1)]
            _softmax_first(_with_near_bias(_nt_dot(q, k), near), v, m_ref.at[c], acc_ref.at[c])
            if stages and (c + 1) % every == 0:
                stages.pop(0)()
        while stages:
            stages.pop(0)()

    @pl.when(i == 0)
    def _():
        tail(True)

    @pl.when(i >= 1)
    def _():
        tail(False)

    def far(j, carry):
        start = pl.multiple_of(j * 2 * tk, 2 * tk)
        kv = {}
        for c, (q, _, kv_id, _) in enumerate(chains):
            if kv_id not in kv:
                kv[kv_id] = kv_of(kv_id, start, 2 * tk)
            k, v = kv[kv_id]
            _softmax_step(_nt_dot(q, k), v, m_ref.at[c], acc_ref.at[c])
        return carry

    lax.fori_loop(0, i - 1, far, 0)


def _normalized(acc_ref, c):
    return acc_ref[c, :, 0:LANES] * (1.0 / acc_ref[c, :, LANES:2 * LANES])


def _store_gated(o_ref, gate_ref, branches, tq, qt):
    lane = lax.broadcasted_iota(jnp.int32, (1, LANES), 1)
    rows = slice(qt * tq, (qt + 1) * tq)
    gates = gate_ref[0, rows, :]
    lane_full = lax.broadcasted_iota(jnp.int32, gates.shape, 1)
    for j in range(NSA_GQ):
        total = None
        for o_of, branch in branches:
            c0 = (0 * NSA_GQ + j) * 3 + branch
            c1 = (1 * NSA_GQ + j) * 3 + branch
            gexp = jnp.take_along_axis(gates, jnp.where(lane_full < HALF, c0, c1), axis=1)
            term = gexp * jnp.where(lane < HALF, o_of(0, j), o_of(1, j))
            total = term if total is None else total + term
        o_ref[0, rows, j * LANES:(j + 1) * LANES] = total.astype(BF16)


def _nsa_chain_index(qt, g, j):
    return (qt * NSA_KV + g) * NSA_GQ + j


N_CHAINS = 2 * NSA_HEADS


def _cmp_stages(qi, tq, q_ref, kc_ref, vc_ref, ovt_ref, gate_ref, o_ref, nsel_ref):
    nblk = kc_ref.shape[1]
    lane = lax.broadcasted_iota(jnp.int32, (1, LANES), 1)
    state = {"outs": [], "imp_t": jnp.zeros((LANES, tq), F32), "dropped": []}

    def attend(g):
        n = lax.broadcasted_iota(jnp.int32, (1, nblk), 1)
        t4 = qi * tq + (lax.broadcasted_iota(jnp.int32, (NSA_GQ * tq, 1), 0) & (tq - 1))
        vis4 = (n * CMP_STRIDE + (CMP_LEN - 1)) <= t4
        has4 = jnp.where(t4 >= CMP_LEN - 1, 1.0, 0.0)
        qg = jnp.concatenate([q_ref[0, :, j * LANES:(j + 1) * LANES] * _half_mask(g) for j in range(NSA_GQ)], axis=0)
        s = jnp.where(vis4, _nt_dot(qg, kc_ref[0]), NEG)
        e = jnp.exp2(s - jnp.max(s, axis=-1, keepdims=True))
        p = e * (has4 / jnp.sum(e, axis=-1, keepdims=True))
        state["outs"].append(_dot(p.astype(BF16), vc_ref[0]))
        ps = p[0:tq] + p[tq:2 * tq] + p[2 * tq:3 * tq] + p[3 * tq:4 * tq]
        hi = ps.astype(BF16)
        r1 = ps - hi.astype(F32)
        mid = r1.astype(BF16)
        lo = (r1 - mid.astype(F32)).astype(BF16)
        state["imp_t"] = state["imp_t"] + (
            _nt_dot(ovt_ref[g], hi) + _nt_dot(ovt_ref[g], mid) + _nt_dot(ovt_ref[g], lo))

    def store_out():
        outs = state["outs"]
        _store_gated(o_ref, gate_ref, [(lambda g, j: outs[g][j * tq:(j + 1) * tq], 0)], tq, 0)

    def rank(base):
        tl = qi * tq + lax.broadcasted_iota(jnp.int32, (N_SLC, tq), 1)
        jrow = lax.broadcasted_iota(jnp.int32, (N_SLC, tq), 0)
        cur = tl >> 6
        valid = (jrow * SLC_BLOCK) <= tl
        forced = (jrow == 0) | (jrow == cur) | (jrow == cur - 1)
        sub = 8
        x = jnp.where(forced, jnp.inf, jnp.where(valid, state["imp_t"][base:base + N_SLC], -jnp.inf))
        others = [jnp.broadcast_to(x[jj:jj + 1], (sub, tq)) for jj in range(N_SLC)]
        chunks = []
        for r0 in range(0, N_SLC, sub):
            xr = x[r0:r0 + sub]
            jr = r0 + lax.broadcasted_iota(jnp.int32, (sub, tq), 0)
            cnt = jnp.zeros((sub, tq), F32)
            for jj in range(N_SLC):
                if jj < r0:
                    beats = others[jj] >= xr
                elif jj >= r0 + sub:
                    beats = others[jj] > xr
                else:
                    beats = (others[jj] > xr) | ((others[jj] == xr) & (jr > jj))
                cnt = cnt + jnp.where(beats, 1.0, 0.0)
            chunks.append(jnp.where(cnt < SLC_TOPK, 0.0, 1.0))
        state["dropped"].append(jnp.concatenate(chunks, axis=0))

    def store_selection():
        zero = jnp.zeros((N_SLC, tq), F32)
        dropped = state["dropped"]
        drop = jnp.concatenate([dropped[0], zero, dropped[1], zero], axis=0).T
        nsel_ref[0, :, 0:LANES] = jnp.where(lane >= HALF, drop, 0.0).astype(BF16)
        nsel_ref[0, :, LANES:2 * LANES] = jnp.where(lane < HALF, drop, 0.0).astype(BF16)

    return [functools.partial(attend, 0), functools.partial(attend, 1), store_out,
            functools.partial(rank, 0), functools.partial(rank, HALF), store_selection]


def _slc_win_kernel(bias_ref, q_ref, ks_ref, vs_ref, kw_ref, vw_ref, nsel_ref, gate_ref, o_ref,
                    m_ref, acc_ref, win_ref, *, tq, tk):
    i = pl.program_id(1)
    chains = [(q, qt, g, g * NSA_GQ + j) for q, qt, g, j in _nsa_chains(q_ref, nsel_ref, tq)]
    win_chains = _nsa_chains(q_ref, None, tq)

    def kv_of(g, start, width):
        return (ks_ref[0, pl.ds(start, width), g * LANES:(g + 1) * LANES], vs_ref[0, pl.ds(start, width), :])

    def window(first):
        kv = {}

        def one(q, qt, g, j):
            ntile = 1 + qt if first else 3
            if (qt, ntile) not in kv:
                start = 0 if first else pl.multiple_of((2 * i - 2 + qt) * tk, tk)
                kv[(qt, ntile)] = (kw_ref[0, pl.ds(start, ntile * tk), :], vw_ref[0, pl.ds(start, ntile * tk), :])
            k, v = kv[(qt, ntile)]
            near = [bias_ref[g * NSA_GQ + j, d] for d in range(min(ntile, 2) - 1, -1, -1)]
            if ntile == 3:
                near.insert(0, bias_ref[g * NSA_GQ + j, 2])
            win_ref[_nsa_chain_index(qt, g, j)] = _softmax_once(_with_near_bias(_nt_dot(q, k), near), v)

        def all_chains():
            for chain in win_chains:
                one(*chain)

        return [all_chains]

    _causal_flash(i, tq, tk, chains, kv_of, lambda h, d: bias_ref[h, d], m_ref, acc_ref, also=window)
    for qt in range(2):
        _store_gated(o_ref, gate_ref,
                     [(lambda g, j: _normalized(acc_ref, _nsa_chain_index(qt, g, j)), 1),
                      (lambda g, j: win_ref[_nsa_chain_index(qt, g, j)], 2)], tq, qt)


def _slc_win_attention(bias, main, nsel, gates):
    bsz, seq, _ = main.shape
    tq, tk = TQ, TK
    return pl.pallas_call(
        functools.partial(_slc_win_kernel, tq=tq, tk=tk),
        grid=(bsz, seq // (2 * tq)),
        in_specs=[pl.BlockSpec((NSA_HEADS, 3, tq, tk), lambda b, i: (0, 0, 0, 0), pipeline_mode=pl.Buffered(1)),
                  pl.BlockSpec((1, 2 * tq, NSA_W), lambda b, i: (b, i, COL_NQ // NSA_W)),
                  pl.BlockSpec((1, seq, 2 * LANES), lambda b, i: (b, 0, COL_KSL // (2 * LANES))),
                  pl.BlockSpec((1, seq, LANES), lambda b, i: (b, 0, COL_VSL // LANES)),
                  pl.BlockSpec((1, seq, LANES), lambda b, i: (b, 0, COL_KW // LANES)),
                  pl.BlockSpec((1, seq, LANES), lambda b, i: (b, 0, COL_VW // LANES)),
                  pl.BlockSpec((1, 2 * tq, 2 * LANES), lambda b, i: (b, i, 0)),
                  pl.BlockSpec((1, 2 * tq, LANES), lambda b, i: (b, i, 0))],
        out_specs=pl.BlockSpec((1, 2 * tq, NSA_W), lambda b, i: (b, i, 0)),
        out_shape=jax.ShapeDtypeStruct((bsz, seq, NSA_W), BF16),
        scratch_shapes=[pltpu.VMEM((N_CHAINS, tq, LANES), F32),
                        pltpu.VMEM((N_CHAINS, tq, 2 * LANES), F32),
                        pltpu.VMEM((N_CHAINS, tq, LANES), F32)],
        compiler_params=_params(("parallel", "parallel")),
    )(bias, main, main, main, main, main, nsel, gates)


def _diff_cmp_kernel(bias_ref, lam_ref, q_ref, k_ref, v_ref, g_ref, nq_ref, kc_ref, vc_ref, ovt_ref, gate_ref,
                     o_ref, oc_ref, nsel_ref, m_ref, acc_ref, *, tq, tk):
    i = pl.program_id(1)

    chains = []
    for qt in range(2):
        for h in range(DIFF_HEADS):
            slab = q_ref[0, qt * tq:(qt + 1) * tq, h * LANES:(h + 1) * LANES]
            for half in range(2):
                chains.append((slab * _half_mask(half), qt, h, h))

    def kv_of(h, start, width):
        cols = slice(h * LANES, (h + 1) * LANES)
        return (k_ref[0, pl.ds(start, width), cols], v_ref[0, pl.ds(start, width), cols])

    def compressed(first):
        return _cmp_stages(i, 2 * tq, nq_ref, kc_ref, vc_ref, ovt_ref, gate_ref, oc_ref, nsel_ref)

    _causal_flash(i, tq, tk, chains, kv_of, lambda h, d: bias_ref[h, d], m_ref, acc_ref, also=compressed)

    lm = lam_ref[...]
    lam = (jnp.exp(jnp.sum(lm[0:1] * lm[1:2], axis=-1, keepdims=True))
           - jnp.exp(jnp.sum(lm[2:3] * lm[3:4], axis=-1, keepdims=True)) + LAMBDA_INIT)
    for qt in range(2):
        for h in range(DIFF_HEADS):
            c = (qt * DIFF_HEADS + h) * 2
            a = _normalized(acc_ref, c) - lam * _normalized(acc_ref, c + 1)
            o_ref[0, qt * tq:(qt + 1) * tq, h * LANES:(h + 1) * LANES] = (
                _rms(a, g_ref[...]) * (1.0 - LAMBDA_INIT)).astype(BF16)


def _diff_cmp_attention(bias, lam_rows, main, subln, kc, vc, ovt, gates):
    bsz, seq, _ = main.shape
    nblk = kc.shape[1]
    tq, tk = TQ, TK
    step = lambda b, i: (b, i, 0)
    return pl.pallas_call(
        functools.partial(_diff_cmp_kernel, tq=tq, tk=tk),
        grid=(bsz, seq // (2 * tq)),
        in_specs=[pl.BlockSpec((DIFF_HEADS, 3, tq, tk), lambda b, i: (NSA_HEADS // DIFF_HEADS, 0, 0, 0),
                               pipeline_mode=pl.Buffered(1)),
                  pl.BlockSpec((8, LANES), lambda b, i: (0, 0)),
                  pl.BlockSpec((1, 2 * tq, DIFF_W), lambda b, i: (b, i, COL_DQ // DIFF_W)),
                  pl.BlockSpec((1, seq, DIFF_W), lambda b, i: (b, 0, COL_DK // DIFF_W)),
                  pl.BlockSpec((1, seq, DIFF_W), lambda b, i: (b, 0, COL_DV // DIFF_W)),
                  pl.BlockSpec((1, LANES), lambda b, i: (0, 0)),
                  pl.BlockSpec((1, 2 * tq, NSA_W), lambda b, i: (b, i, COL_NQ // NSA_W)),
                  pl.BlockSpec((1, nblk, LANES), lambda b, i: (b, 0, 0)),
                  pl.BlockSpec((1, nblk, LANES), lambda b, i: (b, 0, 0)),
                  pl.BlockSpec((2, LANES, nblk), lambda b, i: (0, 0, 0)),
                  pl.BlockSpec((1, 2 * tq, LANES), step)],
        out_specs=[pl.BlockSpec((1, 2 * tq, DIFF_W), step),
                   pl.BlockSpec((1, 2 * tq, NSA_W), step),
                   pl.BlockSpec((1, 2 * tq, 2 * LANES), step)],
        out_shape=[jax.ShapeDtypeStruct((bsz, seq, DIFF_W), BF16),
                   jax.ShapeDtypeStruct((bsz, seq, NSA_W), BF16),
                   jax.ShapeDtypeStruct((bsz, seq, 2 * LANES), BF16)],
        scratch_shapes=[pltpu.VMEM((N_CHAINS, tq, LANES), F32),
                        pltpu.VMEM((N_CHAINS, tq, 2 * LANES), F32)],
        compiler_params=_params(("parallel", "parallel")),
    )(bias, lam_rows, main, main, main, subln, main, kc, vc, ovt, gates)


FF_CHUNK = 256


def _tail_kernel(x_ref, od_ref, oc_ref, osw_ref, gn_ref, wo_ref, g2_ref,
                 wg_ref, wu_ref, wd_ref, gf_ref, o_ref, acc_ref):
    o_nsa = oc_ref[...].astype(F32) + osw_ref[...].astype(F32)
    o_nsa = _rms(o_nsa, gn_ref[...]).astype(BF16)
    attn = _dot(od_ref[...], wo_ref[0:DIFF_W, :]) + _dot(o_nsa, wo_ref[DIFF_W:DIFF_W + NSA_W, :])
    x1 = x_ref[...] + attn
    acc_ref[...] = x1
    h = _rms(x1, g2_ref[...]).astype(BF16)
    for c in range(0, D_FF, FF_CHUNK):
        gate = _dot(h, wg_ref[:, c:c + FF_CHUNK])
        up = _dot(h, wu_ref[:, c:c + FF_CHUNK])
        act = (gate * (1.0 / (1.0 + jnp.exp(-gate))) * up).astype(BF16)
        acc_ref[...] += _dot(act, wd_ref[c:c + FF_CHUNK, :])
    o_ref[...] = _rms(acc_ref[...], gf_ref[...])


def _tail(x2, od, oc, osw, gn, w_out, g2, wg, wu, wd, gf):
    n = x2.shape[0]
    row = lambda i: (i, 0)
    fixed = lambda i: (0, 0)
    once = pl.Buffered(1)
    return pl.pallas_call(
        _tail_kernel,
        grid=(n // TM,),
        in_specs=[pl.BlockSpec((TM, D_MODEL), row),
                  pl.BlockSpec((TM, DIFF_W), row),
                  pl.BlockSpec((TM, NSA_W), row),
                  pl.BlockSpec((TM, NSA_W), row),
                  pl.BlockSpec((1, NSA_W), fixed),
                  pl.BlockSpec((DIFF_W + NSA_W, D_MODEL), fixed, pipeline_mode=once),
                  pl.BlockSpec((1, D_MODEL), fixed),
                  pl.BlockSpec((D_MODEL, D_FF), fixed, pipeline_mode=once),
                  pl.BlockSpec((D_MODEL, D_FF), fixed, pipeline_mode=once),
                  pl.BlockSpec((D_FF, D_MODEL), fixed, pipeline_mode=once),
                  pl.BlockSpec((1, D_MODEL), fixed)],
        out_specs=pl.BlockSpec((TM, D_MODEL), row),
        out_shape=jax.ShapeDtypeStruct((n, D_MODEL), F32),
        scratch_shapes=[pltpu.VMEM((TM, D_MODEL), F32)],
        compiler_params=_params(("parallel",)),
    )(x2, od, oc, osw, gn, w_out, g2, wg, wu, wd, gf)


def _permute_heads(a, axis):
    shape = a.shape
    split = shape[:axis] + (NSA_KV, NSA_GQ, NSA_HD) + shape[axis + 1:]
    return jnp.swapaxes(a.reshape(split), axis, axis + 1).reshape(shape)


def _overlap_matrices(nblk):
    n_cmp = nblk - 1
    nslc = nblk * CMP_STRIDE // SLC_BLOCK
    assert nslc == N_SLC
    cmp_start = np.arange(n_cmp) * CMP_STRIDE
    cmp_end = cmp_start + CMP_LEN - 1
    slc_start = np.arange(nslc) * SLC_BLOCK
    overlap = ((cmp_start[:, None] < slc_start[None, :] + SLC_BLOCK) & (cmp_end[:, None] >= slc_start[None, :]))
    ovt = np.zeros((NSA_KV, LANES, nblk), np.float32)
    for g, base in enumerate((HALF, 0)):
        ovt[g, base:base + nslc, :n_cmp] = overlap.T
    return ovt


def _compress_weights(pos, w1, w2):
    half = CMP_LEN // 2
    shape = (2, half, NSA_KV, NSA_HD, CMP_HIDDEN)
    w1rep = jnp.broadcast_to(w1.reshape(2, half, 1, NSA_HD, CMP_HIDDEN), shape).reshape(2, -1, CMP_HIDDEN)
    w2big = jnp.einsum('jd,pg->pjgd', w2, jnp.eye(NSA_KV, dtype=F32)).reshape(NSA_KV * CMP_HIDDEN, NSA_KV * NSA_HD)
    posr = jnp.broadcast_to(pos.reshape(2, half, 1, NSA_HD), (2, half, NSA_KV, NSA_HD)).reshape(2, -1)
    return posr, w1rep.astype(BF16), w2big.astype(BF16)


def kernel(x, norm1, w_in, lambda_q1, lambda_k1, lambda_q2, lambda_k2, diff_subln, cmp_pos_k, cmp_w1_k, cmp_w2_k,
           cmp_pos_v, cmp_w1_v, cmp_w2_v, nsa_norm, w_out, norm2, w_gate, w_up, w_down, rel_bias, final_norm):
    bsz, seq, dm = x.shape
    assert dm == D_MODEL and seq % (2 * TQ) == 0 and seq % TM == 0 and seq == N_SLC * SLC_BLOCK
    assert norm1.shape[0] == 1, "single layer"
    x2 = x.reshape(bsz * seq, dm)

    w = w_in[0]
    zhalf = jnp.zeros((dm, HALF), F32)
    w_p = jnp.concatenate([w[:, :512] * QSCALE, w[:, 512:1536], _permute_heads(w[:, 1536:2048], 1) * QSCALE,
                           w[:, 2304:2368], zhalf, zhalf, w[:, 2368:2432],
                           w[:, 2432:2816],
                           w[:, 2048:2304],
                           w[:, 2816:N_IN], jnp.zeros((dm, LANES - (N_IN - 2816)), F32)], axis=1).astype(BF16)
    assert w_p.shape[1] == N_PROJ
    wo = w_out[0]
    wo_p = jnp.concatenate([wo[:DIFF_W], _permute_heads(wo[DIFF_W:], 0)], axis=0).astype(BF16)
    gn_p = _permute_heads(nsa_norm[0].reshape(1, NSA_W), 1)
    posk, w1k, w2k = _compress_weights(cmp_pos_k[0], cmp_w1_k[0], cmp_w2_k[0])
    posv, w1v, w2v = _compress_weights(cmp_pos_v[0], cmp_w1_v[0], cmp_w2_v[0])
    pos = jnp.concatenate([posk, posv, jnp.zeros((4, posk.shape[1]), F32)], axis=0)
    lam_rows = jnp.zeros((8, LANES), F32).at[0:4, 0:DIFF_HD].set(
        jnp.stack([lambda_q1[0], lambda_k1[0], lambda_q2[0], lambda_k2[0]]).astype(F32))

    main, kc_grp, vc_grp, gates = _inproj(x2, norm1[0].reshape(1, dm), w_p, seq)
    main = main.reshape(bsz, seq, N_MAIN)

    nrow = seq // CMP_STRIDE
    kc, vc = _compress(kc_grp.reshape(bsz, nrow, CMP_STRIDE * LANES), vc_grp.reshape(bsz, nrow, CMP_STRIDE * LANES),
                       pos, w1k, w2k, w1v, w2v)
    ovt = jnp.asarray(_overlap_matrices(nrow), BF16)
    gates = gates.reshape(bsz, seq, LANES)
    bias = _bias_tiles(rel_bias)
    o_diff, o_cmp, nsel = _diff_cmp_attention(bias, lam_rows, main, diff_subln[0].reshape(1, DIFF_VD),
                                              kc, vc, ovt, gates)
    o_sw = _slc_win_attention(bias, main, nsel, gates)

    n = bsz * seq
    out = _tail(x2, o_diff.reshape(n, DIFF_W), o_cmp.reshape(n, NSA_W), o_sw.reshape(n, NSA_W),
                gn_p, wo_p, norm2[0].reshape(1, dm),
                w_gate[0].astype(BF16), w_up[0].astype(BF16), w_down[0].astype(BF16), final_norm.reshape(1, dm))
    return out.reshape(bsz, seq, dm)
```

```python
import functools
import math

import numpy as np
import jax
import jax.numpy as jnp
from jax import lax
from jax.experimental import pallas as pl
from jax.experimental.pallas import tpu as pltpu

F32 = jnp.float32
BF16 = jnp.bfloat16

D_MODEL = 1024
DIFF_HEADS = 4
DIFF_HD = 64
DIFF_VD = 128
DIFF_W = 512
NSA_HEADS = 8
NSA_KV = 2
NSA_GQ = 4
NSA_HD = 64
NSA_W = 512
CMP_LEN = 32
CMP_STRIDE = 16
CMP_HIDDEN = 256
SLC_BLOCK = 64
SLC_TOPK = 16
N_SLC = 32
WINDOW = 512
N_BUCKETS = 32
MAX_DISTANCE = 128
D_FF = 2816
N_IN = 2840
NEG = -1e30
EPS = 1e-6
LAMBDA_INIT = 0.8 - 0.6 * math.exp(-0.3 * 0)
LOG2E = math.log2(math.e)
QSCALE = NSA_HD ** -0.5 * LOG2E

LANES = 128
HALF = LANES // 2
MXU_N = 256
COL_DQ, COL_DK, COL_DV, COL_NQ, COL_KSL, COL_VSL, COL_KW, COL_VW, N_MAIN = (
    0, 512, 1024, 1536, 2048, 2304, 2432, 2560, 2688)
N_PROJ = N_MAIN + 3 * LANES
VMEM_LIMIT = 48 * 1024 * 1024

TQ = 256
TK = 256
TM = 512


def _t5_thresholds():
    d = np.arange(0, 4 * MAX_DISTANCE)
    max_exact = N_BUCKETS // 2
    val = (np.log(np.maximum(d, 1).astype(np.float32) / np.float32(max_exact))
           / np.float32(math.log(MAX_DISTANCE / max_exact)) * np.float32(N_BUCKETS - max_exact))
    large = np.minimum(max_exact + val.astype(np.int32), N_BUCKETS - 1)
    bucket = np.where(d < max_exact, d, large)
    assert np.all(np.diff(bucket) >= 0) and bucket[-1] == N_BUCKETS - 1
    return [int(np.argmax(bucket >= k)) for k in range(N_BUCKETS)]


T5_THRESH = _t5_thresholds()
assert T5_THRESH[-1] <= TK + 1
assert WINDOW == 2 * TK and TQ == TK


def _nt_dot(a, b):
    return lax.dot_general(a, b, (((1,), (1,)), ((), ())), preferred_element_type=F32)


def _dot(a, b):
    return jnp.dot(a, b, preferred_element_type=F32)


def _rms(x, g):
    ms = jnp.mean(x * x, axis=-1, keepdims=True)
    return x * lax.rsqrt(ms + EPS) * g


def _params(sem):
    return pltpu.CompilerParams(dimension_semantics=sem, vmem_limit_bytes=VMEM_LIMIT)


def _store_grouped(out_ref, tok_ref, y):
    tok_ref[...] = y
    nrow = out_ref.shape[0]
    for l in range(CMP_STRIDE):
        out_ref[:, l * LANES:(l + 1) * LANES] = tok_ref[pl.ds(l, nrow, stride=CMP_STRIDE), :].astype(BF16)


def _arrange_w_in(w_ref, wp_ref):
    halves = []
    halves += [(c, QSCALE) for c in range(0, 512, HALF)]
    halves += [(c, None) for c in range(512, 1536, HALF)]
    halves += [(1536 + (g * NSA_GQ + j) * NSA_HD, QSCALE) for j in range(NSA_GQ) for g in range(NSA_KV)]
    halves += [(2304, None), (None, None), (None, None), (2368, None)]
    halves += [(c, None) for c in range(2432, 2816, HALF)]
    halves += [(c, None) for c in range(2048, 2304, HALF)]
    assert len(halves) * HALF == N_PROJ - LANES
    rows = 256
    for r0 in range(0, D_MODEL, rows):
        rs = slice(r0, r0 + rows)

        def load(entry):
            col, scale = entry
            if col is None:
                return jnp.zeros((rows, HALF), F32)
            val = w_ref[rs, col:col + HALF]
            return val if scale is None else val * scale

        for t in range(0, len(halves), 2):
            tile = jnp.concatenate([load(halves[t]), load(halves[t + 1])], axis=1)
            wp_ref[rs, (t // 2) * LANES:(t // 2 + 1) * LANES] = tile.astype(BF16)
        ngate = N_IN - 2816
        gates = jnp.concatenate([w_ref[rs, 2816:N_IN], jnp.zeros((rows, LANES - ngate), F32)], axis=1)
        wp_ref[rs, N_PROJ - LANES:N_PROJ] = gates.astype(BF16)


def _inproj_kernel(x_ref, g_ref, wraw_ref, main_ref, kc_ref, vc_ref, gate_ref, tok_ref, w_ref, *, seq):
    tm = x_ref.shape[0]

    @pl.when(pl.program_id(0) == 0)
    def _():
        _arrange_w_in(wraw_ref, w_ref)

    h = _rms(x_ref[...], g_ref[...]).astype(BF16)
    for c in range(0, N_PROJ, MXU_N):
        y = _dot(h, w_ref[:, c:c + MXU_N])
        if c == COL_KSL:
            t0 = lax.rem(pl.program_id(0) * tm, seq)
            blk = (t0 + lax.broadcasted_iota(jnp.int32, (tm, 2 * LANES), 0)) >> 6
            lane = lax.broadcasted_iota(jnp.int32, (tm, 2 * LANES), 1)
            slot = jnp.where(lane < LANES, lane - HALF, lane - LANES)
            y = y + jnp.where((slot == blk) & (slot >= 0) & (slot < N_SLC), NEG, 0.0)
        if c + MXU_N <= N_MAIN:
            main_ref[:, c:c + MXU_N] = y.astype(BF16)
        elif c == COL_VW:
            main_ref[:, c:c + LANES] = y[:, 0:LANES].astype(BF16)
            _store_grouped(kc_ref, tok_ref, y[:, LANES:MXU_N])
        else:
            _store_grouped(vc_ref, tok_ref, y[:, 0:LANES])
            gate_ref[...] = 1.0 / (1.0 + jnp.exp(-y[:, LANES:MXU_N]))


def _inproj(x2, g1, w_in, seq):
    n = x2.shape[0]
    return pl.pallas_call(
        functools.partial(_inproj_kernel, seq=seq),
        grid=(n // TM,),
        in_specs=[pl.BlockSpec((TM, D_MODEL), lambda i: (i, 0)),
                  pl.BlockSpec((1, D_MODEL), lambda i: (0, 0)),
                  pl.BlockSpec((D_MODEL, N_IN), lambda i: (0, 0), pipeline_mode=pl.Buffered(1))],
        out_specs=[pl.BlockSpec((TM, N_MAIN), lambda i: (i, 0)),
                   pl.BlockSpec((TM // CMP_STRIDE, CMP_STRIDE * LANES), lambda i: (i, 0)),
                   pl.BlockSpec((TM // CMP_STRIDE, CMP_STRIDE * LANES), lambda i: (i, 0)),
                   pl.BlockSpec((TM, LANES), lambda i: (i, 0))],
        out_shape=[jax.ShapeDtypeStruct((n, N_MAIN), BF16),
                   jax.ShapeDtypeStruct((n // CMP_STRIDE, CMP_STRIDE * LANES), BF16),
                   jax.ShapeDtypeStruct((n // CMP_STRIDE, CMP_STRIDE * LANES), BF16),
                   jax.ShapeDtypeStruct((n, LANES), F32)],
        scratch_shapes=[pltpu.VMEM((TM, LANES), F32),
                        pltpu.VMEM((D_MODEL, N_PROJ), BF16)],
        compiler_params=_params(("arbitrary",)),
    )(x2, g1, w_in)


def _gelu_tanh(x):
    return x * (0.5 * (1.0 + jnp.tanh(math.sqrt(2.0 / math.pi) * (x + 0.044715 * (x * x * x)))))


def _compress_kernel(rk_ref, rv_ref, pos_ref, w1k_ref, w2k_ref, w1v_ref, w2v_ref, kc_ref, vc_ref):
    width = rk_ref.shape[2]
    lane = lax.broadcasted_iota(jnp.int32, (1, width), 1)
    in_group = [jnp.where(((lane >> 6) & 1) == g, 1.0, 0.0) for g in range(NSA_KV)]

    def comp(r_ref, p_top, p_bot, w1_ref, w2_ref):
        r = r_ref[0].astype(F32)
        hid = []
        for g in range(NSA_KV):
            top = _dot(((r + p_top) * in_group[g]).astype(BF16), w1_ref[0])
            bot = _dot(((r + p_bot) * in_group[g]).astype(BF16), w1_ref[1])
            nrow = top.shape[0]
            hid.append(top + pltpu.roll(bot, nrow - 1, 0))
        act = _gelu_tanh(jnp.concatenate(hid, axis=1)).astype(BF16)
        return _dot(act, w2_ref[...]).astype(BF16)

    kc_ref[0] = comp(rk_ref, pos_ref[0:1, :], pos_ref[1:2, :], w1k_ref, w2k_ref)
    vc_ref[0] = comp(rv_ref, pos_ref[2:3, :], pos_ref[3:4, :], w1v_ref, w2v_ref)


def _compress(rk, rv, pos, w1k, w2k, w1v, w2v):
    bsz, nrow, width = rk.shape
    hid2 = 2 * CMP_HIDDEN
    return pl.pallas_call(
        _compress_kernel,
        grid=(bsz,),
        in_specs=[pl.BlockSpec((1, nrow, width), lambda b: (b, 0, 0)),
                  pl.BlockSpec((1, nrow, width), lambda b: (b, 0, 0)),
                  pl.BlockSpec((8, width), lambda b: (0, 0)),
                  pl.BlockSpec((2, width, CMP_HIDDEN), lambda b: (0, 0, 0)),
                  pl.BlockSpec((hid2, LANES), lambda b: (0, 0)),
                  pl.BlockSpec((2, width, CMP_HIDDEN), lambda b: (0, 0, 0)),
                  pl.BlockSpec((hid2, LANES), lambda b: (0, 0))],
        out_specs=[pl.BlockSpec((1, nrow, LANES), lambda b: (b, 0, 0)),
                   pl.BlockSpec((1, nrow, LANES), lambda b: (b, 0, 0))],
        out_shape=[jax.ShapeDtypeStruct((bsz, nrow, LANES), BF16),
                   jax.ShapeDtypeStruct((bsz, nrow, LANES), BF16)],
        compiler_params=_params(("parallel",)),
    )(rk, rv, pos, w1k, w2k, w1v, w2v)


def _half_mask(g):
    lane = lax.broadcasted_iota(jnp.int32, (1, LANES), 1)
    return jnp.where((lane >= g * HALF) & (lane < (g + 1) * HALF), 1.0, 0.0).astype(BF16)


def _nsa_chains(q_ref, extra_ref, tq):
    chains = []
    for qt in range(2):
        rows = slice(qt * tq, (qt + 1) * tq)
        for g in range(NSA_KV):
            keep = _half_mask(g)
            for j in range(NSA_GQ):
                q = q_ref[0, rows, j * LANES:(j + 1) * LANES] * keep
                if extra_ref is not None:
                    q = q + extra_ref[0, rows, g * LANES:(g + 1) * LANES]
                chains.append((q, qt, g, j))
    return chains


def _bias_kernel(rel_ref, o_ref, *, tq, tk):
    row = pl.program_id(0)
    col = jnp.where(row < NSA_HEADS, row + DIFF_HEADS, row - NSA_HEADS)
    r = lax.broadcasted_iota(jnp.int32, (tq, tk), 0)
    c = lax.broadcasted_iota(jnp.int32, (tq, tk), 1)
    far = rel_ref[N_BUCKETS - 1, col]
    for d in range(2):
        dist = r - c + d * tq
        val = jnp.zeros((tq, tk), F32) + rel_ref[0, col]
        for k in range(1, N_BUCKETS):
            val = jnp.where(dist >= T5_THRESH[k], rel_ref[k, col], val)
        val = (val - far) * LOG2E
        if d == 0:
            val = jnp.where(dist >= 0, val, NEG)
        o_ref[0, d] = val
    o_ref[0, 2] = jnp.where(r < c, 0.0, NEG)


def _bias_tiles(rel_bias):
    nhead = rel_bias.shape[1]
    return pl.pallas_call(
        functools.partial(_bias_kernel, tq=TQ, tk=TK),
        grid=(nhead,),
        in_specs=[pl.BlockSpec(memory_space=pltpu.SMEM)],
        out_specs=pl.BlockSpec((1, 3, TQ, TK), lambda h: (h, 0, 0, 0)),
        out_shape=jax.ShapeDtypeStruct((nhead, 3, TQ, TK), F32),
        compiler_params=_params(("parallel",)),
    )(rel_bias)


def _with_near_bias(s, near):
    add = near[0] if len(near) == 1 else jnp.concatenate(near, axis=1)
    width, nw = s.shape[1], add.shape[1]
    if nw == width:
        return s + add
    return jnp.concatenate([s[:, 0:width - nw], s[:, width - nw:width] + add], axis=1)


def _softmax_step(s, v, m_ref, acc_ref):
    tk = s.shape[1]
    m_prev = m_ref[...]
    m_new = jnp.maximum(m_prev, jnp.max(s, axis=-1, keepdims=True))
    alpha = jnp.exp2(m_prev - m_new)
    p = jnp.exp2(s - jnp.concatenate([m_new] * (tk // LANES), axis=1))
    vaug = jnp.concatenate([v, jnp.ones((tk, LANES), BF16)], axis=1)
    acc_ref[...] = jnp.concatenate([alpha, alpha], axis=1) * acc_ref[...] + _dot(p.astype(BF16), vaug)
    m_ref[...] = m_new


def _softmax_first(s, v, m_ref, acc_ref):
    rows, tk = s.shape
    m = jnp.broadcast_to(jnp.max(s, axis=-1, keepdims=True), (rows, LANES))
    p = jnp.exp2(s - jnp.concatenate([m] * (tk // LANES), axis=1))
    vaug = jnp.concatenate([v, jnp.ones((tk, LANES), BF16)], axis=1)
    acc_ref[...] = _dot(p.astype(BF16), vaug)
    m_ref[...] = m


def _softmax_once(s, v):
    rows, tk = s.shape
    m = jnp.broadcast_to(jnp.max(s, axis=-1, keepdims=True), (rows, LANES))
    p = jnp.exp2(s - jnp.concatenate([m] * (tk // LANES), axis=1))
    vaug = jnp.concatenate([v, jnp.ones((tk, LANES), BF16)], axis=1)
    pv = _dot(p.astype(BF16), vaug)
    return pv[:, 0:LANES] * (1.0 / pv[:, LANES:2 * LANES])


def _causal_flash(i, tq, tk, chains, kv_of, bias_of, m_ref, acc_ref, also=None):
    def tail(first):
        stages = list(also(first)) if also is not None else []
        every = max(1, len(chains) // (len(stages) + 1))
        kv = {}
        for c, (q, qt, kv_id, bias_id) in enumerate(chains):
            ntile = (1 if first else 3) + qt
            if (kv_id, ntile) not in kv:
                start = 0 if first else pl.multiple_of((i - 1) * 2 * tk, 2 * tk)
                kv[(kv_id, ntile)] = kv_of(kv_id, start, ntile * tk)
            k, v = kv[(kv_id, ntile)]
            near = [bias_of(bias_id, d) for d in range(min(ntile, 2) - 1, -1, -1)]
            _softmax_first(_with_near_bias(_nt_dot(q, k), near), v, m_ref.at[c], acc_ref.at[c])
            if stages and (c + 1) % every == 0:
                stages.pop(0)()
        while stages:
            stages.pop(0)()

    @pl.when(i == 0)
    def _():
        tail(True)

    @pl.when(i >= 1)
    def _():
        tail(False)

    def far(j, carry):
        start = pl.multiple_of(j * 2 * tk, 2 * tk)
        kv = {}
        for c, (q, _, kv_id, _) in enumerate(chains):
            if kv_id not in kv:
                kv[kv_id] = kv_of(kv_id, start, 2 * tk)
            k, v = kv[kv_id]
            _softmax_step(_nt_dot(q, k), v, m_ref.at[c], acc_ref.at[c])
        return carry

    lax.fori_loop(0, i - 1, far, 0)


def _normalized(acc_ref, c):
    return acc_ref[c, :, 0:LANES] * (1.0 / acc_ref[c, :, LANES:2 * LANES])


def _store_gated(o_ref, gate_ref, branches, tq, qt):
    lane = lax.broadcasted_iota(jnp.int32, (1, LANES), 1)
    rows = slice(qt * tq, (qt + 1) * tq)
    gates = gate_ref[0, rows, :]
    lane_full = lax.broadcasted_iota(jnp.int32, gates.shape, 1)
    for j in range(NSA_GQ):
        total = None
        for o_of, branch in branches:
            c0 = (0 * NSA_GQ + j) * 3 + branch
            c1 = (1 * NSA_GQ + j) * 3 + branch
            gexp = jnp.take_along_axis(gates, jnp.where(lane_full < HALF, c0, c1), axis=1)
            term = gexp * jnp.where(lane < HALF, o_of(0, j), o_of(1, j))
            total = term if total is None else total + term
        o_ref[0, rows, j * LANES:(j + 1) * LANES] = total.astype(BF16)


def _nsa_chain_index(qt, g, j):
    return (qt * NSA_KV + g) * NSA_GQ + j


N_CHAINS = 2 * NSA_HEADS


def _cmp_stages(qi, tq, q_ref, kc_ref, vc_ref, ovt_ref, gate_ref, o_ref, nsel_ref):
    nblk = kc_ref.shape[1]
    lane = lax.broadcasted_iota(jnp.int32, (1, LANES), 1)
    state = {"outs": [], "imp_t": jnp.zeros((LANES, tq), F32), "dropped": []}

    def attend(g):
        n = lax.broadcasted_iota(jnp.int32, (1, nblk), 1)
        t4 = qi * tq + (lax.broadcasted_iota(jnp.int32, (NSA_GQ * tq, 1), 0) & (tq - 1))
        vis4 = (n * CMP_STRIDE + (CMP_LEN - 1)) <= t4
        has4 = jnp.where(t4 >= CMP_LEN - 1, 1.0, 0.0)
        qg = jnp.concatenate([q_ref[0, :, j * LANES:(j + 1) * LANES] * _half_mask(g) for j in range(NSA_GQ)], axis=0)
        s = jnp.where(vis4, _nt_dot(qg, kc_ref[0]), NEG)
        e = jnp.exp2(s - jnp.max(s, axis=-1, keepdims=True))
        p = e * (has4 / jnp.sum(e, axis=-1, keepdims=True))
        state["outs"].append(_dot(p.astype(BF16), vc_ref[0]))
        ps = p[0:tq] + p[tq:2 * tq] + p[2 * tq:3 * tq] + p[3 * tq:4 * tq]
        hi = ps.astype(BF16)
        r1 = ps - hi.astype(F32)
        mid = r1.astype(BF16)
        lo = (r1 - mid.astype(F32)).astype(BF16)
        state["imp_t"] = state["imp_t"] + (
            _nt_dot(ovt_ref[g], hi) + _nt_dot(ovt_ref[g], mid) + _nt_dot(ovt_ref[g], lo))

    def store_out():
        outs = state["outs"]
        _store_gated(o_ref, gate_ref, [(lambda g, j: outs[g][j * tq:(j + 1) * tq], 0)], tq, 0)

    def rank(base):
        tl = qi * tq + lax.broadcasted_iota(jnp.int32, (N_SLC, tq), 1)
        jrow = lax.broadcasted_iota(jnp.int32, (N_SLC, tq), 0)
        cur = tl >> 6
        valid = (jrow * SLC_BLOCK) <= tl
        forced = (jrow == 0) | (jrow == cur) | (jrow == cur - 1)
        sub = 8
        x = jnp.where(forced, jnp.inf, jnp.where(valid, state["imp_t"][base:base + N_SLC], -jnp.inf))
        others = [jnp.broadcast_to(x[jj:jj + 1], (sub, tq)) for jj in range(N_SLC)]
        chunks = []
        for r0 in range(0, N_SLC, sub):
            xr = x[r0:r0 + sub]
            jr = r0 + lax.broadcasted_iota(jnp.int32, (sub, tq), 0)
            cnt = jnp.zeros((sub, tq), F32)
            for jj in range(N_SLC):
                if jj < r0:
                    beats = others[jj] >= xr
                elif jj >= r0 + sub:
                    beats = others[jj] > xr
                else:
                    beats = (others[jj] > xr) | ((others[jj] == xr) & (jr > jj))
                cnt = cnt + jnp.where(beats, 1.0, 0.0)
            chunks.append(jnp.where(cnt < SLC_TOPK, 0.0, 1.0))
        state["dropped"].append(jnp.concatenate(chunks, axis=0))

    def store_selection():
        zero = jnp.zeros((N_SLC, tq), F32)
        dropped = state["dropped"]
        drop = jnp.concatenate([dropped[0], zero, dropped[1], zero], axis=0).T
        nsel_ref[0, :, 0:LANES] = jnp.where(lane >= HALF, drop, 0.0).astype(BF16)
        nsel_ref[0, :, LANES:2 * LANES] = jnp.where(lane < HALF, drop, 0.0).astype(BF16)

    return [functools.partial(attend, 0), functools.partial(attend, 1), store_out,
            functools.partial(rank, 0), functools.partial(rank, HALF), store_selection]


def _slc_win_kernel(bias_ref, q_ref, ks_ref, vs_ref, kw_ref, vw_ref, nsel_ref, gate_ref, o_ref,
                    m_ref, acc_ref, win_ref, *, tq, tk):
    i = pl.program_id(1)
    chains = [(q, qt, g, g * NSA_GQ + j) for q, qt, g, j in _nsa_chains(q_ref, nsel_ref, tq)]
    win_chains = _nsa_chains(q_ref, None, tq)

    def kv_of(g, start, width):
        return (ks_ref[0, pl.ds(start, width), g * LANES:(g + 1) * LANES], vs_ref[0, pl.ds(start, width), :])

    def window(first):
        kv = {}

        def one(q, qt, g, j):
            ntile = 1 + qt if first else 3
            if (qt, ntile) not in kv:
                start = 0 if first else pl.multiple_of((2 * i - 2 + qt) * tk, tk)
                kv[(qt, ntile)] = (kw_ref[0, pl.ds(start, ntile * tk), :], vw_ref[0, pl.ds(start, ntile * tk), :])
            k, v = kv[(qt, ntile)]
            near = [bias_ref[g * NSA_GQ + j, d] for d in range(min(ntile, 2) - 1, -1, -1)]
            if ntile == 3:
                near.insert(0, bias_ref[g * NSA_GQ + j, 2])
            win_ref[_nsa_chain_index(qt, g, j)] = _softmax_once(_with_near_bias(_nt_dot(q, k), near), v)

        def all_chains():
            for chain in win_chains:
                one(*chain)

        return [all_chains]

    _causal_flash(i, tq, tk, chains, kv_of, lambda h, d: bias_ref[h, d], m_ref, acc_ref, also=window)
    for qt in range(2):
        _store_gated(o_ref, gate_ref,
                     [(lambda g, j: _normalized(acc_ref, _nsa_chain_index(qt, g, j)), 1),
                      (lambda g, j: win_ref[_nsa_chain_index(qt, g, j)], 2)], tq, qt)


def _slc_win_attention(bias, main, nsel, gates):
    bsz, seq, _ = main.shape
    tq, tk = TQ, TK
    return pl.pallas_call(
        functools.partial(_slc_win_kernel, tq=tq, tk=tk),
        grid=(bsz, seq // (2 * tq)),
        in_specs=[pl.BlockSpec((NSA_HEADS, 3, tq, tk), lambda b, i: (0, 0, 0, 0), pipeline_mode=pl.Buffered(1)),
                  pl.BlockSpec((1, 2 * tq, NSA_W), lambda b, i: (b, i, COL_NQ // NSA_W)),
                  pl.BlockSpec((1, seq, 2 * LANES), lambda b, i: (b, 0, COL_KSL // (2 * LANES))),
                  pl.BlockSpec((1, seq, LANES), lambda b, i: (b, 0, COL_VSL // LANES)),
                  pl.BlockSpec((1, seq, LANES), lambda b, i: (b, 0, COL_KW // LANES)),
                  pl.BlockSpec((1, seq, LANES), lambda b, i: (b, 0, COL_VW // LANES)),
                  pl.BlockSpec((1, 2 * tq, 2 * LANES), lambda b, i: (b, i, 0)),
                  pl.BlockSpec((1, 2 * tq, LANES), lambda b, i: (b, i, 0))],
        out_specs=pl.BlockSpec((1, 2 * tq, NSA_W), lambda b, i: (b, i, 0)),
        out_shape=jax.ShapeDtypeStruct((bsz, seq, NSA_W), BF16),
        scratch_shapes=[pltpu.VMEM((N_CHAINS, tq, LANES), F32),
                        pltpu.VMEM((N_CHAINS, tq, 2 * LANES), F32),
                        pltpu.VMEM((N_CHAINS, tq, LANES), F32)],
        compiler_params=_params(("parallel", "parallel")),
    )(bias, main, main, main, main, main, nsel, gates)


def _diff_cmp_kernel(bias_ref, lam_ref, q_ref, k_ref, v_ref, g_ref, nq_ref, kc_ref, vc_ref, ovt_ref, gate_ref,
                     o_ref, oc_ref, nsel_ref, m_ref, acc_ref, *, tq, tk):
    i = pl.program_id(1)

    chains = []
    for qt in range(2):
        for h in range(DIFF_HEADS):
            slab = q_ref[0, qt * tq:(qt + 1) * tq, h * LANES:(h + 1) * LANES]
            for half in range(2):
                chains.append((slab * _half_mask(half), qt, h, h))

    def kv_of(h, start, width):
        cols = slice(h * LANES, (h + 1) * LANES)
        return (k_ref[0, pl.ds(start, width), cols], v_ref[0, pl.ds(start, width), cols])

    def compressed(first):
        return _cmp_stages(i, 2 * tq, nq_ref, kc_ref, vc_ref, ovt_ref, gate_ref, oc_ref, nsel_ref)

    _causal_flash(i, tq, tk, chains, kv_of, lambda h, d: bias_ref[h, d], m_ref, acc_ref, also=compressed)

    lm = lam_ref[...]
    lam = (jnp.exp(jnp.sum(lm[0:1] * lm[1:2], axis=-1, keepdims=True))
           - jnp.exp(jnp.sum(lm[2:3] * lm[3:4], axis=-1, keepdims=True)) + LAMBDA_INIT)
    for qt in range(2):
        for h in range(DIFF_HEADS):
            c = (qt * DIFF_HEADS + h) * 2
            a = _normalized(acc_ref, c) - lam * _normalized(acc_ref, c + 1)
            o_ref[0, qt * tq:(qt + 1) * tq, h * LANES:(h + 1) * LANES] = (
                _rms(a, g_ref[...]) * (1.0 - LAMBDA_INIT)).astype(BF16)


def _diff_cmp_attention(bias, lam_rows, main, subln, kc, vc, ovt, gates):
    bsz, seq, _ = main.shape
    nblk = kc.shape[1]
    tq, tk = TQ, TK
    step = lambda b, i: (b, i, 0)
    return pl.pallas_call(
        functools.partial(_diff_cmp_kernel, tq=tq, tk=tk),
        grid=(bsz, seq // (2 * tq)),
        in_specs=[pl.BlockSpec((DIFF_HEADS, 3, tq, tk), lambda b, i: (NSA_HEADS // DIFF_HEADS, 0, 0, 0),
                               pipeline_mode=pl.Buffered(1)),
                  pl.BlockSpec((8, LANES), lambda b, i: (0, 0)),
                  pl.BlockSpec((1, 2 * tq, DIFF_W), lambda b, i: (b, i, COL_DQ // DIFF_W)),
                  pl.BlockSpec((1, seq, DIFF_W), lambda b, i: (b, 0, COL_DK // DIFF_W)),
                  pl.BlockSpec((1, seq, DIFF_W), lambda b, i: (b, 0, COL_DV // DIFF_W)),
                  pl.BlockSpec((1, LANES), lambda b, i: (0, 0)),
                  pl.BlockSpec((1, 2 * tq, NSA_W), lambda b, i: (b, i, COL_NQ // NSA_W)),
                  pl.BlockSpec((1, nblk, LANES), lambda b, i: (b, 0, 0)),
                  pl.BlockSpec((1, nblk, LANES), lambda b, i: (b, 0, 0)),
                  pl.BlockSpec((2, LANES, nblk), lambda b, i: (0, 0, 0)),
                  pl.BlockSpec((1, 2 * tq, LANES), step)],
        out_specs=[pl.BlockSpec((1, 2 * tq, DIFF_W), step),
                   pl.BlockSpec((1, 2 * tq, NSA_W), step),
                   pl.BlockSpec((1, 2 * tq, 2 * LANES), step)],
        out_shape=[jax.ShapeDtypeStruct((bsz, seq, DIFF_W), BF16),
                   jax.ShapeDtypeStruct((bsz, seq, NSA_W), BF16),
                   jax.ShapeDtypeStruct((bsz, seq, 2 * LANES), BF16)],
        scratch_shapes=[pltpu.VMEM((N_CHAINS, tq, LANES), F32),
                        pltpu.VMEM((N_CHAINS, tq, 2 * LANES), F32)],
        compiler_params=_params(("parallel", "parallel")),
    )(bias, lam_rows, main, main, main, subln, main, kc, vc, ovt, gates)


FF_CHUNK = 256


def _tail_kernel(x_ref, od_ref, oc_ref, osw_ref, gn_ref, woraw_ref, g2_ref,
                 wg_ref, wu_ref, wd_ref, gf_ref, o_ref, acc_ref, wo_ref):
    @pl.when(pl.program_id(0) == 0)
    def _():
        wo_ref[0:DIFF_W, :] = woraw_ref[0:DIFF_W, :].astype(BF16)
        for j in range(NSA_GQ):
            for g in range(NSA_KV):
                dst = DIFF_W + (j * NSA_KV + g) * NSA_HD
                src = DIFF_W + (g * NSA_GQ + j) * NSA_HD
                wo_ref[dst:dst + NSA_HD, :] = woraw_ref[src:src + NSA_HD, :].astype(BF16)

    o_nsa = oc_ref[...].astype(F32) + osw_ref[...].astype(F32)
    o_nsa = _rms(o_nsa, gn_ref[...]).astype(BF16)
    attn = _dot(od_ref[...], wo_ref[0:DIFF_W, :]) + _dot(o_nsa, wo_ref[DIFF_W:DIFF_W + NSA_W, :])
    x1 = x_ref[...] + attn
    acc_ref[...] = x1
    h = _rms(x1, g2_ref[...]).astype(BF16)
    for c in range(0, D_FF, FF_CHUNK):
        gate = _dot(h, wg_ref[:, c:c + FF_CHUNK])
        up = _dot(h, wu_ref[:, c:c + FF_CHUNK])
        act = (gate * (1.0 / (1.0 + jnp.exp(-gate))) * up).astype(BF16)
        acc_ref[...] += _dot(act, wd_ref[c:c + FF_CHUNK, :])
    o_ref[...] = _rms(acc_ref[...], gf_ref[...])


def _tail(x2, od, oc, osw, gn, w_out, g2, wg, wu, wd, gf):
    n = x2.shape[0]
    row = lambda i: (i, 0)
    fixed = lambda i: (0, 0)
    once = pl.Buffered(1)
    return pl.pallas_call(
        _tail_kernel,
        grid=(n // TM,),
        in_specs=[pl.BlockSpec((TM, D_MODEL), row),
                  pl.BlockSpec((TM, DIFF_W), row),
                  pl.BlockSpec((TM, NSA_W), row),
                  pl.BlockSpec((TM, NSA_W), row),
                  pl.BlockSpec((1, NSA_W), fixed),
                  pl.BlockSpec((DIFF_W + NSA_W, D_MODEL), fixed, pipeline_mode=once),
                  pl.BlockSpec((1, D_MODEL), fixed),
                  pl.BlockSpec((D_MODEL, D_FF), fixed, pipeline_mode=once),
                  pl.BlockSpec((D_MODEL, D_FF), fixed, pipeline_mode=once),
                  pl.BlockSpec((D_FF, D_MODEL), fixed, pipeline_mode=once),
                  pl.BlockSpec((1, D_MODEL), fixed)],
        out_specs=pl.BlockSpec((TM, D_MODEL), row),
        out_shape=jax.ShapeDtypeStruct((n, D_MODEL), F32),
        scratch_shapes=[pltpu.VMEM((TM, D_MODEL), F32),
                        pltpu.VMEM((DIFF_W + NSA_W, D_MODEL), BF16)],
        compiler_params=_params(("arbitrary",)),
    )(x2, od, oc, osw, gn, w_out, g2, wg, wu, wd, gf)


def _permute_heads(a, axis):
    shape = a.shape
    split = shape[:axis] + (NSA_KV, NSA_GQ, NSA_HD) + shape[axis + 1:]
    return jnp.swapaxes(a.reshape(split), axis, axis + 1).reshape(shape)


def _overlap_matrices(nblk):
    n_cmp = nblk - 1
    nslc = nblk * CMP_STRIDE // SLC_BLOCK
    assert nslc == N_SLC
    cmp_start = np.arange(n_cmp) * CMP_STRIDE
    cmp_end = cmp_start + CMP_LEN - 1
    slc_start = np.arange(nslc) * SLC_BLOCK
    overlap = ((cmp_start[:, None] < slc_start[None, :] + SLC_BLOCK) & (cmp_end[:, None] >= slc_start[None, :]))
    ovt = np.zeros((NSA_KV, LANES, nblk), np.float32)
    for g, base in enumerate((HALF, 0)):
        ovt[g, base:base + nslc, :n_cmp] = overlap.T
    return ovt


def _compress_weights(pos, w1, w2):
    half = CMP_LEN // 2
    shape = (2, half, NSA_KV, NSA_HD, CMP_HIDDEN)
    w1rep = jnp.broadcast_to(w1.reshape(2, half, 1, NSA_HD, CMP_HIDDEN), shape).reshape(2, -1, CMP_HIDDEN)
    w2big = jnp.einsum('jd,pg->pjgd', w2, jnp.eye(NSA_KV, dtype=F32)).reshape(NSA_KV * CMP_HIDDEN, NSA_KV * NSA_HD)
    posr = jnp.broadcast_to(pos.reshape(2, half, 1, NSA_HD), (2, half, NSA_KV, NSA_HD)).reshape(2, -1)
    return posr, w1rep.astype(BF16), w2big.astype(BF16)


def kernel(x, norm1, w_in, lambda_q1, lambda_k1, lambda_q2, lambda_k2, diff_subln, cmp_pos_k, cmp_w1_k, cmp_w2_k,
           cmp_pos_v, cmp_w1_v, cmp_w2_v, nsa_norm, w_out, norm2, w_gate, w_up, w_down, rel_bias, final_norm):
    bsz, seq, dm = x.shape
    assert dm == D_MODEL and seq % (2 * TQ) == 0 and seq % TM == 0 and seq == N_SLC * SLC_BLOCK
    assert norm1.shape[0] == 1, "single layer"
    x2 = x.reshape(bsz * seq, dm)

    gn_p = _permute_heads(nsa_norm[0].reshape(1, NSA_W), 1)
    posk, w1k, w2k = _compress_weights(cmp_pos_k[0], cmp_w1_k[0], cmp_w2_k[0])
    posv, w1v, w2v = _compress_weights(cmp_pos_v[0], cmp_w1_v[0], cmp_w2_v[0])
    pos = jnp.concatenate([posk, posv, jnp.zeros((4, posk.shape[1]), F32)], axis=0)
    lam_rows = jnp.zeros((8, LANES), F32).at[0:4, 0:DIFF_HD].set(
        jnp.stack([lambda_q1[0], lambda_k1[0], lambda_q2[0], lambda_k2[0]]).astype(F32))

    main, kc_grp, vc_grp, gates = _inproj(x2, norm1[0].reshape(1, dm), w_in[0], seq)
    main = main.reshape(bsz, seq, N_MAIN)

    nrow = seq // CMP_STRIDE
    kc, vc = _compress(kc_grp.reshape(bsz, nrow, CMP_STRIDE * LANES), vc_grp.reshape(bsz, nrow, CMP_STRIDE * LANES),
                       pos, w1k, w2k, w1v, w2v)
    ovt = jnp.asarray(_overlap_matrices(nrow), BF16)
    gates = gates.reshape(bsz, seq, LANES)
    bias = _bias_tiles(rel_bias)
    o_diff, o_cmp, nsel = _diff_cmp_attention(bias, lam_rows, main, diff_subln[0].reshape(1, DIFF_VD),
                                              kc, vc, ovt, gates)
    o_sw = _slc_win_attention(bias, main, nsel, gates)

    n = bsz * seq
    out = _tail(x2, o_diff.reshape(n, DIFF_W), o_cmp.reshape(n, NSA_W), o_sw.reshape(n, NSA_W),
                gn_p, w_out[0], norm2[0].reshape(1, dm),
                w_gate[0].astype(BF16), w_up[0].astype(BF16), w_down[0].astype(BF16), final_norm.reshape(1, dm))
    return out.reshape(bsz, seq, dm)
```

```python
import functools
import math

import numpy as np
import jax
import jax.numpy as jnp
from jax import lax
from jax.experimental import pallas as pl
from jax.experimental.pallas import tpu as pltpu

F32 = jnp.float32
BF16 = jnp.bfloat16

D_MODEL = 1024
DIFF_HEADS = 4
DIFF_HD = 64
DIFF_VD = 128
DIFF_W = 512
NSA_HEADS = 8
NSA_KV = 2
NSA_GQ = 4
NSA_HD = 64
NSA_W = 512
CMP_LEN = 32
CMP_STRIDE = 16
CMP_HIDDEN = 256
SLC_BLOCK = 64
SLC_TOPK = 16
N_SLC = 32
WINDOW = 512
N_BUCKETS = 32
MAX_DISTANCE = 128
D_FF = 2816
N_IN = 2840
NEG = -1e30
EPS = 1e-6
LAMBDA_INIT = 0.8 - 0.6 * math.exp(-0.3 * 0)
LOG2E = math.log2(math.e)
QSCALE = NSA_HD ** -0.5 * LOG2E

LANES = 128
HALF = LANES // 2
MXU_N = 256
COL_DQ, COL_DK, COL_DV, COL_NQ, COL_KSL, COL_VSL, COL_KW, COL_VW, N_MAIN = (
    0, 512, 1024, 1536, 2048, 2304, 2432, 2560, 2688)
N_PROJ = N_MAIN + 3 * LANES
VMEM_LIMIT = 48 * 1024 * 1024

TQ = 256
TK = 256
TM = 512


def _t5_thresholds():
    d = np.arange(0, 4 * MAX_DISTANCE)
    max_exact = N_BUCKETS // 2
    val = (np.log(np.maximum(d, 1).astype(np.float32) / np.float32(max_exact))
           / np.float32(math.log(MAX_DISTANCE / max_exact)) * np.float32(N_BUCKETS - max_exact))
    large = np.minimum(max_exact + val.astype(np.int32), N_BUCKETS - 1)
    bucket = np.where(d < max_exact, d, large)
    assert np.all(np.diff(bucket) >= 0) and bucket[-1] == N_BUCKETS - 1
    return [int(np.argmax(bucket >= k)) for k in range(N_BUCKETS)]


T5_THRESH = _t5_thresholds()
assert T5_THRESH[-1] <= TK + 1
assert WINDOW == 2 * TK and TQ == TK


def _nt_dot(a, b):
    return lax.dot_general(a, b, (((1,), (1,)), ((), ())), preferred_element_type=F32)


def _dot(a, b):
    return jnp.dot(a, b, preferred_element_type=F32)


def _rms(x, g):
    ms = jnp.mean(x * x, axis=-1, keepdims=True)
    return x * lax.rsqrt(ms + EPS) * g


def _params(sem):
    return pltpu.CompilerParams(dimension_semantics=sem, vmem_limit_bytes=VMEM_LIMIT)


def _store_grouped(out_ref, tok_ref, y):
    tok_ref[...] = y
    nrow = out_ref.shape[0]
    for l in range(CMP_STRIDE):
        out_ref[:, l * LANES:(l + 1) * LANES] = tok_ref[pl.ds(l, nrow, stride=CMP_STRIDE), :].astype(BF16)


def _arrange_w_in(w_ref, wp_ref):
    src_dk, src_nq, src_kc, src_ksl, src_vsl, src_gt = 512, 1536, 2048, 2304, 2432, 2816
    halves = []
    halves += [(c, QSCALE) for c in range(0, src_dk, HALF)]
    halves += [(c, None) for c in range(src_dk, src_nq, HALF)]
    halves += [(src_nq + (g * NSA_GQ + j) * NSA_HD, QSCALE) for j in range(NSA_GQ) for g in range(NSA_KV)]
    halves += [(src_ksl, None), (None, None), (None, None), (src_ksl + HALF, None)]
    halves += [(c, None) for c in range(src_vsl, src_gt, HALF)]
    halves += [(c, None) for c in range(src_kc, src_ksl, HALF)]
    assert len(halves) * HALF == N_PROJ - LANES
    rows = 256
    for r0 in range(0, D_MODEL, rows):
        rs = slice(r0, r0 + rows)

        def load(entry):
            col, scale = entry
            if col is None:
                return jnp.zeros((rows, HALF), F32)
            val = w_ref[rs, col:col + HALF]
            return val if scale is None else val * scale

        for t in range(0, len(halves), 2):
            tile = jnp.concatenate([load(halves[t]), load(halves[t + 1])], axis=1)
            wp_ref[rs, (t // 2) * LANES:(t // 2 + 1) * LANES] = tile.astype(BF16)
        ngate = N_IN - src_gt
        gates = jnp.concatenate([w_ref[rs, src_gt:N_IN], jnp.zeros((rows, LANES - ngate), F32)], axis=1)
        wp_ref[rs, N_PROJ - LANES:N_PROJ] = gates.astype(BF16)


def _inproj_kernel(x_ref, g_ref, wraw_ref, main_ref, kc_ref, vc_ref, gate_ref, tok_ref, w_ref, *, seq):
    tm = x_ref.shape[0]

    @pl.when(pl.program_id(0) == 0)
    def _():
        _arrange_w_in(wraw_ref, w_ref)

    h = _rms(x_ref[...], g_ref[...]).astype(BF16)
    for c in range(0, N_PROJ, MXU_N):
        y = _dot(h, w_ref[:, c:c + MXU_N])
        if c == COL_KSL:
            t0 = lax.rem(pl.program_id(0) * tm, seq)
            blk = (t0 + lax.broadcasted_iota(jnp.int32, (tm, 2 * LANES), 0)) >> 6
            lane = lax.broadcasted_iota(jnp.int32, (tm, 2 * LANES), 1)
            slot = jnp.where(lane < LANES, lane - HALF, lane - LANES)
            y = y + jnp.where((slot == blk) & (slot >= 0) & (slot < N_SLC), NEG, 0.0)
        if c + MXU_N <= N_MAIN:
            main_ref[:, c:c + MXU_N] = y.astype(BF16)
        elif c == COL_VW:
            main_ref[:, c:c + LANES] = y[:, 0:LANES].astype(BF16)
            _store_grouped(kc_ref, tok_ref, y[:, LANES:MXU_N])
        else:
            _store_grouped(vc_ref, tok_ref, y[:, 0:LANES])
            gate_ref[...] = 1.0 / (1.0 + jnp.exp(-y[:, LANES:MXU_N]))


def _inproj(x2, g1, w_in, seq):
    n = x2.shape[0]
    return pl.pallas_call(
        functools.partial(_inproj_kernel, seq=seq),
        grid=(n // TM,),
        in_specs=[pl.BlockSpec((TM, D_MODEL), lambda i: (i, 0)),
                  pl.BlockSpec((1, D_MODEL), lambda i: (0, 0)),
                  pl.BlockSpec((D_MODEL, N_IN), lambda i: (0, 0), pipeline_mode=pl.Buffered(1))],
        out_specs=[pl.BlockSpec((TM, N_MAIN), lambda i: (i, 0)),
                   pl.BlockSpec((TM // CMP_STRIDE, CMP_STRIDE * LANES), lambda i: (i, 0)),
                   pl.BlockSpec((TM // CMP_STRIDE, CMP_STRIDE * LANES), lambda i: (i, 0)),
                   pl.BlockSpec((TM, LANES), lambda i: (i, 0))],
        out_shape=[jax.ShapeDtypeStruct((n, N_MAIN), BF16),
                   jax.ShapeDtypeStruct((n // CMP_STRIDE, CMP_STRIDE * LANES), BF16),
                   jax.ShapeDtypeStruct((n // CMP_STRIDE, CMP_STRIDE * LANES), BF16),
                   jax.ShapeDtypeStruct((n, LANES), F32)],
        scratch_shapes=[pltpu.VMEM((TM, LANES), F32),
                        pltpu.VMEM((D_MODEL, N_PROJ), BF16)],
        compiler_params=_params(("arbitrary",)),
    )(x2, g1, w_in)


def _gelu_tanh(x):
    return x * (0.5 * (1.0 + jnp.tanh(math.sqrt(2.0 / math.pi) * (x + 0.044715 * (x * x * x)))))


def _compress_kernel(rk_ref, rv_ref, pos_ref, w1k_ref, w2k_ref, w1v_ref, w2v_ref, kc_ref, vc_ref):
    width = rk_ref.shape[2]
    lane = lax.broadcasted_iota(jnp.int32, (1, width), 1)
    in_group = [jnp.where(((lane >> 6) & 1) == g, 1.0, 0.0) for g in range(NSA_KV)]

    def comp(r_ref, p_top, p_bot, w1_ref, w2_ref):
        r = r_ref[0].astype(F32)
        hid = []
        for g in range(NSA_KV):
            top = _dot(((r + p_top) * in_group[g]).astype(BF16), w1_ref[0])
            bot = _dot(((r + p_bot) * in_group[g]).astype(BF16), w1_ref[1])
            nrow = top.shape[0]
            hid.append(top + pltpu.roll(bot, nrow - 1, 0))
        act = _gelu_tanh(jnp.concatenate(hid, axis=1)).astype(BF16)
        return _dot(act, w2_ref[...]).astype(BF16)

    kc_ref[0] = comp(rk_ref, pos_ref[0:1, :], pos_ref[1:2, :], w1k_ref, w2k_ref)
    vc_ref[0] = comp(rv_ref, pos_ref[2:3, :], pos_ref[3:4, :], w1v_ref, w2v_ref)


def _compress(rk, rv, pos, w1k, w2k, w1v, w2v):
    bsz, nrow, width = rk.shape
    hid2 = 2 * CMP_HIDDEN
    return pl.pallas_call(
        _compress_kernel,
        grid=(bsz,),
        in_specs=[pl.BlockSpec((1, nrow, width), lambda b: (b, 0, 0)),
                  pl.BlockSpec((1, nrow, width), lambda b: (b, 0, 0)),
                  pl.BlockSpec((8, width), lambda b: (0, 0)),
                  pl.BlockSpec((2, width, CMP_HIDDEN), lambda b: (0, 0, 0)),
                  pl.BlockSpec((hid2, LANES), lambda b: (0, 0)),
                  pl.BlockSpec((2, width, CMP_HIDDEN), lambda b: (0, 0, 0)),
                  pl.BlockSpec((hid2, LANES), lambda b: (0, 0))],
        out_specs=[pl.BlockSpec((1, nrow, LANES), lambda b: (b, 0, 0)),
                   pl.BlockSpec((1, nrow, LANES), lambda b: (b, 0, 0))],
        out_shape=[jax.ShapeDtypeStruct((bsz, nrow, LANES), BF16),
                   jax.ShapeDtypeStruct((bsz, nrow, LANES), BF16)],
        compiler_params=_params(("parallel",)),
    )(rk, rv, pos, w1k, w2k, w1v, w2v)


def _half_mask(g):
    lane = lax.broadcasted_iota(jnp.int32, (1, LANES), 1)
    return jnp.where((lane >= g * HALF) & (lane < (g + 1) * HALF), 1.0, 0.0).astype(BF16)


def _nsa_chains(q_ref, extra_ref, tq):
    chains = []
    for qt in range(2):
        rows = slice(qt * tq, (qt + 1) * tq)
        for g in range(NSA_KV):
            keep = _half_mask(g)
            for j in range(NSA_GQ):
                q = q_ref[0, rows, j * LANES:(j + 1) * LANES] * keep
                if extra_ref is not None:
                    q = q + extra_ref[0, rows, g * LANES:(g + 1) * LANES]
                chains.append((q, qt, g, j))
    return chains


def _bias_kernel(rel_ref, o_ref, *, tq, tk):
    row = pl.program_id(0)
    col = jnp.where(row < NSA_HEADS, row + DIFF_HEADS, row - NSA_HEADS)
    r = lax.broadcasted_iota(jnp.int32, (tq, tk), 0)
    c = lax.broadcasted_iota(jnp.int32, (tq, tk), 1)
    far = rel_ref[N_BUCKETS - 1, col]
    for d in range(2):
        dist = r - c + d * tq
        val = jnp.zeros((tq, tk), F32) + rel_ref[0, col]
        for k in range(1, N_BUCKETS):
            val = jnp.where(dist >= T5_THRESH[k], rel_ref[k, col], val)
        val = (val - far) * LOG2E
        if d == 0:
            val = jnp.where(dist >= 0, val, NEG)
        o_ref[0, d] = val
    o_ref[0, 2] = jnp.where(r < c, 0.0, NEG)


def _bias_tiles(rel_bias):
    nhead = rel_bias.shape[1]
    return pl.pallas_call(
        functools.partial(_bias_kernel, tq=TQ, tk=TK),
        grid=(nhead,),
        in_specs=[pl.BlockSpec(memory_space=pltpu.SMEM)],
        out_specs=pl.BlockSpec((1, 3, TQ, TK), lambda h: (h, 0, 0, 0)),
        out_shape=jax.ShapeDtypeStruct((nhead, 3, TQ, TK), F32),
        compiler_params=_params(("parallel",)),
    )(rel_bias)


def _with_near_bias(s, near):
    add = near[0] if len(near) == 1 else jnp.concatenate(near, axis=1)
    width, nw = s.shape[1], add.shape[1]
    if nw == width:
        return s + add
    return jnp.concatenate([s[:, 0:width - nw], s[:, width - nw:width] + add], axis=1)


def _softmax_step(s, v, m_ref, acc_ref):
    tk = s.shape[1]
    m_prev = m_ref[...]
    m_new = jnp.maximum(m_prev, jnp.max(s, axis=-1, keepdims=True))
    alpha = jnp.exp2(m_prev - m_new)
    p = jnp.exp2(s - jnp.concatenate([m_new] * (tk // LANES), axis=1))
    vaug = jnp.concatenate([v, jnp.ones((tk, LANES), BF16)], axis=1)
    acc_ref[...] = jnp.concatenate([alpha, alpha], axis=1) * acc_ref[...] + _dot(p.astype(BF16), vaug)
    m_ref[...] = m_new


def _softmax_first(s, v, m_ref, acc_ref):
    rows, tk = s.shape
    m = jnp.broadcast_to(jnp.max(s, axis=-1, keepdims=True), (rows, LANES))
    p = jnp.exp2(s - jnp.concatenate([m] * (tk // LANES), axis=1))
    vaug = jnp.concatenate([v, jnp.ones((tk, LANES), BF16)], axis=1)
    acc_ref[...] = _dot(p.astype(BF16), vaug)
    m_ref[...] = m


def _softmax_once(s, v):
    rows, tk = s.shape
    m = jnp.broadcast_to(jnp.max(s, axis=-1, keepdims=True), (rows, LANES))
    p = jnp.exp2(s - jnp.concatenate([m] * (tk // LANES), axis=1))
    vaug = jnp.concatenate([v, jnp.ones((tk, LANES), BF16)], axis=1)
    pv = _dot(p.astype(BF16), vaug)
    return pv[:, 0:LANES] * (1.0 / pv[:, LANES:2 * LANES])


def _causal_flash(i, tq, tk, chains, kv_of, bias_of, m_ref, acc_ref, also=None):
    def tail(first):
        stages = list(also(first)) if also is not None else []
        every = max(1, len(chains) // (len(stages) + 1))
        kv = {}
        for c, (q, qt, kv_id, bias_id) in enumerate(chains):
            ntile = (1 if first else 3) + qt
            if (kv_id, ntile) not in kv:
                start = 0 if first else pl.multiple_of((i - 1) * 2 * tk, 2 * tk)
                kv[(kv_id, ntile)] = kv_of(kv_id, start, ntile * tk)
            k, v = kv[(kv_id, ntile)]
            near = [bias_of(bias_id, d) for d in range(min(ntile, 2) - 1, -1, -1)]
            _softmax_first(_with_near_bias(_nt_dot(q, k), near), v, m_ref.at[c], acc_ref.at[c])
            if stages and (c + 1) % every == 0:
                stages.pop(0)()
        while stages:
            stages.pop(0)()

    @pl.when(i == 0)
    def _():
        tail(True)

    @pl.when(i >= 1)
    def _():
        tail(False)

    def far(j, carry):
        start = pl.multiple_of(j * 2 * tk, 2 * tk)
        kv = {}
        for c, (q, _, kv_id, _) in enumerate(chains):
            if kv_id not in kv:
                kv[kv_id] = kv_of(kv_id, start, 2 * tk)
            k, v = kv[kv_id]
            _softmax_step(_nt_dot(q, k), v, m_ref.at[c], acc_ref.at[c])
        return carry

    lax.fori_loop(0, i - 1, far, 0)


def _normalized(acc_ref, c):
    return acc_ref[c, :, 0:LANES] * (1.0 / acc_ref[c, :, LANES:2 * LANES])


def _store_gated(o_ref, gate_ref, branches, tq, qt):
    lane = lax.broadcasted_iota(jnp.int32, (1, LANES), 1)
    rows = slice(qt * tq, (qt + 1) * tq)
    gates = gate_ref[0, rows, :]
    lane_full = lax.broadcasted_iota(jnp.int32, gates.shape, 1)
    for j in range(NSA_GQ):
        total = None
        for o_of, branch in branches:
            c0 = (0 * NSA_GQ + j) * 3 + branch
            c1 = (1 * NSA_GQ + j) * 3 + branch
            gexp = jnp.take_along_axis(gates, jnp.where(lane_full < HALF, c0, c1), axis=1)
            term = gexp * jnp.where(lane < HALF, o_of(0, j), o_of(1, j))
            total = term if total is None else total + term
        o_ref[0, rows, j * LANES:(j + 1) * LANES] = total.astype(BF16)


def _nsa_chain_index(qt, g, j):
    return (qt * NSA_KV + g) * NSA_GQ + j


N_CHAINS = 2 * NSA_HEADS


def _cmp_stages(qi, tq, q_ref, kc_ref, vc_ref, ovt_ref, gate_ref, o_ref, nsel_ref):
    nblk = kc_ref.shape[1]
    lane = lax.broadcasted_iota(jnp.int32, (1, LANES), 1)
    state = {"outs": [], "imp_t": jnp.zeros((LANES, tq), F32), "dropped": []}

    def attend(g):
        n = lax.broadcasted_iota(jnp.int32, (1, nblk), 1)
        t4 = qi * tq + (lax.broadcasted_iota(jnp.int32, (NSA_GQ * tq, 1), 0) & (tq - 1))
        vis4 = (n * CMP_STRIDE + (CMP_LEN - 1)) <= t4
        has4 = jnp.where(t4 >= CMP_LEN - 1, 1.0, 0.0)
        qg = jnp.concatenate([q_ref[0, :, j * LANES:(j + 1) * LANES] * _half_mask(g) for j in range(NSA_GQ)], axis=0)
        s = jnp.where(vis4, _nt_dot(qg, kc_ref[0]), NEG)
        e = jnp.exp2(s - jnp.max(s, axis=-1, keepdims=True))
        p = e * (has4 / jnp.sum(e, axis=-1, keepdims=True))
        state["outs"].append(_dot(p.astype(BF16), vc_ref[0]))
        ps = p[0:tq] + p[tq:2 * tq] + p[2 * tq:3 * tq] + p[3 * tq:4 * tq]
        hi = ps.astype(BF16)
        r1 = ps - hi.astype(F32)
        mid = r1.astype(BF16)
        lo = (r1 - mid.astype(F32)).astype(BF16)
        state["imp_t"] = state["imp_t"] + (
            _nt_dot(ovt_ref[g], hi) + _nt_dot(ovt_ref[g], mid) + _nt_dot(ovt_ref[g], lo))

    def store_out():
        outs = state["outs"]
        _store_gated(o_ref, gate_ref, [(lambda g, j: outs[g][j * tq:(j + 1) * tq], 0)], tq, 0)

    def rank(base):
        tl = qi * tq + lax.broadcasted_iota(jnp.int32, (N_SLC, tq), 1)
        jrow = lax.broadcasted_iota(jnp.int32, (N_SLC, tq), 0)
        cur = tl >> 6
        valid = (jrow * SLC_BLOCK) <= tl
        forced = (jrow == 0) | (jrow == cur) | (jrow == cur - 1)
        sub = 8
        x = jnp.where(forced, jnp.inf, jnp.where(valid, state["imp_t"][base:base + N_SLC], -jnp.inf))
        others = [jnp.broadcast_to(x[jj:jj + 1], (sub, tq)) for jj in range(N_SLC)]
        chunks = []
        for r0 in range(0, N_SLC, sub):
            xr = x[r0:r0 + sub]
            jr = r0 + lax.broadcasted_iota(jnp.int32, (sub, tq), 0)
            cnt = jnp.zeros((sub, tq), F32)
            for jj in range(N_SLC):
                if jj < r0:
                    beats = others[jj] >= xr
                elif jj >= r0 + sub:
                    beats = others[jj] > xr
                else:
                    beats = (others[jj] > xr) | ((others[jj] == xr) & (jr > jj))
                cnt = cnt + jnp.where(beats, 1.0, 0.0)
            chunks.append(jnp.where(cnt < SLC_TOPK, 0.0, 1.0))
        state["dropped"].append(jnp.concatenate(chunks, axis=0))

    def store_selection():
        zero = jnp.zeros((N_SLC, tq), F32)
        dropped = state["dropped"]
        drop = jnp.concatenate([dropped[0], zero, dropped[1], zero], axis=0).T
        nsel_ref[0, :, 0:LANES] = jnp.where(lane >= HALF, drop, 0.0).astype(BF16)
        nsel_ref[0, :, LANES:2 * LANES] = jnp.where(lane < HALF, drop, 0.0).astype(BF16)

    return [functools.partial(attend, 0), functools.partial(attend, 1), store_out,
            functools.partial(rank, 0), functools.partial(rank, HALF), store_selection]


def _slc_win_kernel(bias_ref, q_ref, ks_ref, vs_ref, kw_ref, vw_ref, nsel_ref, gate_ref, o_ref,
                    m_ref, acc_ref, win_ref, *, tq, tk):
    i = pl.program_id(1)
    chains = [(q, qt, g, g * NSA_GQ + j) for q, qt, g, j in _nsa_chains(q_ref, nsel_ref, tq)]
    win_chains = _nsa_chains(q_ref, None, tq)

    def kv_of(g, start, width):
        return (ks_ref[0, pl.ds(start, width), g * LANES:(g + 1) * LANES], vs_ref[0, pl.ds(start, width), :])

    def window(first):
        kv = {}

        def one(q, qt, g, j):
            ntile = 1 + qt if first else 3
            if (qt, ntile) not in kv:
                start = 0 if first else pl.multiple_of((2 * i - 2 + qt) * tk, tk)
                kv[(qt, ntile)] = (kw_ref[0, pl.ds(start, ntile * tk), :], vw_ref[0, pl.ds(start, ntile * tk), :])
            k, v = kv[(qt, ntile)]
            near = [bias_ref[g * NSA_GQ + j, d] for d in range(min(ntile, 2) - 1, -1, -1)]
            if ntile == 3:
                near.insert(0, bias_ref[g * NSA_GQ + j, 2])
            win_ref[_nsa_chain_index(qt, g, j)] = _softmax_once(_with_near_bias(_nt_dot(q, k), near), v)

        def all_chains():
            for chain in win_chains:
                one(*chain)

        return [all_chains]

    _causal_flash(i, tq, tk, chains, kv_of, lambda h, d: bias_ref[h, d], m_ref, acc_ref, also=window)
    for qt in range(2):
        _store_gated(o_ref, gate_ref,
                     [(lambda g, j: _normalized(acc_ref, _nsa_chain_index(qt, g, j)), 1),
                      (lambda g, j: win_ref[_nsa_chain_index(qt, g, j)], 2)], tq, qt)


def _slc_win_attention(bias, main, nsel, gates):
    bsz, seq, _ = main.shape
    tq, tk = TQ, TK
    return pl.pallas_call(
        functools.partial(_slc_win_kernel, tq=tq, tk=tk),
        grid=(bsz, seq // (2 * tq)),
        in_specs=[pl.BlockSpec((NSA_HEADS, 3, tq, tk), lambda b, i: (0, 0, 0, 0), pipeline_mode=pl.Buffered(1)),
                  pl.BlockSpec((1, 2 * tq, NSA_W), lambda b, i: (b, i, COL_NQ // NSA_W)),
                  pl.BlockSpec((1, seq, 2 * LANES), lambda b, i: (b, 0, COL_KSL // (2 * LANES))),
                  pl.BlockSpec((1, seq, LANES), lambda b, i: (b, 0, COL_VSL // LANES)),
                  pl.BlockSpec((1, seq, LANES), lambda b, i: (b, 0, COL_KW // LANES)),
                  pl.BlockSpec((1, seq, LANES), lambda b, i: (b, 0, COL_VW // LANES)),
                  pl.BlockSpec((1, 2 * tq, 2 * LANES), lambda b, i: (b, i, 0)),
                  pl.BlockSpec((1, 2 * tq, LANES), lambda b, i: (b, i, 0))],
        out_specs=pl.BlockSpec((1, 2 * tq, NSA_W), lambda b, i: (b, i, 0)),
        out_shape=jax.ShapeDtypeStruct((bsz, seq, NSA_W), BF16),
        scratch_shapes=[pltpu.VMEM((N_CHAINS, tq, LANES), F32),
                        pltpu.VMEM((N_CHAINS, tq, 2 * LANES), F32),
                        pltpu.VMEM((N_CHAINS, tq, LANES), F32)],
        compiler_params=_params(("parallel", "parallel")),
    )(bias, main, main, main, main, main, nsel, gates)


def _diff_cmp_kernel(bias_ref, lam_ref, q_ref, k_ref, v_ref, g_ref, nq_ref, kc_ref, vc_ref, ovt_ref, gate_ref,
                     o_ref, oc_ref, nsel_ref, m_ref, acc_ref, *, tq, tk):
    i = pl.program_id(1)

    chains = []
    for qt in range(2):
        for h in range(DIFF_HEADS):
            slab = q_ref[0, qt * tq:(qt + 1) * tq, h * LANES:(h + 1) * LANES]
            for half in range(2):
                chains.append((slab * _half_mask(half), qt, h, h))

    def kv_of(h, start, width):
        cols = slice(h * LANES, (h + 1) * LANES)
        return (k_ref[0, pl.ds(start, width), cols], v_ref[0, pl.ds(start, width), cols])

    def compressed(first):
        return _cmp_stages(i, 2 * tq, nq_ref, kc_ref, vc_ref, ovt_ref, gate_ref, oc_ref, nsel_ref)

    _causal_flash(i, tq, tk, chains, kv_of, lambda h, d: bias_ref[h, d], m_ref, acc_ref, also=compressed)

    lm = lam_ref[...]
    lam = (jnp.exp(jnp.sum(lm[0:1] * lm[1:2], axis=-1, keepdims=True))
           - jnp.exp(jnp.sum(lm[2:3] * lm[3:4], axis=-1, keepdims=True)) + LAMBDA_INIT)
    for qt in range(2):
        for h in range(DIFF_HEADS):
            c = (qt * DIFF_HEADS + h) * 2
            a = _normalized(acc_ref, c) - lam * _normalized(acc_ref, c + 1)
            o_ref[0, qt * tq:(qt + 1) * tq, h * LANES:(h + 1) * LANES] = (
                _rms(a, g_ref[...]) * (1.0 - LAMBDA_INIT)).astype(BF16)


def _diff_cmp_attention(bias, lam_rows, main, subln, kc, vc, ovt, gates):
    bsz, seq, _ = main.shape
    nblk = kc.shape[1]
    tq, tk = TQ, TK
    step = lambda b, i: (b, i, 0)
    return pl.pallas_call(
        functools.partial(_diff_cmp_kernel, tq=tq, tk=tk),
        grid=(bsz, seq // (2 * tq)),
        in_specs=[pl.BlockSpec((DIFF_HEADS, 3, tq, tk), lambda b, i: (NSA_HEADS // DIFF_HEADS, 0, 0, 0),
                               pipeline_mode=pl.Buffered(1)),
                  pl.BlockSpec((8, LANES), lambda b, i: (0, 0)),
                  pl.BlockSpec((1, 2 * tq, DIFF_W), lambda b, i: (b, i, COL_DQ // DIFF_W)),
                  pl.BlockSpec((1, seq, DIFF_W), lambda b, i: (b, 0, COL_DK // DIFF_W)),
                  pl.BlockSpec((1, seq, DIFF_W), lambda b, i: (b, 0, COL_DV // DIFF_W)),
                  pl.BlockSpec((1, LANES), lambda b, i: (0, 0)),
                  pl.BlockSpec((1, 2 * tq, NSA_W), lambda b, i: (b, i, COL_NQ // NSA_W)),
                  pl.BlockSpec((1, nblk, LANES), lambda b, i: (b, 0, 0)),
                  pl.BlockSpec((1, nblk, LANES), lambda b, i: (b, 0, 0)),
                  pl.BlockSpec((2, LANES, nblk), lambda b, i: (0, 0, 0)),
                  pl.BlockSpec((1, 2 * tq, LANES), step)],
        out_specs=[pl.BlockSpec((1, 2 * tq, DIFF_W), step),
                   pl.BlockSpec((1, 2 * tq, NSA_W), step),
                   pl.BlockSpec((1, 2 * tq, 2 * LANES), step)],
        out_shape=[jax.ShapeDtypeStruct((bsz, seq, DIFF_W), BF16),
                   jax.ShapeDtypeStruct((bsz, seq, NSA_W), BF16),
                   jax.ShapeDtypeStruct((bsz, seq, 2 * LANES), BF16)],
        scratch_shapes=[pltpu.VMEM((N_CHAINS, tq, LANES), F32),
                        pltpu.VMEM((N_CHAINS, tq, 2 * LANES), F32)],
        compiler_params=_params(("parallel", "parallel")),
    )(bias, lam_rows, main, main, main, subln, main, kc, vc, ovt, gates)


FF_CHUNK = 256


def _tail_kernel(x_ref, od_ref, oc_ref, osw_ref, gn_ref, woraw_ref, g2_ref,
                 wg_ref, wu_ref, wd_ref, gf_ref, o_ref, acc_ref, wo_ref):
    @pl.when(pl.program_id(0) == 0)
    def _():
        wo_ref[0:DIFF_W, :] = woraw_ref[0:DIFF_W, :].astype(BF16)
        for j in range(NSA_GQ):
            for g in range(NSA_KV):
                dst = DIFF_W + (j * NSA_KV + g) * NSA_HD
                src = DIFF_W + (g * NSA_GQ + j) * NSA_HD
                wo_ref[dst:dst + NSA_HD, :] = woraw_ref[src:src + NSA_HD, :].astype(BF16)

    o_nsa = oc_ref[...].astype(F32) + osw_ref[...].astype(F32)
    o_nsa = _rms(o_nsa, gn_ref[...]).astype(BF16)
    attn = _dot(od_ref[...], wo_ref[0:DIFF_W, :]) + _dot(o_nsa, wo_ref[DIFF_W:DIFF_W + NSA_W, :])
    x1 = x_ref[...] + attn
    acc_ref[...] = x1
    h = _rms(x1, g2_ref[...]).astype(BF16)
    for c in range(0, D_FF, FF_CHUNK):
        gate = _dot(h, wg_ref[:, c:c + FF_CHUNK])
        up = _dot(h, wu_ref[:, c:c + FF_CHUNK])
        act = (gate * (1.0 / (1.0 + jnp.exp(-gate))) * up).astype(BF16)
        acc_ref[...] += _dot(act, wd_ref[c:c + FF_CHUNK, :])
    o_ref[...] = _rms(acc_ref[...], gf_ref[...])


def _tail(x2, od, oc, osw, gn, w_out, g2, wg, wu, wd, gf):
    n = x2.shape[0]
    row = lambda i: (i, 0)
    fixed = lambda i: (0, 0)
    once = pl.Buffered(1)
    return pl.pallas_call(
        _tail_kernel,
        grid=(n // TM,),
        in_specs=[pl.BlockSpec((TM, D_MODEL), row),
                  pl.BlockSpec((TM, DIFF_W), row),
                  pl.BlockSpec((TM, NSA_W), row),
                  pl.BlockSpec((TM, NSA_W), row),
                  pl.BlockSpec((1, NSA_W), fixed),
                  pl.BlockSpec((DIFF_W + NSA_W, D_MODEL), fixed, pipeline_mode=once),
                  pl.BlockSpec((1, D_MODEL), fixed),
                  pl.BlockSpec((D_MODEL, D_FF), fixed, pipeline_mode=once),
                  pl.BlockSpec((D_MODEL, D_FF), fixed, pipeline_mode=once),
                  pl.BlockSpec((D_FF, D_MODEL), fixed, pipeline_mode=once),
                  pl.BlockSpec((1, D_MODEL), fixed)],
        out_specs=pl.BlockSpec((TM, D_MODEL), row),
        out_shape=jax.ShapeDtypeStruct((n, D_MODEL), F32),
        scratch_shapes=[pltpu.VMEM((TM, D_MODEL), F32),
                        pltpu.VMEM((DIFF_W + NSA_W, D_MODEL), BF16)],
        compiler_params=_params(("arbitrary",)),
    )(x2, od, oc, osw, gn, w_out, g2, wg, wu, wd, gf)


def _permute_heads(a, axis):
    shape = a.shape
    split = shape[:axis] + (NSA_KV, NSA_GQ, NSA_HD) + shape[axis + 1:]
    return jnp.swapaxes(a.reshape(split), axis, axis + 1).reshape(shape)


def _overlap_matrices(nblk):
    n_cmp = nblk - 1
    nslc = nblk * CMP_STRIDE // SLC_BLOCK
    assert nslc == N_SLC
    cmp_start = np.arange(n_cmp) * CMP_STRIDE
    cmp_end = cmp_start + CMP_LEN - 1
    slc_start = np.arange(nslc) * SLC_BLOCK
    overlap = ((cmp_start[:, None] < slc_start[None, :] + SLC_BLOCK) & (cmp_end[:, None] >= slc_start[None, :]))
    ovt = np.zeros((NSA_KV, LANES, nblk), np.float32)
    for g, base in enumerate((HALF, 0)):
        ovt[g, base:base + nslc, :n_cmp] = overlap.T
    return ovt


def _compress_weights(pos, w1, w2):
    half = CMP_LEN // 2
    shape = (2, half, NSA_KV, NSA_HD, CMP_HIDDEN)
    w1rep = jnp.broadcast_to(w1.reshape(2, half, 1, NSA_HD, CMP_HIDDEN), shape).reshape(2, -1, CMP_HIDDEN)
    w2big = jnp.einsum('jd,pg->pjgd', w2, jnp.eye(NSA_KV, dtype=F32)).reshape(NSA_KV * CMP_HIDDEN, NSA_KV * NSA_HD)
    posr = jnp.broadcast_to(pos.reshape(2, half, 1, NSA_HD), (2, half, NSA_KV, NSA_HD)).reshape(2, -1)
    return posr, w1rep.astype(BF16), w2big.astype(BF16)


def kernel(x, norm1, w_in, lambda_q1, lambda_k1, lambda_q2, lambda_k2, diff_subln, cmp_pos_k, cmp_w1_k, cmp_w2_k,
           cmp_pos_v, cmp_w1_v, cmp_w2_v, nsa_norm, w_out, norm2, w_gate, w_up, w_down, rel_bias, final_norm):
    bsz, seq, dm = x.shape
    assert dm == D_MODEL and seq % (2 * TQ) == 0 and seq % TM == 0 and seq == N_SLC * SLC_BLOCK
    assert norm1.shape[0] == 1, "single layer"
    x2 = x.reshape(bsz * seq, dm)

    gn_p = _permute_heads(nsa_norm[0].reshape(1, NSA_W), 1)
    posk, w1k, w2k = _compress_weights(cmp_pos_k[0], cmp_w1_k[0], cmp_w2_k[0])
    posv, w1v, w2v = _compress_weights(cmp_pos_v[0], cmp_w1_v[0], cmp_w2_v[0])
    pos = jnp.concatenate([posk, posv, jnp.zeros((4, posk.shape[1]), F32)], axis=0)
    lam_rows = jnp.zeros((8, LANES), F32).at[0:4, 0:DIFF_HD].set(
        jnp.stack([lambda_q1[0], lambda_k1[0], lambda_q2[0], lambda_k2[0]]).astype(F32))

    main, kc_grp, vc_grp, gates = _inproj(x2, norm1[0].reshape(1, dm), w_in[0], seq)
    main = main.reshape(bsz, seq, N_MAIN)

    nrow = seq // CMP_STRIDE
    kc, vc = _compress(kc_grp.reshape(bsz, nrow, CMP_STRIDE * LANES), vc_grp.reshape(bsz, nrow, CMP_STRIDE * LANES),
                       pos, w1k, w2k, w1v, w2v)
    ovt = jnp.asarray(_overlap_matrices(nrow), BF16)
    gates = gates.reshape(bsz, seq, LANES)
    bias = _bias_tiles(rel_bias)
    o_diff, o_cmp, nsel = _diff_cmp_attention(bias, lam_rows, main, diff_subln[0].reshape(1, DIFF_VD),
                                              kc, vc, ovt, gates)
    o_sw = _slc_win_attention(bias, main, nsel, gates)

    n = bsz * seq
    out = _tail(x2, o_diff.reshape(n, DIFF_W), o_cmp.reshape(n, NSA_W), o_sw.reshape(n, NSA_W),
                gn_p, w_out[0], norm2[0].reshape(1, dm),
                w_gate[0].astype(BF16), w_up[0].astype(BF16), w_down[0].astype(BF16), final_norm.reshape(1, dm))
    return out.reshape(bsz, seq, dm)
```

```python
import functools
import math

import numpy as np
import jax
import jax.numpy as jnp
from jax import lax
from jax.experimental import pallas as pl
from jax.experimental.pallas import tpu as pltpu

F32 = jnp.float32
BF16 = jnp.bfloat16

D_MODEL = 1024
DIFF_HEADS = 4
DIFF_HD = 64
DIFF_VD = 128
DIFF_W = 512
NSA_HEADS = 8
NSA_KV = 2
NSA_GQ = 4
NSA_HD = 64
NSA_W = 512
CMP_LEN = 32
CMP_STRIDE = 16
CMP_HIDDEN = 256
SLC_BLOCK = 64
SLC_TOPK = 16
N_SLC = 32
WINDOW = 512
N_BUCKETS = 32
MAX_DISTANCE = 128
D_FF = 2816
N_IN = 2840
NEG = -1e30
EPS = 1e-6
LAMBDA_INIT = 0.8 - 0.6 * math.exp(-0.3 * 0)
LOG2E = math.log2(math.e)
QSCALE = NSA_HD ** -0.5 * LOG2E

LANES = 128
HALF = LANES // 2
MXU_N = 256
COL_DQ, COL_DK, COL_DV, COL_NQ, COL_KSL, COL_VSL, COL_KW, COL_VW, N_MAIN = (
    0, 512, 1024, 1536, 2048, 2304, 2432, 2560, 2688)
N_PROJ = N_MAIN + 3 * LANES
VMEM_LIMIT = 48 * 1024 * 1024

TQ = 256
TK = 256
TM = 512
NQT_DIFF = 4
NQT_NSA = 2


def _t5_thresholds():
    d = np.arange(0, 4 * MAX_DISTANCE)
    max_exact = N_BUCKETS // 2
    val = (np.log(np.maximum(d, 1).astype(np.float32) / np.float32(max_exact))
           / np.float32(math.log(MAX_DISTANCE / max_exact)) * np.float32(N_BUCKETS - max_exact))
    large = np.minimum(max_exact + val.astype(np.int32), N_BUCKETS - 1)
    bucket = np.where(d < max_exact, d, large)
    assert np.all(np.diff(bucket) >= 0) and bucket[-1] == N_BUCKETS - 1
    return [int(np.argmax(bucket >= k)) for k in range(N_BUCKETS)]


T5_THRESH = _t5_thresholds()
assert T5_THRESH[-1] <= TK + 1
assert WINDOW == 2 * TK and TQ == TK


def _nt_dot(a, b):
    return lax.dot_general(a, b, (((1,), (1,)), ((), ())), preferred_element_type=F32)


def _dot(a, b):
    return jnp.dot(a, b, preferred_element_type=F32)


def _rms(x, g):
    ms = jnp.mean(x * x, axis=-1, keepdims=True)
    return x * lax.rsqrt(ms + EPS) * g


def _params(sem):
    return pltpu.CompilerParams(dimension_semantics=sem, vmem_limit_bytes=VMEM_LIMIT)


def _store_grouped(out_ref, tok_ref, y):
    tok_ref[...] = y
    nrow = out_ref.shape[0]
    for l in range(CMP_STRIDE):
        out_ref[:, l * LANES:(l + 1) * LANES] = tok_ref[pl.ds(l, nrow, stride=CMP_STRIDE), :].astype(BF16)


def _arrange_w_in(w_ref, wp_ref):
    src_dk, src_nq, src_kc, src_ksl, src_vsl, src_gt = 512, 1536, 2048, 2304, 2432, 2816
    halves = []
    halves += [(c, QSCALE) for c in range(0, src_dk, HALF)]
    halves += [(c, None) for c in range(src_dk, src_nq, HALF)]
    halves += [(src_nq + (g * NSA_GQ + j) * NSA_HD, QSCALE) for j in range(NSA_GQ) for g in range(NSA_KV)]
    halves += [(src_ksl, None), (None, None), (None, None), (src_ksl + HALF, None)]
    halves += [(c, None) for c in range(src_vsl, src_gt, HALF)]
    halves += [(c, None) for c in range(src_kc, src_ksl, HALF)]
    assert len(halves) * HALF == N_PROJ - LANES
    rows = 256
    for r0 in range(0, D_MODEL, rows):
        rs = slice(r0, r0 + rows)

        def load(entry):
            col, scale = entry
            if col is None:
                return jnp.zeros((rows, HALF), F32)
            val = w_ref[rs, col:col + HALF]
            return val if scale is None else val * scale

        for t in range(0, len(halves), 2):
            tile = jnp.concatenate([load(halves[t]), load(halves[t + 1])], axis=1)
            wp_ref[rs, (t // 2) * LANES:(t // 2 + 1) * LANES] = tile.astype(BF16)
        ngate = N_IN - src_gt
        gates = jnp.concatenate([w_ref[rs, src_gt:N_IN], jnp.zeros((rows, LANES - ngate), F32)], axis=1)
        wp_ref[rs, N_PROJ - LANES:N_PROJ] = gates.astype(BF16)


def _inproj_kernel(x_ref, g_ref, wraw_ref, main_ref, kc_ref, vc_ref, gate_ref, tok_ref, w_ref, *, seq):
    tm = x_ref.shape[0]

    @pl.when(pl.program_id(0) == 0)
    def _():
        _arrange_w_in(wraw_ref, w_ref)

    h = _rms(x_ref[...], g_ref[...]).astype(BF16)
    for c in range(0, N_PROJ, MXU_N):
        y = _dot(h, w_ref[:, c:c + MXU_N])
        if c == COL_KSL:
            t0 = lax.rem(pl.program_id(0) * tm, seq)
            blk = (t0 + lax.broadcasted_iota(jnp.int32, (tm, 2 * LANES), 0)) >> 6
            lane = lax.broadcasted_iota(jnp.int32, (tm, 2 * LANES), 1)
            slot = jnp.where(lane < LANES, lane - HALF, lane - LANES)
            y = y + jnp.where((slot == blk) & (slot >= 0) & (slot < N_SLC), NEG, 0.0)
        if c + MXU_N <= N_MAIN:
            main_ref[:, c:c + MXU_N] = y.astype(BF16)
        elif c == COL_VW:
            main_ref[:, c:c + LANES] = y[:, 0:LANES].astype(BF16)
            _store_grouped(kc_ref, tok_ref, y[:, LANES:MXU_N])
        else:
            _store_grouped(vc_ref, tok_ref, y[:, 0:LANES])
            gate_ref[...] = 1.0 / (1.0 + jnp.exp(-y[:, LANES:MXU_N]))


def _inproj(x2, g1, w_in, seq):
    n = x2.shape[0]
    return pl.pallas_call(
        functools.partial(_inproj_kernel, seq=seq),
        grid=(n // TM,),
        in_specs=[pl.BlockSpec((TM, D_MODEL), lambda i: (i, 0)),
                  pl.BlockSpec((1, D_MODEL), lambda i: (0, 0)),
                  pl.BlockSpec((D_MODEL, N_IN), lambda i: (0, 0), pipeline_mode=pl.Buffered(1))],
        out_specs=[pl.BlockSpec((TM, N_MAIN), lambda i: (i, 0)),
                   pl.BlockSpec((TM // CMP_STRIDE, CMP_STRIDE * LANES), lambda i: (i, 0)),
                   pl.BlockSpec((TM // CMP_STRIDE, CMP_STRIDE * LANES), lambda i: (i, 0)),
                   pl.BlockSpec((TM, LANES), lambda i: (i, 0))],
        out_shape=[jax.ShapeDtypeStruct((n, N_MAIN), BF16),
                   jax.ShapeDtypeStruct((n // CMP_STRIDE, CMP_STRIDE * LANES), BF16),
                   jax.ShapeDtypeStruct((n // CMP_STRIDE, CMP_STRIDE * LANES), BF16),
                   jax.ShapeDtypeStruct((n, LANES), F32)],
        scratch_shapes=[pltpu.VMEM((TM, LANES), F32),
                        pltpu.VMEM((D_MODEL, N_PROJ), BF16)],
        compiler_params=_params(("arbitrary",)),
    )(x2, g1, w_in)


def _gelu_tanh(x):
    return x * (0.5 * (1.0 + jnp.tanh(math.sqrt(2.0 / math.pi) * (x + 0.044715 * (x * x * x)))))


def _compress_kernel(rk_ref, rv_ref, pos_ref, w1k_ref, w2k_ref, w1v_ref, w2v_ref, kc_ref, vc_ref):
    width = rk_ref.shape[2]
    lane = lax.broadcasted_iota(jnp.int32, (1, width), 1)
    in_group = [jnp.where(((lane >> 6) & 1) == g, 1.0, 0.0) for g in range(NSA_KV)]

    def comp(r_ref, p_top, p_bot, w1_ref, w2_ref):
        r = r_ref[0].astype(F32)
        hid = []
        for g in range(NSA_KV):
            top = _dot(((r + p_top) * in_group[g]).astype(BF16), w1_ref[0])
            bot = _dot(((r + p_bot) * in_group[g]).astype(BF16), w1_ref[1])
            nrow = top.shape[0]
            hid.append(top + pltpu.roll(bot, nrow - 1, 0))
        act = _gelu_tanh(jnp.concatenate(hid, axis=1)).astype(BF16)
        return _dot(act, w2_ref[...]).astype(BF16)

    kc_ref[0] = comp(rk_ref, pos_ref[0:1, :], pos_ref[1:2, :], w1k_ref, w2k_ref)
    vc_ref[0] = comp(rv_ref, pos_ref[2:3, :], pos_ref[3:4, :], w1v_ref, w2v_ref)


def _compress(rk, rv, pos, w1k, w2k, w1v, w2v):
    bsz, nrow, width = rk.shape
    hid2 = 2 * CMP_HIDDEN
    return pl.pallas_call(
        _compress_kernel,
        grid=(bsz,),
        in_specs=[pl.BlockSpec((1, nrow, width), lambda b: (b, 0, 0)),
                  pl.BlockSpec((1, nrow, width), lambda b: (b, 0, 0)),
                  pl.BlockSpec((8, width), lambda b: (0, 0)),
                  pl.BlockSpec((2, width, CMP_HIDDEN), lambda b: (0, 0, 0)),
                  pl.BlockSpec((hid2, LANES), lambda b: (0, 0)),
                  pl.BlockSpec((2, width, CMP_HIDDEN), lambda b: (0, 0, 0)),
                  pl.BlockSpec((hid2, LANES), lambda b: (0, 0))],
        out_specs=[pl.BlockSpec((1, nrow, LANES), lambda b: (b, 0, 0)),
                   pl.BlockSpec((1, nrow, LANES), lambda b: (b, 0, 0))],
        out_shape=[jax.ShapeDtypeStruct((bsz, nrow, LANES), BF16),
                   jax.ShapeDtypeStruct((bsz, nrow, LANES), BF16)],
        compiler_params=_params(("parallel",)),
    )(rk, rv, pos, w1k, w2k, w1v, w2v)


def _half_mask(g):
    lane = lax.broadcasted_iota(jnp.int32, (1, LANES), 1)
    return jnp.where((lane >= g * HALF) & (lane < (g + 1) * HALF), 1.0, 0.0).astype(BF16)


def _nsa_chains(q_ref, extra_ref, tq, nqt):
    chains = []
    for qt in range(nqt):
        rows = slice(qt * tq, (qt + 1) * tq)
        for g in range(NSA_KV):
            keep = _half_mask(g)
            for j in range(NSA_GQ):
                q = q_ref[0, rows, j * LANES:(j + 1) * LANES] * keep
                if extra_ref is not None:
                    q = q + extra_ref[0, rows, g * LANES:(g + 1) * LANES]
                chains.append((q, qt, g, j))
    return chains


def _bias_kernel(rel_ref, o_ref, *, tq, tk):
    row = pl.program_id(0)
    col = jnp.where(row < NSA_HEADS, row + DIFF_HEADS, row - NSA_HEADS)
    r = lax.broadcasted_iota(jnp.int32, (tq, tk), 0)
    c = lax.broadcasted_iota(jnp.int32, (tq, tk), 1)
    far = rel_ref[N_BUCKETS - 1, col]
    for d in range(2):
        dist = r - c + d * tq
        val = jnp.zeros((tq, tk), F32) + rel_ref[0, col]
        for k in range(1, N_BUCKETS):
            val = jnp.where(dist >= T5_THRESH[k], rel_ref[k, col], val)
        val = (val - far) * LOG2E
        if d == 0:
            val = jnp.where(dist >= 0, val, NEG)
        o_ref[0, d] = val
    o_ref[0, 2] = jnp.where(r < c, 0.0, NEG)


def _bias_tiles(rel_bias):
    nhead = rel_bias.shape[1]
    return pl.pallas_call(
        functools.partial(_bias_kernel, tq=TQ, tk=TK),
        grid=(nhead,),
        in_specs=[pl.BlockSpec(memory_space=pltpu.SMEM)],
        out_specs=pl.BlockSpec((1, 3, TQ, TK), lambda h: (h, 0, 0, 0)),
        out_shape=jax.ShapeDtypeStruct((nhead, 3, TQ, TK), F32),
        compiler_params=_params(("parallel",)),
    )(rel_bias)


def _with_near_bias(s, near):
    add = near[0] if len(near) == 1 else jnp.concatenate(near, axis=1)
    width, nw = s.shape[1], add.shape[1]
    if nw == width:
        return s + add
    return jnp.concatenate([s[:, 0:width - nw], s[:, width - nw:width] + add], axis=1)


def _softmax_step(s, v, m_ref, acc_ref):
    tk = s.shape[1]
    m_prev = m_ref[...]
    m_new = jnp.maximum(m_prev, jnp.max(s, axis=-1, keepdims=True))
    alpha = jnp.exp2(m_prev - m_new)
    p = jnp.exp2(s - jnp.concatenate([m_new] * (tk // LANES), axis=1))
    vaug = jnp.concatenate([v, jnp.ones((tk, LANES), BF16)], axis=1)
    acc_ref[...] = jnp.concatenate([alpha, alpha], axis=1) * acc_ref[...] + _dot(p.astype(BF16), vaug)
    m_ref[...] = m_new


def _softmax_first(s, v, m_ref, acc_ref):
    rows, tk = s.shape
    m = jnp.broadcast_to(jnp.max(s, axis=-1, keepdims=True), (rows, LANES))
    p = jnp.exp2(s - jnp.concatenate([m] * (tk // LANES), axis=1))
    vaug = jnp.concatenate([v, jnp.ones((tk, LANES), BF16)], axis=1)
    acc_ref[...] = _dot(p.astype(BF16), vaug)
    m_ref[...] = m


def _softmax_once(s, v):
    rows, tk = s.shape
    m = jnp.broadcast_to(jnp.max(s, axis=-1, keepdims=True), (rows, LANES))
    p = jnp.exp2(s - jnp.concatenate([m] * (tk // LANES), axis=1))
    vaug = jnp.concatenate([v, jnp.ones((tk, LANES), BF16)], axis=1)
    pv = _dot(p.astype(BF16), vaug)
    return pv[:, 0:LANES] * (1.0 / pv[:, LANES:2 * LANES])


def _causal_flash(i, nqt, tq, tk, chains, kv_of, bias_of, m_ref, acc_ref, also=None):
    def tail(first):
        stages = list(also(first)) if also is not None else []
        every = max(1, len(chains) // (len(stages) + 1))
        kv = {}
        for c, (q, qt, kv_id, bias_id) in enumerate(chains):
            ntile = (1 if first else 3) + qt
            if (kv_id, ntile) not in kv:
                start = 0 if first else pl.multiple_of((nqt * i - 2) * tk, 2 * tk)
                kv[(kv_id, ntile)] = kv_of(kv_id, start, ntile * tk)
            k, v = kv[(kv_id, ntile)]
            near = [bias_of(bias_id, d) for d in range(min(ntile, 2) - 1, -1, -1)]
            _softmax_first(_with_near_bias(_nt_dot(q, k), near), v, m_ref.at[c], acc_ref.at[c])
            if stages and (c + 1) % every == 0:
                stages.pop(0)()
        while stages:
            stages.pop(0)()

    @pl.when(i == 0)
    def _():
        tail(True)

    @pl.when(i >= 1)
    def _():
        tail(False)

    def far(j, carry):
        start = pl.multiple_of(j * 2 * tk, 2 * tk)
        kv = {}
        for c, (q, _, kv_id, _) in enumerate(chains):
            if kv_id not in kv:
                kv[kv_id] = kv_of(kv_id, start, 2 * tk)
            k, v = kv[kv_id]
            _softmax_step(_nt_dot(q, k), v, m_ref.at[c], acc_ref.at[c])
        return carry

    lax.fori_loop(0, (nqt * i - 2) // 2, far, 0)


def _normalized(acc_ref, c):
    return acc_ref[c, :, 0:LANES] * (1.0 / acc_ref[c, :, LANES:2 * LANES])


def _store_gated(o_ref, gate_ref, branches, tq, qt):
    lane = lax.broadcasted_iota(jnp.int32, (1, LANES), 1)
    rows = slice(qt * tq, (qt + 1) * tq)
    gates = gate_ref[0, rows, :]
    lane_full = lax.broadcasted_iota(jnp.int32, gates.shape, 1)
    for j in range(NSA_GQ):
        total = None
        for o_of, branch in branches:
            c0 = (0 * NSA_GQ + j) * 3 + branch
            c1 = (1 * NSA_GQ + j) * 3 + branch
            gexp = jnp.take_along_axis(gates, jnp.where(lane_full < HALF, c0, c1), axis=1)
            term = gexp * jnp.where(lane < HALF, o_of(0, j), o_of(1, j))
            total = term if total is None else total + term
        o_ref[0, rows, j * LANES:(j + 1) * LANES] = total.astype(BF16)


def _nsa_chain_index(qt, g, j):
    return (qt * NSA_KV + g) * NSA_GQ + j


def _cmp_stages(qi, tq, q_ref, kc_ref, vc_ref, ovt_ref, gate_ref, o_ref, nsel_ref):
    nblk = kc_ref.shape[1]
    lane = lax.broadcasted_iota(jnp.int32, (1, LANES), 1)
    state = {"outs": [], "imp_t": jnp.zeros((LANES, tq), F32), "dropped": []}

    def attend(g):
        n = lax.broadcasted_iota(jnp.int32, (1, nblk), 1)
        t4 = qi * tq + (lax.broadcasted_iota(jnp.int32, (NSA_GQ * tq, 1), 0) & (tq - 1))
        vis4 = (n * CMP_STRIDE + (CMP_LEN - 1)) <= t4
        has4 = jnp.where(t4 >= CMP_LEN - 1, 1.0, 0.0)
        qg = jnp.concatenate([q_ref[0, :, j * LANES:(j + 1) * LANES] * _half_mask(g) for j in range(NSA_GQ)], axis=0)
        s = jnp.where(vis4, _nt_dot(qg, kc_ref[0]), NEG)
        e = jnp.exp2(s - jnp.max(s, axis=-1, keepdims=True))
        p = e * (has4 / jnp.sum(e, axis=-1, keepdims=True))
        state["outs"].append(_dot(p.astype(BF16), vc_ref[0]))
        ps = p[0:tq] + p[tq:2 * tq] + p[2 * tq:3 * tq] + p[3 * tq:4 * tq]
        hi = ps.astype(BF16)
        r1 = ps - hi.astype(F32)
        mid = r1.astype(BF16)
        lo = (r1 - mid.astype(F32)).astype(BF16)
        state["imp_t"] = state["imp_t"] + (
            _nt_dot(ovt_ref[g], hi) + _nt_dot(ovt_ref[g], mid) + _nt_dot(ovt_ref[g], lo))

    def store_out():
        outs = state["outs"]
        _store_gated(o_ref, gate_ref, [(lambda g, j: outs[g][j * tq:(j + 1) * tq], 0)], tq, 0)

    def rank(base):
        tl = qi * tq + lax.broadcasted_iota(jnp.int32, (N_SLC, tq), 1)
        jrow = lax.broadcasted_iota(jnp.int32, (N_SLC, tq), 0)
        cur = tl >> 6
        valid = (jrow * SLC_BLOCK) <= tl
        forced = (jrow == 0) | (jrow == cur) | (jrow == cur - 1)
        sub = 8
        x = jnp.where(forced, jnp.inf, jnp.where(valid, state["imp_t"][base:base + N_SLC], -jnp.inf))
        others = [jnp.broadcast_to(x[jj:jj + 1], (sub, tq)) for jj in range(N_SLC)]
        chunks = []
        for r0 in range(0, N_SLC, sub):
            xr = x[r0:r0 + sub]
            jr = r0 + lax.broadcasted_iota(jnp.int32, (sub, tq), 0)
            cnt = jnp.zeros((sub, tq), F32)
            for jj in range(N_SLC):
                if jj < r0:
                    beats = others[jj] >= xr
                elif jj >= r0 + sub:
                    beats = others[jj] > xr
                else:
                    beats = (others[jj] > xr) | ((others[jj] == xr) & (jr > jj))
                cnt = cnt + jnp.where(beats, 1.0, 0.0)
            chunks.append(jnp.where(cnt < SLC_TOPK, 0.0, 1.0))
        state["dropped"].append(jnp.concatenate(chunks, axis=0))

    def store_selection():
        zero = jnp.zeros((N_SLC, tq), F32)
        dropped = state["dropped"]
        drop = jnp.concatenate([dropped[0], zero, dropped[1], zero], axis=0).T
        nsel_ref[0, :, 0:LANES] = jnp.where(lane >= HALF, drop, 0.0).astype(BF16)
        nsel_ref[0, :, LANES:2 * LANES] = jnp.where(lane < HALF, drop, 0.0).astype(BF16)

    return [functools.partial(attend, 0), functools.partial(attend, 1), store_out,
            functools.partial(rank, 0), functools.partial(rank, HALF), store_selection]


def _slc_win_kernel(bias_ref, q_ref, ks_ref, vs_ref, kw_ref, vw_ref, nsel_ref, gate_ref, o_ref,
                    m_ref, acc_ref, win_ref, *, tq, tk):
    i = pl.program_id(1)
    nqt = NQT_NSA
    chains = [(q, qt, g, g * NSA_GQ + j) for q, qt, g, j in _nsa_chains(q_ref, nsel_ref, tq, nqt)]
    win_chains = _nsa_chains(q_ref, None, tq, nqt)

    def kv_of(g, start, width):
        return (ks_ref[0, pl.ds(start, width), g * LANES:(g + 1) * LANES], vs_ref[0, pl.ds(start, width), :])

    def window(first):
        kv = {}

        def one(q, qt, g, j):
            ntile = min(1 + qt, 3) if first else 3
            if (qt, ntile) not in kv:
                start = max(qt - 2, 0) * tk if first else pl.multiple_of((nqt * i - 2 + qt) * tk, tk)
                kv[(qt, ntile)] = (kw_ref[0, pl.ds(start, ntile * tk), :], vw_ref[0, pl.ds(start, ntile * tk), :])
            k, v = kv[(qt, ntile)]
            near = [bias_ref[g * NSA_GQ + j, d] for d in range(min(ntile, 2) - 1, -1, -1)]
            if ntile == 3:
                near.insert(0, bias_ref[g * NSA_GQ + j, 2])
            win_ref[_nsa_chain_index(qt, g, j)] = _softmax_once(_with_near_bias(_nt_dot(q, k), near), v)

        def all_chains():
            for chain in win_chains:
                one(*chain)

        return [all_chains]

    _causal_flash(i, nqt, tq, tk, chains, kv_of, lambda h, d: bias_ref[h, d], m_ref, acc_ref, also=window)
    for qt in range(nqt):
        _store_gated(o_ref, gate_ref,
                     [(lambda g, j: _normalized(acc_ref, _nsa_chain_index(qt, g, j)), 1),
                      (lambda g, j: win_ref[_nsa_chain_index(qt, g, j)], 2)], tq, qt)


def _slc_win_attention(bias, main, nsel, gates):
    bsz, seq, _ = main.shape
    tq, tk, nqt = TQ, TK, NQT_NSA
    nchain = nqt * NSA_HEADS
    return pl.pallas_call(
        functools.partial(_slc_win_kernel, tq=tq, tk=tk),
        grid=(bsz, seq // (nqt * tq)),
        in_specs=[pl.BlockSpec((NSA_HEADS, 3, tq, tk), lambda b, i: (0, 0, 0, 0), pipeline_mode=pl.Buffered(1)),
                  pl.BlockSpec((1, nqt * tq, NSA_W), lambda b, i: (b, i, COL_NQ // NSA_W)),
                  pl.BlockSpec((1, seq, 2 * LANES), lambda b, i: (b, 0, COL_KSL // (2 * LANES))),
                  pl.BlockSpec((1, seq, LANES), lambda b, i: (b, 0, COL_VSL // LANES)),
                  pl.BlockSpec((1, seq, LANES), lambda b, i: (b, 0, COL_KW // LANES)),
                  pl.BlockSpec((1, seq, LANES), lambda b, i: (b, 0, COL_VW // LANES)),
                  pl.BlockSpec((1, nqt * tq, 2 * LANES), lambda b, i: (b, i, 0)),
                  pl.BlockSpec((1, nqt * tq, LANES), lambda b, i: (b, i, 0))],
        out_specs=pl.BlockSpec((1, nqt * tq, NSA_W), lambda b, i: (b, i, 0)),
        out_shape=jax.ShapeDtypeStruct((bsz, seq, NSA_W), BF16),
        scratch_shapes=[pltpu.VMEM((nchain, tq, LANES), F32),
                        pltpu.VMEM((nchain, tq, 2 * LANES), F32),
                        pltpu.VMEM((nchain, tq, LANES), F32)],
        compiler_params=_params(("parallel", "parallel")),
    )(bias, main, main, main, main, main, nsel, gates)


def _diff_cmp_kernel(bias_ref, lam_ref, q_ref, k_ref, v_ref, g_ref, nq_ref, kc_ref, vc_ref, ovt_ref, gate_ref,
                     o_ref, oc_ref, nsel_ref, m_ref, acc_ref, *, tq, tk):
    i = pl.program_id(1)
    nqt = NQT_DIFF

    chains = []
    for qt in range(nqt):
        for h in range(DIFF_HEADS):
            slab = q_ref[0, qt * tq:(qt + 1) * tq, h * LANES:(h + 1) * LANES]
            for half in range(2):
                chains.append((slab * _half_mask(half), qt, h, h))

    def kv_of(h, start, width):
        cols = slice(h * LANES, (h + 1) * LANES)
        return (k_ref[0, pl.ds(start, width), cols], v_ref[0, pl.ds(start, width), cols])

    def compressed(first):
        return _cmp_stages(i, nqt * tq, nq_ref, kc_ref, vc_ref, ovt_ref, gate_ref, oc_ref, nsel_ref)

    _causal_flash(i, nqt, tq, tk, chains, kv_of, lambda h, d: bias_ref[h, d], m_ref, acc_ref, also=compressed)

    lm = lam_ref[...]
    lam = (jnp.exp(jnp.sum(lm[0:1] * lm[1:2], axis=-1, keepdims=True))
           - jnp.exp(jnp.sum(lm[2:3] * lm[3:4], axis=-1, keepdims=True)) + LAMBDA_INIT)
    for qt in range(nqt):
        for h in range(DIFF_HEADS):
            c = (qt * DIFF_HEADS + h) * 2
            a = _normalized(acc_ref, c) - lam * _normalized(acc_ref, c + 1)
            o_ref[0, qt * tq:(qt + 1) * tq, h * LANES:(h + 1) * LANES] = (
                _rms(a, g_ref[...]) * (1.0 - LAMBDA_INIT)).astype(BF16)


def _diff_cmp_attention(bias, lam_rows, main, subln, kc, vc, ovt, gates):
    bsz, seq, _ = main.shape
    nblk = kc.shape[1]
    tq, tk, nqt = TQ, TK, NQT_DIFF
    nchain = nqt * DIFF_HEADS * 2
    step = lambda b, i: (b, i, 0)
    return pl.pallas_call(
        functools.partial(_diff_cmp_kernel, tq=tq, tk=tk),
        grid=(bsz, seq // (nqt * tq)),
        in_specs=[pl.BlockSpec((DIFF_HEADS, 3, tq, tk), lambda b, i: (NSA_HEADS // DIFF_HEADS, 0, 0, 0),
                               pipeline_mode=pl.Buffered(1)),
                  pl.BlockSpec((8, LANES), lambda b, i: (0, 0)),
                  pl.BlockSpec((1, nqt * tq, DIFF_W), lambda b, i: (b, i, COL_DQ // DIFF_W)),
                  pl.BlockSpec((1, seq, DIFF_W), lambda b, i: (b, 0, COL_DK // DIFF_W)),
                  pl.BlockSpec((1, seq, DIFF_W), lambda b, i: (b, 0, COL_DV // DIFF_W)),
                  pl.BlockSpec((1, LANES), lambda b, i: (0, 0)),
                  pl.BlockSpec((1, nqt * tq, NSA_W), lambda b, i: (b, i, COL_NQ // NSA_W)),
                  pl.BlockSpec((1, nblk, LANES), lambda b, i: (b, 0, 0)),
                  pl.BlockSpec((1, nblk, LANES), lambda b, i: (b, 0, 0)),
                  pl.BlockSpec((2, LANES, nblk), lambda b, i: (0, 0, 0)),
                  pl.BlockSpec((1, nqt * tq, LANES), step)],
        out_specs=[pl.BlockSpec((1, nqt * tq, DIFF_W), step),
                   pl.BlockSpec((1, nqt * tq, NSA_W), step),
                   pl.BlockSpec((1, nqt * tq, 2 * LANES), step)],
        out_shape=[jax.ShapeDtypeStruct((bsz, seq, DIFF_W), BF16),
                   jax.ShapeDtypeStruct((bsz, seq, NSA_W), BF16),
                   jax.ShapeDtypeStruct((bsz, seq, 2 * LANES), BF16)],
        scratch_shapes=[pltpu.VMEM((nchain, tq, LANES), F32),
                        pltpu.VMEM((nchain, tq, 2 * LANES), F32)],
        compiler_params=_params(("parallel", "parallel")),
    )(bias, lam_rows, main, main, main, subln, main, kc, vc, ovt, gates)


FF_CHUNK = 256


def _tail_kernel(x_ref, od_ref, oc_ref, osw_ref, gn_ref, woraw_ref, g2_ref,
                 wg_ref, wu_ref, wd_ref, gf_ref, o_ref, acc_ref, wo_ref):
    @pl.when(pl.program_id(0) == 0)
    def _():
        wo_ref[0:DIFF_W, :] = woraw_ref[0:DIFF_W, :].astype(BF16)
        for j in range(NSA_GQ):
            for g in range(NSA_KV):
                dst = DIFF_W + (j * NSA_KV + g) * NSA_HD
                src = DIFF_W + (g * NSA_GQ + j) * NSA_HD
                wo_ref[dst:dst + NSA_HD, :] = woraw_ref[src:src + NSA_HD, :].astype(BF16)

    o_nsa = oc_ref[...].astype(F32) + osw_ref[...].astype(F32)
    o_nsa = _rms(o_nsa, gn_ref[...]).astype(BF16)
    attn = _dot(od_ref[...], wo_ref[0:DIFF_W, :]) + _dot(o_nsa, wo_ref[DIFF_W:DIFF_W + NSA_W, :])
    x1 = x_ref[...] + attn
    acc_ref[...] = x1
    h = _rms(x1, g2_ref[...]).astype(BF16)
    for c in range(0, D_FF, FF_CHUNK):
        gate = _dot(h, wg_ref[:, c:c + FF_CHUNK])
        up = _dot(h, wu_ref[:, c:c + FF_CHUNK])
        act = (gate * (1.0 / (1.0 + jnp.exp(-gate))) * up).astype(BF16)
        acc_ref[...] += _dot(act, wd_ref[c:c + FF_CHUNK, :])
    o_ref[...] = _rms(acc_ref[...], gf_ref[...])


def _tail(x2, od, oc, osw, gn, w_out, g2, wg, wu, wd, gf):
    n = x2.shape[0]
    row = lambda i: (i, 0)
    fixed = lambda i: (0, 0)
    once = pl.Buffered(1)
    return pl.pallas_call(
        _tail_kernel,
        grid=(n // TM,),
        in_specs=[pl.BlockSpec((TM, D_MODEL), row),
                  pl.BlockSpec((TM, DIFF_W), row),
                  pl.BlockSpec((TM, NSA_W), row),
                  pl.BlockSpec((TM, NSA_W), row),
                  pl.BlockSpec((1, NSA_W), fixed),
                  pl.BlockSpec((DIFF_W + NSA_W, D_MODEL), fixed, pipeline_mode=once),
                  pl.BlockSpec((1, D_MODEL), fixed),
                  pl.BlockSpec((D_MODEL, D_FF), fixed, pipeline_mode=once),
                  pl.BlockSpec((D_MODEL, D_FF), fixed, pipeline_mode=once),
                  pl.BlockSpec((D_FF, D_MODEL), fixed, pipeline_mode=once),
                  pl.BlockSpec((1, D_MODEL), fixed)],
        out_specs=pl.BlockSpec((TM, D_MODEL), row),
        out_shape=jax.ShapeDtypeStruct((n, D_MODEL), F32),
        scratch_shapes=[pltpu.VMEM((TM, D_MODEL), F32),
                        pltpu.VMEM((DIFF_W + NSA_W, D_MODEL), BF16)],
        compiler_params=_params(("arbitrary",)),
    )(x2, od, oc, osw, gn, w_out, g2, wg, wu, wd, gf)


def _permute_heads(a, axis):
    shape = a.shape
    split = shape[:axis] + (NSA_KV, NSA_GQ, NSA_HD) + shape[axis + 1:]
    return jnp.swapaxes(a.reshape(split), axis, axis + 1).reshape(shape)


def _overlap_matrices(nblk):
    n_cmp = nblk - 1
    nslc = nblk * CMP_STRIDE // SLC_BLOCK
    assert nslc == N_SLC
    cmp_start = np.arange(n_cmp) * CMP_STRIDE
    cmp_end = cmp_start + CMP_LEN - 1
    slc_start = np.arange(nslc) * SLC_BLOCK
    overlap = ((cmp_start[:, None] < slc_start[None, :] + SLC_BLOCK) & (cmp_end[:, None] >= slc_start[None, :]))
    ovt = np.zeros((NSA_KV, LANES, nblk), np.float32)
    for g, base in enumerate((HALF, 0)):
        ovt[g, base:base + nslc, :n_cmp] = overlap.T
    return ovt


def _compress_weights(pos, w1, w2):
    half = CMP_LEN // 2
    shape = (2, half, NSA_KV, NSA_HD, CMP_HIDDEN)
    w1rep = jnp.broadcast_to(w1.reshape(2, half, 1, NSA_HD, CMP_HIDDEN), shape).reshape(2, -1, CMP_HIDDEN)
    w2big = jnp.einsum('jd,pg->pjgd', w2, jnp.eye(NSA_KV, dtype=F32)).reshape(NSA_KV * CMP_HIDDEN, NSA_KV * NSA_HD)
    posr = jnp.broadcast_to(pos.reshape(2, half, 1, NSA_HD), (2, half, NSA_KV, NSA_HD)).reshape(2, -1)
    return posr, w1rep.astype(BF16), w2big.astype(BF16)


def kernel(x, norm1, w_in, lambda_q1, lambda_k1, lambda_q2, lambda_k2, diff_subln, cmp_pos_k, cmp_w1_k, cmp_w2_k,
           cmp_pos_v, cmp_w1_v, cmp_w2_v, nsa_norm, w_out, norm2, w_gate, w_up, w_down, rel_bias, final_norm):
    bsz, seq, dm = x.shape
    assert dm == D_MODEL and seq % (max(NQT_DIFF, NQT_NSA) * TQ) == 0 and seq % TM == 0 and seq == N_SLC * SLC_BLOCK
    assert norm1.shape[0] == 1, "single layer"
    x2 = x.reshape(bsz * seq, dm)

    gn_p = _permute_heads(nsa_norm[0].reshape(1, NSA_W), 1)
    posk, w1k, w2k = _compress_weights(cmp_pos_k[0], cmp_w1_k[0], cmp_w2_k[0])
    posv, w1v, w2v = _compress_weights(cmp_pos_v[0], cmp_w1_v[0], cmp_w2_v[0])
    pos = jnp.concatenate([posk, posv, jnp.zeros((4, posk.shape[1]), F32)], axis=0)
    lam_rows = jnp.zeros((8, LANES), F32).at[0:4, 0:DIFF_HD].set(
        jnp.stack([lambda_q1[0], lambda_k1[0], lambda_q2[0], lambda_k2[0]]).astype(F32))

    main, kc_grp, vc_grp, gates = _inproj(x2, norm1[0].reshape(1, dm), w_in[0], seq)
    main = main.reshape(bsz, seq, N_MAIN)

    nrow = seq // CMP_STRIDE
    kc, vc = _compress(kc_grp.reshape(bsz, nrow, CMP_STRIDE * LANES), vc_grp.reshape(bsz, nrow, CMP_STRIDE * LANES),
                       pos, w1k, w2k, w1v, w2v)
    ovt = jnp.asarray(_overlap_matrices(nrow), BF16)
    gates = gates.reshape(bsz, seq, LANES)
    bias = _bias_tiles(rel_bias)
    o_diff, o_cmp, nsel = _diff_cmp_attention(bias, lam_rows, main, diff_subln[0].reshape(1, DIFF_VD),
                                              kc, vc, ovt, gates)
    o_sw = _slc_win_attention(bias, main, nsel, gates)

    n = bsz * seq
    out = _tail(x2, o_diff.reshape(n, DIFF_W), o_cmp.reshape(n, NSA_W), o_sw.reshape(n, NSA_W),
                gn_p, w_out[0], norm2[0].reshape(1, dm),
                w_gate[0].astype(BF16), w_up[0].astype(BF16), w_down[0].astype(BF16), final_norm.reshape(1, dm))
    return out.reshape(bsz, seq, dm)
```

```python
import functools
import math

import numpy as np
import jax
import jax.numpy as jnp
from jax import lax
from jax.experimental import pallas as pl
from jax.experimental.pallas import tpu as pltpu

F32 = jnp.float32
BF16 = jnp.bfloat16

D_MODEL = 1024
DIFF_HEADS = 4
DIFF_HD = 64
DIFF_VD = 128
DIFF_W = 512
NSA_HEADS = 8
NSA_KV = 2
NSA_GQ = 4
NSA_HD = 64
NSA_W = 512
CMP_LEN = 32
CMP_STRIDE = 16
CMP_HIDDEN = 256
SLC_BLOCK = 64
SLC_TOPK = 16
N_SLC = 32
WINDOW = 512
N_BUCKETS = 32
MAX_DISTANCE = 128
D_FF = 2816
N_IN = 2840
NEG = -1e30
EPS = 1e-6
LAMBDA_INIT = 0.8 - 0.6 * math.exp(-0.3 * 0)
LOG2E = math.log2(math.e)
QSCALE = NSA_HD ** -0.5 * LOG2E

LANES = 128
HALF = LANES // 2
MXU_N = 256
COL_DQ, COL_DK, COL_DV, COL_NQ, COL_KSL, COL_VSL, COL_KW, COL_VW, N_MAIN = (
    0, 512, 1024, 1536, 2048, 2304, 2432, 2560, 2688)
N_PROJ = N_MAIN + 3 * LANES
VMEM_LIMIT = 48 * 1024 * 1024

TQ = 256
TK = 256
TM = 512
NQT_DIFF = 4
NQT_NSA = 4


def _t5_thresholds():
    d = np.arange(0, 4 * MAX_DISTANCE)
    max_exact = N_BUCKETS // 2
    val = (np.log(np.maximum(d, 1).astype(np.float32) / np.float32(max_exact))
           / np.float32(math.log(MAX_DISTANCE / max_exact)) * np.float32(N_BUCKETS - max_exact))
    large = np.minimum(max_exact + val.astype(np.int32), N_BUCKETS - 1)
    bucket = np.where(d < max_exact, d, large)
    assert np.all(np.diff(bucket) >= 0) and bucket[-1] == N_BUCKETS - 1
    return [int(np.argmax(bucket >= k)) for k in range(N_BUCKETS)]


T5_THRESH = _t5_thresholds()
assert T5_THRESH[-1] <= TK + 1
assert WINDOW == 2 * TK and TQ == TK


def _nt_dot(a, b):
    return lax.dot_general(a, b, (((1,), (1,)), ((), ())), preferred_element_type=F32)


def _dot(a, b):
    return jnp.dot(a, b, preferred_element_type=F32)


def _rms(x, g):
    ms = jnp.mean(x * x, axis=-1, keepdims=True)
    return x * lax.rsqrt(ms + EPS) * g


def _params(sem):
    return pltpu.CompilerParams(dimension_semantics=sem, vmem_limit_bytes=VMEM_LIMIT)


def _store_grouped(out_ref, tok_ref, y):
    tok_ref[...] = y
    nrow = out_ref.shape[0]
    for l in range(CMP_STRIDE):
        out_ref[:, l * LANES:(l + 1) * LANES] = tok_ref[pl.ds(l, nrow, stride=CMP_STRIDE), :].astype(BF16)


def _arrange_w_in(w_ref, wp_ref):
    src_dk, src_nq, src_kc, src_ksl, src_vsl, src_gt = 512, 1536, 2048, 2304, 2432, 2816
    halves = []
    halves += [(c, QSCALE) for c in range(0, src_dk, HALF)]
    halves += [(c, None) for c in range(src_dk, src_nq, HALF)]
    halves += [(src_nq + (g * NSA_GQ + j) * NSA_HD, QSCALE) for j in range(NSA_GQ) for g in range(NSA_KV)]
    halves += [(src_ksl, None), (None, None), (None, None), (src_ksl + HALF, None)]
    halves += [(c, None) for c in range(src_vsl, src_gt, HALF)]
    halves += [(c, None) for c in range(src_kc, src_ksl, HALF)]
    assert len(halves) * HALF == N_PROJ - LANES
    rows = 256
    for r0 in range(0, D_MODEL, rows):
        rs = slice(r0, r0 + rows)

        def load(entry):
            col, scale = entry
            if col is None:
                return jnp.zeros((rows, HALF), F32)
            val = w_ref[rs, col:col + HALF]
            return val if scale is None else val * scale

        for t in range(0, len(halves), 2):
            tile = jnp.concatenate([load(halves[t]), load(halves[t + 1])], axis=1)
            wp_ref[rs, (t // 2) * LANES:(t // 2 + 1) * LANES] = tile.astype(BF16)
        ngate = N_IN - src_gt
        gates = jnp.concatenate([w_ref[rs, src_gt:N_IN], jnp.zeros((rows, LANES - ngate), F32)], axis=1)
        wp_ref[rs, N_PROJ - LANES:N_PROJ] = gates.astype(BF16)


def _inproj_kernel(x_ref, g_ref, wraw_ref, main_ref, kc_ref, vc_ref, gate_ref, tok_ref, w_ref, *, seq):
    tm = x_ref.shape[0]

    @pl.when(pl.program_id(0) == 0)
    def _():
        _arrange_w_in(wraw_ref, w_ref)

    h = _rms(x_ref[...], g_ref[...]).astype(BF16)
    for c in range(0, N_PROJ, MXU_N):
        y = _dot(h, w_ref[:, c:c + MXU_N])
        if c == COL_KSL:
            t0 = lax.rem(pl.program_id(0) * tm, seq)
            blk = (t0 + lax.broadcasted_iota(jnp.int32, (tm, 2 * LANES), 0)) >> 6
            lane = lax.broadcasted_iota(jnp.int32, (tm, 2 * LANES), 1)
            slot = jnp.where(lane < LANES, lane - HALF, lane - LANES)
            y = y + jnp.where((slot == blk) & (slot >= 0) & (slot < N_SLC), NEG, 0.0)
        if c + MXU_N <= N_MAIN:
            main_ref[:, c:c + MXU_N] = y.astype(BF16)
        elif c == COL_VW:
            main_ref[:, c:c + LANES] = y[:, 0:LANES].astype(BF16)
            _store_grouped(kc_ref, tok_ref, y[:, LANES:MXU_N])
        else:
            _store_grouped(vc_ref, tok_ref, y[:, 0:LANES])
            gate_ref[...] = 1.0 / (1.0 + jnp.exp(-y[:, LANES:MXU_N]))


def _inproj(x2, g1, w_in, seq):
    n = x2.shape[0]
    return pl.pallas_call(
        functools.partial(_inproj_kernel, seq=seq),
        grid=(n // TM,),
        in_specs=[pl.BlockSpec((TM, D_MODEL), lambda i: (i, 0)),
                  pl.BlockSpec((1, D_MODEL), lambda i: (0, 0)),
                  pl.BlockSpec((D_MODEL, N_IN), lambda i: (0, 0), pipeline_mode=pl.Buffered(1))],
        out_specs=[pl.BlockSpec((TM, N_MAIN), lambda i: (i, 0)),
                   pl.BlockSpec((TM // CMP_STRIDE, CMP_STRIDE * LANES), lambda i: (i, 0)),
                   pl.BlockSpec((TM // CMP_STRIDE, CMP_STRIDE * LANES), lambda i: (i, 0)),
                   pl.BlockSpec((TM, LANES), lambda i: (i, 0))],
        out_shape=[jax.ShapeDtypeStruct((n, N_MAIN), BF16),
                   jax.ShapeDtypeStruct((n // CMP_STRIDE, CMP_STRIDE * LANES), BF16),
                   jax.ShapeDtypeStruct((n // CMP_STRIDE, CMP_STRIDE * LANES), BF16),
                   jax.ShapeDtypeStruct((n, LANES), F32)],
        scratch_shapes=[pltpu.VMEM((TM, LANES), F32),
                        pltpu.VMEM((D_MODEL, N_PROJ), BF16)],
        compiler_params=_params(("arbitrary",)),
    )(x2, g1, w_in)


def _gelu_tanh(x):
    return x * (0.5 * (1.0 + jnp.tanh(math.sqrt(2.0 / math.pi) * (x + 0.044715 * (x * x * x)))))


def _compress_kernel(rk_ref, rv_ref, pos_ref, w1k_ref, w2k_ref, w1v_ref, w2v_ref, kc_ref, vc_ref):
    width = rk_ref.shape[2]
    lane = lax.broadcasted_iota(jnp.int32, (1, width), 1)
    in_group = [jnp.where(((lane >> 6) & 1) == g, 1.0, 0.0) for g in range(NSA_KV)]

    def comp(r_ref, p_top, p_bot, w1_ref, w2_ref):
        r = r_ref[0].astype(F32)
        hid = []
        for g in range(NSA_KV):
            top = _dot(((r + p_top) * in_group[g]).astype(BF16), w1_ref[0])
            bot = _dot(((r + p_bot) * in_group[g]).astype(BF16), w1_ref[1])
            nrow = top.shape[0]
            hid.append(top + pltpu.roll(bot, nrow - 1, 0))
        act = _gelu_tanh(jnp.concatenate(hid, axis=1)).astype(BF16)
        return _dot(act, w2_ref[...]).astype(BF16)

    kc_ref[0] = comp(rk_ref, pos_ref[0:1, :], pos_ref[1:2, :], w1k_ref, w2k_ref)
    vc_ref[0] = comp(rv_ref, pos_ref[2:3, :], pos_ref[3:4, :], w1v_ref, w2v_ref)


def _compress(rk, rv, pos, w1k, w2k, w1v, w2v):
    bsz, nrow, width = rk.shape
    hid2 = 2 * CMP_HIDDEN
    return pl.pallas_call(
        _compress_kernel,
        grid=(bsz,),
        in_specs=[pl.BlockSpec((1, nrow, width), lambda b: (b, 0, 0)),
                  pl.BlockSpec((1, nrow, width), lambda b: (b, 0, 0)),
                  pl.BlockSpec((8, width), lambda b: (0, 0)),
                  pl.BlockSpec((2, width, CMP_HIDDEN), lambda b: (0, 0, 0)),
                  pl.BlockSpec((hid2, LANES), lambda b: (0, 0)),
                  pl.BlockSpec((2, width, CMP_HIDDEN), lambda b: (0, 0, 0)),
                  pl.BlockSpec((hid2, LANES), lambda b: (0, 0))],
        out_specs=[pl.BlockSpec((1, nrow, LANES), lambda b: (b, 0, 0)),
                   pl.BlockSpec((1, nrow, LANES), lambda b: (b, 0, 0))],
        out_shape=[jax.ShapeDtypeStruct((bsz, nrow, LANES), BF16),
                   jax.ShapeDtypeStruct((bsz, nrow, LANES), BF16)],
        compiler_params=_params(("parallel",)),
    )(rk, rv, pos, w1k, w2k, w1v, w2v)


def _half_mask(g):
    lane = lax.broadcasted_iota(jnp.int32, (1, LANES), 1)
    return jnp.where((lane >= g * HALF) & (lane < (g + 1) * HALF), 1.0, 0.0).astype(BF16)


def _nsa_chains(q_ref, extra_ref, tq, nqt):
    chains = []
    for qt in range(nqt):
        rows = slice(qt * tq, (qt + 1) * tq)
        for g in range(NSA_KV):
            keep = _half_mask(g)
            for j in range(NSA_GQ):
                q = q_ref[0, rows, j * LANES:(j + 1) * LANES] * keep
                if extra_ref is not None:
                    q = q + extra_ref[0, rows, g * LANES:(g + 1) * LANES]
                chains.append((q, qt, g, j))
    return chains


def _bias_kernel(rel_ref, o_ref, *, tq, tk):
    row = pl.program_id(0)
    col = jnp.where(row < NSA_HEADS, row + DIFF_HEADS, row - NSA_HEADS)
    r = lax.broadcasted_iota(jnp.int32, (tq, tk), 0)
    c = lax.broadcasted_iota(jnp.int32, (tq, tk), 1)
    far = rel_ref[N_BUCKETS - 1, col]
    for d in range(2):
        dist = r - c + d * tq
        val = jnp.zeros((tq, tk), F32) + rel_ref[0, col]
        for k in range(1, N_BUCKETS):
            val = jnp.where(dist >= T5_THRESH[k], rel_ref[k, col], val)
        val = (val - far) * LOG2E
        if d == 0:
            val = jnp.where(dist >= 0, val, NEG)
        o_ref[0, d] = val
    o_ref[0, 2] = jnp.where(r < c, 0.0, NEG)


def _bias_tiles(rel_bias):
    nhead = rel_bias.shape[1]
    return pl.pallas_call(
        functools.partial(_bias_kernel, tq=TQ, tk=TK),
        grid=(nhead,),
        in_specs=[pl.BlockSpec(memory_space=pltpu.SMEM)],
        out_specs=pl.BlockSpec((1, 3, TQ, TK), lambda h: (h, 0, 0, 0)),
        out_shape=jax.ShapeDtypeStruct((nhead, 3, TQ, TK), F32),
        compiler_params=_params(("parallel",)),
    )(rel_bias)


def _with_near_bias(s, near):
    add = near[0] if len(near) == 1 else jnp.concatenate(near, axis=1)
    width, nw = s.shape[1], add.shape[1]
    if nw == width:
        return s + add
    return jnp.concatenate([s[:, 0:width - nw], s[:, width - nw:width] + add], axis=1)


def _softmax_step(s, v, m_ref, acc_ref):
    tk = s.shape[1]
    m_prev = m_ref[...]
    m_new = jnp.maximum(m_prev, jnp.max(s, axis=-1, keepdims=True))
    alpha = jnp.exp2(m_prev - m_new)
    p = jnp.exp2(s - jnp.concatenate([m_new] * (tk // LANES), axis=1))
    vaug = jnp.concatenate([v, jnp.ones((tk, LANES), BF16)], axis=1)
    acc_ref[...] = jnp.concatenate([alpha, alpha], axis=1) * acc_ref[...] + _dot(p.astype(BF16), vaug)
    m_ref[...] = m_new


def _softmax_first(s, v, m_ref, acc_ref):
    rows, tk = s.shape
    m = jnp.broadcast_to(jnp.max(s, axis=-1, keepdims=True), (rows, LANES))
    p = jnp.exp2(s - jnp.concatenate([m] * (tk // LANES), axis=1))
    vaug = jnp.concatenate([v, jnp.ones((tk, LANES), BF16)], axis=1)
    acc_ref[...] = _dot(p.astype(BF16), vaug)
    m_ref[...] = m


def _softmax_once(s, v):
    rows, tk = s.shape
    m = jnp.broadcast_to(jnp.max(s, axis=-1, keepdims=True), (rows, LANES))
    p = jnp.exp2(s - jnp.concatenate([m] * (tk // LANES), axis=1))
    vaug = jnp.concatenate([v, jnp.ones((tk, LANES), BF16)], axis=1)
    pv = _dot(p.astype(BF16), vaug)
    return pv[:, 0:LANES] * (1.0 / pv[:, LANES:2 * LANES])


def _causal_flash(i, nqt, tq, tk, chains, kv_of, bias_of, m_ref, acc_ref, also=None):
    def tail(first):
        stages = list(also(first)) if also is not None else []
        every = max(1, len(chains) // (len(stages) + 1))
        kv = {}
        for c, (q, qt, kv_id, bias_id) in enumerate(chains):
            ntile = (1 if first else 3) + qt
            if (kv_id, ntile) not in kv:
                start = 0 if first else pl.multiple_of((nqt * i - 2) * tk, 2 * tk)
                kv[(kv_id, ntile)] = kv_of(kv_id, start, ntile * tk)
            k, v = kv[(kv_id, ntile)]
            near = [bias_of(bias_id, d) for d in range(min(ntile, 2) - 1, -1, -1)]
            _softmax_first(_with_near_bias(_nt_dot(q, k), near), v, m_ref.at[c], acc_ref.at[c])
            if stages and (c + 1) % every == 0:
                stages.pop(0)()
        while stages:
            stages.pop(0)()

    @pl.when(i == 0)
    def _():
        tail(True)

    @pl.when(i >= 1)
    def _():
        tail(False)

    def far(j, carry):
        start = pl.multiple_of(j * 2 * tk, 2 * tk)
        kv = {}
        for c, (q, _, kv_id, _) in enumerate(chains):
            if kv_id not in kv:
                kv[kv_id] = kv_of(kv_id, start, 2 * tk)
            k, v = kv[kv_id]
            _softmax_step(_nt_dot(q, k), v, m_ref.at[c], acc_ref.at[c])
        return carry

    lax.fori_loop(0, (nqt * i - 2) // 2, far, 0)


def _normalized(acc_ref, c):
    return acc_ref[c, :, 0:LANES] * (1.0 / acc_ref[c, :, LANES:2 * LANES])


def _store_gated(o_ref, gate_ref, branches, tq, qt):
    lane = lax.broadcasted_iota(jnp.int32, (1, LANES), 1)
    rows = slice(qt * tq, (qt + 1) * tq)
    gates = gate_ref[0, rows, :]
    lane_full = lax.broadcasted_iota(jnp.int32, gates.shape, 1)
    for j in range(NSA_GQ):
        total = None
        for o_of, branch in branches:
            c0 = (0 * NSA_GQ + j) * 3 + branch
            c1 = (1 * NSA_GQ + j) * 3 + branch
            gexp = jnp.take_along_axis(gates, jnp.where(lane_full < HALF, c0, c1), axis=1)
            term = gexp * jnp.where(lane < HALF, o_of(0, j), o_of(1, j))
            total = term if total is None else total + term
        o_ref[0, rows, j * LANES:(j + 1) * LANES] = total.astype(BF16)


def _nsa_chain_index(qt, g, j):
    return (qt * NSA_KV + g) * NSA_GQ + j


def _cmp_stages(qi, tq, q_ref, kc_ref, vc_ref, ovt_ref, gate_ref, o_ref, nsel_ref):
    nblk = kc_ref.shape[1]
    lane = lax.broadcasted_iota(jnp.int32, (1, LANES), 1)
    state = {"outs": [], "imp_t": jnp.zeros((LANES, tq), F32), "dropped": []}

    def attend(g):
        n = lax.broadcasted_iota(jnp.int32, (1, nblk), 1)
        t4 = qi * tq + (lax.broadcasted_iota(jnp.int32, (NSA_GQ * tq, 1), 0) & (tq - 1))
        vis4 = (n * CMP_STRIDE + (CMP_LEN - 1)) <= t4
        has4 = jnp.where(t4 >= CMP_LEN - 1, 1.0, 0.0)
        qg = jnp.concatenate([q_ref[0, :, j * LANES:(j + 1) * LANES] * _half_mask(g) for j in range(NSA_GQ)], axis=0)
        s = jnp.where(vis4, _nt_dot(qg, kc_ref[0]), NEG)
        e = jnp.exp2(s - jnp.max(s, axis=-1, keepdims=True))
        p = e * (has4 / jnp.sum(e, axis=-1, keepdims=True))
        state["outs"].append(_dot(p.astype(BF16), vc_ref[0]))
        ps = p[0:tq] + p[tq:2 * tq] + p[2 * tq:3 * tq] + p[3 * tq:4 * tq]
        hi = ps.astype(BF16)
        r1 = ps - hi.astype(F32)
        mid = r1.astype(BF16)
        lo = (r1 - mid.astype(F32)).astype(BF16)
        state["imp_t"] = state["imp_t"] + (
            _nt_dot(ovt_ref[g], hi) + _nt_dot(ovt_ref[g], mid) + _nt_dot(ovt_ref[g], lo))

    def store_out():
        outs = state["outs"]
        _store_gated(o_ref, gate_ref, [(lambda g, j: outs[g][j * tq:(j + 1) * tq], 0)], tq, 0)

    def rank(base):
        tl = qi * tq + lax.broadcasted_iota(jnp.int32, (N_SLC, tq), 1)
        jrow = lax.broadcasted_iota(jnp.int32, (N_SLC, tq), 0)
        cur = tl >> 6
        valid = (jrow * SLC_BLOCK) <= tl
        forced = (jrow == 0) | (jrow == cur) | (jrow == cur - 1)
        sub = 8
        x = jnp.where(forced, jnp.inf, jnp.where(valid, state["imp_t"][base:base + N_SLC], -jnp.inf))
        others = [jnp.broadcast_to(x[jj:jj + 1], (sub, tq)) for jj in range(N_SLC)]
        chunks = []
        for r0 in range(0, N_SLC, sub):
            xr = x[r0:r0 + sub]
            jr = r0 + lax.broadcasted_iota(jnp.int32, (sub, tq), 0)
            cnt = jnp.zeros((sub, tq), F32)
            for jj in range(N_SLC):
                if jj < r0:
                    beats = others[jj] >= xr
                elif jj >= r0 + sub:
                    beats = others[jj] > xr
                else:
                    beats = (others[jj] > xr) | ((others[jj] == xr) & (jr > jj))
                cnt = cnt + jnp.where(beats, 1.0, 0.0)
            chunks.append(jnp.where(cnt < SLC_TOPK, 0.0, 1.0))
        state["dropped"].append(jnp.concatenate(chunks, axis=0))

    def store_selection():
        zero = jnp.zeros((N_SLC, tq), F32)
        dropped = state["dropped"]
        drop = jnp.concatenate([dropped[0], zero, dropped[1], zero], axis=0).T
        nsel_ref[0, :, 0:LANES] = jnp.where(lane >= HALF, drop, 0.0).astype(BF16)
        nsel_ref[0, :, LANES:2 * LANES] = jnp.where(lane < HALF, drop, 0.0).astype(BF16)

    return [functools.partial(attend, 0), functools.partial(attend, 1), store_out,
            functools.partial(rank, 0), functools.partial(rank, HALF), store_selection]


def _slc_win_kernel(bias_ref, q_ref, ks_ref, vs_ref, kw_ref, vw_ref, nsel_ref, gate_ref, o_ref,
                    m_ref, acc_ref, win_ref, *, tq, tk):
    i = pl.program_id(1)
    nqt = NQT_NSA
    chains = [(q, qt, g, g * NSA_GQ + j) for q, qt, g, j in _nsa_chains(q_ref, nsel_ref, tq, nqt)]
    win_chains = _nsa_chains(q_ref, None, tq, nqt)

    def kv_of(g, start, width):
        return (ks_ref[0, pl.ds(start, width), g * LANES:(g + 1) * LANES], vs_ref[0, pl.ds(start, width), :])

    def window(first):
        kv = {}

        def one(q, qt, g, j):
            ntile = min(1 + qt, 3) if first else 3
            if (qt, ntile) not in kv:
                start = max(qt - 2, 0) * tk if first else pl.multiple_of((nqt * i - 2 + qt) * tk, tk)
                kv[(qt, ntile)] = (kw_ref[0, pl.ds(start, ntile * tk), :], vw_ref[0, pl.ds(start, ntile * tk), :])
            k, v = kv[(qt, ntile)]
            near = [bias_ref[g * NSA_GQ + j, d] for d in range(min(ntile, 2) - 1, -1, -1)]
            if ntile == 3:
                near.insert(0, bias_ref[g * NSA_GQ + j, 2])
            win_ref[_nsa_chain_index(qt, g, j)] = _softmax_once(_with_near_bias(_nt_dot(q, k), near), v)

        def all_chains():
            for chain in win_chains:
                one(*chain)

        return [all_chains]

    _causal_flash(i, nqt, tq, tk, chains, kv_of, lambda h, d: bias_ref[h, d], m_ref, acc_ref, also=window)
    for qt in range(nqt):
        _store_gated(o_ref, gate_ref,
                     [(lambda g, j: _normalized(acc_ref, _nsa_chain_index(qt, g, j)), 1),
                      (lambda g, j: win_ref[_nsa_chain_index(qt, g, j)], 2)], tq, qt)


def _slc_win_attention(bias, main, nsel, gates):
    bsz, seq, _ = main.shape
    tq, tk, nqt = TQ, TK, NQT_NSA
    nchain = nqt * NSA_HEADS
    return pl.pallas_call(
        functools.partial(_slc_win_kernel, tq=tq, tk=tk),
        grid=(bsz, seq // (nqt * tq)),
        in_specs=[pl.BlockSpec((NSA_HEADS, 3, tq, tk), lambda b, i: (0, 0, 0, 0), pipeline_mode=pl.Buffered(1)),
                  pl.BlockSpec((1, nqt * tq, NSA_W), lambda b, i: (b, i, COL_NQ // NSA_W)),
                  pl.BlockSpec((1, seq, 2 * LANES), lambda b, i: (b, 0, COL_KSL // (2 * LANES))),
                  pl.BlockSpec((1, seq, LANES), lambda b, i: (b, 0, COL_VSL // LANES)),
                  pl.BlockSpec((1, seq, LANES), lambda b, i: (b, 0, COL_KW // LANES)),
                  pl.BlockSpec((1, seq, LANES), lambda b, i: (b, 0, COL_VW // LANES)),
                  pl.BlockSpec((1, nqt * tq, 2 * LANES), lambda b, i: (b, i, 0)),
                  pl.BlockSpec((1, nqt * tq, LANES), lambda b, i: (b, i, 0))],
        out_specs=pl.BlockSpec((1, nqt * tq, NSA_W), lambda b, i: (b, i, 0)),
        out_shape=jax.ShapeDtypeStruct((bsz, seq, NSA_W), BF16),
        scratch_shapes=[pltpu.VMEM((nchain, tq, LANES), F32),
                        pltpu.VMEM((nchain, tq, 2 * LANES), F32),
                        pltpu.VMEM((nchain, tq, LANES), F32)],
        compiler_params=_params(("parallel", "parallel")),
    )(bias, main, main, main, main, main, nsel, gates)


def _diff_cmp_kernel(bias_ref, lam_ref, q_ref, k_ref, v_ref, g_ref, nq_ref, kc_ref, vc_ref, ovt_ref, gate_ref,
                     o_ref, oc_ref, nsel_ref, m_ref, acc_ref, *, tq, tk):
    i = pl.program_id(1)
    nqt = NQT_DIFF

    chains = []
    for qt in range(nqt):
        for h in range(DIFF_HEADS):
            slab = q_ref[0, qt * tq:(qt + 1) * tq, h * LANES:(h + 1) * LANES]
            for half in range(2):
                chains.append((slab * _half_mask(half), qt, h, h))

    def kv_of(h, start, width):
        cols = slice(h * LANES, (h + 1) * LANES)
        return (k_ref[0, pl.ds(start, width), cols], v_ref[0, pl.ds(start, width), cols])

    def compressed(first):
        return _cmp_stages(i, nqt * tq, nq_ref, kc_ref, vc_ref, ovt_ref, gate_ref, oc_ref, nsel_ref)

    _causal_flash(i, nqt, tq, tk, chains, kv_of, lambda h, d: bias_ref[h, d], m_ref, acc_ref, also=compressed)

    lm = lam_ref[...]
    lam = (jnp.exp(jnp.sum(lm[0:1] * lm[1:2], axis=-1, keepdims=True))
           - jnp.exp(jnp.sum(lm[2:3] * lm[3:4], axis=-1, keepdims=True)) + LAMBDA_INIT)
    for qt in range(nqt):
        for h in range(DIFF_HEADS):
            c = (qt * DIFF_HEADS + h) * 2
            a = _normalized(acc_ref, c) - lam * _normalized(acc_ref, c + 1)
            o_ref[0, qt * tq:(qt + 1) * tq, h * LANES:(h + 1) * LANES] = (
                _rms(a, g_ref[...]) * (1.0 - LAMBDA_INIT)).astype(BF16)


def _diff_cmp_attention(bias, lam_rows, main, subln, kc, vc, ovt, gates):
    bsz, seq, _ = main.shape
    nblk = kc.shape[1]
    tq, tk, nqt = TQ, TK, NQT_DIFF
    nchain = nqt * DIFF_HEADS * 2
    step = lambda b, i: (b, i, 0)
    return pl.pallas_call(
        functools.partial(_diff_cmp_kernel, tq=tq, tk=tk),
        grid=(bsz, seq // (nqt * tq)),
        in_specs=[pl.BlockSpec((DIFF_HEADS, 3, tq, tk), lambda b, i: (NSA_HEADS // DIFF_HEADS, 0, 0, 0),
                               pipeline_mode=pl.Buffered(1)),
                  pl.BlockSpec((8, LANES), lambda b, i: (0, 0)),
                  pl.BlockSpec((1, nqt * tq, DIFF_W), lambda b, i: (b, i, COL_DQ // DIFF_W)),
                  pl.BlockSpec((1, seq, DIFF_W), lambda b, i: (b, 0, COL_DK // DIFF_W)),
                  pl.BlockSpec((1, seq, DIFF_W), lambda b, i: (b, 0, COL_DV // DIFF_W)),
                  pl.BlockSpec((1, LANES), lambda b, i: (0, 0)),
                  pl.BlockSpec((1, nqt * tq, NSA_W), lambda b, i: (b, i, COL_NQ // NSA_W)),
                  pl.BlockSpec((1, nblk, LANES), lambda b, i: (b, 0, 0)),
                  pl.BlockSpec((1, nblk, LANES), lambda b, i: (b, 0, 0)),
                  pl.BlockSpec((2, LANES, nblk), lambda b, i: (0, 0, 0)),
                  pl.BlockSpec((1, nqt * tq, LANES), step)],
        out_specs=[pl.BlockSpec((1, nqt * tq, DIFF_W), step),
                   pl.BlockSpec((1, nqt * tq, NSA_W), step),
                   pl.BlockSpec((1, nqt * tq, 2 * LANES), step)],
        out_shape=[jax.ShapeDtypeStruct((bsz, seq, DIFF_W), BF16),
                   jax.ShapeDtypeStruct((bsz, seq, NSA_W), BF16),
                   jax.ShapeDtypeStruct((bsz, seq, 2 * LANES), BF16)],
        scratch_shapes=[pltpu.VMEM((nchain, tq, LANES), F32),
                        pltpu.VMEM((nchain, tq, 2 * LANES), F32)],
        compiler_params=_params(("parallel", "parallel")),
    )(bias, lam_rows, main, main, main, subln, main, kc, vc, ovt, gates)


FF_CHUNK = 256


def _tail_kernel(x_ref, od_ref, oc_ref, osw_ref, gn_ref, woraw_ref, g2_ref,
                 wg_ref, wu_ref, wd_ref, gf_ref, o_ref, acc_ref, wo_ref):
    @pl.when(pl.program_id(0) == 0)
    def _():
        wo_ref[0:DIFF_W, :] = woraw_ref[0:DIFF_W, :].astype(BF16)
        for j in range(NSA_GQ):
            for g in range(NSA_KV):
                dst = DIFF_W + (j * NSA_KV + g) * NSA_HD
                src = DIFF_W + (g * NSA_GQ + j) * NSA_HD
                wo_ref[dst:dst + NSA_HD, :] = woraw_ref[src:src + NSA_HD, :].astype(BF16)

    o_nsa = oc_ref[...].astype(F32) + osw_ref[...].astype(F32)
    o_nsa = _rms(o_nsa, gn_ref[...]).astype(BF16)
    attn = _dot(od_ref[...], wo_ref[0:DIFF_W, :]) + _dot(o_nsa, wo_ref[DIFF_W:DIFF_W + NSA_W, :])
    x1 = x_ref[...] + attn
    acc_ref[...] = x1
    h = _rms(x1, g2_ref[...]).astype(BF16)
    for c in range(0, D_FF, FF_CHUNK):
        gate = _dot(h, wg_ref[:, c:c + FF_CHUNK])
        up = _dot(h, wu_ref[:, c:c + FF_CHUNK])
        act = (gate * (1.0 / (1.0 + jnp.exp(-gate))) * up).astype(BF16)
        acc_ref[...] += _dot(act, wd_ref[c:c + FF_CHUNK, :])
    o_ref[...] = _rms(acc_ref[...], gf_ref[...])


def _tail(x2, od, oc, osw, gn, w_out, g2, wg, wu, wd, gf):
    n = x2.shape[0]
    row = lambda i: (i, 0)
    fixed = lambda i: (0, 0)
    once = pl.Buffered(1)
    return pl.pallas_call(
        _tail_kernel,
        grid=(n // TM,),
        in_specs=[pl.BlockSpec((TM, D_MODEL), row),
                  pl.BlockSpec((TM, DIFF_W), row),
                  pl.BlockSpec((TM, NSA_W), row),
                  pl.BlockSpec((TM, NSA_W), row),
                  pl.BlockSpec((1, NSA_W), fixed),
                  pl.BlockSpec((DIFF_W + NSA_W, D_MODEL), fixed, pipeline_mode=once),
                  pl.BlockSpec((1, D_MODEL), fixed),
                  pl.BlockSpec((D_MODEL, D_FF), fixed, pipeline_mode=once),
                  pl.BlockSpec((D_MODEL, D_FF), fixed, pipeline_mode=once),
                  pl.BlockSpec((D_FF, D_MODEL), fixed, pipeline_mode=once),
                  pl.BlockSpec((1, D_MODEL), fixed)],
        out_specs=pl.BlockSpec((TM, D_MODEL), row),
        out_shape=jax.ShapeDtypeStruct((n, D_MODEL), F32),
        scratch_shapes=[pltpu.VMEM((TM, D_MODEL), F32),
                        pltpu.VMEM((DIFF_W + NSA_W, D_MODEL), BF16)],
        compiler_params=_params(("arbitrary",)),
    )(x2, od, oc, osw, gn, w_out, g2, wg, wu, wd, gf)


def _permute_heads(a, axis):
    shape = a.shape
    split = shape[:axis] + (NSA_KV, NSA_GQ, NSA_HD) + shape[axis + 1:]
    return jnp.swapaxes(a.reshape(split), axis, axis + 1).reshape(shape)


def _overlap_matrices(nblk):
    n_cmp = nblk - 1
    nslc = nblk * CMP_STRIDE // SLC_BLOCK
    assert nslc == N_SLC
    cmp_start = np.arange(n_cmp) * CMP_STRIDE
    cmp_end = cmp_start + CMP_LEN - 1
    slc_start = np.arange(nslc) * SLC_BLOCK
    overlap = ((cmp_start[:, None] < slc_start[None, :] + SLC_BLOCK) & (cmp_end[:, None] >= slc_start[None, :]))
    ovt = np.zeros((NSA_KV, LANES, nblk), np.float32)
    for g, base in enumerate((HALF, 0)):
        ovt[g, base:base + nslc, :n_cmp] = overlap.T
    return ovt


def _compress_weights(pos, w1, w2):
    half = CMP_LEN // 2
    shape = (2, half, NSA_KV, NSA_HD, CMP_HIDDEN)
    w1rep = jnp.broadcast_to(w1.reshape(2, half, 1, NSA_HD, CMP_HIDDEN), shape).reshape(2, -1, CMP_HIDDEN)
    w2big = jnp.einsum('jd,pg->pjgd', w2, jnp.eye(NSA_KV, dtype=F32)).reshape(NSA_KV * CMP_HIDDEN, NSA_KV * NSA_HD)
    posr = jnp.broadcast_to(pos.reshape(2, half, 1, NSA_HD), (2, half, NSA_KV, NSA_HD)).reshape(2, -1)
    return posr, w1rep.astype(BF16), w2big.astype(BF16)


def kernel(x, norm1, w_in, lambda_q1, lambda_k1, lambda_q2, lambda_k2, diff_subln, cmp_pos_k, cmp_w1_k, cmp_w2_k,
           cmp_pos_v, cmp_w1_v, cmp_w2_v, nsa_norm, w_out, norm2, w_gate, w_up, w_down, rel_bias, final_norm):
    bsz, seq, dm = x.shape
    assert dm == D_MODEL and seq % (max(NQT_DIFF, NQT_NSA) * TQ) == 0 and seq % TM == 0 and seq == N_SLC * SLC_BLOCK
    assert norm1.shape[0] == 1, "single layer"
    x2 = x.reshape(bsz * seq, dm)

    gn_p = _permute_heads(nsa_norm[0].reshape(1, NSA_W), 1)
    posk, w1k, w2k = _compress_weights(cmp_pos_k[0], cmp_w1_k[0], cmp_w2_k[0])
    posv, w1v, w2v = _compress_weights(cmp_pos_v[0], cmp_w1_v[0], cmp_w2_v[0])
    pos = jnp.concatenate([posk, posv, jnp.zeros((4, posk.shape[1]), F32)], axis=0)
    lam_rows = jnp.zeros((8, LANES), F32).at[0:4, 0:DIFF_HD].set(
        jnp.stack([lambda_q1[0], lambda_k1[0], lambda_q2[0], lambda_k2[0]]).astype(F32))

    main, kc_grp, vc_grp, gates = _inproj(x2, norm1[0].reshape(1, dm), w_in[0], seq)
    main = main.reshape(bsz, seq, N_MAIN)

    nrow = seq // CMP_STRIDE
    kc, vc = _compress(kc_grp.reshape(bsz, nrow, CMP_STRIDE * LANES), vc_grp.reshape(bsz, nrow, CMP_STRIDE * LANES),
                       pos, w1k, w2k, w1v, w2v)
    ovt = jnp.asarray(_overlap_matrices(nrow), BF16)
    gates = gates.reshape(bsz, seq, LANES)
    bias = _bias_tiles(rel_bias)
    o_diff, o_cmp, nsel = _diff_cmp_attention(bias, lam_rows, main, diff_subln[0].reshape(1, DIFF_VD),
                                              kc, vc, ovt, gates)
    o_sw = _slc_win_attention(bias, main, nsel, gates)

    n = bsz * seq
    out = _tail(x2, o_diff.reshape(n, DIFF_W), o_cmp.reshape(n, NSA_W), o_sw.reshape(n, NSA_W),
                gn_p, w_out[0], norm2[0].reshape(1, dm),
                w_gate[0].astype(BF16), w_up[0].astype(BF16), w_down[0].astype(BF16), final_norm.reshape(1, dm))
    return out.reshape(bsz, seq, dm)
```

```python
import functools
import math

import numpy as np
import jax
import jax.numpy as jnp
from jax import lax
from jax.experimental import pallas as pl
from jax.experimental.pallas import tpu as pltpu

F32 = jnp.float32
BF16 = jnp.bfloat16

D_MODEL = 1024
DIFF_HEADS = 4
DIFF_HD = 64
DIFF_VD = 128
DIFF_W = 512
NSA_HEADS = 8
NSA_KV = 2
NSA_GQ = 4
NSA_HD = 64
NSA_W = 512
CMP_LEN = 32
CMP_STRIDE = 16
CMP_HIDDEN = 256
SLC_BLOCK = 64
SLC_TOPK = 16
N_SLC = 32
WINDOW = 512
N_BUCKETS = 32
MAX_DISTANCE = 128
D_FF = 2816
N_IN = 2840
NEG = -1e30
EPS = 1e-6
LAMBDA_INIT = 0.8 - 0.6 * math.exp(-0.3 * 0)
LOG2E = math.log2(math.e)
QSCALE = NSA_HD ** -0.5 * LOG2E

LANES = 128
HALF = LANES // 2
MXU_N = 256
COL_DQ, COL_DK, COL_DV, COL_NQ, COL_KSL, COL_VSL, COL_KW, COL_VW, N_MAIN = (
    0, 512, 1024, 1536, 2048, 2304, 2432, 2560, 2688)
N_PROJ = N_MAIN + 3 * LANES
VMEM_LIMIT = 48 * 1024 * 1024

TQ = 256
TK = 256
TM = 512
NQT_DIFF = 4
NQT_NSA = 2


def _t5_thresholds():
    d = np.arange(0, 4 * MAX_DISTANCE)
    max_exact = N_BUCKETS // 2
    val = (np.log(np.maximum(d, 1).astype(np.float32) / np.float32(max_exact))
           / np.float32(math.log(MAX_DISTANCE / max_exact)) * np.float32(N_BUCKETS - max_exact))
    large = np.minimum(max_exact + val.astype(np.int32), N_BUCKETS - 1)
    bucket = np.where(d < max_exact, d, large)
    assert np.all(np.diff(bucket) >= 0) and bucket[-1] == N_BUCKETS - 1
    return [int(np.argmax(bucket >= k)) for k in range(N_BUCKETS)]


T5_THRESH = _t5_thresholds()
assert T5_THRESH[-1] <= TK + 1
assert WINDOW == 2 * TK and TQ == TK


def _nt_dot(a, b):
    return lax.dot_general(a, b, (((1,), (1,)), ((), ())), preferred_element_type=F32)


def _dot(a, b):
    return jnp.dot(a, b, preferred_element_type=F32)


def _rms(x, g):
    ms = jnp.mean(x * x, axis=-1, keepdims=True)
    return x * lax.rsqrt(ms + EPS) * g


def _params(sem):
    return pltpu.CompilerParams(dimension_semantics=sem, vmem_limit_bytes=VMEM_LIMIT)


def _store_grouped(out_ref, tok_ref, y):
    tok_ref[...] = y
    nrow = out_ref.shape[0]
    for l in range(CMP_STRIDE):
        out_ref[:, l * LANES:(l + 1) * LANES] = tok_ref[pl.ds(l, nrow, stride=CMP_STRIDE), :].astype(BF16)


def _arrange_w_in(w_ref, wp_ref):
    src_dk, src_nq, src_kc, src_ksl, src_vsl, src_gt = 512, 1536, 2048, 2304, 2432, 2816
    halves = []
    halves += [(c, QSCALE) for c in range(0, src_dk, HALF)]
    halves += [(c, None) for c in range(src_dk, src_nq, HALF)]
    halves += [(src_nq + (g * NSA_GQ + j) * NSA_HD, QSCALE) for j in range(NSA_GQ) for g in range(NSA_KV)]
    halves += [(src_ksl, None), (None, None), (None, None), (src_ksl + HALF, None)]
    halves += [(c, None) for c in range(src_vsl, src_gt, HALF)]
    halves += [(c, None) for c in range(src_kc, src_ksl, HALF)]
    assert len(halves) * HALF == N_PROJ - LANES
    rows = 256
    for r0 in range(0, D_MODEL, rows):
        rs = slice(r0, r0 + rows)

        def load(entry):
            col, scale = entry
            if col is None:
                return jnp.zeros((rows, HALF), F32)
            val = w_ref[rs, col:col + HALF]
            return val if scale is None else val * scale

        for t in range(0, len(halves), 2):
            tile = jnp.concatenate([load(halves[t]), load(halves[t + 1])], axis=1)
            wp_ref[rs, (t // 2) * LANES:(t // 2 + 1) * LANES] = tile.astype(BF16)
        ngate = N_IN - src_gt
        gates = jnp.concatenate([w_ref[rs, src_gt:N_IN], jnp.zeros((rows, LANES - ngate), F32)], axis=1)
        wp_ref[rs, N_PROJ - LANES:N_PROJ] = gates.astype(BF16)


def _inproj_kernel(x_ref, g_ref, wraw_ref, main_ref, kc_ref, vc_ref, gate_ref, tok_ref, w_ref, *, seq):
    tm = x_ref.shape[0]

    @pl.when(pl.program_id(0) == 0)
    def _():
        _arrange_w_in(wraw_ref, w_ref)

    h = _rms(x_ref[...], g_ref[...]).astype(BF16)
    for c in range(0, N_PROJ, MXU_N):
        y = _dot(h, w_ref[:, c:c + MXU_N])
        if c == COL_KSL:
            t0 = lax.rem(pl.program_id(0) * tm, seq)
            blk = (t0 + lax.broadcasted_iota(jnp.int32, (tm, 2 * LANES), 0)) >> 6
            lane = lax.broadcasted_iota(jnp.int32, (tm, 2 * LANES), 1)
            slot = jnp.where(lane < LANES, lane - HALF, lane - LANES)
            y = y + jnp.where((slot == blk) & (slot >= 0) & (slot < N_SLC), NEG, 0.0)
        if c + MXU_N <= N_MAIN:
            main_ref[:, c:c + MXU_N] = y.astype(BF16)
        elif c == COL_VW:
            main_ref[:, c:c + LANES] = y[:, 0:LANES].astype(BF16)
            _store_grouped(kc_ref, tok_ref, y[:, LANES:MXU_N])
        else:
            _store_grouped(vc_ref, tok_ref, y[:, 0:LANES])
            gate_ref[...] = 1.0 / (1.0 + jnp.exp(-y[:, LANES:MXU_N]))


def _inproj(x2, g1, w_in, seq):
    n = x2.shape[0]
    return pl.pallas_call(
        functools.partial(_inproj_kernel, seq=seq),
        grid=(n // TM,),
        in_specs=[pl.BlockSpec((TM, D_MODEL), lambda i: (i, 0)),
                  pl.BlockSpec((1, D_MODEL), lambda i: (0, 0)),
                  pl.BlockSpec((D_MODEL, N_IN), lambda i: (0, 0), pipeline_mode=pl.Buffered(1))],
        out_specs=[pl.BlockSpec((TM, N_MAIN), lambda i: (i, 0)),
                   pl.BlockSpec((TM // CMP_STRIDE, CMP_STRIDE * LANES), lambda i: (i, 0)),
                   pl.BlockSpec((TM // CMP_STRIDE, CMP_STRIDE * LANES), lambda i: (i, 0)),
                   pl.BlockSpec((TM, LANES), lambda i: (i, 0))],
        out_shape=[jax.ShapeDtypeStruct((n, N_MAIN), BF16),
                   jax.ShapeDtypeStruct((n // CMP_STRIDE, CMP_STRIDE * LANES), BF16),
                   jax.ShapeDtypeStruct((n // CMP_STRIDE, CMP_STRIDE * LANES), BF16),
                   jax.ShapeDtypeStruct((n, LANES), F32)],
        scratch_shapes=[pltpu.VMEM((TM, LANES), F32),
                        pltpu.VMEM((D_MODEL, N_PROJ), BF16)],
        compiler_params=_params(("arbitrary",)),
    )(x2, g1, w_in)


def _gelu_tanh(x):
    return x * (0.5 * (1.0 + jnp.tanh(math.sqrt(2.0 / math.pi) * (x + 0.044715 * (x * x * x)))))


def _compress_kernel(rk_ref, rv_ref, pos_ref, w1k_ref, w2k_ref, w1v_ref, w2v_ref, kc_ref, vc_ref):
    width = rk_ref.shape[2]
    lane = lax.broadcasted_iota(jnp.int32, (1, width), 1)
    in_group = [jnp.where(((lane >> 6) & 1) == g, 1.0, 0.0) for g in range(NSA_KV)]

    def comp(r_ref, p_top, p_bot, w1_ref, w2_ref):
        r = r_ref[0].astype(F32)
        hid = []
        for g in range(NSA_KV):
            top = _dot(((r + p_top) * in_group[g]).astype(BF16), w1_ref[0])
            bot = _dot(((r + p_bot) * in_group[g]).astype(BF16), w1_ref[1])
            nrow = top.shape[0]
            hid.append(top + pltpu.roll(bot, nrow - 1, 0))
        act = _gelu_tanh(jnp.concatenate(hid, axis=1)).astype(BF16)
        return _dot(act, w2_ref[...]).astype(BF16)

    kc_ref[0] = comp(rk_ref, pos_ref[0:1, :], pos_ref[1:2, :], w1k_ref, w2k_ref)
    vc_ref[0] = comp(rv_ref, pos_ref[2:3, :], pos_ref[3:4, :], w1v_ref, w2v_ref)


def _compress(rk, rv, pos, w1k, w2k, w1v, w2v):
    bsz, nrow, width = rk.shape
    hid2 = 2 * CMP_HIDDEN
    return pl.pallas_call(
        _compress_kernel,
        grid=(bsz,),
        in_specs=[pl.BlockSpec((1, nrow, width), lambda b: (b, 0, 0)),
                  pl.BlockSpec((1, nrow, width), lambda b: (b, 0, 0)),
                  pl.BlockSpec((8, width), lambda b: (0, 0)),
                  pl.BlockSpec((2, width, CMP_HIDDEN), lambda b: (0, 0, 0)),
                  pl.BlockSpec((hid2, LANES), lambda b: (0, 0)),
                  pl.BlockSpec((2, width, CMP_HIDDEN), lambda b: (0, 0, 0)),
                  pl.BlockSpec((hid2, LANES), lambda b: (0, 0))],
        out_specs=[pl.BlockSpec((1, nrow, LANES), lambda b: (b, 0, 0)),
                   pl.BlockSpec((1, nrow, LANES), lambda b: (b, 0, 0))],
        out_shape=[jax.ShapeDtypeStruct((bsz, nrow, LANES), BF16),
                   jax.ShapeDtypeStruct((bsz, nrow, LANES), BF16)],
        compiler_params=_params(("parallel",)),
    )(rk, rv, pos, w1k, w2k, w1v, w2v)


def _half_mask(g):
    lane = lax.broadcasted_iota(jnp.int32, (1, LANES), 1)
    return jnp.where((lane >= g * HALF) & (lane < (g + 1) * HALF), 1.0, 0.0).astype(BF16)


def _nsa_chains(q_ref, extra_ref, tq, nqt):
    chains = []
    for qt in range(nqt):
        rows = slice(qt * tq, (qt + 1) * tq)
        for g in range(NSA_KV):
            keep = _half_mask(g)
            for j in range(NSA_GQ):
                q = q_ref[0, rows, j * LANES:(j + 1) * LANES] * keep
                if extra_ref is not None:
                    q = q + extra_ref[0, rows, g * LANES:(g + 1) * LANES]
                chains.append((q, qt, g, j))
    return chains


def _bias_kernel(rel_ref, o_ref, *, tq, tk):
    row = pl.program_id(0)
    col = jnp.where(row < NSA_HEADS, row + DIFF_HEADS, row - NSA_HEADS)
    r = lax.broadcasted_iota(jnp.int32, (tq, tk), 0)
    c = lax.broadcasted_iota(jnp.int32, (tq, tk), 1)
    far = rel_ref[N_BUCKETS - 1, col]
    for d in range(2):
        dist = r - c + d * tq
        val = jnp.zeros((tq, tk), F32) + rel_ref[0, col]
        for k in range(1, N_BUCKETS):
            val = jnp.where(dist >= T5_THRESH[k], rel_ref[k, col], val)
        val = (val - far) * LOG2E
        if d == 0:
            val = jnp.where(dist >= 0, val, NEG)
        o_ref[0, d] = val
    o_ref[0, 2] = jnp.where(r < c, 0.0, NEG)


def _bias_tiles(rel_bias):
    nhead = rel_bias.shape[1]
    return pl.pallas_call(
        functools.partial(_bias_kernel, tq=TQ, tk=TK),
        grid=(nhead,),
        in_specs=[pl.BlockSpec(memory_space=pltpu.SMEM)],
        out_specs=pl.BlockSpec((1, 3, TQ, TK), lambda h: (h, 0, 0, 0)),
        out_shape=jax.ShapeDtypeStruct((nhead, 3, TQ, TK), F32),
        compiler_params=_params(("parallel",)),
    )(rel_bias)


def _with_near_bias(s, near):
    add = near[0] if len(near) == 1 else jnp.concatenate(near, axis=1)
    width, nw = s.shape[1], add.shape[1]
    if nw == width:
        return s + add
    return jnp.concatenate([s[:, 0:width - nw], s[:, width - nw:width] + add], axis=1)


def _softmax_step(s, v, m_ref, acc_ref):
    tk = s.shape[1]
    m_prev = m_ref[...]
    m_new = jnp.maximum(m_prev, jnp.max(s, axis=-1, keepdims=True))
    alpha = jnp.exp2(m_prev - m_new)
    p = jnp.exp2(s - jnp.concatenate([m_new] * (tk // LANES), axis=1))
    vaug = jnp.concatenate([v, jnp.ones((tk, LANES), BF16)], axis=1)
    acc_ref[...] = jnp.concatenate([alpha, alpha], axis=1) * acc_ref[...] + _dot(p.astype(BF16), vaug)
    m_ref[...] = m_new


def _softmax_first(s, v, m_ref, acc_ref):
    rows, tk = s.shape
    m = jnp.broadcast_to(jnp.max(s, axis=-1, keepdims=True), (rows, LANES))
    p = jnp.exp2(s - jnp.concatenate([m] * (tk // LANES), axis=1))
    vaug = jnp.concatenate([v, jnp.ones((tk, LANES), BF16)], axis=1)
    acc_ref[...] = _dot(p.astype(BF16), vaug)
    m_ref[...] = m


def _softmax_once(s, v):
    rows, tk = s.shape
    m = jnp.broadcast_to(jnp.max(s, axis=-1, keepdims=True), (rows, LANES))
    p = jnp.exp2(s - jnp.concatenate([m] * (tk // LANES), axis=1))
    vaug = jnp.concatenate([v, jnp.ones((tk, LANES), BF16)], axis=1)
    pv = _dot(p.astype(BF16), vaug)
    return pv[:, 0:LANES] * (1.0 / pv[:, LANES:2 * LANES])


def _causal_flash(i, nqt, tq, tk, chains, kv_of, bias_of, m_ref, acc_ref, also=None):
    def tail(first):
        stages = list(also(first)) if also is not None else []
        every = max(1, len(chains) // (len(stages) + 1))
        kv = {}
        for c, (q, qt, kv_id, bias_id) in enumerate(chains):
            ntile = (1 if first else 3) + qt
            if (kv_id, ntile) not in kv:
                start = 0 if first else pl.multiple_of((nqt * i - 2) * tk, 2 * tk)
                kv[(kv_id, ntile)] = kv_of(kv_id, start, ntile * tk)
            k, v = kv[(kv_id, ntile)]
            near = [bias_of(bias_id, d) for d in range(min(ntile, 2) - 1, -1, -1)]
            _softmax_first(_with_near_bias(_nt_dot(q, k), near), v, m_ref.at[c], acc_ref.at[c])
            if stages and (c + 1) % every == 0:
                stages.pop(0)()
        while stages:
            stages.pop(0)()

    @pl.when(i == 0)
    def _():
        tail(True)

    @pl.when(i >= 1)
    def _():
        tail(False)

    def far(j, carry):
        start = pl.multiple_of(j * 2 * tk, 2 * tk)
        kv = {}
        for c, (q, _, kv_id, _) in enumerate(chains):
            if kv_id not in kv:
                kv[kv_id] = kv_of(kv_id, start, 2 * tk)
            k, v = kv[kv_id]
            _softmax_step(_nt_dot(q, k), v, m_ref.at[c], acc_ref.at[c])
        return carry

    lax.fori_loop(0, (nqt * i - 2) // 2, far, 0)


def _normalized(acc_ref, c):
    return acc_ref[c, :, 0:LANES] * (1.0 / acc_ref[c, :, LANES:2 * LANES])


def _store_gated(o_ref, gate_ref, branches, tq, qt):
    lane = lax.broadcasted_iota(jnp.int32, (1, LANES), 1)
    rows = slice(qt * tq, (qt + 1) * tq)
    gates = gate_ref[0, rows, :]
    lane_full = lax.broadcasted_iota(jnp.int32, gates.shape, 1)
    for j in range(NSA_GQ):
        total = None
        for o_of, branch in branches:
            c0 = (0 * NSA_GQ + j) * 3 + branch
            c1 = (1 * NSA_GQ + j) * 3 + branch
            gexp = jnp.take_along_axis(gates, jnp.where(lane_full < HALF, c0, c1), axis=1)
            term = gexp * jnp.where(lane < HALF, o_of(0, j), o_of(1, j))
            total = term if total is None else total + term
        o_ref[0, rows, j * LANES:(j + 1) * LANES] = total.astype(BF16)


def _nsa_chain_index(qt, g, j):
    return (qt * NSA_KV + g) * NSA_GQ + j


def _cmp_stages(qi, tq, q_ref, kc_ref, vc_ref, ovt_ref, gate_ref, o_ref, nsel_ref):
    nblk = kc_ref.shape[1]
    lane = lax.broadcasted_iota(jnp.int32, (1, LANES), 1)
    state = {"outs": [], "imp_t": jnp.zeros((LANES, tq), F32), "dropped": []}

    def attend(g):
        n = lax.broadcasted_iota(jnp.int32, (1, nblk), 1)
        t4 = qi * tq + (lax.broadcasted_iota(jnp.int32, (NSA_GQ * tq, 1), 0) & (tq - 1))
        vis4 = (n * CMP_STRIDE + (CMP_LEN - 1)) <= t4
        has4 = jnp.where(t4 >= CMP_LEN - 1, 1.0, 0.0)
        qg = jnp.concatenate([q_ref[0, :, j * LANES:(j + 1) * LANES] * _half_mask(g) for j in range(NSA_GQ)], axis=0)
        s = jnp.where(vis4, _nt_dot(qg, kc_ref[0]), NEG)
        e = jnp.exp2(s - jnp.max(s, axis=-1, keepdims=True))
        p = e * (has4 / jnp.sum(e, axis=-1, keepdims=True))
        state["outs"].append(_dot(p.astype(BF16), vc_ref[0]))
        ps = p[0:tq] + p[tq:2 * tq] + p[2 * tq:3 * tq] + p[3 * tq:4 * tq]
        hi = ps.astype(BF16)
        r1 = ps - hi.astype(F32)
        mid = r1.astype(BF16)
        lo = (r1 - mid.astype(F32)).astype(BF16)
        state["imp_t"] = state["imp_t"] + (
            _nt_dot(ovt_ref[g], hi) + _nt_dot(ovt_ref[g], mid) + _nt_dot(ovt_ref[g], lo))

    def store_out():
        outs = state["outs"]
        _store_gated(o_ref, gate_ref, [(lambda g, j: outs[g][j * tq:(j + 1) * tq], 0)], tq, 0)

    def rank(base):
        tl = qi * tq + lax.broadcasted_iota(jnp.int32, (N_SLC, tq), 1)
        jrow = lax.broadcasted_iota(jnp.int32, (N_SLC, tq), 0)
        cur = tl >> 6
        valid = (jrow * SLC_BLOCK) <= tl
        forced = (jrow == 0) | (jrow == cur) | (jrow == cur - 1)
        sub = 8
        x = jnp.where(forced, jnp.inf, jnp.where(valid, state["imp_t"][base:base + N_SLC], -jnp.inf))
        others = [jnp.broadcast_to(x[jj:jj + 1], (sub, tq)) for jj in range(N_SLC)]
        chunks = []
        for r0 in range(0, N_SLC, sub):
            xr = x[r0:r0 + sub]
            jr = r0 + lax.broadcasted_iota(jnp.int32, (sub, tq), 0)
            cnt = jnp.zeros((sub, tq), F32)
            for jj in range(N_SLC):
                if jj < r0:
                    beats = others[jj] >= xr
                elif jj >= r0 + sub:
                    beats = others[jj] > xr
                else:
                    beats = (others[jj] > xr) | ((others[jj] == xr) & (jr > jj))
                cnt = cnt + jnp.where(beats, 1.0, 0.0)
            chunks.append(jnp.where(cnt < SLC_TOPK, 0.0, 1.0))
        state["dropped"].append(jnp.concatenate(chunks, axis=0))

    def store_selection():
        zero = jnp.zeros((N_SLC, tq), F32)
        dropped = state["dropped"]
        drop = jnp.concatenate([dropped[0], zero, dropped[1], zero], axis=0).T
        nsel_ref[0, :, 0:LANES] = jnp.where(lane >= HALF, drop, 0.0).astype(BF16)
        nsel_ref[0, :, LANES:2 * LANES] = jnp.where(lane < HALF, drop, 0.0).astype(BF16)

    return [functools.partial(attend, 0), functools.partial(attend, 1), store_out,
            functools.partial(rank, 0), functools.partial(rank, HALF), store_selection]


def _slc_win_kernel(bias_ref, q_ref, ks_ref, vs_ref, kw_ref, vw_ref, nsel_ref, gate_ref, o_ref,
                    m_ref, acc_ref, win_ref, *, tq, tk):
    i = pl.program_id(1)
    nqt = NQT_NSA
    chains = [(q, qt, g, g * NSA_GQ + j) for q, qt, g, j in _nsa_chains(q_ref, nsel_ref, tq, nqt)]
    win_chains = _nsa_chains(q_ref, None, tq, nqt)

    def kv_of(g, start, width):
        return (ks_ref[0, pl.ds(start, width), g * LANES:(g + 1) * LANES], vs_ref[0, pl.ds(start, width), :])

    def window(first):
        kv = {}

        def one(q, qt, g, j):
            ntile = min(1 + qt, 3) if first else 3
            if (qt, ntile) not in kv:
                start = max(qt - 2, 0) * tk if first else pl.multiple_of((nqt * i - 2 + qt) * tk, tk)
                kv[(qt, ntile)] = (kw_ref[0, pl.ds(start, ntile * tk), :], vw_ref[0, pl.ds(start, ntile * tk), :])
            k, v = kv[(qt, ntile)]
            near = [bias_ref[g * NSA_GQ + j, d] for d in range(min(ntile, 2) - 1, -1, -1)]
            if ntile == 3:
                near.insert(0, bias_ref[g * NSA_GQ + j, 2])
            win_ref[_nsa_chain_index(qt, g, j)] = _softmax_once(_with_near_bias(_nt_dot(q, k), near), v)

        def all_chains():
            for chain in win_chains:
                one(*chain)

        return [all_chains]

    _causal_flash(i, nqt, tq, tk, chains, kv_of, lambda h, d: bias_ref[h, d], m_ref, acc_ref, also=window)
    for qt in range(nqt):
        _store_gated(o_ref, gate_ref,
                     [(lambda g, j: _normalized(acc_ref, _nsa_chain_index(qt, g, j)), 1),
                      (lambda g, j: win_ref[_nsa_chain_index(qt, g, j)], 2)], tq, qt)


def _slc_win_attention(bias, main, nsel, gates):
    bsz, seq, _ = main.shape
    tq, tk, nqt = TQ, TK, NQT_NSA
    nchain = nqt * NSA_HEADS
    return pl.pallas_call(
        functools.partial(_slc_win_kernel, tq=tq, tk=tk),
        grid=(bsz, seq // (nqt * tq)),
        in_specs=[pl.BlockSpec((NSA_HEADS, 3, tq, tk), lambda b, i: (0, 0, 0, 0), pipeline_mode=pl.Buffered(1)),
                  pl.BlockSpec((1, nqt * tq, NSA_W), lambda b, i: (b, i, COL_NQ // NSA_W)),
                  pl.BlockSpec((1, seq, 2 * LANES), lambda b, i: (b, 0, COL_KSL // (2 * LANES))),
                  pl.BlockSpec((1, seq, LANES), lambda b, i: (b, 0, COL_VSL // LANES)),
                  pl.BlockSpec((1, seq, LANES), lambda b, i: (b, 0, COL_KW // LANES)),
                  pl.BlockSpec((1, seq, LANES), lambda b, i: (b, 0, COL_VW // LANES)),
                  pl.BlockSpec((1, nqt * tq, 2 * LANES), lambda b, i: (b, i, 0)),
                  pl.BlockSpec((1, nqt * tq, LANES), lambda b, i: (b, i, 0))],
        out_specs=pl.BlockSpec((1, nqt * tq, NSA_W), lambda b, i: (b, i, 0)),
        out_shape=jax.ShapeDtypeStruct((bsz, seq, NSA_W), BF16),
        scratch_shapes=[pltpu.VMEM((nchain, tq, LANES), F32),
                        pltpu.VMEM((nchain, tq, 2 * LANES), F32),
                        pltpu.VMEM((nchain, tq, LANES), F32)],
        compiler_params=_params(("parallel", "parallel")),
    )(bias, main, main, main, main, main, nsel, gates)


def _diff_cmp_kernel(bias_ref, lam_ref, q_ref, k_ref, v_ref, g_ref, nq_ref, kc_ref, vc_ref, ovt_ref, gate_ref,
                     o_ref, oc_ref, nsel_ref, m_ref, acc_ref, *, tq, tk):
    i = pl.program_id(1)
    nqt = NQT_DIFF

    chains = []
    for qt in range(nqt):
        for h in range(DIFF_HEADS):
            slab = q_ref[0, qt * tq:(qt + 1) * tq, h * LANES:(h + 1) * LANES]
            for half in range(2):
                chains.append((slab * _half_mask(half), qt, h, h))

    def kv_of(h, start, width):
        cols = slice(h * LANES, (h + 1) * LANES)
        return (k_ref[0, pl.ds(start, width), cols], v_ref[0, pl.ds(start, width), cols])

    def compressed(first):
        return _cmp_stages(i, nqt * tq, nq_ref, kc_ref, vc_ref, ovt_ref, gate_ref, oc_ref, nsel_ref)

    _causal_flash(i, nqt, tq, tk, chains, kv_of, lambda h, d: bias_ref[h, d], m_ref, acc_ref, also=compressed)

    lm = lam_ref[...]
    lam = (jnp.exp(jnp.sum(lm[0:1] * lm[1:2], axis=-1, keepdims=True))
           - jnp.exp(jnp.sum(lm[2:3] * lm[3:4], axis=-1, keepdims=True)) + LAMBDA_INIT)
    for qt in range(nqt):
        for h in range(DIFF_HEADS):
            c = (qt * DIFF_HEADS + h) * 2
            a = _normalized(acc_ref, c) - lam * _normalized(acc_ref, c + 1)
            o_ref[0, qt * tq:(qt + 1) * tq, h * LANES:(h + 1) * LANES] = (
                _rms(a, g_ref[...]) * (1.0 - LAMBDA_INIT)).astype(BF16)


def _diff_cmp_attention(bias, lam_rows, main, subln, kc, vc, ovt, gates):
    bsz, seq, _ = main.shape
    nblk = kc.shape[1]
    tq, tk, nqt = TQ, TK, NQT_DIFF
    nchain = nqt * DIFF_HEADS * 2
    step = lambda b, i: (b, i, 0)
    return pl.pallas_call(
        functools.partial(_diff_cmp_kernel, tq=tq, tk=tk),
        grid=(bsz, seq // (nqt * tq)),
        in_specs=[pl.BlockSpec((DIFF_HEADS, 3, tq, tk), lambda b, i: (NSA_HEADS // DIFF_HEADS, 0, 0, 0),
                               pipeline_mode=pl.Buffered(1)),
                  pl.BlockSpec((8, LANES), lambda b, i: (0, 0)),
                  pl.BlockSpec((1, nqt * tq, DIFF_W), lambda b, i: (b, i, COL_DQ // DIFF_W)),
                  pl.BlockSpec((1, seq, DIFF_W), lambda b, i: (b, 0, COL_DK // DIFF_W)),
                  pl.BlockSpec((1, seq, DIFF_W), lambda b, i: (b, 0, COL_DV // DIFF_W)),
                  pl.BlockSpec((1, LANES), lambda b, i: (0, 0)),
                  pl.BlockSpec((1, nqt * tq, NSA_W), lambda b, i: (b, i, COL_NQ // NSA_W)),
                  pl.BlockSpec((1, nblk, LANES), lambda b, i: (b, 0, 0)),
                  pl.BlockSpec((1, nblk, LANES), lambda b, i: (b, 0, 0)),
                  pl.BlockSpec((2, LANES, nblk), lambda b, i: (0, 0, 0)),
                  pl.BlockSpec((1, nqt * tq, LANES), step)],
        out_specs=[pl.BlockSpec((1, nqt * tq, DIFF_W), step),
                   pl.BlockSpec((1, nqt * tq, NSA_W), step),
                   pl.BlockSpec((1, nqt * tq, 2 * LANES), step)],
        out_shape=[jax.ShapeDtypeStruct((bsz, seq, DIFF_W), BF16),
                   jax.ShapeDtypeStruct((bsz, seq, NSA_W), BF16),
                   jax.ShapeDtypeStruct((bsz, seq, 2 * LANES), BF16)],
        scratch_shapes=[pltpu.VMEM((nchain, tq, LANES), F32),
                        pltpu.VMEM((nchain, tq, 2 * LANES), F32)],
        compiler_params=_params(("parallel", "parallel")),
    )(bias, lam_rows, main, main, main, subln, main, kc, vc, ovt, gates)


FF_CHUNK = 256


def _tail_kernel(x_ref, od_ref, oc_ref, osw_ref, gn_ref, woraw_ref, g2_ref,
                 wg_ref, wu_ref, wd_ref, gf_ref, o_ref, acc_ref, wo_ref):
    @pl.when(pl.program_id(0) == 0)
    def _():
        wo_ref[0:DIFF_W, :] = woraw_ref[0:DIFF_W, :].astype(BF16)
        for j in range(NSA_GQ):
            for g in range(NSA_KV):
                dst = DIFF_W + (j * NSA_KV + g) * NSA_HD
                src = DIFF_W + (g * NSA_GQ + j) * NSA_HD
                wo_ref[dst:dst + NSA_HD, :] = woraw_ref[src:src + NSA_HD, :].astype(BF16)

    o_nsa = oc_ref[...].astype(F32) + osw_ref[...].astype(F32)
    o_nsa = _rms(o_nsa, gn_ref[...]).astype(BF16)
    attn = _dot(od_ref[...], wo_ref[0:DIFF_W, :]) + _dot(o_nsa, wo_ref[DIFF_W:DIFF_W + NSA_W, :])
    x1 = x_ref[...] + attn
    acc_ref[...] = x1
    h = _rms(x1, g2_ref[...]).astype(BF16)
    for c in range(0, D_FF, FF_CHUNK):
        gate = _dot(h, wg_ref[:, c:c + FF_CHUNK])
        up = _dot(h, wu_ref[:, c:c + FF_CHUNK])
        act = (gate * (1.0 / (1.0 + jnp.exp(-gate))) * up).astype(BF16)
        acc_ref[...] += _dot(act, wd_ref[c:c + FF_CHUNK, :])
    o_ref[...] = _rms(acc_ref[...], gf_ref[...])


def _tail(x2, od, oc, osw, gn, w_out, g2, wg, wu, wd, gf):
    n = x2.shape[0]
    row = lambda i: (i, 0)
    fixed = lambda i: (0, 0)
    once = pl.Buffered(1)
    return pl.pallas_call(
        _tail_kernel,
        grid=(n // TM,),
        in_specs=[pl.BlockSpec((TM, D_MODEL), row),
                  pl.BlockSpec((TM, DIFF_W), row),
                  pl.BlockSpec((TM, NSA_W), row),
                  pl.BlockSpec((TM, NSA_W), row),
                  pl.BlockSpec((1, NSA_W), fixed),
                  pl.BlockSpec((DIFF_W + NSA_W, D_MODEL), fixed, pipeline_mode=once),
                  pl.BlockSpec((1, D_MODEL), fixed),
                  pl.BlockSpec((D_MODEL, D_FF), fixed, pipeline_mode=once),
                  pl.BlockSpec((D_MODEL, D_FF), fixed, pipeline_mode=once),
                  pl.BlockSpec((D_FF, D_MODEL), fixed, pipeline_mode=once),
                  pl.BlockSpec((1, D_MODEL), fixed)],
        out_specs=pl.BlockSpec((TM, D_MODEL), row),
        out_shape=jax.ShapeDtypeStruct((n, D_MODEL), F32),
        scratch_shapes=[pltpu.VMEM((TM, D_MODEL), F32),
                        pltpu.VMEM((DIFF_W + NSA_W, D_MODEL), BF16)],
        compiler_params=_params(("arbitrary",)),
    )(x2, od, oc, osw, gn, w_out, g2, wg, wu, wd, gf)


def _permute_heads(a, axis):
    shape = a.shape
    split = shape[:axis] + (NSA_KV, NSA_GQ, NSA_HD) + shape[axis + 1:]
    return jnp.swapaxes(a.reshape(split), axis, axis + 1).reshape(shape)


def _overlap_matrices(nblk):
    n_cmp = nblk - 1
    nslc = nblk * CMP_STRIDE // SLC_BLOCK
    assert nslc == N_SLC
    cmp_start = np.arange(n_cmp) * CMP_STRIDE
    cmp_end = cmp_start + CMP_LEN - 1
    slc_start = np.arange(nslc) * SLC_BLOCK
    overlap = ((cmp_start[:, None] < slc_start[None, :] + SLC_BLOCK) & (cmp_end[:, None] >= slc_start[None, :]))
    ovt = np.zeros((NSA_KV, LANES, nblk), np.float32)
    for g, base in enumerate((HALF, 0)):
        ovt[g, base:base + nslc, :n_cmp] = overlap.T
    return ovt


def _compress_weights(pos, w1, w2):
    half = CMP_LEN // 2
    shape = (2, half, NSA_KV, NSA_HD, CMP_HIDDEN)
    w1rep = jnp.broadcast_to(w1.reshape(2, half, 1, NSA_HD, CMP_HIDDEN), shape).reshape(2, -1, CMP_HIDDEN)
    w2big = jnp.einsum('jd,pg->pjgd', w2, jnp.eye(NSA_KV, dtype=F32)).reshape(NSA_KV * CMP_HIDDEN, NSA_KV * NSA_HD)
    posr = jnp.broadcast_to(pos.reshape(2, half, 1, NSA_HD), (2, half, NSA_KV, NSA_HD)).reshape(2, -1)
    return posr, w1rep.astype(BF16), w2big.astype(BF16)


def kernel(x, norm1, w_in, lambda_q1, lambda_k1, lambda_q2, lambda_k2, diff_subln, cmp_pos_k, cmp_w1_k, cmp_w2_k,
           cmp_pos_v, cmp_w1_v, cmp_w2_v, nsa_norm, w_out, norm2, w_gate, w_up, w_down, rel_bias, final_norm):
    bsz, seq, dm = x.shape
    assert dm == D_MODEL and seq % (max(NQT_DIFF, NQT_NSA) * TQ) == 0 and seq % TM == 0 and seq == N_SLC * SLC_BLOCK
    assert norm1.shape[0] == 1, "single layer"
    x2 = x.reshape(bsz * seq, dm)

    gn_p = _permute_heads(nsa_norm[0].reshape(1, NSA_W), 1)
    posk, w1k, w2k = _compress_weights(cmp_pos_k[0], cmp_w1_k[0], cmp_w2_k[0])
    posv, w1v, w2v = _compress_weights(cmp_pos_v[0], cmp_w1_v[0], cmp_w2_v[0])
    pos = jnp.concatenate([posk, posv, jnp.zeros((4, posk.shape[1]), F32)], axis=0)
    lam_rows = jnp.zeros((8, LANES), F32).at[0:4, 0:DIFF_HD].set(
        jnp.stack([lambda_q1[0], lambda_k1[0], lambda_q2[0], lambda_k2[0]]).astype(F32))

    main, kc_grp, vc_grp, gates = _inproj(x2, norm1[0].reshape(1, dm), w_in[0], seq)
    main = main.reshape(bsz, seq, N_MAIN)

    nrow = seq // CMP_STRIDE
    kc, vc = _compress(kc_grp.reshape(bsz, nrow, CMP_STRIDE * LANES), vc_grp.reshape(bsz, nrow, CMP_STRIDE * LANES),
                       pos, w1k, w2k, w1v, w2v)
    ovt = jnp.asarray(_overlap_matrices(nrow), BF16)
    gates = gates.reshape(bsz, seq, LANES)
    bias = _bias_tiles(rel_bias)
    o_diff, o_cmp, nsel = _diff_cmp_attention(bias, lam_rows, main, diff_subln[0].reshape(1, DIFF_VD),
                                              kc, vc, ovt, gates)
    o_sw = _slc_win_attention(bias, main, nsel, gates)

    n = bsz * seq
    out = _tail(x2, o_diff.reshape(n, DIFF_W), o_cmp.reshape(n, NSA_W), o_sw.reshape(n, NSA_W),
                gn_p, w_out[0], norm2[0].reshape(1, dm),
                w_gate[0].astype(BF16), w_up[0].astype(BF16), w_down[0].astype(BF16), final_norm.reshape(1, dm))
    return out.reshape(bsz, seq, dm)
```

```python
import functools
import math

import numpy as np
import jax
import jax.numpy as jnp
from jax import lax
from jax.experimental import pallas as pl
from jax.experimental.pallas import tpu as pltpu

F32 = jnp.float32
BF16 = jnp.bfloat16

D_MODEL = 1024
DIFF_HEADS = 4
DIFF_HD = 64
DIFF_VD = 128
DIFF_W = 512
NSA_HEADS = 8
NSA_KV = 2
NSA_GQ = 4
NSA_HD = 64
NSA_W = 512
CMP_LEN = 32
CMP_STRIDE = 16
CMP_HIDDEN = 256
SLC_BLOCK = 64
SLC_TOPK = 16
N_SLC = 32
WINDOW = 512
N_BUCKETS = 32
MAX_DISTANCE = 128
D_FF = 2816
N_IN = 2840
NEG = -1e30
EPS = 1e-6
LAMBDA_INIT = 0.8 - 0.6 * math.exp(-0.3 * 0)
LOG2E = math.log2(math.e)
QSCALE = NSA_HD ** -0.5 * LOG2E

LANES = 128
HALF = LANES // 2
MXU_N = 256
COL_DQ, COL_DK, COL_DV, COL_NQ, COL_KSL, COL_VSL, COL_KW, COL_VW, N_MAIN = (
    0, 512, 1024, 1536, 2048, 2304, 2432, 2560, 2688)
N_PROJ = N_MAIN + 3 * LANES
VMEM_LIMIT = 48 * 1024 * 1024

TQ = 256
TK = 256
TM = 512
NQT_DIFF = 4
NQT_NSA = 2


def _t5_thresholds():
    d = np.arange(0, 4 * MAX_DISTANCE)
    max_exact = N_BUCKETS // 2
    val = (np.log(np.maximum(d, 1).astype(np.float32) / np.float32(max_exact))
           / np.float32(math.log(MAX_DISTANCE / max_exact)) * np.float32(N_BUCKETS - max_exact))
    large = np.minimum(max_exact + val.astype(np.int32), N_BUCKETS - 1)
    bucket = np.where(d < max_exact, d, large)
    assert np.all(np.diff(bucket) >= 0) and bucket[-1] == N_BUCKETS - 1
    return [int(np.argmax(bucket >= k)) for k in range(N_BUCKETS)]


T5_THRESH = _t5_thresholds()
assert T5_THRESH[-1] <= TK + 1
assert WINDOW == 2 * TK and TQ == TK


def _nt_dot(a, b):
    return lax.dot_general(a, b, (((1,), (1,)), ((), ())), preferred_element_type=F32)


def _dot(a, b):
    return jnp.dot(a, b, preferred_element_type=F32)


def _rms(x, g):
    ms = jnp.mean(x * x, axis=-1, keepdims=True)
    return x * lax.rsqrt(ms + EPS) * g


def _params(sem):
    return pltpu.CompilerParams(dimension_semantics=sem, vmem_limit_bytes=VMEM_LIMIT)


def _store_grouped(out_ref, tok_ref, y):
    tok_ref[...] = y
    nrow = out_ref.shape[0]
    for l in range(CMP_STRIDE):
        out_ref[:, l * LANES:(l + 1) * LANES] = tok_ref[pl.ds(l, nrow, stride=CMP_STRIDE), :].astype(BF16)


def _arrange_w_in(w_ref, wp_ref):
    src_dk, src_nq, src_kc, src_ksl, src_vsl, src_gt = 512, 1536, 2048, 2304, 2432, 2816
    halves = []
    halves += [(c, QSCALE) for c in range(0, src_dk, HALF)]
    halves += [(c, None) for c in range(src_dk, src_nq, HALF)]
    halves += [(src_nq + (g * NSA_GQ + j) * NSA_HD, QSCALE) for j in range(NSA_GQ) for g in range(NSA_KV)]
    halves += [(src_ksl, None), (None, None), (None, None), (src_ksl + HALF, None)]
    halves += [(c, None) for c in range(src_vsl, src_gt, HALF)]
    halves += [(c, None) for c in range(src_kc, src_ksl, HALF)]
    assert len(halves) * HALF == N_PROJ - LANES
    rows = 256
    for r0 in range(0, D_MODEL, rows):
        rs = slice(r0, r0 + rows)

        def load(entry):
            col, scale = entry
            if col is None:
                return jnp.zeros((rows, HALF), F32)
            val = w_ref[rs, col:col + HALF]
            return val if scale is None else val * scale

        for t in range(0, len(halves), 2):
            tile = jnp.concatenate([load(halves[t]), load(halves[t + 1])], axis=1)
            wp_ref[rs, (t // 2) * LANES:(t // 2 + 1) * LANES] = tile.astype(BF16)
        ngate = N_IN - src_gt
        gates = jnp.concatenate([w_ref[rs, src_gt:N_IN], jnp.zeros((rows, LANES - ngate), F32)], axis=1)
        wp_ref[rs, N_PROJ - LANES:N_PROJ] = gates.astype(BF16)


def _inproj_kernel(x_ref, g_ref, wraw_ref, main_ref, kc_ref, vc_ref, gate_ref, tok_ref, w_ref, *, seq):
    tm = x_ref.shape[0]

    @pl.when(pl.program_id(0) == 0)
    def _():
        _arrange_w_in(wraw_ref, w_ref)

    h = _rms(x_ref[...], g_ref[...]).astype(BF16)
    for c in range(0, N_PROJ, MXU_N):
        y = _dot(h, w_ref[:, c:c + MXU_N])
        if c == COL_KSL:
            t0 = lax.rem(pl.program_id(0) * tm, seq)
            blk = (t0 + lax.broadcasted_iota(jnp.int32, (tm, 2 * LANES), 0)) >> 6
            lane = lax.broadcasted_iota(jnp.int32, (tm, 2 * LANES), 1)
            slot = jnp.where(lane < LANES, lane - HALF, lane - LANES)
            y = y + jnp.where((slot == blk) & (slot >= 0) & (slot < N_SLC), NEG, 0.0)
        if c + MXU_N <= N_MAIN:
            main_ref[:, c:c + MXU_N] = y.astype(BF16)
        elif c == COL_VW:
            main_ref[:, c:c + LANES] = y[:, 0:LANES].astype(BF16)
            _store_grouped(kc_ref, tok_ref, y[:, LANES:MXU_N])
        else:
            _store_grouped(vc_ref, tok_ref, y[:, 0:LANES])
            gate_ref[...] = 1.0 / (1.0 + jnp.exp(-y[:, LANES:MXU_N]))


def _inproj(x2, g1, w_in, seq):
    n = x2.shape[0]
    return pl.pallas_call(
        functools.partial(_inproj_kernel, seq=seq),
        grid=(n // TM,),
        in_specs=[pl.BlockSpec((TM, D_MODEL), lambda i: (i, 0)),
                  pl.BlockSpec((1, D_MODEL), lambda i: (0, 0)),
                  pl.BlockSpec((D_MODEL, N_IN), lambda i: (0, 0), pipeline_mode=pl.Buffered(1))],
        out_specs=[pl.BlockSpec((TM, N_MAIN), lambda i: (i, 0)),
                   pl.BlockSpec((TM // CMP_STRIDE, CMP_STRIDE * LANES), lambda i: (i, 0)),
                   pl.BlockSpec((TM // CMP_STRIDE, CMP_STRIDE * LANES), lambda i: (i, 0)),
                   pl.BlockSpec((TM, LANES), lambda i: (i, 0))],
        out_shape=[jax.ShapeDtypeStruct((n, N_MAIN), BF16),
                   jax.ShapeDtypeStruct((n // CMP_STRIDE, CMP_STRIDE * LANES), BF16),
                   jax.ShapeDtypeStruct((n // CMP_STRIDE, CMP_STRIDE * LANES), BF16),
                   jax.ShapeDtypeStruct((n, LANES), F32)],
        scratch_shapes=[pltpu.VMEM((TM, LANES), F32),
                        pltpu.VMEM((D_MODEL, N_PROJ), BF16)],
        compiler_params=_params(("arbitrary",)),
    )(x2, g1, w_in)


def _gelu_tanh(x):
    return x * (0.5 * (1.0 + jnp.tanh(math.sqrt(2.0 / math.pi) * (x + 0.044715 * (x * x * x)))))


def _compress_kernel(rk_ref, rv_ref, pos_ref, w1k_ref, w2k_ref, w1v_ref, w2v_ref, kc_ref, vc_ref):
    width = rk_ref.shape[2]
    lane = lax.broadcasted_iota(jnp.int32, (1, width), 1)
    in_group = [jnp.where(((lane >> 6) & 1) == g, 1.0, 0.0) for g in range(NSA_KV)]

    def comp(r_ref, p_top, p_bot, w1_ref, w2_ref):
        r = r_ref[0].astype(F32)
        hid = []
        for g in range(NSA_KV):
            top = _dot(((r + p_top) * in_group[g]).astype(BF16), w1_ref[0])
            bot = _dot(((r + p_bot) * in_group[g]).astype(BF16), w1_ref[1])
            nrow = top.shape[0]
            hid.append(top + pltpu.roll(bot, nrow - 1, 0))
        act = _gelu_tanh(jnp.concatenate(hid, axis=1)).astype(BF16)
        return _dot(act, w2_ref[...]).astype(BF16)

    kc_ref[0] = comp(rk_ref, pos_ref[0:1, :], pos_ref[1:2, :], w1k_ref, w2k_ref)
    vc_ref[0] = comp(rv_ref, pos_ref[2:3, :], pos_ref[3:4, :], w1v_ref, w2v_ref)


def _compress(rk, rv, pos, w1k, w2k, w1v, w2v):
    bsz, nrow, width = rk.shape
    hid2 = 2 * CMP_HIDDEN
    return pl.pallas_call(
        _compress_kernel,
        grid=(bsz,),
        in_specs=[pl.BlockSpec((1, nrow, width), lambda b: (b, 0, 0)),
                  pl.BlockSpec((1, nrow, width), lambda b: (b, 0, 0)),
                  pl.BlockSpec((8, width), lambda b: (0, 0)),
                  pl.BlockSpec((2, width, CMP_HIDDEN), lambda b: (0, 0, 0)),
                  pl.BlockSpec((hid2, LANES), lambda b: (0, 0)),
                  pl.BlockSpec((2, width, CMP_HIDDEN), lambda b: (0, 0, 0)),
                  pl.BlockSpec((hid2, LANES), lambda b: (0, 0))],
        out_specs=[pl.BlockSpec((1, nrow, LANES), lambda b: (b, 0, 0)),
                   pl.BlockSpec((1, nrow, LANES), lambda b: (b, 0, 0))],
        out_shape=[jax.ShapeDtypeStruct((bsz, nrow, LANES), BF16),
                   jax.ShapeDtypeStruct((bsz, nrow, LANES), BF16)],
        compiler_params=_params(("parallel",)),
    )(rk, rv, pos, w1k, w2k, w1v, w2v)


def _half_mask(g):
    lane = lax.broadcasted_iota(jnp.int32, (1, LANES), 1)
    return jnp.where((lane >= g * HALF) & (lane < (g + 1) * HALF), 1.0, 0.0).astype(BF16)


def _nsa_chains(q_ref, extra_ref, tq, nqt):
    chains = []
    for qt in range(nqt):
        rows = slice(qt * tq, (qt + 1) * tq)
        for g in range(NSA_KV):
            keep = _half_mask(g)
            for j in range(NSA_GQ):
                q = q_ref[0, rows, j * LANES:(j + 1) * LANES] * keep
                if extra_ref is not None:
                    q = q + extra_ref[0, rows, g * LANES:(g + 1) * LANES]
                chains.append((q, qt, g, j))
    return chains


def _bias_kernel(rel_ref, o_ref, *, tq, tk):
    row = pl.program_id(0)
    col = jnp.where(row < NSA_HEADS, row + DIFF_HEADS, row - NSA_HEADS)
    r = lax.broadcasted_iota(jnp.int32, (tq, tk), 0)
    c = lax.broadcasted_iota(jnp.int32, (tq, tk), 1)
    far = rel_ref[N_BUCKETS - 1, col]
    for d in range(2):
        dist = r - c + d * tq
        val = jnp.zeros((tq, tk), F32) + rel_ref[0, col]
        for k in range(1, N_BUCKETS):
            val = jnp.where(dist >= T5_THRESH[k], rel_ref[k, col], val)
        val = (val - far) * LOG2E
        if d == 0:
            val = jnp.where(dist >= 0, val, NEG)
        o_ref[0, d] = val
    o_ref[0, 2] = jnp.where(r < c, 0.0, NEG)


def _bias_tiles(rel_bias):
    nhead = rel_bias.shape[1]
    return pl.pallas_call(
        functools.partial(_bias_kernel, tq=TQ, tk=TK),
        grid=(nhead,),
        in_specs=[pl.BlockSpec(memory_space=pltpu.SMEM)],
        out_specs=pl.BlockSpec((1, 3, TQ, TK), lambda h: (h, 0, 0, 0)),
        out_shape=jax.ShapeDtypeStruct((nhead, 3, TQ, TK), F32),
        compiler_params=_params(("parallel",)),
    )(rel_bias)


def _with_near_bias(s, near):
    add = near[0] if len(near) == 1 else jnp.concatenate(near, axis=1)
    width, nw = s.shape[1], add.shape[1]
    if nw == width:
        return s + add
    return jnp.concatenate([s[:, 0:width - nw], s[:, width - nw:width] + add], axis=1)


def _softmax_step(s, v, m_ref, acc_ref):
    tk = s.shape[1]
    m_prev = m_ref[...]
    m_new = jnp.maximum(m_prev, jnp.max(s, axis=-1, keepdims=True))
    alpha = jnp.exp2(m_prev - m_new)
    p = jnp.exp2(s - jnp.concatenate([m_new] * (tk // LANES), axis=1))
    vaug = jnp.concatenate([v, jnp.ones((tk, LANES), BF16)], axis=1)
    acc_ref[...] = jnp.concatenate([alpha, alpha], axis=1) * acc_ref[...] + _dot(p.astype(BF16), vaug)
    m_ref[...] = m_new


def _softmax_first(s, v, m_ref, acc_ref):
    rows, tk = s.shape
    m = jnp.broadcast_to(jnp.max(s, axis=-1, keepdims=True), (rows, LANES))
    p = jnp.exp2(s - jnp.concatenate([m] * (tk // LANES), axis=1))
    vaug = jnp.concatenate([v, jnp.ones((tk, LANES), BF16)], axis=1)
    acc_ref[...] = _dot(p.astype(BF16), vaug)
    m_ref[...] = m


def _softmax_once(s, v):
    rows, tk = s.shape
    m = jnp.broadcast_to(jnp.max(s, axis=-1, keepdims=True), (rows, LANES))
    p = jnp.exp2(s - jnp.concatenate([m] * (tk // LANES), axis=1))
    vaug = jnp.concatenate([v, jnp.ones((tk, LANES), BF16)], axis=1)
    pv = _dot(p.astype(BF16), vaug)
    return pv[:, 0:LANES] * (1.0 / pv[:, LANES:2 * LANES])


def _causal_flash(i, nqt, tq, tk, chains, kv_of, bias_of, m_ref, acc_ref, also=None, far_quads=False):
    def tail(first):
        stages = list(also(first)) if also is not None else []
        every = max(1, len(chains) // (len(stages) + 1))
        kv = {}
        for c, (q, qt, kv_id, bias_id) in enumerate(chains):
            ntile = (1 if first else 3) + qt
            if (kv_id, ntile) not in kv:
                start = 0 if first else pl.multiple_of((nqt * i - 2) * tk, 2 * tk)
                kv[(kv_id, ntile)] = kv_of(kv_id, start, ntile * tk)
            k, v = kv[(kv_id, ntile)]
            near = [bias_of(bias_id, d) for d in range(min(ntile, 2) - 1, -1, -1)]
            _softmax_first(_with_near_bias(_nt_dot(q, k), near), v, m_ref.at[c], acc_ref.at[c])
            if stages and (c + 1) % every == 0:
                stages.pop(0)()
        while stages:
            stages.pop(0)()

    @pl.when(i == 0)
    def _():
        tail(True)

    @pl.when(i >= 1)
    def _():
        tail(False)

    def far(first_tile, ntile):
        start = pl.multiple_of(first_tile * tk, 2 * tk)
        kv = {}
        for c, (q, _, kv_id, _) in enumerate(chains):
            if kv_id not in kv:
                kv[kv_id] = kv_of(kv_id, start, ntile * tk)
            k, v = kv[kv_id]
            _softmax_step(_nt_dot(q, k), v, m_ref.at[c], acc_ref.at[c])

    npair = jnp.maximum((nqt * i - 2) // 2, 0)
    if far_quads:
        def quad(j, carry):
            far(4 * j, 4)
            return carry

        lax.fori_loop(0, npair >> 1, quad, 0)

        @pl.when((npair & 1) == 1)
        def _():
            far(2 * (npair - 1), 2)
    else:
        def pair(j, carry):
            far(2 * j, 2)
            return carry

        lax.fori_loop(0, npair, pair, 0)


def _normalized(acc_ref, c):
    return acc_ref[c, :, 0:LANES] * (1.0 / acc_ref[c, :, LANES:2 * LANES])


def _store_gated(o_ref, gate_ref, branches, tq, qt):
    lane = lax.broadcasted_iota(jnp.int32, (1, LANES), 1)
    rows = slice(qt * tq, (qt + 1) * tq)
    gates = gate_ref[0, rows, :]
    lane_full = lax.broadcasted_iota(jnp.int32, gates.shape, 1)
    for j in range(NSA_GQ):
        total = None
        for o_of, branch in branches:
            c0 = (0 * NSA_GQ + j) * 3 + branch
            c1 = (1 * NSA_GQ + j) * 3 + branch
            gexp = jnp.take_along_axis(gates, jnp.where(lane_full < HALF, c0, c1), axis=1)
            term = gexp * jnp.where(lane < HALF, o_of(0, j), o_of(1, j))
            total = term if total is None else total + term
        o_ref[0, rows, j * LANES:(j + 1) * LANES] = total.astype(BF16)


def _nsa_chain_index(qt, g, j):
    return (qt * NSA_KV + g) * NSA_GQ + j


def _cmp_stages(qi, tq, q_ref, kc_ref, vc_ref, ovt_ref, gate_ref, o_ref, nsel_ref):
    nblk = kc_ref.shape[1]
    lane = lax.broadcasted_iota(jnp.int32, (1, LANES), 1)
    state = {"outs": [], "imp_t": jnp.zeros((LANES, tq), F32), "dropped": []}

    def attend(g):
        n = lax.broadcasted_iota(jnp.int32, (1, nblk), 1)
        t4 = qi * tq + (lax.broadcasted_iota(jnp.int32, (NSA_GQ * tq, 1), 0) & (tq - 1))
        vis4 = (n * CMP_STRIDE + (CMP_LEN - 1)) <= t4
        has4 = jnp.where(t4 >= CMP_LEN - 1, 1.0, 0.0)
        qg = jnp.concatenate([q_ref[0, :, j * LANES:(j + 1) * LANES] * _half_mask(g) for j in range(NSA_GQ)], axis=0)
        s = jnp.where(vis4, _nt_dot(qg, kc_ref[0]), NEG)
        e = jnp.exp2(s - jnp.max(s, axis=-1, keepdims=True))
        p = e * (has4 / jnp.sum(e, axis=-1, keepdims=True))
        state["outs"].append(_dot(p.astype(BF16), vc_ref[0]))
        ps = p[0:tq] + p[tq:2 * tq] + p[2 * tq:3 * tq] + p[3 * tq:4 * tq]
        hi = ps.astype(BF16)
        r1 = ps - hi.astype(F32)
        mid = r1.astype(BF16)
        lo = (r1 - mid.astype(F32)).astype(BF16)
        state["imp_t"] = state["imp_t"] + (
            _nt_dot(ovt_ref[g], hi) + _nt_dot(ovt_ref[g], mid) + _nt_dot(ovt_ref[g], lo))

    def store_out():
        outs = state["outs"]
        _store_gated(o_ref, gate_ref, [(lambda g, j: outs[g][j * tq:(j + 1) * tq], 0)], tq, 0)

    def rank(base):
        tl = qi * tq + lax.broadcasted_iota(jnp.int32, (N_SLC, tq), 1)
        jrow = lax.broadcasted_iota(jnp.int32, (N_SLC, tq), 0)
        cur = tl >> 6
        valid = (jrow * SLC_BLOCK) <= tl
        forced = (jrow == 0) | (jrow == cur) | (jrow == cur - 1)
        sub = 8
        x = jnp.where(forced, jnp.inf, jnp.where(valid, state["imp_t"][base:base + N_SLC], -jnp.inf))
        others = [jnp.broadcast_to(x[jj:jj + 1], (sub, tq)) for jj in range(N_SLC)]
        chunks = []
        for r0 in range(0, N_SLC, sub):
            xr = x[r0:r0 + sub]
            jr = r0 + lax.broadcasted_iota(jnp.int32, (sub, tq), 0)
            cnt = jnp.zeros((sub, tq), F32)
            for jj in range(N_SLC):
                if jj < r0:
                    beats = others[jj] >= xr
                elif jj >= r0 + sub:
                    beats = others[jj] > xr
                else:
                    beats = (others[jj] > xr) | ((others[jj] == xr) & (jr > jj))
                cnt = cnt + jnp.where(beats, 1.0, 0.0)
            chunks.append(jnp.where(cnt < SLC_TOPK, 0.0, 1.0))
        state["dropped"].append(jnp.concatenate(chunks, axis=0))

    def store_selection():
        zero = jnp.zeros((N_SLC, tq), F32)
        dropped = state["dropped"]
        drop = jnp.concatenate([dropped[0], zero, dropped[1], zero], axis=0).T
        nsel_ref[0, :, 0:LANES] = jnp.where(lane >= HALF, drop, 0.0).astype(BF16)
        nsel_ref[0, :, LANES:2 * LANES] = jnp.where(lane < HALF, drop, 0.0).astype(BF16)

    return [functools.partial(attend, 0), functools.partial(attend, 1), store_out,
            functools.partial(rank, 0), functools.partial(rank, HALF), store_selection]


def _slc_win_kernel(bias_ref, q_ref, ks_ref, vs_ref, kw_ref, vw_ref, nsel_ref, gate_ref, o_ref,
                    m_ref, acc_ref, win_ref, *, tq, tk):
    i = pl.program_id(1)
    nqt = NQT_NSA
    chains = [(q, qt, g, g * NSA_GQ + j) for q, qt, g, j in _nsa_chains(q_ref, nsel_ref, tq, nqt)]
    win_chains = _nsa_chains(q_ref, None, tq, nqt)

    def kv_of(g, start, width):
        return (ks_ref[0, pl.ds(start, width), g * LANES:(g + 1) * LANES], vs_ref[0, pl.ds(start, width), :])

    def window(first):
        kv = {}

        def one(q, qt, g, j):
            ntile = min(1 + qt, 3) if first else 3
            if (qt, ntile) not in kv:
                start = max(qt - 2, 0) * tk if first else pl.multiple_of((nqt * i - 2 + qt) * tk, tk)
                kv[(qt, ntile)] = (kw_ref[0, pl.ds(start, ntile * tk), :], vw_ref[0, pl.ds(start, ntile * tk), :])
            k, v = kv[(qt, ntile)]
            near = [bias_ref[g * NSA_GQ + j, d] for d in range(min(ntile, 2) - 1, -1, -1)]
            if ntile == 3:
                near.insert(0, bias_ref[g * NSA_GQ + j, 2])
            win_ref[_nsa_chain_index(qt, g, j)] = _softmax_once(_with_near_bias(_nt_dot(q, k), near), v)

        def all_chains():
            for chain in win_chains:
                one(*chain)

        return [all_chains]

    _causal_flash(i, nqt, tq, tk, chains, kv_of, lambda h, d: bias_ref[h, d], m_ref, acc_ref, also=window,
                  far_quads=True)
    for qt in range(nqt):
        _store_gated(o_ref, gate_ref,
                     [(lambda g, j: _normalized(acc_ref, _nsa_chain_index(qt, g, j)), 1),
                      (lambda g, j: win_ref[_nsa_chain_index(qt, g, j)], 2)], tq, qt)


def _slc_win_attention(bias, main, nsel, gates):
    bsz, seq, _ = main.shape
    tq, tk, nqt = TQ, TK, NQT_NSA
    nchain = nqt * NSA_HEADS
    return pl.pallas_call(
        functools.partial(_slc_win_kernel, tq=tq, tk=tk),
        grid=(bsz, seq // (nqt * tq)),
        in_specs=[pl.BlockSpec((NSA_HEADS, 3, tq, tk), lambda b, i: (0, 0, 0, 0), pipeline_mode=pl.Buffered(1)),
                  pl.BlockSpec((1, nqt * tq, NSA_W), lambda b, i: (b, i, COL_NQ // NSA_W)),
                  pl.BlockSpec((1, seq, 2 * LANES), lambda b, i: (b, 0, COL_KSL // (2 * LANES))),
                  pl.BlockSpec((1, seq, LANES), lambda b, i: (b, 0, COL_VSL // LANES)),
                  pl.BlockSpec((1, seq, LANES), lambda b, i: (b, 0, COL_KW // LANES)),
                  pl.BlockSpec((1, seq, LANES), lambda b, i: (b, 0, COL_VW // LANES)),
                  pl.BlockSpec((1, nqt * tq, 2 * LANES), lambda b, i: (b, i, 0)),
                  pl.BlockSpec((1, nqt * tq, LANES), lambda b, i: (b, i, 0))],
        out_specs=pl.BlockSpec((1, nqt * tq, NSA_W), lambda b, i: (b, i, 0)),
        out_shape=jax.ShapeDtypeStruct((bsz, seq, NSA_W), BF16),
        scratch_shapes=[pltpu.VMEM((nchain, tq, LANES), F32),
                        pltpu.VMEM((nchain, tq, 2 * LANES), F32),
                        pltpu.VMEM((nchain, tq, LANES), F32)],
        compiler_params=_params(("parallel", "parallel")),
    )(bias, main, main, main, main, main, nsel, gates)


def _diff_cmp_kernel(bias_ref, lam_ref, q_ref, k_ref, v_ref, g_ref, nq_ref, kc_ref, vc_ref, ovt_ref, gate_ref,
                     o_ref, oc_ref, nsel_ref, m_ref, acc_ref, *, tq, tk):
    i = pl.program_id(1)
    nqt = NQT_DIFF

    chains = []
    for qt in range(nqt):
        for h in range(DIFF_HEADS):
            slab = q_ref[0, qt * tq:(qt + 1) * tq, h * LANES:(h + 1) * LANES]
            for half in range(2):
                chains.append((slab * _half_mask(half), qt, h, h))

    def kv_of(h, start, width):
        cols = slice(h * LANES, (h + 1) * LANES)
        return (k_ref[0, pl.ds(start, width), cols], v_ref[0, pl.ds(start, width), cols])

    def compressed(first):
        return _cmp_stages(i, nqt * tq, nq_ref, kc_ref, vc_ref, ovt_ref, gate_ref, oc_ref, nsel_ref)

    _causal_flash(i, nqt, tq, tk, chains, kv_of, lambda h, d: bias_ref[h, d], m_ref, acc_ref, also=compressed)

    lm = lam_ref[...]
    lam = (jnp.exp(jnp.sum(lm[0:1] * lm[1:2], axis=-1, keepdims=True))
           - jnp.exp(jnp.sum(lm[2:3] * lm[3:4], axis=-1, keepdims=True)) + LAMBDA_INIT)
    for qt in range(nqt):
        for h in range(DIFF_HEADS):
            c = (qt * DIFF_HEADS + h) * 2
            a = _normalized(acc_ref, c) - lam * _normalized(acc_ref, c + 1)
            o_ref[0, qt * tq:(qt + 1) * tq, h * LANES:(h + 1) * LANES] = (
                _rms(a, g_ref[...]) * (1.0 - LAMBDA_INIT)).astype(BF16)


def _diff_cmp_attention(bias, lam_rows, main, subln, kc, vc, ovt, gates):
    bsz, seq, _ = main.shape
    nblk = kc.shape[1]
    tq, tk, nqt = TQ, TK, NQT_DIFF
    nchain = nqt * DIFF_HEADS * 2
    step = lambda b, i: (b, i, 0)
    return pl.pallas_call(
        functools.partial(_diff_cmp_kernel, tq=tq, tk=tk),
        grid=(bsz, seq // (nqt * tq)),
        in_specs=[pl.BlockSpec((DIFF_HEADS, 3, tq, tk), lambda b, i: (NSA_HEADS // DIFF_HEADS, 0, 0, 0),
                               pipeline_mode=pl.Buffered(1)),
                  pl.BlockSpec((8, LANES), lambda b, i: (0, 0)),
                  pl.BlockSpec((1, nqt * tq, DIFF_W), lambda b, i: (b, i, COL_DQ // DIFF_W)),
                  pl.BlockSpec((1, seq, DIFF_W), lambda b, i: (b, 0, COL_DK // DIFF_W)),
                  pl.BlockSpec((1, seq, DIFF_W), lambda b, i: (b, 0, COL_DV // DIFF_W)),
                  pl.BlockSpec((1, LANES), lambda b, i: (0, 0)),
                  pl.BlockSpec((1, nqt * tq, NSA_W), lambda b, i: (b, i, COL_NQ // NSA_W)),
                  pl.BlockSpec((1, nblk, LANES), lambda b, i: (b, 0, 0)),
                  pl.BlockSpec((1, nblk, LANES), lambda b, i: (b, 0, 0)),
                  pl.BlockSpec((2, LANES, nblk), lambda b, i: (0, 0, 0)),
                  pl.BlockSpec((1, nqt * tq, LANES), step)],
        out_specs=[pl.BlockSpec((1, nqt * tq, DIFF_W), step),
                   pl.BlockSpec((1, nqt * tq, NSA_W), step),
                   pl.BlockSpec((1, nqt * tq, 2 * LANES), step)],
        out_shape=[jax.ShapeDtypeStruct((bsz, seq, DIFF_W), BF16),
                   jax.ShapeDtypeStruct((bsz, seq, NSA_W), BF16),
                   jax.ShapeDtypeStruct((bsz, seq, 2 * LANES), BF16)],
        scratch_shapes=[pltpu.VMEM((nchain, tq, LANES), F32),
                        pltpu.VMEM((nchain, tq, 2 * LANES), F32)],
        compiler_params=_params(("parallel", "parallel")),
    )(bias, lam_rows, main, main, main, subln, main, kc, vc, ovt, gates)


FF_CHUNK = 256


def _tail_kernel(x_ref, od_ref, oc_ref, osw_ref, gn_ref, woraw_ref, g2_ref,
                 wg_ref, wu_ref, wd_ref, gf_ref, o_ref, acc_ref, wo_ref):
    @pl.when(pl.program_id(0) == 0)
    def _():
        wo_ref[0:DIFF_W, :] = woraw_ref[0:DIFF_W, :].astype(BF16)
        for j in range(NSA_GQ):
            for g in range(NSA_KV):
                dst = DIFF_W + (j * NSA_KV + g) * NSA_HD
                src = DIFF_W + (g * NSA_GQ + j) * NSA_HD
                wo_ref[dst:dst + NSA_HD, :] = woraw_ref[src:src + NSA_HD, :].astype(BF16)

    o_nsa = oc_ref[...].astype(F32) + osw_ref[...].astype(F32)
    o_nsa = _rms(o_nsa, gn_ref[...]).astype(BF16)
    attn = _dot(od_ref[...], wo_ref[0:DIFF_W, :]) + _dot(o_nsa, wo_ref[DIFF_W:DIFF_W + NSA_W, :])
    x1 = x_ref[...] + attn
    acc_ref[...] = x1
    h = _rms(x1, g2_ref[...]).astype(BF16)
    for c in range(0, D_FF, FF_CHUNK):
        gate = _dot(h, wg_ref[:, c:c + FF_CHUNK])
        up = _dot(h, wu_ref[:, c:c + FF_CHUNK])
        act = (gate * (1.0 / (1.0 + jnp.exp(-gate))) * up).astype(BF16)
        acc_ref[...] += _dot(act, wd_ref[c:c + FF_CHUNK, :])
    o_ref[...] = _rms(acc_ref[...], gf_ref[...])


def _tail(x2, od, oc, osw, gn, w_out, g2, wg, wu, wd, gf):
    n = x2.shape[0]
    row = lambda i: (i, 0)
    fixed = lambda i: (0, 0)
    once = pl.Buffered(1)
    return pl.pallas_call(
        _tail_kernel,
        grid=(n // TM,),
        in_specs=[pl.BlockSpec((TM, D_MODEL), row),
                  pl.BlockSpec((TM, DIFF_W), row),
                  pl.BlockSpec((TM, NSA_W), row),
                  pl.BlockSpec((TM, NSA_W), row),
                  pl.BlockSpec((1, NSA_W), fixed),
                  pl.BlockSpec((DIFF_W + NSA_W, D_MODEL), fixed, pipeline_mode=once),
                  pl.BlockSpec((1, D_MODEL), fixed),
                  pl.BlockSpec((D_MODEL, D_FF), fixed, pipeline_mode=once),
                  pl.BlockSpec((D_MODEL, D_FF), fixed, pipeline_mode=once),
                  pl.BlockSpec((D_FF, D_MODEL), fixed, pipeline_mode=once),
                  pl.BlockSpec((1, D_MODEL), fixed)],
        out_specs=pl.BlockSpec((TM, D_MODEL), row),
        out_shape=jax.ShapeDtypeStruct((n, D_MODEL), F32),
        scratch_shapes=[pltpu.VMEM((TM, D_MODEL), F32),
                        pltpu.VMEM((DIFF_W + NSA_W, D_MODEL), BF16)],
        compiler_params=_params(("arbitrary",)),
    )(x2, od, oc, osw, gn, w_out, g2, wg, wu, wd, gf)


def _permute_heads(a, axis):
    shape = a.shape
    split = shape[:axis] + (NSA_KV, NSA_GQ, NSA_HD) + shape[axis + 1:]
    return jnp.swapaxes(a.reshape(split), axis, axis + 1).reshape(shape)


def _overlap_matrices(nblk):
    n_cmp = nblk - 1
    nslc = nblk * CMP_STRIDE // SLC_BLOCK
    assert nslc == N_SLC
    cmp_start = np.arange(n_cmp) * CMP_STRIDE
    cmp_end = cmp_start + CMP_LEN - 1
    slc_start = np.arange(nslc) * SLC_BLOCK
    overlap = ((cmp_start[:, None] < slc_start[None, :] + SLC_BLOCK) & (cmp_end[:, None] >= slc_start[None, :]))
    ovt = np.zeros((NSA_KV, LANES, nblk), np.float32)
    for g, base in enumerate((HALF, 0)):
        ovt[g, base:base + nslc, :n_cmp] = overlap.T
    return ovt


def _compress_weights(pos, w1, w2):
    half = CMP_LEN // 2
    shape = (2, half, NSA_KV, NSA_HD, CMP_HIDDEN)
    w1rep = jnp.broadcast_to(w1.reshape(2, half, 1, NSA_HD, CMP_HIDDEN), shape).reshape(2, -1, CMP_HIDDEN)
    w2big = jnp.einsum('jd,pg->pjgd', w2, jnp.eye(NSA_KV, dtype=F32)).reshape(NSA_KV * CMP_HIDDEN, NSA_KV * NSA_HD)
    posr = jnp.broadcast_to(pos.reshape(2, half, 1, NSA_HD), (2, half, NSA_KV, NSA_HD)).reshape(2, -1)
    return posr, w1rep.astype(BF16), w2big.astype(BF16)


def kernel(x, norm1, w_in, lambda_q1, lambda_k1, lambda_q2, lambda_k2, diff_subln, cmp_pos_k, cmp_w1_k, cmp_w2_k,
           cmp_pos_v, cmp_w1_v, cmp_w2_v, nsa_norm, w_out, norm2, w_gate, w_up, w_down, rel_bias, final_norm):
    bsz, seq, dm = x.shape
    assert dm == D_MODEL and seq % (max(NQT_DIFF, NQT_NSA) * TQ) == 0 and seq % TM == 0 and seq == N_SLC * SLC_BLOCK
    assert norm1.shape[0] == 1, "single layer"
    x2 = x.reshape(bsz * seq, dm)

    gn_p = _permute_heads(nsa_norm[0].reshape(1, NSA_W), 1)
    posk, w1k, w2k = _compress_weights(cmp_pos_k[0], cmp_w1_k[0], cmp_w2_k[0])
    posv, w1v, w2v = _compress_weights(cmp_pos_v[0], cmp_w1_v[0], cmp_w2_v[0])
    pos = jnp.concatenate([posk, posv, jnp.zeros((4, posk.shape[1]), F32)], axis=0)
    lam_rows = jnp.zeros((8, LANES), F32).at[0:4, 0:DIFF_HD].set(
        jnp.stack([lambda_q1[0], lambda_k1[0], lambda_q2[0], lambda_k2[0]]).astype(F32))

    main, kc_grp, vc_grp, gates = _inproj(x2, norm1[0].reshape(1, dm), w_in[0], seq)
    main = main.reshape(bsz, seq, N_MAIN)

    nrow = seq // CMP_STRIDE
    kc, vc = _compress(kc_grp.reshape(bsz, nrow, CMP_STRIDE * LANES), vc_grp.reshape(bsz, nrow, CMP_STRIDE * LANES),
                       pos, w1k, w2k, w1v, w2v)
    ovt = jnp.asarray(_overlap_matrices(nrow), BF16)
    gates = gates.reshape(bsz, seq, LANES)
    bias = _bias_tiles(rel_bias)
    o_diff, o_cmp, nsel = _diff_cmp_attention(bias, lam_rows, main, diff_subln[0].reshape(1, DIFF_VD),
                                              kc, vc, ovt, gates)
    o_sw = _slc_win_attention(bias, main, nsel, gates)

    n = bsz * seq
    out = _tail(x2, o_diff.reshape(n, DIFF_W), o_cmp.reshape(n, NSA_W), o_sw.reshape(n, NSA_W),
                gn_p, w_out[0], norm2[0].reshape(1, dm),
                w_gate[0].astype(BF16), w_up[0].astype(BF16), w_down[0].astype(BF16), final_norm.reshape(1, dm))
    return out.reshape(bsz, seq, dm)
```

```python
import functools
import math

import numpy as np
import jax
import jax.numpy as jnp
from jax import lax
from jax.experimental import pallas as pl
from jax.experimental.pallas import tpu as pltpu

F32 = jnp.float32
BF16 = jnp.bfloat16

D_MODEL = 1024
DIFF_HEADS = 4
DIFF_HD = 64
DIFF_VD = 128
DIFF_W = 512
NSA_HEADS = 8
NSA_KV = 2
NSA_GQ = 4
NSA_HD = 64
NSA_W = 512
CMP_LEN = 32
CMP_STRIDE = 16
CMP_HIDDEN = 256
SLC_BLOCK = 64
SLC_TOPK = 16
N_SLC = 32
WINDOW = 512
N_BUCKETS = 32
MAX_DISTANCE = 128
D_FF = 2816
N_IN = 2840
NEG = -1e30
EPS = 1e-6
LAMBDA_INIT = 0.8 - 0.6 * math.exp(-0.3 * 0)
LOG2E = math.log2(math.e)
QSCALE = NSA_HD ** -0.5 * LOG2E

LANES = 128
HALF = LANES // 2
MXU_N = 256
COL_DQ, COL_DK, COL_DV, COL_NQ, COL_KSL, COL_VSL, COL_KW, COL_VW, N_MAIN = (
    0, 512, 1024, 1536, 2048, 2304, 2432, 2560, 2688)
N_PROJ = N_MAIN + 3 * LANES
VMEM_LIMIT = 48 * 1024 * 1024

TQ = 256
TK = 256
TM = 512
TM_IN = 1024
NQT_DIFF = 4
NQT_NSA = 2


def _t5_thresholds():
    d = np.arange(0, 4 * MAX_DISTANCE)
    max_exact = N_BUCKETS // 2
    val = (np.log(np.maximum(d, 1).astype(np.float32) / np.float32(max_exact))
           / np.float32(math.log(MAX_DISTANCE / max_exact)) * np.float32(N_BUCKETS - max_exact))
    large = np.minimum(max_exact + val.astype(np.int32), N_BUCKETS - 1)
    bucket = np.where(d < max_exact, d, large)
    assert np.all(np.diff(bucket) >= 0) and bucket[-1] == N_BUCKETS - 1
    return [int(np.argmax(bucket >= k)) for k in range(N_BUCKETS)]


T5_THRESH = _t5_thresholds()
assert T5_THRESH[-1] <= TK + 1
assert WINDOW == 2 * TK and TQ == TK


def _nt_dot(a, b):
    return lax.dot_general(a, b, (((1,), (1,)), ((), ())), preferred_element_type=F32)


def _dot(a, b):
    return jnp.dot(a, b, preferred_element_type=F32)


def _rms(x, g):
    ms = jnp.mean(x * x, axis=-1, keepdims=True)
    return x * lax.rsqrt(ms + EPS) * g


def _params(sem):
    return pltpu.CompilerParams(dimension_semantics=sem, vmem_limit_bytes=VMEM_LIMIT)


def _store_grouped(out_ref, tok_ref, y):
    tok_ref[...] = y
    nrow = out_ref.shape[0]
    for l in range(CMP_STRIDE):
        out_ref[:, l * LANES:(l + 1) * LANES] = tok_ref[pl.ds(l, nrow, stride=CMP_STRIDE), :].astype(BF16)


def _arrange_w_in(w_ref, wp_ref):
    src_dk, src_nq, src_kc, src_ksl, src_vsl, src_gt = 512, 1536, 2048, 2304, 2432, 2816
    halves = []
    halves += [(c, QSCALE) for c in range(0, src_dk, HALF)]
    halves += [(c, None) for c in range(src_dk, src_nq, HALF)]
    halves += [(src_nq + (g * NSA_GQ + j) * NSA_HD, QSCALE) for j in range(NSA_GQ) for g in range(NSA_KV)]
    halves += [(src_ksl, None), (None, None), (None, None), (src_ksl + HALF, None)]
    halves += [(c, None) for c in range(src_vsl, src_gt, HALF)]
    halves += [(c, None) for c in range(src_kc, src_ksl, HALF)]
    assert len(halves) * HALF == N_PROJ - LANES
    rows = 256
    for r0 in range(0, D_MODEL, rows):
        rs = slice(r0, r0 + rows)

        def load(entry):
            col, scale = entry
            if col is None:
                return jnp.zeros((rows, HALF), F32)
            val = w_ref[rs, col:col + HALF]
            return val if scale is None else val * scale

        for t in range(0, len(halves), 2):
            tile = jnp.concatenate([load(halves[t]), load(halves[t + 1])], axis=1)
            wp_ref[rs, (t // 2) * LANES:(t // 2 + 1) * LANES] = tile.astype(BF16)
        ngate = N_IN - src_gt
        gates = jnp.concatenate([w_ref[rs, src_gt:N_IN], jnp.zeros((rows, LANES - ngate), F32)], axis=1)
        wp_ref[rs, N_PROJ - LANES:N_PROJ] = gates.astype(BF16)


def _inproj_kernel(x_ref, g_ref, wraw_ref, main_ref, kc_ref, vc_ref, gate_ref, tok_ref, w_ref, *, seq):
    tm = x_ref.shape[0]

    @pl.when(pl.program_id(0) == 0)
    def _():
        _arrange_w_in(wraw_ref, w_ref)

    h = _rms(x_ref[...], g_ref[...]).astype(BF16)
    for c in range(0, N_PROJ, MXU_N):
        y = _dot(h, w_ref[:, c:c + MXU_N])
        if c == COL_KSL:
            t0 = lax.rem(pl.program_id(0) * tm, seq)
            blk = (t0 + lax.broadcasted_iota(jnp.int32, (tm, 2 * LANES), 0)) >> 6
            lane = lax.broadcasted_iota(jnp.int32, (tm, 2 * LANES), 1)
            slot = jnp.where(lane < LANES, lane - HALF, lane - LANES)
            y = y + jnp.where((slot == blk) & (slot >= 0) & (slot < N_SLC), NEG, 0.0)
        if c + MXU_N <= N_MAIN:
            main_ref[:, c:c + MXU_N] = y.astype(BF16)
        elif c == COL_VW:
            main_ref[:, c:c + LANES] = y[:, 0:LANES].astype(BF16)
            _store_grouped(kc_ref, tok_ref, y[:, LANES:MXU_N])
        else:
            _store_grouped(vc_ref, tok_ref, y[:, 0:LANES])
            gate_ref[...] = 1.0 / (1.0 + jnp.exp(-y[:, LANES:MXU_N]))


def _inproj(x2, g1, w_in, seq):
    n = x2.shape[0]
    tm = TM_IN
    return pl.pallas_call(
        functools.partial(_inproj_kernel, seq=seq),
        grid=(n // tm,),
        in_specs=[pl.BlockSpec((tm, D_MODEL), lambda i: (i, 0)),
                  pl.BlockSpec((1, D_MODEL), lambda i: (0, 0)),
                  pl.BlockSpec((D_MODEL, N_IN), lambda i: (0, 0), pipeline_mode=pl.Buffered(1))],
        out_specs=[pl.BlockSpec((tm, N_MAIN), lambda i: (i, 0)),
                   pl.BlockSpec((tm // CMP_STRIDE, CMP_STRIDE * LANES), lambda i: (i, 0)),
                   pl.BlockSpec((tm // CMP_STRIDE, CMP_STRIDE * LANES), lambda i: (i, 0)),
                   pl.BlockSpec((tm, LANES), lambda i: (i, 0))],
        out_shape=[jax.ShapeDtypeStruct((n, N_MAIN), BF16),
                   jax.ShapeDtypeStruct((n // CMP_STRIDE, CMP_STRIDE * LANES), BF16),
                   jax.ShapeDtypeStruct((n // CMP_STRIDE, CMP_STRIDE * LANES), BF16),
                   jax.ShapeDtypeStruct((n, LANES), F32)],
        scratch_shapes=[pltpu.VMEM((tm, LANES), F32),
                        pltpu.VMEM((D_MODEL, N_PROJ), BF16)],
        compiler_params=_params(("arbitrary",)),
    )(x2, g1, w_in)


def _gelu_tanh(x):
    return x * (0.5 * (1.0 + jnp.tanh(math.sqrt(2.0 / math.pi) * (x + 0.044715 * (x * x * x)))))


def _compress_kernel(rk_ref, rv_ref, pos_ref, w1k_ref, w2k_ref, w1v_ref, w2v_ref, kc_ref, vc_ref):
    width = rk_ref.shape[2]
    lane = lax.broadcasted_iota(jnp.int32, (1, width), 1)
    in_group = [jnp.where(((lane >> 6) & 1) == g, 1.0, 0.0) for g in range(NSA_KV)]

    def comp(r_ref, p_top, p_bot, w1_ref, w2_ref):
        r = r_ref[0].astype(F32)
        hid = []
        for g in range(NSA_KV):
            top = _dot(((r + p_top) * in_group[g]).astype(BF16), w1_ref[0])
            bot = _dot(((r + p_bot) * in_group[g]).astype(BF16), w1_ref[1])
            nrow = top.shape[0]
            hid.append(top + pltpu.roll(bot, nrow - 1, 0))
        act = _gelu_tanh(jnp.concatenate(hid, axis=1)).astype(BF16)
        return _dot(act, w2_ref[...]).astype(BF16)

    kc_ref[0] = comp(rk_ref, pos_ref[0:1, :], pos_ref[1:2, :], w1k_ref, w2k_ref)
    vc_ref[0] = comp(rv_ref, pos_ref[2:3, :], pos_ref[3:4, :], w1v_ref, w2v_ref)


def _compress(rk, rv, pos, w1k, w2k, w1v, w2v):
    bsz, nrow, width = rk.shape
    hid2 = 2 * CMP_HIDDEN
    return pl.pallas_call(
        _compress_kernel,
        grid=(bsz,),
        in_specs=[pl.BlockSpec((1, nrow, width), lambda b: (b, 0, 0)),
                  pl.BlockSpec((1, nrow, width), lambda b: (b, 0, 0)),
                  pl.BlockSpec((8, width), lambda b: (0, 0)),
                  pl.BlockSpec((2, width, CMP_HIDDEN), lambda b: (0, 0, 0)),
                  pl.BlockSpec((hid2, LANES), lambda b: (0, 0)),
                  pl.BlockSpec((2, width, CMP_HIDDEN), lambda b: (0, 0, 0)),
                  pl.BlockSpec((hid2, LANES), lambda b: (0, 0))],
        out_specs=[pl.BlockSpec((1, nrow, LANES), lambda b: (b, 0, 0)),
                   pl.BlockSpec((1, nrow, LANES), lambda b: (b, 0, 0))],
        out_shape=[jax.ShapeDtypeStruct((bsz, nrow, LANES), BF16),
                   jax.ShapeDtypeStruct((bsz, nrow, LANES), BF16)],
        compiler_params=_params(("parallel",)),
    )(rk, rv, pos, w1k, w2k, w1v, w2v)


def _half_mask(g):
    lane = lax.broadcasted_iota(jnp.int32, (1, LANES), 1)
    return jnp.where((lane >= g * HALF) & (lane < (g + 1) * HALF), 1.0, 0.0).astype(BF16)


def _nsa_chains(q_ref, extra_ref, tq, nqt):
    chains = []
    for qt in range(nqt):
        rows = slice(qt * tq, (qt + 1) * tq)
        for g in range(NSA_KV):
            keep = _half_mask(g)
            for j in range(NSA_GQ):
                q = q_ref[0, rows, j * LANES:(j + 1) * LANES] * keep
                if extra_ref is not None:
                    q = q + extra_ref[0, rows, g * LANES:(g + 1) * LANES]
                chains.append((q, qt, g, j))
    return chains


def _bias_kernel(rel_ref, o_ref, *, tq, tk):
    row = pl.program_id(0)
    col = jnp.where(row < NSA_HEADS, row + DIFF_HEADS, row - NSA_HEADS)
    r = lax.broadcasted_iota(jnp.int32, (tq, tk), 0)
    c = lax.broadcasted_iota(jnp.int32, (tq, tk), 1)
    far = rel_ref[N_BUCKETS - 1, col]
    for d in range(2):
        dist = r - c + d * tq
        val = jnp.zeros((tq, tk), F32) + rel_ref[0, col]
        for k in range(1, N_BUCKETS):
            val = jnp.where(dist >= T5_THRESH[k], rel_ref[k, col], val)
        val = (val - far) * LOG2E
        if d == 0:
            val = jnp.where(dist >= 0, val, NEG)
        o_ref[0, d] = val
    o_ref[0, 2] = jnp.where(r < c, 0.0, NEG)


def _bias_tiles(rel_bias):
    nhead = rel_bias.shape[1]
    return pl.pallas_call(
        functools.partial(_bias_kernel, tq=TQ, tk=TK),
        grid=(nhead,),
        in_specs=[pl.BlockSpec(memory_space=pltpu.SMEM)],
        out_specs=pl.BlockSpec((1, 3, TQ, TK), lambda h: (h, 0, 0, 0)),
        out_shape=jax.ShapeDtypeStruct((nhead, 3, TQ, TK), F32),
        compiler_params=_params(("parallel",)),
    )(rel_bias)


def _with_near_bias(s, near):
    add = near[0] if len(near) == 1 else jnp.concatenate(near, axis=1)
    width, nw = s.shape[1], add.shape[1]
    if nw == width:
        return s + add
    return jnp.concatenate([s[:, 0:width - nw], s[:, width - nw:width] + add], axis=1)


def _softmax_step(s, v, m_ref, acc_ref):
    tk = s.shape[1]
    m_prev = m_ref[...]
    m_new = jnp.maximum(m_prev, jnp.max(s, axis=-1, keepdims=True))
    alpha = jnp.exp2(m_prev - m_new)
    p = jnp.exp2(s - jnp.concatenate([m_new] * (tk // LANES), axis=1))
    vaug = jnp.concatenate([v, jnp.ones((tk, LANES), BF16)], axis=1)
    acc_ref[...] = jnp.concatenate([alpha, alpha], axis=1) * acc_ref[...] + _dot(p.astype(BF16), vaug)
    m_ref[...] = m_new


def _softmax_first(s, v, m_ref, acc_ref):
    rows, tk = s.shape
    m = jnp.broadcast_to(jnp.max(s, axis=-1, keepdims=True), (rows, LANES))
    p = jnp.exp2(s - jnp.concatenate([m] * (tk // LANES), axis=1))
    vaug = jnp.concatenate([v, jnp.ones((tk, LANES), BF16)], axis=1)
    acc_ref[...] = _dot(p.astype(BF16), vaug)
    m_ref[...] = m


def _softmax_once(s, v):
    rows, tk = s.shape
    m = jnp.broadcast_to(jnp.max(s, axis=-1, keepdims=True), (rows, LANES))
    p = jnp.exp2(s - jnp.concatenate([m] * (tk // LANES), axis=1))
    vaug = jnp.concatenate([v, jnp.ones((tk, LANES), BF16)], axis=1)
    pv = _dot(p.astype(BF16), vaug)
    return pv[:, 0:LANES] * (1.0 / pv[:, LANES:2 * LANES])


def _causal_flash(i, nqt, tq, tk, chains, kv_of, bias_of, m_ref, acc_ref, also=None, far_quads=False):
    def tail(first):
        stages = list(also(first)) if also is not None else []
        every = max(1, len(chains) // (len(stages) + 1))
        kv = {}
        for c, (q, qt, kv_id, bias_id) in enumerate(chains):
            ntile = (1 if first else 3) + qt
            if (kv_id, ntile) not in kv:
                start = 0 if first else pl.multiple_of((nqt * i - 2) * tk, 2 * tk)
                kv[(kv_id, ntile)] = kv_of(kv_id, start, ntile * tk)
            k, v = kv[(kv_id, ntile)]
            near = [bias_of(bias_id, d) for d in range(min(ntile, 2) - 1, -1, -1)]
            _softmax_first(_with_near_bias(_nt_dot(q, k), near), v, m_ref.at[c], acc_ref.at[c])
            if stages and (c + 1) % every == 0:
                stages.pop(0)()
        while stages:
            stages.pop(0)()

    @pl.when(i == 0)
    def _():
        tail(True)

    @pl.when(i >= 1)
    def _():
        tail(False)

    def far(first_tile, ntile):
        start = pl.multiple_of(first_tile * tk, 2 * tk)
        kv = {}
        for c, (q, _, kv_id, _) in enumerate(chains):
            if kv_id not in kv:
                kv[kv_id] = kv_of(kv_id, start, ntile * tk)
            k, v = kv[kv_id]
            _softmax_step(_nt_dot(q, k), v, m_ref.at[c], acc_ref.at[c])

    npair = jnp.maximum((nqt * i - 2) // 2, 0)
    if far_quads:
        def quad(j, carry):
            far(4 * j, 4)
            return carry

        lax.fori_loop(0, npair >> 1, quad, 0)

        @pl.when((npair & 1) == 1)
        def _():
            far(2 * (npair - 1), 2)
    else:
        def pair(j, carry):
            far(2 * j, 2)
            return carry

        lax.fori_loop(0, npair, pair, 0)


def _normalized(acc_ref, c):
    return acc_ref[c, :, 0:LANES] * (1.0 / acc_ref[c, :, LANES:2 * LANES])


def _store_gated(o_ref, gate_ref, branches, tq, qt):
    lane = lax.broadcasted_iota(jnp.int32, (1, LANES), 1)
    rows = slice(qt * tq, (qt + 1) * tq)
    gates = gate_ref[0, rows, :]
    lane_full = lax.broadcasted_iota(jnp.int32, gates.shape, 1)
    for j in range(NSA_GQ):
        total = None
        for o_of, branch in branches:
            c0 = (0 * NSA_GQ + j) * 3 + branch
            c1 = (1 * NSA_GQ + j) * 3 + branch
            gexp = jnp.take_along_axis(gates, jnp.where(lane_full < HALF, c0, c1), axis=1)
            term = gexp * jnp.where(lane < HALF, o_of(0, j), o_of(1, j))
            total = term if total is None else total + term
        o_ref[0, rows, j * LANES:(j + 1) * LANES] = total.astype(BF16)


def _nsa_chain_index(qt, g, j):
    return (qt * NSA_KV + g) * NSA_GQ + j


def _cmp_stages(qi, tq, q_ref, kc_ref, vc_ref, ovt_ref, gate_ref, o_ref, nsel_ref):
    nblk = kc_ref.shape[1]
    lane = lax.broadcasted_iota(jnp.int32, (1, LANES), 1)
    state = {"outs": [], "imp_t": jnp.zeros((LANES, tq), F32), "dropped": []}

    def attend(g):
        n = lax.broadcasted_iota(jnp.int32, (1, nblk), 1)
        t4 = qi * tq + (lax.broadcasted_iota(jnp.int32, (NSA_GQ * tq, 1), 0) & (tq - 1))
        vis4 = (n * CMP_STRIDE + (CMP_LEN - 1)) <= t4
        has4 = jnp.where(t4 >= CMP_LEN - 1, 1.0, 0.0)
        qg = jnp.concatenate([q_ref[0, :, j * LANES:(j + 1) * LANES] * _half_mask(g) for j in range(NSA_GQ)], axis=0)
        s = jnp.where(vis4, _nt_dot(qg, kc_ref[0]), NEG)
        e = jnp.exp2(s - jnp.max(s, axis=-1, keepdims=True))
        p = e * (has4 / jnp.sum(e, axis=-1, keepdims=True))
        state["outs"].append(_dot(p.astype(BF16), vc_ref[0]))
        ps = p[0:tq] + p[tq:2 * tq] + p[2 * tq:3 * tq] + p[3 * tq:4 * tq]
        hi = ps.astype(BF16)
        r1 = ps - hi.astype(F32)
        mid = r1.astype(BF16)
        lo = (r1 - mid.astype(F32)).astype(BF16)
        state["imp_t"] = state["imp_t"] + (
            _nt_dot(ovt_ref[g], hi) + _nt_dot(ovt_ref[g], mid) + _nt_dot(ovt_ref[g], lo))

    def store_out():
        outs = state["outs"]
        _store_gated(o_ref, gate_ref, [(lambda g, j: outs[g][j * tq:(j + 1) * tq], 0)], tq, 0)

    def rank(base):
        tl = qi * tq + lax.broadcasted_iota(jnp.int32, (N_SLC, tq), 1)
        jrow = lax.broadcasted_iota(jnp.int32, (N_SLC, tq), 0)
        cur = tl >> 6
        valid = (jrow * SLC_BLOCK) <= tl
        forced = (jrow == 0) | (jrow == cur) | (jrow == cur - 1)
        sub = 8
        x = jnp.where(forced, jnp.inf, jnp.where(valid, state["imp_t"][base:base + N_SLC], -jnp.inf))
        others = [jnp.broadcast_to(x[jj:jj + 1], (sub, tq)) for jj in range(N_SLC)]
        chunks = []
        for r0 in range(0, N_SLC, sub):
            xr = x[r0:r0 + sub]
            jr = r0 + lax.broadcasted_iota(jnp.int32, (sub, tq), 0)
            cnt = jnp.zeros((sub, tq), F32)
            for jj in range(N_SLC):
                if jj < r0:
                    beats = others[jj] >= xr
                elif jj >= r0 + sub:
                    beats = others[jj] > xr
                else:
                    beats = (others[jj] > xr) | ((others[jj] == xr) & (jr > jj))
                cnt = cnt + jnp.where(beats, 1.0, 0.0)
            chunks.append(jnp.where(cnt < SLC_TOPK, 0.0, 1.0))
        state["dropped"].append(jnp.concatenate(chunks, axis=0))

    def store_selection():
        zero = jnp.zeros((N_SLC, tq), F32)
        dropped = state["dropped"]
        drop = jnp.concatenate([dropped[0], zero, dropped[1], zero], axis=0).T
        nsel_ref[0, :, 0:LANES] = jnp.where(lane >= HALF, drop, 0.0).astype(BF16)
        nsel_ref[0, :, LANES:2 * LANES] = jnp.where(lane < HALF, drop, 0.0).astype(BF16)

    return [functools.partial(attend, 0), functools.partial(attend, 1), store_out,
            functools.partial(rank, 0), functools.partial(rank, HALF), store_selection]


def _slc_win_kernel(bias_ref, q_ref, ks_ref, vs_ref, kw_ref, vw_ref, nsel_ref, gate_ref, o_ref,
                    m_ref, acc_ref, win_ref, *, tq, tk):
    i = pl.program_id(1)
    nqt = NQT_NSA
    chains = [(q, qt, g, g * NSA_GQ + j) for q, qt, g, j in _nsa_chains(q_ref, nsel_ref, tq, nqt)]
    win_chains = _nsa_chains(q_ref, None, tq, nqt)

    def kv_of(g, start, width):
        return (ks_ref[0, pl.ds(start, width), g * LANES:(g + 1) * LANES], vs_ref[0, pl.ds(start, width), :])

    def window(first):
        kv = {}

        def one(q, qt, g, j):
            ntile = min(1 + qt, 3) if first else 3
            if (qt, ntile) not in kv:
                start = max(qt - 2, 0) * tk if first else pl.multiple_of((nqt * i - 2 + qt) * tk, tk)
                kv[(qt, ntile)] = (kw_ref[0, pl.ds(start, ntile * tk), :], vw_ref[0, pl.ds(start, ntile * tk), :])
            k, v = kv[(qt, ntile)]
            near = [bias_ref[g * NSA_GQ + j, d] for d in range(min(ntile, 2) - 1, -1, -1)]
            if ntile == 3:
                near.insert(0, bias_ref[g * NSA_GQ + j, 2])
            win_ref[_nsa_chain_index(qt, g, j)] = _softmax_once(_with_near_bias(_nt_dot(q, k), near), v)

        def all_chains():
            for chain in win_chains:
                one(*chain)

        return [all_chains]

    _causal_flash(i, nqt, tq, tk, chains, kv_of, lambda h, d: bias_ref[h, d], m_ref, acc_ref, also=window,
                  far_quads=True)
    for qt in range(nqt):
        _store_gated(o_ref, gate_ref,
                     [(lambda g, j: _normalized(acc_ref, _nsa_chain_index(qt, g, j)), 1),
                      (lambda g, j: win_ref[_nsa_chain_index(qt, g, j)], 2)], tq, qt)


def _slc_win_attention(bias, main, nsel, gates):
    bsz, seq, _ = main.shape
    tq, tk, nqt = TQ, TK, NQT_NSA
    nchain = nqt * NSA_HEADS
    return pl.pallas_call(
        functools.partial(_slc_win_kernel, tq=tq, tk=tk),
        grid=(bsz, seq // (nqt * tq)),
        in_specs=[pl.BlockSpec((NSA_HEADS, 3, tq, tk), lambda b, i: (0, 0, 0, 0), pipeline_mode=pl.Buffered(1)),
                  pl.BlockSpec((1, nqt * tq, NSA_W), lambda b, i: (b, i, COL_NQ // NSA_W)),
                  pl.BlockSpec((1, seq, 2 * LANES), lambda b, i: (b, 0, COL_KSL // (2 * LANES))),
                  pl.BlockSpec((1, seq, LANES), lambda b, i: (b, 0, COL_VSL // LANES)),
                  pl.BlockSpec((1, seq, LANES), lambda b, i: (b, 0, COL_KW // LANES)),
                  pl.BlockSpec((1, seq, LANES), lambda b, i: (b, 0, COL_VW // LANES)),
                  pl.BlockSpec((1, nqt * tq, 2 * LANES), lambda b, i: (b, i, 0)),
                  pl.BlockSpec((1, nqt * tq, LANES), lambda b, i: (b, i, 0))],
        out_specs=pl.BlockSpec((1, nqt * tq, NSA_W), lambda b, i: (b, i, 0)),
        out_shape=jax.ShapeDtypeStruct((bsz, seq, NSA_W), BF16),
        scratch_shapes=[pltpu.VMEM((nchain, tq, LANES), F32),
                        pltpu.VMEM((nchain, tq, 2 * LANES), F32),
                        pltpu.VMEM((nchain, tq, LANES), F32)],
        compiler_params=_params(("parallel", "parallel")),
    )(bias, main, main, main, main, main, nsel, gates)


def _diff_cmp_kernel(bias_ref, lam_ref, q_ref, k_ref, v_ref, g_ref, nq_ref, kc_ref, vc_ref, ovt_ref, gate_ref,
                     o_ref, oc_ref, nsel_ref, m_ref, acc_ref, *, tq, tk):
    i = pl.program_id(1)
    nqt = NQT_DIFF

    chains = []
    for qt in range(nqt):
        for h in range(DIFF_HEADS):
            slab = q_ref[0, qt * tq:(qt + 1) * tq, h * LANES:(h + 1) * LANES]
            for half in range(2):
                chains.append((slab * _half_mask(half), qt, h, h))

    def kv_of(h, start, width):
        cols = slice(h * LANES, (h + 1) * LANES)
        return (k_ref[0, pl.ds(start, width), cols], v_ref[0, pl.ds(start, width), cols])

    def compressed(first):
        return _cmp_stages(i, nqt * tq, nq_ref, kc_ref, vc_ref, ovt_ref, gate_ref, oc_ref, nsel_ref)

    _causal_flash(i, nqt, tq, tk, chains, kv_of, lambda h, d: bias_ref[h, d], m_ref, acc_ref, also=compressed)

    lm = lam_ref[...]
    lam = (jnp.exp(jnp.sum(lm[0:1] * lm[1:2], axis=-1, keepdims=True))
           - jnp.exp(jnp.sum(lm[2:3] * lm[3:4], axis=-1, keepdims=True)) + LAMBDA_INIT)
    for qt in range(nqt):
        for h in range(DIFF_HEADS):
            c = (qt * DIFF_HEADS + h) * 2
            a = _normalized(acc_ref, c) - lam * _normalized(acc_ref, c + 1)
            o_ref[0, qt * tq:(qt + 1) * tq, h * LANES:(h + 1) * LANES] = (
                _rms(a, g_ref[...]) * (1.0 - LAMBDA_INIT)).astype(BF16)


def _diff_cmp_attention(bias, lam_rows, main, subln, kc, vc, ovt, gates):
    bsz, seq, _ = main.shape
    nblk = kc.shape[1]
    tq, tk, nqt = TQ, TK, NQT_DIFF
    nchain = nqt * DIFF_HEADS * 2
    step = lambda b, i: (b, i, 0)
    return pl.pallas_call(
        functools.partial(_diff_cmp_kernel, tq=tq, tk=tk),
        grid=(bsz, seq // (nqt * tq)),
        in_specs=[pl.BlockSpec((DIFF_HEADS, 3, tq, tk), lambda b, i: (NSA_HEADS // DIFF_HEADS, 0, 0, 0),
                               pipeline_mode=pl.Buffered(1)),
                  pl.BlockSpec((8, LANES), lambda b, i: (0, 0)),
                  pl.BlockSpec((1, nqt * tq, DIFF_W), lambda b, i: (b, i, COL_DQ // DIFF_W)),
                  pl.BlockSpec((1, seq, DIFF_W), lambda b, i: (b, 0, COL_DK // DIFF_W)),
                  pl.BlockSpec((1, seq, DIFF_W), lambda b, i: (b, 0, COL_DV // DIFF_W)),
                  pl.BlockSpec((1, LANES), lambda b, i: (0, 0)),
                  pl.BlockSpec((1, nqt * tq, NSA_W), lambda b, i: (b, i, COL_NQ // NSA_W)),
                  pl.BlockSpec((1, nblk, LANES), lambda b, i: (b, 0, 0)),
                  pl.BlockSpec((1, nblk, LANES), lambda b, i: (b, 0, 0)),
                  pl.BlockSpec((2, LANES, nblk), lambda b, i: (0, 0, 0)),
                  pl.BlockSpec((1, nqt * tq, LANES), step)],
        out_specs=[pl.BlockSpec((1, nqt * tq, DIFF_W), step),
                   pl.BlockSpec((1, nqt * tq, NSA_W), step),
                   pl.BlockSpec((1, nqt * tq, 2 * LANES), step)],
        out_shape=[jax.ShapeDtypeStruct((bsz, seq, DIFF_W), BF16),
                   jax.ShapeDtypeStruct((bsz, seq, NSA_W), BF16),
                   jax.ShapeDtypeStruct((bsz, seq, 2 * LANES), BF16)],
        scratch_shapes=[pltpu.VMEM((nchain, tq, LANES), F32),
                        pltpu.VMEM((nchain, tq, 2 * LANES), F32)],
        compiler_params=_params(("parallel", "parallel")),
    )(bias, lam_rows, main, main, main, subln, main, kc, vc, ovt, gates)


FF_CHUNK = 256


def _tail_kernel(x_ref, od_ref, oc_ref, osw_ref, gn_ref, woraw_ref, g2_ref,
                 wg_ref, wu_ref, wd_ref, gf_ref, o_ref, acc_ref, wo_ref):
    @pl.when(pl.program_id(0) == 0)
    def _():
        wo_ref[0:DIFF_W, :] = woraw_ref[0:DIFF_W, :].astype(BF16)
        for j in range(NSA_GQ):
            for g in range(NSA_KV):
                dst = DIFF_W + (j * NSA_KV + g) * NSA_HD
                src = DIFF_W + (g * NSA_GQ + j) * NSA_HD
                wo_ref[dst:dst + NSA_HD, :] = woraw_ref[src:src + NSA_HD, :].astype(BF16)

    o_nsa = oc_ref[...].astype(F32) + osw_ref[...].astype(F32)
    o_nsa = _rms(o_nsa, gn_ref[...]).astype(BF16)
    attn = _dot(od_ref[...], wo_ref[0:DIFF_W, :]) + _dot(o_nsa, wo_ref[DIFF_W:DIFF_W + NSA_W, :])
    x1 = x_ref[...] + attn
    acc_ref[...] = x1
    h = _rms(x1, g2_ref[...]).astype(BF16)
    for c in range(0, D_FF, FF_CHUNK):
        gate = _dot(h, wg_ref[:, c:c + FF_CHUNK])
        up = _dot(h, wu_ref[:, c:c + FF_CHUNK])
        act = (gate * (1.0 / (1.0 + jnp.exp(-gate))) * up).astype(BF16)
        acc_ref[...] += _dot(act, wd_ref[c:c + FF_CHUNK, :])
    o_ref[...] = _rms(acc_ref[...], gf_ref[...])


def _tail(x2, od, oc, osw, gn, w_out, g2, wg, wu, wd, gf):
    n = x2.shape[0]
    row = lambda i: (i, 0)
    fixed = lambda i: (0, 0)
    once = pl.Buffered(1)
    return pl.pallas_call(
        _tail_kernel,
        grid=(n // TM,),
        in_specs=[pl.BlockSpec((TM, D_MODEL), row),
                  pl.BlockSpec((TM, DIFF_W), row),
                  pl.BlockSpec((TM, NSA_W), row),
                  pl.BlockSpec((TM, NSA_W), row),
                  pl.BlockSpec((1, NSA_W), fixed),
                  pl.BlockSpec((DIFF_W + NSA_W, D_MODEL), fixed, pipeline_mode=once),
                  pl.BlockSpec((1, D_MODEL), fixed),
                  pl.BlockSpec((D_MODEL, D_FF), fixed, pipeline_mode=once),
                  pl.BlockSpec((D_MODEL, D_FF), fixed, pipeline_mode=once),
                  pl.BlockSpec((D_FF, D_MODEL), fixed, pipeline_mode=once),
                  pl.BlockSpec((1, D_MODEL), fixed)],
        out_specs=pl.BlockSpec((TM, D_MODEL), row),
        out_shape=jax.ShapeDtypeStruct((n, D_MODEL), F32),
        scratch_shapes=[pltpu.VMEM((TM, D_MODEL), F32),
                        pltpu.VMEM((DIFF_W + NSA_W, D_MODEL), BF16)],
        compiler_params=_params(("arbitrary",)),
    )(x2, od, oc, osw, gn, w_out, g2, wg, wu, wd, gf)


def _permute_heads(a, axis):
    shape = a.shape
    split = shape[:axis] + (NSA_KV, NSA_GQ, NSA_HD) + shape[axis + 1:]
    return jnp.swapaxes(a.reshape(split), axis, axis + 1).reshape(shape)


def _overlap_matrices(nblk):
    n_cmp = nblk - 1
    nslc = nblk * CMP_STRIDE // SLC_BLOCK
    assert nslc == N_SLC
    cmp_start = np.arange(n_cmp) * CMP_STRIDE
    cmp_end = cmp_start + CMP_LEN - 1
    slc_start = np.arange(nslc) * SLC_BLOCK
    overlap = ((cmp_start[:, None] < slc_start[None, :] + SLC_BLOCK) & (cmp_end[:, None] >= slc_start[None, :]))
    ovt = np.zeros((NSA_KV, LANES, nblk), np.float32)
    for g, base in enumerate((HALF, 0)):
        ovt[g, base:base + nslc, :n_cmp] = overlap.T
    return ovt


def _compress_weights(pos, w1, w2):
    half = CMP_LEN // 2
    shape = (2, half, NSA_KV, NSA_HD, CMP_HIDDEN)
    w1rep = jnp.broadcast_to(w1.reshape(2, half, 1, NSA_HD, CMP_HIDDEN), shape).reshape(2, -1, CMP_HIDDEN)
    w2big = jnp.einsum('jd,pg->pjgd', w2, jnp.eye(NSA_KV, dtype=F32)).reshape(NSA_KV * CMP_HIDDEN, NSA_KV * NSA_HD)
    posr = jnp.broadcast_to(pos.reshape(2, half, 1, NSA_HD), (2, half, NSA_KV, NSA_HD)).reshape(2, -1)
    return posr, w1rep.astype(BF16), w2big.astype(BF16)


def kernel(x, norm1, w_in, lambda_q1, lambda_k1, lambda_q2, lambda_k2, diff_subln, cmp_pos_k, cmp_w1_k, cmp_w2_k,
           cmp_pos_v, cmp_w1_v, cmp_w2_v, nsa_norm, w_out, norm2, w_gate, w_up, w_down, rel_bias, final_norm):
    bsz, seq, dm = x.shape
    assert dm == D_MODEL and seq % (max(NQT_DIFF, NQT_NSA) * TQ) == 0 and seq % TM == 0 and seq == N_SLC * SLC_BLOCK
    assert norm1.shape[0] == 1, "single layer"
    x2 = x.reshape(bsz * seq, dm)

    gn_p = _permute_heads(nsa_norm[0].reshape(1, NSA_W), 1)
    posk, w1k, w2k = _compress_weights(cmp_pos_k[0], cmp_w1_k[0], cmp_w2_k[0])
    posv, w1v, w2v = _compress_weights(cmp_pos_v[0], cmp_w1_v[0], cmp_w2_v[0])
    pos = jnp.concatenate([posk, posv, jnp.zeros((4, posk.shape[1]), F32)], axis=0)
    lam_rows = jnp.zeros((8, LANES), F32).at[0:4, 0:DIFF_HD].set(
        jnp.stack([lambda_q1[0], lambda_k1[0], lambda_q2[0], lambda_k2[0]]).astype(F32))

    main, kc_grp, vc_grp, gates = _inproj(x2, norm1[0].reshape(1, dm), w_in[0], seq)
    main = main.reshape(bsz, seq, N_MAIN)

    nrow = seq // CMP_STRIDE
    kc, vc = _compress(kc_grp.reshape(bsz, nrow, CMP_STRIDE * LANES), vc_grp.reshape(bsz, nrow, CMP_STRIDE * LANES),
                       pos, w1k, w2k, w1v, w2v)
    ovt = jnp.asarray(_overlap_matrices(nrow), BF16)
    gates = gates.reshape(bsz, seq, LANES)
    bias = _bias_tiles(rel_bias)
    o_diff, o_cmp, nsel = _diff_cmp_attention(bias, lam_rows, main, diff_subln[0].reshape(1, DIFF_VD),
                                              kc, vc, ovt, gates)
    o_sw = _slc_win_attention(bias, main, nsel, gates)

    n = bsz * seq
    out = _tail(x2, o_diff.reshape(n, DIFF_W), o_cmp.reshape(n, NSA_W), o_sw.reshape(n, NSA_W),
                gn_p, w_out[0], norm2[0].reshape(1, dm),
                w_gate[0].astype(BF16), w_up[0].astype(BF16), w_down[0].astype(BF16), final_norm.reshape(1, dm))
    return out.reshape(bsz, seq, dm)
```
